```python
import math
import jax
import jax.numpy as jnp
from jax import lax
import numpy as np

D_MODEL = 1024
BATCH = 16
SEQ = 4096
DEPTH = 1
DEC_BATCH = 32
DEC_SEQ = 32
PAST_LEN = 4096

CHUNK = 64
D_MIX = D_MODEL
GDN_WIDTH = D_MIX // 2
GDN_HEAD_DIM = 128
GDN_HEADS = GDN_WIDTH // GDN_HEAD_DIM
CONV_W = 4
QKV_A = 3 * GDN_WIDTH
DIFF_WIDTH = D_MIX - GDN_WIDTH
DIFF_HEAD_DIM = 64
DIFF_HEADS = DIFF_WIDTH // (2 * DIFF_HEAD_DIM)
DIFF_V_DIM = 2 * DIFF_HEAD_DIM
Q_BLOCK = 128
N_BUCKETS = 32
REL_MAX_DIST = 128
PEER_HEADS = 8
N_KEYS = 128
N_EXPERTS = N_KEYS * N_KEYS
PEER_D_KEY = 256
PEER_HALF = PEER_D_KEY // 2
PEER_TOPK = 16
PEER_BLOCK = 256
EPS = 1e-6
F32 = jnp.float32
OFF_Z = QKV_A
OFF_A = OFF_Z + GDN_WIDTH
OFF_B = OFF_A + GDN_HEADS
OFF_QB = OFF_B + GDN_HEADS
OFF_KB = OFF_QB + DIFF_WIDTH
OFF_VB = OFF_KB + DIFF_WIDTH
IN_COLS = OFF_VB + DIFF_WIDTH

kernel_name = 'hybrid_gdn_diffattn_peer_stream'


def rmsnorm(x, w):
    xf = x.astype(F32)
    y = xf * lax.rsqrt(jnp.mean(xf * xf, axis=-1, keepdims=True) + EPS)
    return (y * w.astype(F32)).astype(x.dtype)


def l2norm(x):
    xf = x.astype(F32)
    return xf * lax.rsqrt(jnp.sum(xf * xf, axis=-1, keepdims=True) + EPS)


def causal_conv(x, buf, w):
    L = x.shape[1]
    xp = jnp.concatenate([buf.astype(x.dtype), x], axis=1)
    y = xp[:, 0:L] * w[0]
    for j in range(1, CONV_W):
        y = y + xp[:, j:j + L] * w[j]
    return jax.nn.silu(y), xp[:, -(CONV_W - 1):]


def gated_delta_chunked(q, k, v, g, beta, s0, chunk):
    B, L, H, _ = q.shape
    dv = v.shape[-1]
    nc = L // chunk

    def blk(t):
        t = t.reshape((B, nc, chunk, H) + t.shape[3:])
        return jnp.moveaxis(jnp.moveaxis(t, 1, 0), 3, 2)

    q, k, v, g, beta = blk(q), blk(k), blk(v), blk(g), blk(beta)
    gc = jnp.cumsum(g, axis=-1)
    idx = jnp.arange(chunk)
    lower = idx[:, None] >= idx[None, :]
    strict = idx[:, None] > idx[None, :]
    decay = jnp.exp(jnp.where(lower, gc[..., :, None] - gc[..., None, :], -jnp.inf))
    kb = k * beta[..., None]
    a_mat = jnp.where(strict, jnp.einsum('nbhid,nbhjd->nbhij', kb, k) * decay, 0.0)
    m = a_mat + jnp.eye(chunk, dtype=F32)
    rhs = jnp.concatenate([v * beta[..., None], kb * jnp.exp(gc)[..., None]], axis=-1)
    sol = lax.linalg.triangular_solve(m, rhs, left_side=True, lower=True, unit_diagonal=True)
    u_v, w = sol[..., :dv], sol[..., dv:]
    qk = jnp.einsum('nbhid,nbhjd->nbhij', q, k) * decay
    q_dec = q * jnp.exp(gc)[..., None]
    k_dec = k * jnp.exp(gc[..., -1:] - gc)[..., None]
    g_last = jnp.exp(gc[..., -1])

    def step(s, inp):
        u_c, w_c, qk_c, qd_c, kd_c, gl_c = inp
        v_new = u_c - jnp.einsum('bhck,bhkv->bhcv', w_c, s)
        o = jnp.einsum('bhck,bhkv->bhcv', qd_c, s) + jnp.einsum('bhij,bhjv->bhiv', qk_c, v_new)
        s = s * gl_c[..., None, None] + jnp.einsum('bhck,bhcv->bhkv', kd_c, v_new)
        return s, o

    s_final, o = lax.scan(step, s0, (u_v, w, qk, q_dec, k_dec, g_last))
    o = jnp.swapaxes(jnp.moveaxis(o, 0, 1), 2, 3).reshape(B, L, H, dv)
    return o, s_final


def gdn_mixer(qkv, z, a, b, conv_buf, s0, lp, chunk):
    B, L, _ = qkv.shape
    qkv_c, conv_new = causal_conv(qkv, conv_buf, lp['conv_w'])
    q, k, v = jnp.split(qkv_c, 3, axis=-1)
    shp = (B, L, GDN_HEADS, GDN_HEAD_DIM)
    q = l2norm(q.reshape(shp)) * (GDN_HEAD_DIM ** -0.5)
    k = l2norm(k.reshape(shp))
    v = v.reshape(shp).astype(F32)
    g = -jnp.exp(lp['a_log'].astype(F32)) * jax.nn.softplus(a.astype(F32) + lp['dt_bias'].astype(F32))
    beta = jax.nn.sigmoid(b.astype(F32))
    o, s_new = gated_delta_chunked(q, k, v, g, beta, s0.astype(F32), chunk)
    o = rmsnorm(o, lp['gdn_onorm']) * jax.nn.silu(z.reshape(shp).astype(F32))
    return o.reshape(B, L, GDN_WIDTH), conv_new, s_new


def rel_bucket(rel):
    nb = N_BUCKETS // 2
    max_exact = nb // 2
    ret = jnp.where(rel > 0, nb, 0)
    n = jnp.abs(rel)
    large = max_exact + (jnp.log(jnp.maximum(n, 1).astype(F32) / max_exact)
                         / math.log(REL_MAX_DIST / max_exact) * (nb - max_exact)).astype(jnp.int32)
    large = jnp.minimum(large, nb - 1)
    return ret + jnp.where(n < max_exact, n, large)


def diff_attn_core(q, k, v, q_pos, k_pos, rel_table, lam):
    s = jnp.einsum('bqhtd,bkhtd->bthqk', q, k).astype(F32) * (DIFF_HEAD_DIM ** -0.5)
    bias = jnp.moveaxis(rel_table[rel_bucket(k_pos[None, :] - q_pos[:, None])].astype(F32), -1, 0)
    visible = (k_pos[None, :] // CHUNK) <= (q_pos[:, None] // CHUNK)
    p = jax.nn.softmax(jnp.where(visible, s + bias, -jnp.inf), axis=-1)
    attn = p[:, 0] - lam * p[:, 1]
    return jnp.einsum('bhqk,bkhv->bqhv', attn.astype(v.dtype), v)


def diff_attn_prompt(q, k, v, rel_table, lam):
    B, L = q.shape[:2]
    nb = L // Q_BLOCK
    qb = jnp.moveaxis(q.reshape((B, nb, Q_BLOCK) + q.shape[2:]), 1, 0)
    k_pos = jnp.arange(L)
    starts = jnp.arange(nb) * Q_BLOCK

    def one(args):
        qi, st = args
        return diff_attn_core(qi, k, v, st + jnp.arange(Q_BLOCK), k_pos, rel_table, lam)

    o = lax.map(one, (qb, starts))
    return jnp.moveaxis(o, 0, 1).reshape(B, L, DIFF_HEADS, DIFF_V_DIM)


def peer(x, lp):
    T = x.shape[0]
    pad = (-T) % PEER_BLOCK
    xp = jnp.pad(x, ((0, pad), (0, 0))).reshape(-1, PEER_BLOCK, D_MODEL)
    w_q, sub_keys, u_tab, v_tab = lp['peer_wq'], lp['peer_keys'], lp['peer_u'], lp['peer_v']

    def one(xb):
        qh = (xb @ w_q).reshape(PEER_BLOCK, PEER_HEADS, 2, PEER_HALF)
        sc = jnp.einsum('thpd,hpnd->thpn', qh, sub_keys).astype(F32)
        s1, i1 = lax.top_k(sc[:, :, 0], PEER_TOPK)
        s2, i2 = lax.top_k(sc[:, :, 1], PEER_TOPK)
        cand_s = (s1[..., :, None] + s2[..., None, :]).reshape(PEER_BLOCK, PEER_HEADS, PEER_TOPK * PEER_TOPK)
        cand_i = (i1[..., :, None] * N_KEYS + i2[..., None, :]).reshape(PEER_BLOCK, PEER_HEADS, PEER_TOPK * PEER_TOPK)
        top_s, pos = lax.top_k(cand_s, PEER_TOPK)
        eidx = jnp.take_along_axis(cand_i, pos, axis=-1)
        gate = jax.nn.softmax(top_s, axis=-1)
        u = jnp.take(u_tab, eidx, axis=0)
        act = jax.nn.gelu(jnp.einsum('thkd,td->thk', u, xb).astype(F32), approximate=False)
        coef = (gate * act).astype(xb.dtype)
        vv = jnp.take(v_tab, eidx, axis=0)
        return jnp.einsum('thk,thkd->td', coef, vv)

    return lax.map(one, xp).reshape(-1, D_MODEL)[:T]


def trunk_layer(x, c, conv_buf, s0, k_past, v_past, lp, rel_table, lam_init, prompt):
    B, L, _ = x.shape
    mod = (jax.nn.silu(c) @ lp['w_ada'] + lp['b_ada'])[:, None, :]
    sh1, sc1, g1, sh2, sc2, g2 = jnp.split(mod, 6, axis=-1)
    h = rmsnorm(x, lp['norm1']) * (1.0 + sc1) + sh1
    qkv_a, z_a, a_a, b_a, q_b, k_b, v_b = jnp.split(
        h @ lp['w_in'], [OFF_Z, OFF_A, OFF_B, OFF_QB, OFF_KB, OFF_VB], axis=-1)
    if prompt:
        conv_buf = jnp.zeros((B, CONV_W - 1, QKV_A), x.dtype)
        s0 = jnp.zeros((B, GDN_HEADS, GDN_HEAD_DIM, GDN_HEAD_DIM), F32)
    o_a, conv_new, s_new = gdn_mixer(qkv_a, z_a, a_a, b_a, conv_buf, s0, lp, CHUNK if prompt else L)
    lam = (jnp.exp(jnp.sum(lp['lam_q1'].astype(F32) * lp['lam_k1'].astype(F32)))
           - jnp.exp(jnp.sum(lp['lam_q2'].astype(F32) * lp['lam_k2'].astype(F32))) + lam_init)
    q = q_b.reshape(B, L, DIFF_HEADS, 2, DIFF_HEAD_DIM)
    k_new = k_b.reshape(B, L, DIFF_HEADS, 2 * DIFF_HEAD_DIM)
    v_new = v_b.reshape(B, L, DIFF_HEADS, DIFF_V_DIM)
    if prompt:
        o_b = diff_attn_prompt(q, k_new.reshape(B, L, DIFF_HEADS, 2, DIFF_HEAD_DIM), v_new, rel_table, lam)
    else:
        P = k_past.shape[1]
        k_all = jnp.concatenate([k_past.astype(k_new.dtype), k_new], axis=1).reshape(
            B, P + L, DIFF_HEADS, 2, DIFF_HEAD_DIM)
        v_all = jnp.concatenate([v_past.astype(v_new.dtype), v_new], axis=1)
        o_b = diff_attn_core(q, k_all, v_all, P + jnp.arange(L), jnp.arange(P + L), rel_table, lam)
    o_b = rmsnorm(o_b, lp['diff_subln']) * (1.0 - lam_init)
    mixed = jnp.concatenate([o_a.astype(x.dtype), o_b.reshape(B, L, DIFF_WIDTH).astype(x.dtype)], axis=-1)
    x = x + g1 * (mixed @ lp['w_out'])
    h2 = rmsnorm(x, lp['norm2']) * (1.0 + sc2) + sh2
    x = x + g2 * peer(h2.reshape(B * L, D_MODEL), lp).reshape(B, L, D_MODEL)
    return x, (k_new, v_new, s_new.astype(x.dtype), conv_new.astype(x.dtype))


def setup_inputs(seed: int = 0) -> dict:
    key = jax.random.key(seed)
    ks = jax.random.split(key, 32)

    def nrm(k, shape, s):
        return jax.random.normal(k, shape, F32) * s

    def gain(k, shape):
        return 1.0 + 0.02 * jax.random.normal(k, shape, F32)

    dt = jnp.exp(jax.random.uniform(ks[10], (DEPTH, GDN_HEADS), F32) * (math.log(0.1) - math.log(0.001))
                 + math.log(0.001))
    return {
        'x_prompt': nrm(ks[0], (BATCH, SEQ, D_MODEL), 1.0),
        'x_sample': nrm(ks[1], (DEC_BATCH, DEC_SEQ, D_MODEL), 1.0),
        'c_prompt': nrm(ks[2], (BATCH, D_MODEL), 1.0),
        'c_sample': nrm(ks[3], (DEC_BATCH, D_MODEL), 1.0),
        'cache_k': nrm(ks[4], (DEPTH, DEC_BATCH, PAST_LEN, DIFF_HEADS, 2 * DIFF_HEAD_DIM), 1.0),
        'cache_v': nrm(ks[5], (DEPTH, DEC_BATCH, PAST_LEN, DIFF_HEADS, DIFF_V_DIM), 1.0),
        'state_gdn': nrm(ks[6], (DEPTH, DEC_BATCH, GDN_HEADS, GDN_HEAD_DIM, GDN_HEAD_DIM), 0.1),
        'state_conv': nrm(ks[7], (DEPTH, DEC_BATCH, CONV_W - 1, QKV_A), 1.0),
        'w_ada': nrm(ks[8], (DEPTH, D_MODEL, 6 * D_MODEL), 0.5 * D_MODEL ** -0.5),
        'b_ada': nrm(ks[9], (DEPTH, 6 * D_MODEL), 0.02),
        'norm1': gain(ks[11], (DEPTH, D_MODEL)),
        'norm2': gain(ks[12], (DEPTH, D_MODEL)),
        'w_in': nrm(ks[13], (DEPTH, D_MODEL, IN_COLS), D_MODEL ** -0.5),
        'conv_w': nrm(ks[14], (DEPTH, CONV_W, QKV_A), CONV_W ** -0.5),
        'a_log': jnp.log(jax.random.uniform(ks[15], (DEPTH, GDN_HEADS), F32, 1.0, 16.0)),
        'dt_bias': dt + jnp.log(-jnp.expm1(-dt)),
        'gdn_onorm': gain(ks[16], (DEPTH, GDN_HEAD_DIM)),
        'lam_q1': nrm(ks[17], (DEPTH, DIFF_HEAD_DIM), 0.1),
        'lam_k1': nrm(ks[18], (DEPTH, DIFF_HEAD_DIM), 0.1),
        'lam_q2': nrm(ks[19], (DEPTH, DIFF_HEAD_DIM), 0.1),
        'lam_k2': nrm(ks[20], (DEPTH, DIFF_HEAD_DIM), 0.1),
        'diff_subln': gain(ks[21], (DEPTH, DIFF_V_DIM)),
        'w_out': nrm(ks[22], (DEPTH, D_MIX, D_MODEL), D_MIX ** -0.5),
        'peer_wq': nrm(ks[23], (DEPTH, D_MODEL, PEER_HEADS * PEER_D_KEY), D_MODEL ** -0.5),
        'peer_keys': nrm(ks[24], (DEPTH, PEER_HEADS, 2, N_KEYS, PEER_HALF), PEER_HALF ** -0.5),
        'peer_u': nrm(ks[25], (DEPTH, N_EXPERTS, D_MODEL), D_MODEL ** -0.5),
        'peer_v': nrm(ks[26], (DEPTH, N_EXPERTS, D_MODEL), PEER_HEADS ** -0.5),
        'rel_table': nrm(ks[27], (N_BUCKETS, DIFF_HEADS), 0.5),
        'final_norm': gain(ks[28], (D_MODEL,)),
    }


def reference(x_prompt, x_sample, c_prompt, c_sample, cache_k, cache_v, state_gdn, state_conv,
              w_ada, b_ada, norm1, norm2, w_in, conv_w, a_log, dt_bias, gdn_onorm,
              lam_q1, lam_k1, lam_q2, lam_k2, diff_subln, w_out,
              peer_wq, peer_keys, peer_u, peer_v, rel_table, final_norm):
    hp, hs = x_prompt, x_sample
    kp_l, vp_l, sp_l, cp_l = [], [], [], []
    ks_l, vs_l, ss_l, cs_l = [], [], [], []
    for l in range(DEPTH):
        lp = dict(w_ada=w_ada[l], b_ada=b_ada[l], norm1=norm1[l], norm2=norm2[l], w_in=w_in[l],
                  conv_w=conv_w[l], a_log=a_log[l], dt_bias=dt_bias[l], gdn_onorm=gdn_onorm[l],
                  lam_q1=lam_q1[l], lam_k1=lam_k1[l], lam_q2=lam_q2[l], lam_k2=lam_k2[l],
                  diff_subln=diff_subln[l], w_out=w_out[l], peer_wq=peer_wq[l],
                  peer_keys=peer_keys[l], peer_u=peer_u[l], peer_v=peer_v[l])
        lam_init = 0.8 - 0.6 * math.exp(-0.3 * l)
        hp, (kp, vp, sp, cp) = trunk_layer(hp, c_prompt, None, None, None, None, lp, rel_table,
                                           lam_init, True)
        hs, (ksm, vsm, ssm, csm) = trunk_layer(hs, c_sample, state_conv[l], state_gdn[l], cache_k[l],
                                               cache_v[l], lp, rel_table, lam_init, False)
        kp_l.append(kp); vp_l.append(vp); sp_l.append(sp); cp_l.append(cp)
        ks_l.append(ksm); vs_l.append(vsm); ss_l.append(ssm); cs_l.append(csm)
    y_prompt = rmsnorm(hp, final_norm)
    y_sample = rmsnorm(hs, final_norm)
    return (y_prompt, y_sample,
            jnp.stack(kp_l), jnp.stack(vp_l), jnp.stack(sp_l), jnp.stack(cp_l),
            jnp.stack(ks_l), jnp.stack(vs_l), jnp.stack(ss_l), jnp.stack(cs_l))
```

```python
import functools
import math

import jax
import jax.numpy as jnp
from jax import lax
from jax.experimental import pallas as pl
from jax.experimental.pallas import tpu as pltpu
from jax.experimental.pallas import tpu_sc as plsc

F32 = jnp.float32
BF16 = jnp.bfloat16
EPS = 1e-6

D_MODEL = 1024
CHUNK = 64
GDN_HEADS = 4
GDN_HEAD_DIM = 128
GDN_WIDTH = GDN_HEADS * GDN_HEAD_DIM
CONV_W = 4
QKV_A = 3 * GDN_WIDTH
DIFF_HEADS = 4
DIFF_HEAD_DIM = 64
DIFF_V_DIM = 128
DIFF_WIDTH = DIFF_HEADS * 2 * DIFF_HEAD_DIM
ATT_BLOCK = 128
N_BUCKETS = 32
REL_MAX_DIST = 128
PEER_HEADS = 8
N_KEYS = 128
PEER_HALF = 128
PEER_TOPK = 16
PEER_SLOTS = PEER_HEADS * PEER_TOPK
LANES = 128
NEG_BIG = -1e30
VMEM_LIMIT = 56 * 1024 * 1024

_C_QKV, _C_Z, _C_AB, _C_QB, _C_KB, _C_VB = 0, 1536, 2048, 2176, 2688, 3200
_C_END = 3712


def _params(sem):
    return pltpu.CompilerParams(dimension_semantics=sem, vmem_limit_bytes=VMEM_LIMIT)


def _dot(a, b, precision=None):
    return jnp.dot(a, b, preferred_element_type=F32, precision=precision)


def _dot_nt(a, b, precision=None):
    return lax.dot_general(a, b, (((1,), (1,)), ((), ())), preferred_element_type=F32, precision=precision)


def _silu(x):
    return x * jax.nn.sigmoid(x)


def _ada_kernel(c_ref, w_ref, b_ref, o_ref):
    a = _silu(c_ref[...]).astype(BF16)
    o_ref[...] = _dot(a, w_ref[...].astype(BF16)) + b_ref[...]


def _ada(c, w_ada, b_ada):
    n, d = c.shape
    cols = w_ada.shape[1]
    tn = 1024
    return pl.pallas_call(
        _ada_kernel,
        grid=(cols // tn,),
        in_specs=[pl.BlockSpec((n, d), lambda j: (0, 0)),
                  pl.BlockSpec((d, tn), lambda j: (0, j)),
                  pl.BlockSpec((1, tn), lambda j: (0, j))],
        out_specs=pl.BlockSpec((n, tn), lambda j: (0, j)),
        out_shape=jax.ShapeDtypeStruct((n, cols), F32),
        compiler_params=_params(("parallel",)),
        name="ada",
    )(c, w_ada, b_ada.reshape(1, cols))


def _modulated_norm(x, gain, shift, scale):
    y = x * lax.rsqrt(jnp.mean(x * x, axis=-1, keepdims=True) + EPS)
    return (y * gain) * (1.0 + scale) + shift


def _inproj_kernel(x_ref, mod_ref, n1_ref, w_ref, qkv_ref, z_ref, ab_ref, qb_ref, kb_ref, vb_ref):
    h = _modulated_norm(x_ref[0], n1_ref[...], mod_ref[0, 0:1, :], mod_ref[0, 1:2, :]).astype(BF16)
    qkv_ref[0] = _dot(h, w_ref[:, _C_QKV:_C_Z])
    z_ref[0] = _dot(h, w_ref[:, _C_Z:_C_AB])
    ab_ref[0] = _dot(h, w_ref[:, _C_AB:_C_QB])
    qb_ref[0] = _dot(h, w_ref[:, _C_QB:_C_KB])
    kb_ref[0] = _dot(h, w_ref[:, _C_KB:_C_VB])
    vb_ref[0] = _dot(h, w_ref[:, _C_VB:_C_END])


def _inproj(x, mod, norm1, w_packed):
    b, l, d = x.shape
    tl = min(l, 256)
    widths = (QKV_A, GDN_WIDTH, LANES, DIFF_WIDTH, DIFF_WIDTH, DIFF_WIDTH)
    return pl.pallas_call(
        _inproj_kernel,
        grid=(b, l // tl),
        in_specs=[pl.BlockSpec((1, tl, d), lambda i, j: (i, j, 0)),
                  pl.BlockSpec((1, 6, d), lambda i, j: (i, 0, 0)),
                  pl.BlockSpec((1, d), lambda i, j: (0, 0)),
                  pl.BlockSpec((d, _C_END), lambda i, j: (0, 0))],
        out_specs=[pl.BlockSpec((1, tl, w), lambda i, j: (i, j, 0)) for w in widths],
        out_shape=[jax.ShapeDtypeStruct((b, l, w), F32) for w in widths],
        compiler_params=_params(("parallel", "parallel")),
        name="inproj",
    )(x, mod, norm1.reshape(1, d), w_packed)


_HI = lax.Precision.HIGHEST


def _unit_lower_inverse(a, n):
    r = lax.broadcasted_iota(jnp.int32, (n, n), 0)
    c = lax.broadcasted_iota(jnp.int32, (n, n), 1)
    eye = (r == c).astype(F32)
    ad = jnp.where((r // 8) == (c // 8), a, 0.0)
    a2 = _dot(ad, ad, _HI)
    a4 = _dot(a2, a2, _HI)
    x = eye - ad
    x = x + _dot(x, a2, _HI)
    x = x + _dot(x, a4, _HI)
    bs = 8
    while bs < n:
        off = ((r // (2 * bs)) == (c // (2 * bs))) & ((r // bs) != (c // bs))
        x = x - _dot(x, _dot(jnp.where(off, a, 0.0), x, _HI), _HI)
        bs *= 2
    return x


def _gdn_kernel(qkv_ref, z_ref, ab_ref, conv0_ref, s0_ref, cw_ref, alog_ref, dtb_ref, onorm_ref,
                o_ref, s_ref, xbuf, *, chunk):
    ci = pl.program_id(1)
    hd = GDN_HEAD_DIM

    @pl.when(ci == 0)
    def _():
        xbuf[5:8, :] = conv0_ref[0]
        s_ref[0] = s0_ref[0]

    x = qkv_ref[0]
    xbuf[8:8 + chunk, :] = x
    y = (xbuf[5:5 + chunk, :] * cw_ref[0:1, :] + xbuf[6:6 + chunk, :] * cw_ref[1:2, :]
         + xbuf[7:7 + chunk, :] * cw_ref[2:3, :] + x * cw_ref[3:4, :])
    xbuf[5:8, :] = x[chunk - 3:chunk, :]
    y = _silu(y)

    ab = ab_ref[0]
    t = ab + dtb_ref[...]
    softplus = jnp.maximum(t, 0.0) + jnp.log(1.0 + jnp.exp(-jnp.abs(t)))
    g = -jnp.exp(alog_ref[...]) * softplus
    beta = jax.nn.sigmoid(ab)

    r = lax.broadcasted_iota(jnp.int32, (chunk, chunk), 0)
    c = lax.broadcasted_iota(jnp.int32, (chunk, chunk), 1)
    lower = r >= c
    gc = _dot(lower.astype(F32), g, _HI)
    gc_t = gc.T

    for h in range(GDN_HEADS):
        q = y[:, h * hd:(h + 1) * hd]
        k = y[:, GDN_WIDTH + h * hd:GDN_WIDTH + (h + 1) * hd]
        v = y[:, 2 * GDN_WIDTH + h * hd:2 * GDN_WIDTH + (h + 1) * hd]
        q = q * lax.rsqrt(jnp.sum(q * q, axis=-1, keepdims=True) + EPS) * (hd ** -0.5)
        k = k * lax.rsqrt(jnp.sum(k * k, axis=-1, keepdims=True) + EPS)
        gcol = gc[:, h:h + 1]
        grow = gc_t[h:h + 1, :]
        bcol = beta[:, GDN_HEADS + h:GDN_HEADS + h + 1]
        decay = jnp.exp(jnp.where(lower, gcol - grow, NEG_BIG))
        kb = k * bcol
        a_mat = jnp.where(r > c, _dot_nt(kb, k, _HI) * decay, 0.0)
        tinv = _unit_lower_inverse(a_mat, chunk)
        egc = jnp.exp(gcol)
        u_v = _dot(tinv, v * bcol, _HI)
        w = _dot(tinv, kb * egc, _HI)
        qk = _dot_nt(q, k, _HI) * decay
        g_last = gcol[chunk - 1:chunk, :]
        k_dec = k * jnp.exp(g_last - gcol)
        s = s_ref[0, h]
        v_new = u_v - _dot(w, s, _HI)
        o = _dot(q * egc, s, _HI) + _dot(qk, v_new, _HI)
        s_ref[0, h] = s * jnp.exp(g_last) + _dot(k_dec.T, v_new, _HI)
        o = o * lax.rsqrt(jnp.mean(o * o, axis=-1, keepdims=True) + EPS) * onorm_ref[...]
        o_ref[0, :, h * hd:(h + 1) * hd] = o * _silu(z_ref[0, :, h * hd:(h + 1) * hd])


def _gdn(qkv, z, ab, conv0, s0, conv_w, alog_pad, dtb_pad, onorm, chunk):
    b, l, _ = qkv.shape
    hd = GDN_HEAD_DIM
    return pl.pallas_call(
        functools.partial(_gdn_kernel, chunk=chunk),
        grid=(b, l // chunk),
        in_specs=[pl.BlockSpec((1, chunk, QKV_A), lambda i, j: (i, j, 0)),
                  pl.BlockSpec((1, chunk, GDN_WIDTH), lambda i, j: (i, j, 0)),
                  pl.BlockSpec((1, chunk, LANES), lambda i, j: (i, j, 0)),
                  pl.BlockSpec((1, CONV_W - 1, QKV_A), lambda i, j: (i, 0, 0)),
                  pl.BlockSpec((1, GDN_HEADS, hd, hd), lambda i, j: (i, 0, 0, 0)),
                  pl.BlockSpec((CONV_W, QKV_A), lambda i, j: (0, 0)),
                  pl.BlockSpec((1, LANES), lambda i, j: (0, 0)),
                  pl.BlockSpec((1, LANES), lambda i, j: (0, 0)),
                  pl.BlockSpec((1, hd), lambda i, j: (0, 0))],
        out_specs=[pl.BlockSpec((1, chunk, GDN_WIDTH), lambda i, j: (i, j, 0)),
                   pl.BlockSpec((1, GDN_HEADS, hd, hd), lambda i, j: (i, 0, 0, 0))],
        out_shape=[jax.ShapeDtypeStruct((b, l, GDN_WIDTH), F32),
                   jax.ShapeDtypeStruct((b, GDN_HEADS, hd, hd), F32)],
        scratch_shapes=[pltpu.VMEM((8 + chunk, QKV_A), F32)],
        compiler_params=_params(("parallel", "arbitrary")),
        name="gdn",
    )(qkv, z, ab, conv0, s0, conv_w, alog_pad, dtb_pad, onorm.reshape(1, hd))


def _rel_bucket(rel):
    nb = N_BUCKETS // 2
    max_exact = nb // 2
    ret = jnp.where(rel > 0, nb, 0)
    n = jnp.abs(rel)
    large = max_exact + (jnp.log(jnp.maximum(n, 1).astype(F32) / max_exact)
                         / math.log(REL_MAX_DIST / max_exact) * (nb - max_exact)).astype(jnp.int32)
    large = jnp.minimum(large, nb - 1)
    return ret + jnp.where(n < max_exact, n, large)


def _softmax_step(s, m, l, acc, v):
    m_new = jnp.maximum(m, jnp.max(s, axis=-1, keepdims=True))
    alpha = jnp.exp(m - m_new)
    p = jnp.exp(s - m_new)
    l = alpha * l + jnp.sum(p, axis=-1, keepdims=True)
    acc = alpha * acc + _dot(p.astype(BF16), v)
    return m_new, l, acc


def _diff_finish(o1, o2, lam_ref, subln_ref, out_scale):
    o = o1 - lam_ref[...] * o2
    return o * lax.rsqrt(jnp.mean(o * o, axis=-1, keepdims=True) + EPS) * subln_ref[...] * out_scale


def _attn_prompt_kernel(q_ref, k_ref, v_ref, bias_ref, lam_ref, subln_ref, o_ref, *, out_scale):
    i = pl.program_id(2)
    tb = ATT_BLOCK
    dh = DIFF_HEAD_DIM
    q = q_ref[0] * (dh ** -0.5)
    q1 = q[:, :dh].astype(BF16)
    q2 = q[:, dh:].astype(BF16)

    def block(j, bias, carry):
        m1, l1, a1, m2, l2, a2 = carry
        start = pl.multiple_of(j * tb, tb)
        kb = k_ref[0, pl.ds(start, tb), :].astype(BF16)
        vb = v_ref[0, pl.ds(start, tb), :].astype(BF16)
        m1, l1, a1 = _softmax_step(_dot_nt(q1, kb[:, :dh]) + bias, m1, l1, a1, vb)
        m2, l2, a2 = _softmax_step(_dot_nt(q2, kb[:, dh:]) + bias, m2, l2, a2, vb)
        return m1, l1, a1, m2, l2, a2

    zcol = jnp.zeros((tb, 1), F32)
    zacc = jnp.zeros((tb, DIFF_V_DIM), F32)
    ncol = jnp.full((tb, 1), NEG_BIG, F32)
    carry = (ncol, zcol, zacc, ncol, zcol, zacc)
    carry = lax.fori_loop(0, jnp.maximum(i - 1, 0), lambda j, cr: block(j, bias_ref[0, 0], cr), carry)
    carry = lax.cond(i > 0, lambda cr: block(i - 1, bias_ref[0, 1], cr), lambda cr: cr, carry)
    m1, l1, a1, m2, l2, a2 = block(i, bias_ref[0, 2], carry)
    o_ref[0] = _diff_finish(a1 / l1, a2 / l2, lam_ref, subln_ref, out_scale)


def _attn_prompt(qb, kb, vb, bias_tiles, lam_row, subln, out_scale):
    b, l, _ = qb.shape
    tb = ATT_BLOCK
    hw = 2 * DIFF_HEAD_DIM
    return pl.pallas_call(
        functools.partial(_attn_prompt_kernel, out_scale=out_scale),
        grid=(b, DIFF_HEADS, l // tb),
        in_specs=[pl.BlockSpec((1, tb, hw), lambda bi, h, i: (bi, i, h)),
                  pl.BlockSpec((1, l, hw), lambda bi, h, i: (bi, 0, h)),
                  pl.BlockSpec((1, l, DIFF_V_DIM), lambda bi, h, i: (bi, 0, h)),
                  pl.BlockSpec((1, 3, tb, tb), lambda bi, h, i: (h, 0, 0, 0)),
                  pl.BlockSpec((1, DIFF_V_DIM), lambda bi, h, i: (0, 0)),
                  pl.BlockSpec((1, DIFF_V_DIM), lambda bi, h, i: (0, 0))],
        out_specs=pl.BlockSpec((1, tb, DIFF_V_DIM), lambda bi, h, i: (bi, i, h)),
        out_shape=jax.ShapeDtypeStruct((b, l, DIFF_WIDTH), F32),
        compiler_params=_params(("parallel", "parallel", "arbitrary")),
        name="attn_prompt",
    )(qb, kb, vb, bias_tiles, lam_row, subln.reshape(1, DIFF_V_DIM))


def _prompt_bias_tiles(rel_table):
    tb = ATT_BLOCK
    qi = jnp.arange(tb)[:, None]
    ki = jnp.arange(tb)[None, :]
    far = jnp.broadcast_to(rel_table[_rel_bucket(jnp.array(-2 * tb))], (tb, tb, DIFF_HEADS))
    prev = rel_table[_rel_bucket(ki - qi - tb)]
    diag = jnp.where(((ki // CHUNK) <= (qi // CHUNK))[..., None], rel_table[_rel_bucket(ki - qi)], NEG_BIG)
    return jnp.moveaxis(jnp.stack([far, prev, diag]).astype(F32), -1, 0)


def _attn_sample_kernel(q_ref, kp_ref, vp_ref, kn_ref, vn_ref, bp_ref, bn_ref, lam_ref, subln_ref, o_ref,
                        *, out_scale):
    dh = DIFF_HEAD_DIM
    q = q_ref[0] * (dh ** -0.5)
    kp = kp_ref[0].astype(BF16)
    kn = kn_ref[0].astype(BF16)
    vp = vp_ref[0].astype(BF16)
    vn = vn_ref[0].astype(BF16)
    outs = []
    for t in range(2):
        qt = q[:, t * dh:(t + 1) * dh].astype(BF16)
        sp = _dot_nt(qt, kp[:, t * dh:(t + 1) * dh]) + bp_ref[0]
        sn = _dot_nt(qt, kn[:, t * dh:(t + 1) * dh]) + bn_ref[0]
        m = jnp.maximum(jnp.max(sp, axis=-1, keepdims=True), jnp.max(sn, axis=-1, keepdims=True))
        pp = jnp.exp(sp - m)
        pn = jnp.exp(sn - m)
        den = jnp.sum(pp, axis=-1, keepdims=True) + jnp.sum(pn, axis=-1, keepdims=True)
        outs.append((_dot(pp.astype(BF16), vp) + _dot(pn.astype(BF16), vn)) / den)
    o_ref[0] = _diff_finish(outs[0], outs[1], lam_ref, subln_ref, out_scale)


def _attn_sample(qb, k_past, v_past, k_new, v_new, bias_past, bias_new, lam_row, subln, out_scale):
    b, l, _ = qb.shape
    p = k_past.shape[1]
    hw = 2 * DIFF_HEAD_DIM
    return pl.pallas_call(
        functools.partial(_attn_sample_kernel, out_scale=out_scale),
        grid=(b, DIFF_HEADS),
        in_specs=[pl.BlockSpec((1, l, hw), lambda bi, h: (bi, 0, h)),
                  pl.BlockSpec((1, p, hw), lambda bi, h: (bi, 0, h)),
                  pl.BlockSpec((1, p, DIFF_V_DIM), lambda bi, h: (bi, 0, h)),
                  pl.BlockSpec((1, l, hw), lambda bi, h: (bi, 0, h)),
                  pl.BlockSpec((1, l, DIFF_V_DIM), lambda bi, h: (bi, 0, h)),
                  pl.BlockSpec((1, l, p), lambda bi, h: (h, 0, 0)),
                  pl.BlockSpec((1, l, l), lambda bi, h: (h, 0, 0)),
                  pl.BlockSpec((1, DIFF_V_DIM), lambda bi, h: (0, 0)),
                  pl.BlockSpec((1, DIFF_V_DIM), lambda bi, h: (0, 0))],
        out_specs=pl.BlockSpec((1, l, DIFF_V_DIM), lambda bi, h: (bi, 0, h)),
        out_shape=jax.ShapeDtypeStruct((b, l, DIFF_WIDTH), F32),
        compiler_params=_params(("parallel", "parallel")),
        name="attn_sample",
    )(qb, k_past, v_past, k_new, v_new, bias_past, bias_new, lam_row, subln.reshape(1, DIFF_V_DIM))


def _sample_bias(rel_table, p, l):
    rel = jnp.arange(p + l)[None, :] - (p + jnp.arange(l))[:, None]
    bias = jnp.moveaxis(rel_table[_rel_bucket(rel)].astype(F32), -1, 0)
    return bias[:, :, :p], bias[:, :, p:]


def _outproj_kernel(oa_ref, ob_ref, x_ref, mod_ref, n2_ref, wo_ref, wq_ref, keys_ref,
                    x1_ref, h2_ref, sc_ref):
    mixed = jnp.concatenate([oa_ref[0], ob_ref[0]], axis=-1).astype(BF16)
    x1 = x_ref[0] + mod_ref[0, 2:3, :] * _dot(mixed, wo_ref[...])
    x1_ref[0] = x1
    h2 = _modulated_norm(x1, n2_ref[...], mod_ref[0, 3:4, :], mod_ref[0, 4:5, :])
    h2_ref[0] = h2
    qh = _dot(h2.astype(BF16), wq_ref[...]).astype(BF16)
    for hp in range(2 * PEER_HEADS):
        sc_ref[0, hp] = _dot_nt(keys_ref[hp], qh[:, hp * PEER_HALF:(hp + 1) * PEER_HALF])


def _outproj(o_a, o_b, x, mod, norm2, w_out, w_q, keys):
    b, l, d = x.shape
    tl = min(l, 256)
    nhp = 2 * PEER_HEADS
    return pl.pallas_call(
        _outproj_kernel,
        grid=(b, l // tl),
        in_specs=[pl.BlockSpec((1, tl, GDN_WIDTH), lambda i, j: (i, j, 0)),
                  pl.BlockSpec((1, tl, DIFF_WIDTH), lambda i, j: (i, j, 0)),
                  pl.BlockSpec((1, tl, d), lambda i, j: (i, j, 0)),
                  pl.BlockSpec((1, 6, d), lambda i, j: (i, 0, 0)),
                  pl.BlockSpec((1, d), lambda i, j: (0, 0)),
                  pl.BlockSpec((d, d), lambda i, j: (0, 0)),
                  pl.BlockSpec((d, nhp * PEER_HALF), lambda i, j: (0, 0)),
                  pl.BlockSpec((nhp, N_KEYS, PEER_HALF), lambda i, j: (0, 0, 0))],
        out_specs=[pl.BlockSpec((1, tl, d), lambda i, j: (i, j, 0)),
                   pl.BlockSpec((1, tl, d), lambda i, j: (i, j, 0)),
                   pl.BlockSpec((1, nhp, N_KEYS, tl), lambda i, j: (i, 0, 0, j))],
        out_shape=[jax.ShapeDtypeStruct((b, l, d), F32),
                   jax.ShapeDtypeStruct((b, l, d), F32),
                   jax.ShapeDtypeStruct((b, nhp, N_KEYS, l), F32)],
        compiler_params=_params(("parallel", "parallel")),
        name="outproj",
    )(o_a, o_b, x, mod, norm2.reshape(1, d), w_out, w_q, keys)


def _top16_rows(s, n):
    row = lax.broadcasted_iota(jnp.int32, s.shape, 0)
    vals, idxs = [], []
    for _ in range(PEER_TOPK):
        m = jnp.max(s, axis=0, keepdims=True)
        i = jnp.min(jnp.where(s == m, row, n), axis=0, keepdims=True)
        vals.append(m)
        idxs.append(i)
        s = jnp.where(row == i, -jnp.inf, s)
    return jnp.concatenate(vals, axis=0), jnp.concatenate(idxs, axis=0)


def _pick_rows(table, sel):
    out = jnp.zeros_like(table)
    for a in range(PEER_TOPK):
        out = jnp.where(sel == a, table[a:a + 1, :], out)
    return out


def _topk_kernel(sc_ref, eidx_ref, gate_ref):
    eidx, gates = [], []
    for h in range(PEER_HEADS):
        s1, i1 = _top16_rows(sc_ref[0, 2 * h], N_KEYS)
        s2, i2 = _top16_rows(sc_ref[0, 2 * h + 1], N_KEYS)
        cand = jnp.concatenate([s1[a:a + 1, :] + s2 for a in range(PEER_TOPK)], axis=0)
        top_s, pos = _top16_rows(cand, PEER_TOPK * PEER_TOPK)
        eidx.append(_pick_rows(i1, pos // PEER_TOPK) * N_KEYS + _pick_rows(i2, pos % PEER_TOPK))
        e = jnp.exp(top_s - top_s[0:1, :])
        gates.append(e / jnp.sum(e, axis=0, keepdims=True))
    eidx_ref[...] = jnp.concatenate(eidx, axis=0).T
    gate_ref[...] = jnp.concatenate(gates, axis=0).T


def _topk(scores):
    b, nhp, nk, l = scores.shape
    tt = min(l, LANES)
    nt = l // tt
    return pl.pallas_call(
        _topk_kernel,
        grid=(b, nt),
        in_specs=[pl.BlockSpec((1, nhp, nk, tt), lambda i, j: (i, 0, 0, j))],
        out_specs=[pl.BlockSpec((tt, PEER_SLOTS), lambda i, j: (i * nt + j, 0)),
                   pl.BlockSpec((tt, PEER_SLOTS), lambda i, j: (i * nt + j, 0))],
        out_shape=[jax.ShapeDtypeStruct((b * l, PEER_SLOTS), jnp.int32),
                   jax.ShapeDtypeStruct((b * l, PEER_SLOTS), F32)],
        compiler_params=_params(("parallel", "parallel")),
        name="topk",
    )(scores)


_SC_ROWS = 32


def _sc_gather(table, idx):
    n = idx.shape[0]
    d = table.shape[1]
    info = plsc.get_sparse_core_info()
    nw = info.num_cores * info.num_subcores
    per_w = n // nw
    assert n % (nw * _SC_ROWS) == 0
    mesh = plsc.VectorSubcoreMesh(core_axis_name="c", subcore_axis_name="s")

    @functools.partial(
        pl.kernel, mesh=mesh,
        out_type=jax.ShapeDtypeStruct((n, d), table.dtype),
        scratch_types=[pltpu.VMEM((_SC_ROWS,), jnp.int32),
                       pltpu.VMEM((_SC_ROWS, d), table.dtype),
                       pltpu.SemaphoreType.DMA],
    )
    def k(table_hbm, idx_hbm, out_hbm, idx_v, rows_v, sem):
        wid = lax.axis_index("s") * info.num_cores + lax.axis_index("c")
        base = wid * per_w

        @pl.loop(0, per_w // _SC_ROWS)
        def _(it):
            off = pl.multiple_of(base + it * _SC_ROWS, _SC_ROWS)
            pltpu.sync_copy(idx_hbm.at[pl.ds(off, _SC_ROWS)], idx_v)
            pltpu.async_copy(table_hbm.at[idx_v], rows_v, sem).wait()
            pltpu.sync_copy(rows_v, out_hbm.at[pl.ds(off, _SC_ROWS)])

    return k(table, idx)


_PEER_TOKENS = 8


def _peer_kernel(ug_ref, vg_ref, gate_ref, h2_ref, x1_ref, g2_ref, fn_ref, y_ref):
    gate_t = gate_ref[...].T
    rows = []
    for t in range(_PEER_TOKENS):
        u = ug_ref[t * PEER_SLOTS:(t + 1) * PEER_SLOTS, :]
        pre = jnp.sum(u * h2_ref[t:t + 1, :], axis=-1, keepdims=True)
        act = 0.5 * pre * (1.0 + lax.erf(pre * (2.0 ** -0.5)))
        coef = gate_t[:, t:t + 1] * act
        v = vg_ref[t * PEER_SLOTS:(t + 1) * PEER_SLOTS, :]
        rows.append(jnp.sum(coef * v, axis=0, keepdims=True))
    x2 = x1_ref[...] + g2_ref[0, 5:6, :] * jnp.concatenate(rows, axis=0)
    y_ref[...] = x2 * lax.rsqrt(jnp.mean(x2 * x2, axis=-1, keepdims=True) + EPS) * fn_ref[...]


def _peer_combine(ug, vg, gate, h2, x1, mod_b, final_norm):
    l, d = h2.shape
    tp = _PEER_TOKENS
    return pl.pallas_call(
        _peer_kernel,
        grid=(l // tp,),
        in_specs=[pl.BlockSpec((tp * PEER_SLOTS, d), lambda i: (i, 0)),
                  pl.BlockSpec((tp * PEER_SLOTS, d), lambda i: (i, 0)),
                  pl.BlockSpec((tp, PEER_SLOTS), lambda i: (i, 0)),
                  pl.BlockSpec((tp, d), lambda i: (i, 0)),
                  pl.BlockSpec((tp, d), lambda i: (i, 0)),
                  pl.BlockSpec((1, 6, d), lambda i: (0, 0, 0)),
                  pl.BlockSpec((1, d), lambda i: (0, 0))],
        out_specs=pl.BlockSpec((tp, d), lambda i: (i, 0)),
        out_shape=jax.ShapeDtypeStruct((l, d), F32),
        compiler_params=_params(("parallel",)),
        name="peer",
    )(ug, vg, gate, h2, x1, mod_b, final_norm.reshape(1, d))


def _peer_and_norm(eidx, gate, h2, x1, mod, peer_u, peer_v, final_norm):
    b, l, d = h2.shape
    eidx = eidx.reshape(b, l * PEER_SLOTS)
    gate = gate.reshape(b, l, PEER_SLOTS)

    def gather(i):
        return _sc_gather(peer_u, eidx[i]), _sc_gather(peer_v, eidx[i])

    def step(carry, i):
        ug, vg = carry
        nxt = gather(jnp.minimum(i + 1, b - 1))
        y = _peer_combine(ug, vg, gate[i], h2[i], x1[i], lax.dynamic_slice_in_dim(mod, i, 1, 0), final_norm)
        return nxt, y

    _, ys = lax.scan(step, gather(0), jnp.arange(b))
    return ys


def _layer(x, mod, conv0, s0, k_past, v_past, wts, prompt):
    b, l, d = x.shape
    qkv, z, ab, qb, kb, vb = _inproj(x, mod, wts["norm1"], wts["w_in"])
    chunk = CHUNK if prompt else l
    o_a, s_new = _gdn(qkv, z, ab, conv0, s0, wts["conv_w"], wts["alog"], wts["dtb"], wts["onorm"], chunk)
    conv_new = qkv[:, l - (CONV_W - 1):, :]
    if prompt:
        o_b = _attn_prompt(qb, kb, vb, wts["bias_prompt"], wts["lam"], wts["subln"], wts["out_scale"])
    else:
        p = k_past.shape[1]
        bias_past, bias_new = _sample_bias(wts["rel_table"], p, l)
        o_b = _attn_sample(qb, k_past.reshape(b, p, DIFF_WIDTH), v_past.reshape(b, p, DIFF_WIDTH), kb, vb,
                           bias_past, bias_new, wts["lam"], wts["subln"], wts["out_scale"])
    x1, h2, scores = _outproj(o_a, o_b, x, mod, wts["norm2"], wts["w_out"], wts["w_q"], wts["keys"])
    eidx, gate = _topk(scores)
    y = _peer_and_norm(eidx, gate, h2, x1, mod, wts["peer_u"], wts["peer_v"], wts["final_norm"])
    k_new = kb.reshape(1, b, l, DIFF_HEADS, 2 * DIFF_HEAD_DIM)
    v_new = vb.reshape(1, b, l, DIFF_HEADS, DIFF_V_DIM)
    return y, k_new, v_new, s_new[None], conv_new[None]


def kernel(x_prompt, x_sample, c_prompt, c_sample, cache_k, cache_v, state_gdn, state_conv, w_ada, b_ada,
           norm1, norm2, w_in, conv_w, a_log, dt_bias, gdn_onorm, lam_q1, lam_k1, lam_q2, lam_k2, diff_subln,
           w_out, peer_wq, peer_keys, peer_u, peer_v, rel_table, final_norm):
    assert w_ada.shape[0] == 1, "single-layer step"
    bp = x_prompt.shape[0]
    d = D_MODEL
    lam_init = 0.8 - 0.6 * math.exp(-0.3 * 0)
    lam = (jnp.exp(jnp.sum(lam_q1[0] * lam_k1[0])) - jnp.exp(jnp.sum(lam_q2[0] * lam_k2[0])) + lam_init)
    w = w_in[0]
    w_packed = jnp.concatenate(
        [w[:, :_C_AB], jnp.pad(w[:, 2048:2056], ((0, 0), (0, LANES - 2 * GDN_HEADS))), w[:, 2056:]],
        axis=1).astype(BF16)
    wts = dict(
        norm1=norm1[0], norm2=norm2[0], w_in=w_packed, conv_w=conv_w[0],
        alog=jnp.pad(a_log[0], (0, LANES - GDN_HEADS)).reshape(1, LANES),
        dtb=jnp.pad(dt_bias[0], (0, LANES - GDN_HEADS)).reshape(1, LANES),
        onorm=gdn_onorm[0], lam=jnp.full((1, DIFF_V_DIM), lam, F32), subln=diff_subln[0],
        out_scale=1.0 - lam_init, bias_prompt=_prompt_bias_tiles(rel_table), rel_table=rel_table,
        w_out=w_out[0].astype(BF16), w_q=peer_wq[0].astype(BF16),
        keys=peer_keys[0].reshape(2 * PEER_HEADS, N_KEYS, PEER_HALF).astype(BF16),
        peer_u=peer_u[0], peer_v=peer_v[0], final_norm=final_norm)

    mod = _ada(jnp.concatenate([c_prompt, c_sample], axis=0), w_ada[0], b_ada[0]).reshape(-1, 6, d)
    zeros_conv = jnp.zeros((bp, CONV_W - 1, QKV_A), F32)
    zeros_s = jnp.zeros((bp, GDN_HEADS, GDN_HEAD_DIM, GDN_HEAD_DIM), F32)
    yp, kp, vp, sp, cp = _layer(x_prompt, mod[:bp], zeros_conv, zeros_s, None, None, wts, True)
    ys, ks, vs, ss, cs = _layer(x_sample, mod[bp:], state_conv[0], state_gdn[0], cache_k[0], cache_v[0],
                                wts, False)
    return yp, ys, kp, vp, sp, cp, ks, vs, ss, cs
```

```python
import functools
import math

import jax
import jax.numpy as jnp
from jax import lax
from jax.experimental import pallas as pl
from jax.experimental.pallas import tpu as pltpu
from jax.experimental.pallas import tpu_sc as plsc

F32 = jnp.float32
BF16 = jnp.bfloat16
EPS = 1e-6

D_MODEL = 1024
CHUNK = 64
GDN_HEADS = 4
GDN_HEAD_DIM = 128
GDN_WIDTH = GDN_HEADS * GDN_HEAD_DIM
CONV_W = 4
QKV_A = 3 * GDN_WIDTH
DIFF_HEADS = 4
DIFF_HEAD_DIM = 64
DIFF_V_DIM = 128
DIFF_WIDTH = DIFF_HEADS * 2 * DIFF_HEAD_DIM
ATT_BLOCK = 128
N_BUCKETS = 32
REL_MAX_DIST = 128
PEER_HEADS = 8
N_KEYS = 128
PEER_HALF = 128
PEER_TOPK = 16
PEER_SLOTS = PEER_HEADS * PEER_TOPK
LANES = 128
NEG_BIG = -1e30
VMEM_LIMIT = 56 * 1024 * 1024

_C_QKV, _C_Z, _C_AB, _C_QB, _C_KB, _C_VB = 0, 1536, 2048, 2176, 2688, 3200
_C_END = 3712


def _params(sem):
    return pltpu.CompilerParams(dimension_semantics=sem, vmem_limit_bytes=VMEM_LIMIT)


def _dot(a, b, precision=None):
    return jnp.dot(a, b, preferred_element_type=F32, precision=precision)


def _dot_nt(a, b, precision=None):
    return lax.dot_general(a, b, (((1,), (1,)), ((), ())), preferred_element_type=F32, precision=precision)


def _silu(x):
    return x * jax.nn.sigmoid(x)


def _ada_kernel(c_ref, w_ref, b_ref, o_ref):
    a = _silu(c_ref[...]).astype(BF16)
    o_ref[...] = _dot(a, w_ref[...].astype(BF16)) + b_ref[...]


def _ada(c, w_ada, b_ada):
    n, d = c.shape
    cols = w_ada.shape[1]
    tn = 1024
    return pl.pallas_call(
        _ada_kernel,
        grid=(cols // tn,),
        in_specs=[pl.BlockSpec((n, d), lambda j: (0, 0)),
                  pl.BlockSpec((d, tn), lambda j: (0, j)),
                  pl.BlockSpec((1, tn), lambda j: (0, j))],
        out_specs=pl.BlockSpec((n, tn), lambda j: (0, j)),
        out_shape=jax.ShapeDtypeStruct((n, cols), F32),
        compiler_params=_params(("parallel",)),
        name="ada",
    )(c, w_ada, b_ada.reshape(1, cols))


def _modulated_norm(x, gain, shift, scale):
    y = x * lax.rsqrt(jnp.mean(x * x, axis=-1, keepdims=True) + EPS)
    return (y * gain) * (1.0 + scale) + shift


def _inproj_kernel(x_ref, mod_ref, n1_ref, w_ref, qkv_ref, z_ref, ab_ref, qb_ref, kb_ref, vb_ref):
    h = _modulated_norm(x_ref[0], n1_ref[...], mod_ref[0, 0:1, :], mod_ref[0, 1:2, :]).astype(BF16)
    qkv_ref[0] = _dot(h, w_ref[:, _C_QKV:_C_Z])
    z_ref[0] = _dot(h, w_ref[:, _C_Z:_C_AB])
    ab_ref[0] = _dot(h, w_ref[:, _C_AB:_C_QB])
    qb_ref[0] = _dot(h, w_ref[:, _C_QB:_C_KB])
    kb_ref[0] = _dot(h, w_ref[:, _C_KB:_C_VB])
    vb_ref[0] = _dot(h, w_ref[:, _C_VB:_C_END])


def _inproj(x, mod, norm1, w_packed):
    b, l, d = x.shape
    tl = min(l, 256)
    widths = (QKV_A, GDN_WIDTH, LANES, DIFF_WIDTH, DIFF_WIDTH, DIFF_WIDTH)
    return pl.pallas_call(
        _inproj_kernel,
        grid=(b, l // tl),
        in_specs=[pl.BlockSpec((1, tl, d), lambda i, j: (i, j, 0)),
                  pl.BlockSpec((1, 6, d), lambda i, j: (i, 0, 0)),
                  pl.BlockSpec((1, d), lambda i, j: (0, 0)),
                  pl.BlockSpec((d, _C_END), lambda i, j: (0, 0))],
        out_specs=[pl.BlockSpec((1, tl, w), lambda i, j: (i, j, 0)) for w in widths],
        out_shape=[jax.ShapeDtypeStruct((b, l, w), F32) for w in widths],
        compiler_params=_params(("parallel", "parallel")),
        name="inproj",
    )(x, mod, norm1.reshape(1, d), w_packed)


_HI = lax.Precision.HIGHEST


def _unit_lower_inverse(a, n):
    r = lax.broadcasted_iota(jnp.int32, (n, n), 0)
    c = lax.broadcasted_iota(jnp.int32, (n, n), 1)
    eye = (r == c).astype(F32)
    ad = jnp.where((r // 8) == (c // 8), a, 0.0)
    a2 = _dot(ad, ad, _HI)
    a4 = _dot(a2, a2, _HI)
    x = eye - ad
    x = x + _dot(x, a2, _HI)
    x = x + _dot(x, a4, _HI)
    bs = 8
    while bs < n:
        off = ((r // (2 * bs)) == (c // (2 * bs))) & ((r // bs) != (c // bs))
        x = x - _dot(x, _dot(jnp.where(off, a, 0.0), x, _HI), _HI)
        bs *= 2
    return x


def _gdn_kernel(qkv_ref, z_ref, ab_ref, conv0_ref, s0_ref, cw_ref, alog_ref, dtb_ref, onorm_ref,
                o_ref, s_ref, xbuf, *, chunk):
    ci = pl.program_id(1)
    hd = GDN_HEAD_DIM

    @pl.when(ci == 0)
    def _():
        xbuf[5:8, :] = conv0_ref[0]
        s_ref[0] = s0_ref[0]

    x = qkv_ref[0]
    xbuf[8:8 + chunk, :] = x
    y = (xbuf[5:5 + chunk, :] * cw_ref[0:1, :] + xbuf[6:6 + chunk, :] * cw_ref[1:2, :]
         + xbuf[7:7 + chunk, :] * cw_ref[2:3, :] + x * cw_ref[3:4, :])
    xbuf[5:8, :] = x[chunk - 3:chunk, :]
    y = _silu(y)

    ab = ab_ref[0]
    t = ab + dtb_ref[...]
    softplus = jnp.maximum(t, 0.0) + jnp.log(1.0 + jnp.exp(-jnp.abs(t)))
    g = -jnp.exp(alog_ref[...]) * softplus
    beta = jax.nn.sigmoid(ab)

    r = lax.broadcasted_iota(jnp.int32, (chunk, chunk), 0)
    c = lax.broadcasted_iota(jnp.int32, (chunk, chunk), 1)
    lower = r >= c
    gc = _dot(lower.astype(F32), g, _HI)
    gc_t = gc.T

    for h in range(GDN_HEADS):
        q = y[:, h * hd:(h + 1) * hd]
        k = y[:, GDN_WIDTH + h * hd:GDN_WIDTH + (h + 1) * hd]
        v = y[:, 2 * GDN_WIDTH + h * hd:2 * GDN_WIDTH + (h + 1) * hd]
        q = q * lax.rsqrt(jnp.sum(q * q, axis=-1, keepdims=True) + EPS) * (hd ** -0.5)
        k = k * lax.rsqrt(jnp.sum(k * k, axis=-1, keepdims=True) + EPS)
        gcol = gc[:, h:h + 1]
        grow = gc_t[h:h + 1, :]
        bcol = beta[:, GDN_HEADS + h:GDN_HEADS + h + 1]
        decay = jnp.exp(jnp.where(lower, gcol - grow, NEG_BIG))
        kb = k * bcol
        a_mat = jnp.where(r > c, _dot_nt(kb, k, _HI) * decay, 0.0)
        tinv = _unit_lower_inverse(a_mat, chunk)
        egc = jnp.exp(gcol)
        u_v = _dot(tinv, v * bcol, _HI)
        w = _dot(tinv, kb * egc, _HI)
        qk = _dot_nt(q, k, _HI) * decay
        g_last = gcol[chunk - 1:chunk, :]
        k_dec = k * jnp.exp(g_last - gcol)
        s = s_ref[0, h]
        v_new = u_v - _dot(w, s, _HI)
        o = _dot(q * egc, s, _HI) + _dot(qk, v_new, _HI)
        s_ref[0, h] = s * jnp.exp(g_last) + _dot(k_dec.T, v_new, _HI)
        o = o * lax.rsqrt(jnp.mean(o * o, axis=-1, keepdims=True) + EPS) * onorm_ref[...]
        o_ref[0, :, h * hd:(h + 1) * hd] = o * _silu(z_ref[0, :, h * hd:(h + 1) * hd])


def _gdn(qkv, z, ab, conv0, s0, conv_w, alog_pad, dtb_pad, onorm, chunk):
    b, l, _ = qkv.shape
    hd = GDN_HEAD_DIM
    return pl.pallas_call(
        functools.partial(_gdn_kernel, chunk=chunk),
        grid=(b, l // chunk),
        in_specs=[pl.BlockSpec((1, chunk, QKV_A), lambda i, j: (i, j, 0)),
                  pl.BlockSpec((1, chunk, GDN_WIDTH), lambda i, j: (i, j, 0)),
                  pl.BlockSpec((1, chunk, LANES), lambda i, j: (i, j, 0)),
                  pl.BlockSpec((1, CONV_W - 1, QKV_A), lambda i, j: (i, 0, 0)),
                  pl.BlockSpec((1, GDN_HEADS, hd, hd), lambda i, j: (i, 0, 0, 0)),
                  pl.BlockSpec((CONV_W, QKV_A), lambda i, j: (0, 0)),
                  pl.BlockSpec((1, LANES), lambda i, j: (0, 0)),
                  pl.BlockSpec((1, LANES), lambda i, j: (0, 0)),
                  pl.BlockSpec((1, hd), lambda i, j: (0, 0))],
        out_specs=[pl.BlockSpec((1, chunk, GDN_WIDTH), lambda i, j: (i, j, 0)),
                   pl.BlockSpec((1, GDN_HEADS, hd, hd), lambda i, j: (i, 0, 0, 0))],
        out_shape=[jax.ShapeDtypeStruct((b, l, GDN_WIDTH), F32),
                   jax.ShapeDtypeStruct((b, GDN_HEADS, hd, hd), F32)],
        scratch_shapes=[pltpu.VMEM((8 + chunk, QKV_A), F32)],
        compiler_params=_params(("parallel", "arbitrary")),
        name="gdn",
    )(qkv, z, ab, conv0, s0, conv_w, alog_pad, dtb_pad, onorm.reshape(1, hd))


def _rel_bucket(rel):
    nb = N_BUCKETS // 2
    max_exact = nb // 2
    ret = jnp.where(rel > 0, nb, 0)
    n = jnp.abs(rel)
    large = max_exact + (jnp.log(jnp.maximum(n, 1).astype(F32) / max_exact)
                         / math.log(REL_MAX_DIST / max_exact) * (nb - max_exact)).astype(jnp.int32)
    large = jnp.minimum(large, nb - 1)
    return ret + jnp.where(n < max_exact, n, large)


def _softmax_step(s, m, l, acc, v):
    m_new = jnp.maximum(m, jnp.max(s, axis=-1, keepdims=True))
    alpha = jnp.exp(m - m_new)
    p = jnp.exp(s - m_new)
    l = alpha * l + jnp.sum(p, axis=-1, keepdims=True)
    acc = alpha * acc + _dot(p.astype(BF16), v)
    return m_new, l, acc


def _diff_finish(o1, o2, lam_ref, subln_ref, out_scale):
    o = o1 - lam_ref[...] * o2
    return o * lax.rsqrt(jnp.mean(o * o, axis=-1, keepdims=True) + EPS) * subln_ref[...] * out_scale


def _attn_prompt_kernel(q_ref, k_ref, v_ref, bias_ref, lam_ref, subln_ref, o_ref, *, out_scale):
    i = pl.program_id(2)
    tb = ATT_BLOCK
    dh = DIFF_HEAD_DIM
    q = q_ref[0] * (dh ** -0.5)
    q1 = q[:, :dh].astype(BF16)
    q2 = q[:, dh:].astype(BF16)

    def block(j, bias, carry):
        m1, l1, a1, m2, l2, a2 = carry
        start = pl.multiple_of(j * tb, tb)
        kb = k_ref[0, pl.ds(start, tb), :].astype(BF16)
        vb = v_ref[0, pl.ds(start, tb), :].astype(BF16)
        m1, l1, a1 = _softmax_step(_dot_nt(q1, kb[:, :dh]) + bias, m1, l1, a1, vb)
        m2, l2, a2 = _softmax_step(_dot_nt(q2, kb[:, dh:]) + bias, m2, l2, a2, vb)
        return m1, l1, a1, m2, l2, a2

    zcol = jnp.zeros((tb, 1), F32)
    zacc = jnp.zeros((tb, DIFF_V_DIM), F32)
    ncol = jnp.full((tb, 1), NEG_BIG, F32)
    carry = (ncol, zcol, zacc, ncol, zcol, zacc)
    carry = lax.fori_loop(0, jnp.maximum(i - 1, 0), lambda j, cr: block(j, bias_ref[0, 0], cr), carry)
    carry = lax.cond(i > 0, lambda cr: block(i - 1, bias_ref[0, 1], cr), lambda cr: cr, carry)
    m1, l1, a1, m2, l2, a2 = block(i, bias_ref[0, 2], carry)
    o_ref[0] = _diff_finish(a1 / l1, a2 / l2, lam_ref, subln_ref, out_scale)


def _attn_prompt(qb, kb, vb, bias_tiles, lam_row, subln, out_scale):
    b, l, _ = qb.shape
    tb = ATT_BLOCK
    hw = 2 * DIFF_HEAD_DIM
    return pl.pallas_call(
        functools.partial(_attn_prompt_kernel, out_scale=out_scale),
        grid=(b, DIFF_HEADS, l // tb),
        in_specs=[pl.BlockSpec((1, tb, hw), lambda bi, h, i: (bi, i, h)),
                  pl.BlockSpec((1, l, hw), lambda bi, h, i: (bi, 0, h)),
                  pl.BlockSpec((1, l, DIFF_V_DIM), lambda bi, h, i: (bi, 0, h)),
                  pl.BlockSpec((1, 3, tb, tb), lambda bi, h, i: (h, 0, 0, 0)),
                  pl.BlockSpec((1, DIFF_V_DIM), lambda bi, h, i: (0, 0)),
                  pl.BlockSpec((1, DIFF_V_DIM), lambda bi, h, i: (0, 0))],
        out_specs=pl.BlockSpec((1, tb, DIFF_V_DIM), lambda bi, h, i: (bi, i, h)),
        out_shape=jax.ShapeDtypeStruct((b, l, DIFF_WIDTH), F32),
        compiler_params=_params(("parallel", "parallel", "arbitrary")),
        name="attn_prompt",
    )(qb, kb, vb, bias_tiles, lam_row, subln.reshape(1, DIFF_V_DIM))


def _prompt_bias_tiles(rel_table):
    tb = ATT_BLOCK
    qi = jnp.arange(tb)[:, None]
    ki = jnp.arange(tb)[None, :]
    far = jnp.broadcast_to(rel_table[_rel_bucket(jnp.array(-2 * tb))], (tb, tb, DIFF_HEADS))
    prev = rel_table[_rel_bucket(ki - qi - tb)]
    diag = jnp.where(((ki // CHUNK) <= (qi // CHUNK))[..., None], rel_table[_rel_bucket(ki - qi)], NEG_BIG)
    return jnp.moveaxis(jnp.stack([far, prev, diag]).astype(F32), -1, 0)


def _attn_sample_kernel(q_ref, kp_ref, vp_ref, kn_ref, vn_ref, bp_ref, bn_ref, lam_ref, subln_ref, o_ref,
                        *, out_scale):
    dh = DIFF_HEAD_DIM
    q = q_ref[0] * (dh ** -0.5)
    kp = kp_ref[0].astype(BF16)
    kn = kn_ref[0].astype(BF16)
    vp = vp_ref[0].astype(BF16)
    vn = vn_ref[0].astype(BF16)
    outs = []
    for t in range(2):
        qt = q[:, t * dh:(t + 1) * dh].astype(BF16)
        sp = _dot_nt(qt, kp[:, t * dh:(t + 1) * dh]) + bp_ref[0]
        sn = _dot_nt(qt, kn[:, t * dh:(t + 1) * dh]) + bn_ref[0]
        m = jnp.maximum(jnp.max(sp, axis=-1, keepdims=True), jnp.max(sn, axis=-1, keepdims=True))
        pp = jnp.exp(sp - m)
        pn = jnp.exp(sn - m)
        den = jnp.sum(pp, axis=-1, keepdims=True) + jnp.sum(pn, axis=-1, keepdims=True)
        outs.append((_dot(pp.astype(BF16), vp) + _dot(pn.astype(BF16), vn)) / den)
    o_ref[0] = _diff_finish(outs[0], outs[1], lam_ref, subln_ref, out_scale)


def _attn_sample(qb, k_past, v_past, k_new, v_new, bias_past, bias_new, lam_row, subln, out_scale):
    b, l, _ = qb.shape
    p = k_past.shape[1]
    hw = 2 * DIFF_HEAD_DIM
    return pl.pallas_call(
        functools.partial(_attn_sample_kernel, out_scale=out_scale),
        grid=(b, DIFF_HEADS),
        in_specs=[pl.BlockSpec((1, l, hw), lambda bi, h: (bi, 0, h)),
                  pl.BlockSpec((1, p, hw), lambda bi, h: (bi, 0, h)),
                  pl.BlockSpec((1, p, DIFF_V_DIM), lambda bi, h: (bi, 0, h)),
                  pl.BlockSpec((1, l, hw), lambda bi, h: (bi, 0, h)),
                  pl.BlockSpec((1, l, DIFF_V_DIM), lambda bi, h: (bi, 0, h)),
                  pl.BlockSpec((1, l, p), lambda bi, h: (h, 0, 0)),
                  pl.BlockSpec((1, l, l), lambda bi, h: (h, 0, 0)),
                  pl.BlockSpec((1, DIFF_V_DIM), lambda bi, h: (0, 0)),
                  pl.BlockSpec((1, DIFF_V_DIM), lambda bi, h: (0, 0))],
        out_specs=pl.BlockSpec((1, l, DIFF_V_DIM), lambda bi, h: (bi, 0, h)),
        out_shape=jax.ShapeDtypeStruct((b, l, DIFF_WIDTH), F32),
        compiler_params=_params(("parallel", "parallel")),
        name="attn_sample",
    )(qb, k_past, v_past, k_new, v_new, bias_past, bias_new, lam_row, subln.reshape(1, DIFF_V_DIM))


def _sample_bias(rel_table, p, l):
    rel = jnp.arange(p + l)[None, :] - (p + jnp.arange(l))[:, None]
    bias = jnp.moveaxis(rel_table[_rel_bucket(rel)].astype(F32), -1, 0)
    return bias[:, :, :p], bias[:, :, p:]


def _outproj_kernel(oa_ref, ob_ref, x_ref, mod_ref, n2_ref, wo_ref, wq_ref, keys_ref,
                    x1_ref, h2_ref, sc_ref):
    mixed = jnp.concatenate([oa_ref[0], ob_ref[0]], axis=-1).astype(BF16)
    x1 = x_ref[0] + mod_ref[0, 2:3, :] * _dot(mixed, wo_ref[...])
    x1_ref[0] = x1
    h2 = _modulated_norm(x1, n2_ref[...], mod_ref[0, 3:4, :], mod_ref[0, 4:5, :])
    h2_ref[0] = h2
    qh = _dot(h2.astype(BF16), wq_ref[...]).astype(BF16)
    for hp in range(2 * PEER_HEADS):
        sc_ref[0, hp] = _dot_nt(keys_ref[hp], qh[:, hp * PEER_HALF:(hp + 1) * PEER_HALF])


def _outproj(o_a, o_b, x, mod, norm2, w_out, w_q, keys):
    b, l, d = x.shape
    tl = min(l, 256)
    nhp = 2 * PEER_HEADS
    return pl.pallas_call(
        _outproj_kernel,
        grid=(b, l // tl),
        in_specs=[pl.BlockSpec((1, tl, GDN_WIDTH), lambda i, j: (i, j, 0)),
                  pl.BlockSpec((1, tl, DIFF_WIDTH), lambda i, j: (i, j, 0)),
                  pl.BlockSpec((1, tl, d), lambda i, j: (i, j, 0)),
                  pl.BlockSpec((1, 6, d), lambda i, j: (i, 0, 0)),
                  pl.BlockSpec((1, d), lambda i, j: (0, 0)),
                  pl.BlockSpec((d, d), lambda i, j: (0, 0)),
                  pl.BlockSpec((d, nhp * PEER_HALF), lambda i, j: (0, 0)),
                  pl.BlockSpec((nhp, N_KEYS, PEER_HALF), lambda i, j: (0, 0, 0))],
        out_specs=[pl.BlockSpec((1, tl, d), lambda i, j: (i, j, 0)),
                   pl.BlockSpec((1, tl, d), lambda i, j: (i, j, 0)),
                   pl.BlockSpec((1, nhp, N_KEYS, tl), lambda i, j: (i, 0, 0, j))],
        out_shape=[jax.ShapeDtypeStruct((b, l, d), F32),
                   jax.ShapeDtypeStruct((b, l, d), F32),
                   jax.ShapeDtypeStruct((b, nhp, N_KEYS, l), F32)],
        compiler_params=_params(("parallel", "parallel")),
        name="outproj",
    )(o_a, o_b, x, mod, norm2.reshape(1, d), w_out, w_q, keys)


def _top16_rows(s, n):
    row = lax.broadcasted_iota(jnp.int32, s.shape, 0)
    vals, idxs = [], []
    for _ in range(PEER_TOPK):
        m = jnp.max(s, axis=0, keepdims=True)
        i = jnp.min(jnp.where(s == m, row, n), axis=0, keepdims=True)
        vals.append(m)
        idxs.append(i)
        s = jnp.where(row == i, -jnp.inf, s)
    return jnp.concatenate(vals, axis=0), jnp.concatenate(idxs, axis=0)


def _pick_rows(table, sel):
    out = jnp.zeros_like(table)
    for a in range(PEER_TOPK):
        out = jnp.where(sel == a, table[a:a + 1, :], out)
    return out


def _topk_kernel(sc_ref, eidx_ref, gate_ref):
    eidx, gates = [], []
    for h in range(PEER_HEADS):
        s1, i1 = _top16_rows(sc_ref[0, 2 * h], N_KEYS)
        s2, i2 = _top16_rows(sc_ref[0, 2 * h + 1], N_KEYS)
        cand = jnp.concatenate([s1[a:a + 1, :] + s2 for a in range(PEER_TOPK)], axis=0)
        top_s, pos = _top16_rows(cand, PEER_TOPK * PEER_TOPK)
        eidx.append(_pick_rows(i1, pos // PEER_TOPK) * N_KEYS + _pick_rows(i2, pos % PEER_TOPK))
        e = jnp.exp(top_s - top_s[0:1, :])
        gates.append(e / jnp.sum(e, axis=0, keepdims=True))
    eidx_ref[...] = jnp.concatenate(eidx, axis=0).T
    gate_ref[...] = jnp.concatenate(gates, axis=0).T


def _topk(scores):
    b, nhp, nk, l = scores.shape
    tt = min(l, LANES)
    nt = l // tt
    return pl.pallas_call(
        _topk_kernel,
        grid=(b, nt),
        in_specs=[pl.BlockSpec((1, nhp, nk, tt), lambda i, j: (i, 0, 0, j))],
        out_specs=[pl.BlockSpec((tt, PEER_SLOTS), lambda i, j: (i * nt + j, 0)),
                   pl.BlockSpec((tt, PEER_SLOTS), lambda i, j: (i * nt + j, 0))],
        out_shape=[jax.ShapeDtypeStruct((b * l, PEER_SLOTS), jnp.int32),
                   jax.ShapeDtypeStruct((b * l, PEER_SLOTS), F32)],
        compiler_params=_params(("parallel", "parallel")),
        name="topk",
    )(scores)


_SC_ROWS = 32


def _pack_bf16_halves(t):
    half = t.shape[1] // 2
    tb = lax.bitcast_convert_type(t.astype(BF16), jnp.uint16).astype(jnp.uint32)
    return lax.bitcast_convert_type(tb[:, :half] | (tb[:, half:] << 16), jnp.int32)


def _sc_gather2(tab_u, tab_v, idx):
    n = idx.shape[0]
    w = tab_u.shape[1]
    info = plsc.get_sparse_core_info()
    nw = info.num_cores * info.num_subcores
    per_w = n // nw
    nch = per_w // _SC_ROWS
    assert n % (nw * _SC_ROWS * 2) == 0
    mesh = plsc.VectorSubcoreMesh(core_axis_name="c", subcore_axis_name="s")
    rows = pltpu.VMEM((_SC_ROWS, w), tab_u.dtype)
    out = jax.ShapeDtypeStruct((n, w), tab_u.dtype)

    @functools.partial(
        pl.kernel, mesh=mesh, out_type=[out, out],
        scratch_types=[pltpu.VMEM((_SC_ROWS,), jnp.int32)] * 2 + [rows] * 4 + [pltpu.SemaphoreType.DMA] * 8,
    )
    def k(u_hbm, v_hbm, idx_hbm, uo_hbm, vo_hbm, i0, i1, u0, u1, v0, v1, gu0, gu1, gv0, gv1, wu0, wu1, wv0, wv1):
        idx_v, ub, vb = (i0, i1), (u0, u1), (v0, v1)
        gsu, gsv, wsu, wsv = (gu0, gu1), (gv0, gv1), (wu0, wu1), (wv0, wv1)
        wid = lax.axis_index("s") * info.num_cores + lax.axis_index("c")
        base = wid * per_w

        def span(c):
            return pl.ds(pl.multiple_of(base + c * _SC_ROWS, _SC_ROWS), _SC_ROWS)

        def gathers(s):
            return (pltpu.make_async_copy(u_hbm.at[idx_v[s]], ub[s], gsu[s]),
                    pltpu.make_async_copy(v_hbm.at[idx_v[s]], vb[s], gsv[s]))

        def writes(c, s):
            return (pltpu.make_async_copy(ub[s], uo_hbm.at[span(c)], wsu[s]),
                    pltpu.make_async_copy(vb[s], vo_hbm.at[span(c)], wsv[s]))

        def start(copies):
            for cp in copies:
                cp.start()

        def wait(copies):
            for cp in copies:
                cp.wait()

        def fetch(c, s):
            pltpu.sync_copy(idx_hbm.at[span(c)], idx_v[s])
            start(gathers(s))

        fetch(0, 0)

        @pl.loop(0, nch // 2)
        def _(kk):
            c0 = 2 * kk
            wait(gathers(0))
            start(writes(c0, 0))

            @pl.when(kk > 0)
            def _():
                wait(writes(c0 - 1, 1))

            fetch(c0 + 1, 1)
            wait(gathers(1))
            start(writes(c0 + 1, 1))
            wait(writes(c0, 0))

            @pl.when(kk < nch // 2 - 1)
            def _():
                fetch(c0 + 2, 0)

        wait(writes(nch - 1, 1))

    return k(tab_u, tab_v, idx)


_PEER_TOKENS = 16


def _unpack_bf16_halves(w):
    lo = lax.bitcast_convert_type(w << 16, F32)
    hi = lax.bitcast_convert_type(w & jnp.int32(-65536), F32)
    return lo, hi


def _peer_kernel(ug_ref, vg_ref, gate_ref, h2_ref, x1_ref, g2_ref, fn_ref, y_ref):
    half = D_MODEL // 2
    gate_t = gate_ref[...].T
    rows = []
    for t in range(_PEER_TOKENS):
        u_lo, u_hi = _unpack_bf16_halves(ug_ref[t * PEER_SLOTS:(t + 1) * PEER_SLOTS, :])
        pre = jnp.sum(u_lo * h2_ref[t:t + 1, :half] + u_hi * h2_ref[t:t + 1, half:], axis=-1, keepdims=True)
        act = 0.5 * pre * (1.0 + lax.erf(pre * (2.0 ** -0.5)))
        coef = gate_t[:, t:t + 1] * act
        v_lo, v_hi = _unpack_bf16_halves(vg_ref[t * PEER_SLOTS:(t + 1) * PEER_SLOTS, :])
        rows.append(jnp.concatenate([jnp.sum(coef * v_lo, axis=0, keepdims=True),
                                     jnp.sum(coef * v_hi, axis=0, keepdims=True)], axis=-1))
    x2 = x1_ref[...] + g2_ref[0, 5:6, :] * jnp.concatenate(rows, axis=0)
    y_ref[...] = x2 * lax.rsqrt(jnp.mean(x2 * x2, axis=-1, keepdims=True) + EPS) * fn_ref[...]


def _peer_combine(ug, vg, gate, h2, x1, mod_b, final_norm):
    l, d = h2.shape
    tp = _PEER_TOKENS
    return pl.pallas_call(
        _peer_kernel,
        grid=(l // tp,),
        in_specs=[pl.BlockSpec((tp * PEER_SLOTS, d // 2), lambda i: (i, 0)),
                  pl.BlockSpec((tp * PEER_SLOTS, d // 2), lambda i: (i, 0)),
                  pl.BlockSpec((tp, PEER_SLOTS), lambda i: (i, 0)),
                  pl.BlockSpec((tp, d), lambda i: (i, 0)),
                  pl.BlockSpec((tp, d), lambda i: (i, 0)),
                  pl.BlockSpec((1, 6, d), lambda i: (0, 0, 0)),
                  pl.BlockSpec((1, d), lambda i: (0, 0))],
        out_specs=pl.BlockSpec((tp, d), lambda i: (i, 0)),
        out_shape=jax.ShapeDtypeStruct((l, d), F32),
        compiler_params=_params(("parallel",)),
        name="peer",
    )(ug, vg, gate, h2, x1, mod_b, final_norm.reshape(1, d))


def _peer_and_norm(eidx, gate, h2, x1, mod, peer_u, peer_v, final_norm):
    b, l, d = h2.shape
    eidx = eidx.reshape(b, l * PEER_SLOTS)
    gate = gate.reshape(b, l, PEER_SLOTS)

    def gather(i):
        return tuple(_sc_gather2(peer_u, peer_v, eidx[i]))

    def step(carry, i):
        ug, vg = carry
        nxt = gather(jnp.minimum(i + 1, b - 1))
        y = _peer_combine(ug, vg, gate[i], h2[i], x1[i], lax.dynamic_slice_in_dim(mod, i, 1, 0), final_norm)
        return nxt, y

    _, ys = lax.scan(step, gather(0), jnp.arange(b))
    return ys


def _layer(x, mod, conv0, s0, k_past, v_past, wts, prompt):
    b, l, d = x.shape
    qkv, z, ab, qb, kb, vb = _inproj(x, mod, wts["norm1"], wts["w_in"])
    chunk = CHUNK if prompt else l
    o_a, s_new = _gdn(qkv, z, ab, conv0, s0, wts["conv_w"], wts["alog"], wts["dtb"], wts["onorm"], chunk)
    conv_new = qkv[:, l - (CONV_W - 1):, :]
    if prompt:
        o_b = _attn_prompt(qb, kb, vb, wts["bias_prompt"], wts["lam"], wts["subln"], wts["out_scale"])
    else:
        p = k_past.shape[1]
        bias_past, bias_new = _sample_bias(wts["rel_table"], p, l)
        o_b = _attn_sample(qb, k_past.reshape(b, p, DIFF_WIDTH), v_past.reshape(b, p, DIFF_WIDTH), kb, vb,
                           bias_past, bias_new, wts["lam"], wts["subln"], wts["out_scale"])
    x1, h2, scores = _outproj(o_a, o_b, x, mod, wts["norm2"], wts["w_out"], wts["w_q"], wts["keys"])
    eidx, gate = _topk(scores)
    y = _peer_and_norm(eidx, gate, h2, x1, mod, wts["peer_u"], wts["peer_v"], wts["final_norm"])
    k_new = kb.reshape(1, b, l, DIFF_HEADS, 2 * DIFF_HEAD_DIM)
    v_new = vb.reshape(1, b, l, DIFF_HEADS, DIFF_V_DIM)
    return y, k_new, v_new, s_new[None], conv_new[None]


def kernel(x_prompt, x_sample, c_prompt, c_sample, cache_k, cache_v, state_gdn, state_conv, w_ada, b_ada,
           norm1, norm2, w_in, conv_w, a_log, dt_bias, gdn_onorm, lam_q1, lam_k1, lam_q2, lam_k2, diff_subln,
           w_out, peer_wq, peer_keys, peer_u, peer_v, rel_table, final_norm):
    assert w_ada.shape[0] == 1, "single-layer step"
    bp = x_prompt.shape[0]
    d = D_MODEL
    lam_init = 0.8 - 0.6 * math.exp(-0.3 * 0)
    lam = (jnp.exp(jnp.sum(lam_q1[0] * lam_k1[0])) - jnp.exp(jnp.sum(lam_q2[0] * lam_k2[0])) + lam_init)
    w = w_in[0]
    w_packed = jnp.concatenate(
        [w[:, :_C_AB], jnp.pad(w[:, 2048:2056], ((0, 0), (0, LANES - 2 * GDN_HEADS))), w[:, 2056:]],
        axis=1).astype(BF16)
    wts = dict(
        norm1=norm1[0], norm2=norm2[0], w_in=w_packed, conv_w=conv_w[0],
        alog=jnp.pad(a_log[0], (0, LANES - GDN_HEADS)).reshape(1, LANES),
        dtb=jnp.pad(dt_bias[0], (0, LANES - GDN_HEADS)).reshape(1, LANES),
        onorm=gdn_onorm[0], lam=jnp.full((1, DIFF_V_DIM), lam, F32), subln=diff_subln[0],
        out_scale=1.0 - lam_init, bias_prompt=_prompt_bias_tiles(rel_table), rel_table=rel_table,
        w_out=w_out[0].astype(BF16), w_q=peer_wq[0].astype(BF16),
        keys=peer_keys[0].reshape(2 * PEER_HEADS, N_KEYS, PEER_HALF).astype(BF16),
        peer_u=_pack_bf16_halves(peer_u[0]), peer_v=_pack_bf16_halves(peer_v[0]), final_norm=final_norm)

    mod = _ada(jnp.concatenate([c_prompt, c_sample], axis=0), w_ada[0], b_ada[0]).reshape(-1, 6, d)
    zeros_conv = jnp.zeros((bp, CONV_W - 1, QKV_A), F32)
    zeros_s = jnp.zeros((bp, GDN_HEADS, GDN_HEAD_DIM, GDN_HEAD_DIM), F32)
    yp, kp, vp, sp, cp = _layer(x_prompt, mod[:bp], zeros_conv, zeros_s, None, None, wts, True)
    ys, ks, vs, ss, cs = _layer(x_sample, mod[bp:], state_conv[0], state_gdn[0], cache_k[0], cache_v[0],
                                wts, False)
    return yp, ys, kp, vp, sp, cp, ks, vs, ss, cs
```

```python
import functools
import math

import jax
import jax.numpy as jnp
from jax import lax
from jax.experimental import pallas as pl
from jax.experimental.pallas import tpu as pltpu
from jax.experimental.pallas import tpu_sc as plsc

F32 = jnp.float32
BF16 = jnp.bfloat16
EPS = 1e-6

D_MODEL = 1024
CHUNK = 64
GDN_HEADS = 4
GDN_HEAD_DIM = 128
GDN_WIDTH = GDN_HEADS * GDN_HEAD_DIM
CONV_W = 4
QKV_A = 3 * GDN_WIDTH
DIFF_HEADS = 4
DIFF_HEAD_DIM = 64
DIFF_V_DIM = 128
DIFF_WIDTH = DIFF_HEADS * 2 * DIFF_HEAD_DIM
ATT_BLOCK = 256
N_BUCKETS = 32
REL_MAX_DIST = 128
PEER_HEADS = 8
N_KEYS = 128
PEER_HALF = 128
PEER_TOPK = 16
PEER_SLOTS = PEER_HEADS * PEER_TOPK
LANES = 128
NEG_BIG = -1e30
VMEM_LIMIT = 56 * 1024 * 1024

_C_QKV, _C_Z, _C_AB, _C_QB, _C_KB, _C_VB = 0, 1536, 2048, 2176, 2688, 3200
_C_END = 3712


def _params(sem):
    return pltpu.CompilerParams(dimension_semantics=sem, vmem_limit_bytes=VMEM_LIMIT)


def _dot(a, b, precision=None):
    return jnp.dot(a, b, preferred_element_type=F32, precision=precision)


def _dot_nt(a, b, precision=None):
    return lax.dot_general(a, b, (((1,), (1,)), ((), ())), preferred_element_type=F32, precision=precision)


def _silu(x):
    return x * jax.nn.sigmoid(x)


def _ada_kernel(c_ref, w_ref, b_ref, o_ref):
    a = _silu(c_ref[...]).astype(BF16)
    o_ref[...] = _dot(a, w_ref[...].astype(BF16)) + b_ref[...]


def _ada(c, w_ada, b_ada):
    n, d = c.shape
    cols = w_ada.shape[1]
    tn = 1024
    return pl.pallas_call(
        _ada_kernel,
        grid=(cols // tn,),
        in_specs=[pl.BlockSpec((n, d), lambda j: (0, 0)),
                  pl.BlockSpec((d, tn), lambda j: (0, j)),
                  pl.BlockSpec((1, tn), lambda j: (0, j))],
        out_specs=pl.BlockSpec((n, tn), lambda j: (0, j)),
        out_shape=jax.ShapeDtypeStruct((n, cols), F32),
        compiler_params=_params(("parallel",)),
        name="ada",
    )(c, w_ada, b_ada.reshape(1, cols))


def _modulated_norm(x, gain, shift, scale):
    y = x * lax.rsqrt(jnp.mean(x * x, axis=-1, keepdims=True) + EPS)
    return (y * gain) * (1.0 + scale) + shift


def _inproj_kernel(x_ref, mod_ref, n1_ref, w_ref, qkv_ref, z_ref, ab_ref, qb_ref, kb_ref, vb_ref, kh_ref, vh_ref):
    h = _modulated_norm(x_ref[0], n1_ref[...], mod_ref[0, 0:1, :], mod_ref[0, 1:2, :]).astype(BF16)
    qkv_ref[0] = _dot(h, w_ref[:, _C_QKV:_C_Z])
    z_ref[0] = _dot(h, w_ref[:, _C_Z:_C_AB])
    ab_ref[0] = _dot(h, w_ref[:, _C_AB:_C_QB])
    qb_ref[0] = _dot(h, w_ref[:, _C_QB:_C_KB])
    kb = _dot(h, w_ref[:, _C_KB:_C_VB])
    vb = _dot(h, w_ref[:, _C_VB:_C_END])
    kb_ref[0] = kb
    vb_ref[0] = vb
    kh_ref[0] = kb.astype(BF16)
    vh_ref[0] = vb.astype(BF16)


def _inproj(x, mod, norm1, w_packed):
    b, l, d = x.shape
    tl = min(l, 256)
    widths = (QKV_A, GDN_WIDTH, LANES, DIFF_WIDTH, DIFF_WIDTH, DIFF_WIDTH, DIFF_WIDTH, DIFF_WIDTH)
    dtypes = (F32,) * 6 + (BF16,) * 2
    return pl.pallas_call(
        _inproj_kernel,
        grid=(b, l // tl),
        in_specs=[pl.BlockSpec((1, tl, d), lambda i, j: (i, j, 0)),
                  pl.BlockSpec((1, 6, d), lambda i, j: (i, 0, 0)),
                  pl.BlockSpec((1, d), lambda i, j: (0, 0)),
                  pl.BlockSpec((d, _C_END), lambda i, j: (0, 0))],
        out_specs=[pl.BlockSpec((1, tl, w), lambda i, j: (i, j, 0)) for w in widths],
        out_shape=[jax.ShapeDtypeStruct((b, l, w), dt) for w, dt in zip(widths, dtypes)],
        compiler_params=_params(("parallel", "parallel")),
        name="inproj",
    )(x, mod, norm1.reshape(1, d), w_packed)


_HI = lax.Precision.HIGHEST


def _unit_lower_inverses(mats, n):
    r = lax.broadcasted_iota(jnp.int32, (n, n), 0)
    c = lax.broadcasted_iota(jnp.int32, (n, n), 1)
    eye = (r == c).astype(F32)
    ad = [jnp.where((r // 8) == (c // 8), a, 0.0) for a in mats]
    a2 = [_dot(m, m, _HI) for m in ad]
    a4 = [_dot(m, m, _HI) for m in a2]
    xs = [eye - m for m in ad]
    xs = [x + _dot(x, m, _HI) for x, m in zip(xs, a2)]
    xs = [x + _dot(x, m, _HI) for x, m in zip(xs, a4)]
    bs = 8
    while bs < n:
        off = ((r // (2 * bs)) == (c // (2 * bs))) & ((r // bs) != (c // bs))
        ys = [_dot(jnp.where(off, a, 0.0), x, _HI) for a, x in zip(mats, xs)]
        xs = [x - _dot(x, y, _HI) for x, y in zip(xs, ys)]
        bs *= 2
    return xs


def _mm(a, b):
    return _dot(a.astype(BF16), b.astype(BF16))


def _mm_nt(a, b):
    return _dot_nt(a.astype(BF16), b.astype(BF16))


def _gdn_kernel(qkv_ref, z_ref, ab_ref, conv0_ref, s0_ref, cw_ref, alog_ref, dtb_ref, onorm_ref,
                o_ref, s_ref, xbuf, *, chunk, nch):
    hd = GDN_HEAD_DIM
    rows = chunk * nch

    @pl.when(pl.program_id(1) == 0)
    def _():
        xbuf[5:8, :] = conv0_ref[0]
        s_ref[0] = s0_ref[0]

    x = qkv_ref[0]
    xbuf[8:8 + rows, :] = x
    y = (xbuf[5:5 + rows, :] * cw_ref[0:1, :] + xbuf[6:6 + rows, :] * cw_ref[1:2, :]
         + xbuf[7:7 + rows, :] * cw_ref[2:3, :] + x * cw_ref[3:4, :])
    xbuf[5:8, :] = x[rows - 3:rows, :]
    y = _silu(y)

    ab = ab_ref[0]
    t = ab + dtb_ref[...]
    softplus = jnp.maximum(t, 0.0) + jnp.log(1.0 + jnp.exp(-jnp.abs(t)))
    g = -jnp.exp(alog_ref[...]) * softplus
    beta = jax.nn.sigmoid(ab)

    r = lax.broadcasted_iota(jnp.int32, (chunk, chunk), 0)
    c = lax.broadcasted_iota(jnp.int32, (chunk, chunk), 1)
    lower = r >= c
    tri = lower.astype(F32)

    heads = range(GDN_HEADS)
    pairs = [(ci, h) for ci in range(nch) for h in heads]
    rows_of = {ci: slice(ci * chunk, (ci + 1) * chunk) for ci in range(nch)}
    gc = {ci: _dot(tri, g[rows_of[ci]], _HI) for ci in range(nch)}
    gc_t = {ci: gc[ci].T for ci in range(nch)}
    q, k, vb, kb, decay, egc, g_last = {}, {}, {}, {}, {}, {}, {}
    for ci, h in pairs:
        sl = rows_of[ci]
        qh = y[sl, h * hd:(h + 1) * hd]
        kh = y[sl, GDN_WIDTH + h * hd:GDN_WIDTH + (h + 1) * hd]
        p = ci, h
        q[p] = qh * lax.rsqrt(jnp.sum(qh * qh, axis=-1, keepdims=True) + EPS) * (hd ** -0.5)
        k[p] = kh * lax.rsqrt(jnp.sum(kh * kh, axis=-1, keepdims=True) + EPS)
        gcol = gc[ci][:, h:h + 1]
        bcol = beta[sl, GDN_HEADS + h:GDN_HEADS + h + 1]
        decay[p] = jnp.exp(jnp.where(lower, gcol - gc_t[ci][h:h + 1, :], NEG_BIG))
        kb[p] = k[p] * bcol
        vb[p] = y[sl, 2 * GDN_WIDTH + h * hd:2 * GDN_WIDTH + (h + 1) * hd] * bcol
        egc[p] = jnp.exp(gcol)
        g_last[p] = gcol[chunk - 1:chunk, :]
    kk = {p: _mm_nt(kb[p], k[p]) for p in pairs}
    tinv = dict(zip(pairs, _unit_lower_inverses([jnp.where(r > c, kk[p] * decay[p], 0.0) for p in pairs], chunk)))
    u_v = {p: _dot(tinv[p], vb[p], _HI) for p in pairs}
    w = {p: _dot(tinv[p], kb[p] * egc[p], _HI) for p in pairs}
    qk = {p: _mm_nt(q[p], k[p]) * decay[p] for p in pairs}
    k_dec_t = {p: (k[p] * jnp.exp(g_last[p] - gc[p[0]][:, p[1]:p[1] + 1])).T for p in pairs}

    s = [s_ref[0, h] for h in heads]
    for ci in range(nch):
        ws = [_mm(w[ci, h], s[h]) for h in heads]
        qs = [_mm(q[ci, h] * egc[ci, h], s[h]) for h in heads]
        v_new = [u_v[ci, h] - ws[h] for h in heads]
        o = [qs[h] + _mm(qk[ci, h], v_new[h]) for h in heads]
        s = [s[h] * jnp.exp(g_last[ci, h]) + _mm(k_dec_t[ci, h], v_new[h]) for h in heads]
        for h in heads:
            oh = o[h] * lax.rsqrt(jnp.mean(o[h] * o[h], axis=-1, keepdims=True) + EPS) * onorm_ref[...]
            o_ref[0, rows_of[ci], h * hd:(h + 1) * hd] = oh * _silu(z_ref[0, rows_of[ci], h * hd:(h + 1) * hd])
    for h in heads:
        s_ref[0, h] = s[h]


def _gdn(qkv, z, ab, conv0, s0, conv_w, alog_pad, dtb_pad, onorm, chunk):
    b, l, _ = qkv.shape
    hd = GDN_HEAD_DIM
    nch = 2 if l % (2 * chunk) == 0 else 1
    rows = chunk * nch
    return pl.pallas_call(
        functools.partial(_gdn_kernel, chunk=chunk, nch=nch),
        grid=(b, l // rows),
        in_specs=[pl.BlockSpec((1, rows, QKV_A), lambda i, j: (i, j, 0)),
                  pl.BlockSpec((1, rows, GDN_WIDTH), lambda i, j: (i, j, 0)),
                  pl.BlockSpec((1, rows, LANES), lambda i, j: (i, j, 0)),
                  pl.BlockSpec((1, CONV_W - 1, QKV_A), lambda i, j: (i, 0, 0)),
                  pl.BlockSpec((1, GDN_HEADS, hd, hd), lambda i, j: (i, 0, 0, 0)),
                  pl.BlockSpec((CONV_W, QKV_A), lambda i, j: (0, 0)),
                  pl.BlockSpec((1, LANES), lambda i, j: (0, 0)),
                  pl.BlockSpec((1, LANES), lambda i, j: (0, 0)),
                  pl.BlockSpec((1, hd), lambda i, j: (0, 0))],
        out_specs=[pl.BlockSpec((1, rows, GDN_WIDTH), lambda i, j: (i, j, 0)),
                   pl.BlockSpec((1, GDN_HEADS, hd, hd), lambda i, j: (i, 0, 0, 0))],
        out_shape=[jax.ShapeDtypeStruct((b, l, GDN_WIDTH), F32),
                   jax.ShapeDtypeStruct((b, GDN_HEADS, hd, hd), F32)],
        scratch_shapes=[pltpu.VMEM((8 + rows, QKV_A), F32)],
        compiler_params=_params(("parallel", "arbitrary")),
        name="gdn",
    )(qkv, z, ab, conv0, s0, conv_w, alog_pad, dtb_pad, onorm.reshape(1, hd))


def _rel_bucket(rel):
    nb = N_BUCKETS // 2
    max_exact = nb // 2
    ret = jnp.where(rel > 0, nb, 0)
    n = jnp.abs(rel)
    large = max_exact + (jnp.log(jnp.maximum(n, 1).astype(F32) / max_exact)
                         / math.log(REL_MAX_DIST / max_exact) * (nb - max_exact)).astype(jnp.int32)
    large = jnp.minimum(large, nb - 1)
    return ret + jnp.where(n < max_exact, n, large)


def _diff_finish(o1, o2, lam_ref, subln_ref, out_scale):
    o = o1 - lam_ref[...] * o2
    return o * lax.rsqrt(jnp.mean(o * o, axis=-1, keepdims=True) + EPS) * subln_ref[...] * out_scale


def _attn_prompt_kernel(q_ref, k_ref, v_ref, bias_ref, lam_ref, subln_ref, o_ref, m_ref, l_ref, acc_ref,
                        *, out_scale):
    i = pl.program_id(2)
    tb = ATT_BLOCK
    dh = DIFF_HEAD_DIM
    q = q_ref[0] * (dh ** -0.5)
    lane = lax.broadcasted_iota(jnp.int32, q.shape, 1)
    q2s = jnp.concatenate([jnp.where(lane < dh, q, 0.0), jnp.where(lane >= dh, q, 0.0)], axis=0).astype(BF16)

    def score_tiles(j, tile):
        start = pl.multiple_of(j * tb, tb)
        s = _dot_nt(q2s, k_ref[0, pl.ds(start, tb), :]) + bias_ref[0, tile]
        return [s[:, c:c + LANES] for c in range(0, tb, LANES)]

    def visible_blocks(fn, unroll):
        n_far = jnp.maximum(i - 1, 0)

        @pl.loop(0, n_far // unroll)
        def _(g):
            fn([(unroll * g + u, 0) for u in range(unroll)])

        @pl.loop((n_far // unroll) * unroll, n_far)
        def _(j):
            fn([(j, 0)])

        @pl.when(i > 0)
        def _():
            fn([(i - 1, 1), (i, 2)])

        @pl.when(i == 0)
        def _():
            fn([(i, 2)])

    m_ref[...] = jnp.full(m_ref.shape, NEG_BIG, F32)

    def track_max(blocks):
        tiles = [s for j, tile in blocks for s in score_tiles(j, tile)]
        m_ref[...] = functools.reduce(jnp.maximum, tiles, m_ref[...])

    visible_blocks(track_max, 4)
    m_ref[...] = jnp.broadcast_to(jnp.max(m_ref[...], axis=-1, keepdims=True), m_ref.shape)

    l_ref[...] = jnp.zeros(l_ref.shape, F32)
    acc_ref[...] = jnp.zeros(acc_ref.shape, F32)

    def accumulate(blocks):
        m = m_ref[...]
        l_add, acc_add = [], []
        for j, tile in blocks:
            p = [jnp.exp(s - m) for s in score_tiles(j, tile)]
            l_add.extend(p)
            start = pl.multiple_of(j * tb, tb)
            acc_add.append(_dot(jnp.concatenate(p, axis=-1).astype(BF16), v_ref[0, pl.ds(start, tb), :]))
        l_ref[...] += sum(l_add)
        acc_ref[...] += sum(acc_add)

    visible_blocks(accumulate, 4)
    o = acc_ref[...] / jnp.sum(l_ref[...], axis=-1, keepdims=True)
    o_ref[0] = _diff_finish(o[:tb], o[tb:], lam_ref, subln_ref, out_scale)


def _attn_prompt(qb, kh, vh, bias_tiles, lam_row, subln, out_scale):
    b, l, _ = qb.shape
    tb = ATT_BLOCK
    hw = 2 * DIFF_HEAD_DIM
    return pl.pallas_call(
        functools.partial(_attn_prompt_kernel, out_scale=out_scale),
        grid=(b, DIFF_HEADS, l // tb),
        in_specs=[pl.BlockSpec((1, tb, hw), lambda bi, h, i: (bi, i, h)),
                  pl.BlockSpec((1, l, hw), lambda bi, h, i: (bi, 0, h)),
                  pl.BlockSpec((1, l, DIFF_V_DIM), lambda bi, h, i: (bi, 0, h)),
                  pl.BlockSpec((1, 3, 2 * tb, tb), lambda bi, h, i: (h, 0, 0, 0)),
                  pl.BlockSpec((1, DIFF_V_DIM), lambda bi, h, i: (0, 0)),
                  pl.BlockSpec((1, DIFF_V_DIM), lambda bi, h, i: (0, 0))],
        out_specs=pl.BlockSpec((1, tb, DIFF_V_DIM), lambda bi, h, i: (bi, i, h)),
        out_shape=jax.ShapeDtypeStruct((b, l, DIFF_WIDTH), F32),
        scratch_shapes=[pltpu.VMEM((2 * tb, LANES), F32), pltpu.VMEM((2 * tb, LANES), F32),
                        pltpu.VMEM((2 * tb, DIFF_V_DIM), F32)],
        compiler_params=_params(("parallel", "parallel", "arbitrary")),
        name="attn_prompt",
    )(qb, kh, vh, bias_tiles, lam_row, subln.reshape(1, DIFF_V_DIM))


def _prompt_bias_tiles(rel_table):
    tb = ATT_BLOCK
    qi = jnp.arange(tb)[:, None]
    ki = jnp.arange(tb)[None, :]
    far = jnp.broadcast_to(rel_table[_rel_bucket(jnp.array(-2 * tb))], (tb, tb, DIFF_HEADS))
    prev = rel_table[_rel_bucket(ki - qi - tb)]
    diag = jnp.where(((ki // CHUNK) <= (qi // CHUNK))[..., None], rel_table[_rel_bucket(ki - qi)], NEG_BIG)
    tiles = jnp.moveaxis(jnp.stack([far, prev, diag]).astype(F32), -1, 0)
    return jnp.concatenate([tiles, tiles], axis=2)


def _attn_sample_kernel(q_ref, kp_ref, vp_ref, kn_ref, vn_ref, bp_ref, bn_ref, lam_ref, subln_ref, o_ref,
                        *, out_scale):
    dh = DIFF_HEAD_DIM
    q = q_ref[0] * (dh ** -0.5)
    kp = kp_ref[0].astype(BF16)
    kn = kn_ref[0].astype(BF16)
    vp = vp_ref[0].astype(BF16)
    vn = vn_ref[0].astype(BF16)
    outs = []
    for t in range(2):
        qt = q[:, t * dh:(t + 1) * dh].astype(BF16)
        sp = _dot_nt(qt, kp[:, t * dh:(t + 1) * dh]) + bp_ref[0]
        sn = _dot_nt(qt, kn[:, t * dh:(t + 1) * dh]) + bn_ref[0]
        m = jnp.maximum(jnp.max(sp, axis=-1, keepdims=True), jnp.max(sn, axis=-1, keepdims=True))
        pp = jnp.exp(sp - m)
        pn = jnp.exp(sn - m)
        den = jnp.sum(pp, axis=-1, keepdims=True) + jnp.sum(pn, axis=-1, keepdims=True)
        outs.append((_dot(pp.astype(BF16), vp) + _dot(pn.astype(BF16), vn)) / den)
    o_ref[0] = _diff_finish(outs[0], outs[1], lam_ref, subln_ref, out_scale)


def _attn_sample(qb, k_past, v_past, k_new, v_new, bias_past, bias_new, lam_row, subln, out_scale):
    b, l, _ = qb.shape
    p = k_past.shape[1]
    hw = 2 * DIFF_HEAD_DIM
    return pl.pallas_call(
        functools.partial(_attn_sample_kernel, out_scale=out_scale),
        grid=(b, DIFF_HEADS),
        in_specs=[pl.BlockSpec((1, l, hw), lambda bi, h: (bi, 0, h)),
                  pl.BlockSpec((1, p, hw), lambda bi, h: (bi, 0, h)),
                  pl.BlockSpec((1, p, DIFF_V_DIM), lambda bi, h: (bi, 0, h)),
                  pl.BlockSpec((1, l, hw), lambda bi, h: (bi, 0, h)),
                  pl.BlockSpec((1, l, DIFF_V_DIM), lambda bi, h: (bi, 0, h)),
                  pl.BlockSpec((1, l, p), lambda bi, h: (h, 0, 0)),
                  pl.BlockSpec((1, l, l), lambda bi, h: (h, 0, 0)),
                  pl.BlockSpec((1, DIFF_V_DIM), lambda bi, h: (0, 0)),
                  pl.BlockSpec((1, DIFF_V_DIM), lambda bi, h: (0, 0))],
        out_specs=pl.BlockSpec((1, l, DIFF_V_DIM), lambda bi, h: (bi, 0, h)),
        out_shape=jax.ShapeDtypeStruct((b, l, DIFF_WIDTH), F32),
        compiler_params=_params(("parallel", "parallel")),
        name="attn_sample",
    )(qb, k_past, v_past, k_new, v_new, bias_past, bias_new, lam_row, subln.reshape(1, DIFF_V_DIM))


def _sample_bias(rel_table, p, l):
    rel = jnp.arange(p + l)[None, :] - (p + jnp.arange(l))[:, None]
    bias = jnp.moveaxis(rel_table[_rel_bucket(rel)].astype(F32), -1, 0)
    return bias[:, :, :p], bias[:, :, p:]


def _outproj_kernel(oa_ref, ob_ref, x_ref, mod_ref, n2_ref, wo_ref, wq_ref, keys_ref,
                    x1_ref, h2_ref, sc_ref):
    mixed = jnp.concatenate([oa_ref[0], ob_ref[0]], axis=-1).astype(BF16)
    x1 = x_ref[0] + mod_ref[0, 2:3, :] * _dot(mixed, wo_ref[...])
    x1_ref[0] = x1
    h2 = _modulated_norm(x1, n2_ref[...], mod_ref[0, 3:4, :], mod_ref[0, 4:5, :])
    h2_ref[0] = h2
    qh = _dot(h2.astype(BF16), wq_ref[...]).astype(BF16)
    for hp in range(2 * PEER_HEADS):
        sc_ref[0, hp] = _dot_nt(keys_ref[hp], qh[:, hp * PEER_HALF:(hp + 1) * PEER_HALF])


def _outproj(o_a, o_b, x, mod, norm2, w_out, w_q, keys):
    b, l, d = x.shape
    tl = min(l, 256)
    nhp = 2 * PEER_HEADS
    return pl.pallas_call(
        _outproj_kernel,
        grid=(b, l // tl),
        in_specs=[pl.BlockSpec((1, tl, GDN_WIDTH), lambda i, j: (i, j, 0)),
                  pl.BlockSpec((1, tl, DIFF_WIDTH), lambda i, j: (i, j, 0)),
                  pl.BlockSpec((1, tl, d), lambda i, j: (i, j, 0)),
                  pl.BlockSpec((1, 6, d), lambda i, j: (i, 0, 0)),
                  pl.BlockSpec((1, d), lambda i, j: (0, 0)),
                  pl.BlockSpec((d, d), lambda i, j: (0, 0)),
                  pl.BlockSpec((d, nhp * PEER_HALF), lambda i, j: (0, 0)),
                  pl.BlockSpec((nhp, N_KEYS, PEER_HALF), lambda i, j: (0, 0, 0))],
        out_specs=[pl.BlockSpec((1, tl, d), lambda i, j: (i, j, 0)),
                   pl.BlockSpec((1, tl, d), lambda i, j: (i, j, 0)),
                   pl.BlockSpec((1, nhp, N_KEYS, tl), lambda i, j: (i, 0, 0, j))],
        out_shape=[jax.ShapeDtypeStruct((b, l, d), F32),
                   jax.ShapeDtypeStruct((b, l, d), F32),
                   jax.ShapeDtypeStruct((b, nhp, N_KEYS, l), F32)],
        compiler_params=_params(("parallel", "parallel")),
        name="outproj",
    )(o_a, o_b, x, mod, norm2.reshape(1, d), w_out, w_q, keys)


def _top16_rows(s, n):
    row = lax.broadcasted_iota(jnp.int32, s.shape, 0)
    vals, idxs = [], []
    for _ in range(PEER_TOPK):
        m = jnp.max(s, axis=0, keepdims=True)
        i = jnp.min(jnp.where(s == m, row, n), axis=0, keepdims=True)
        vals.append(m)
        idxs.append(i)
        s = jnp.where(row == i, -jnp.inf, s)
    return jnp.concatenate(vals, axis=0), jnp.concatenate(idxs, axis=0)


def _pick_rows(table, sel):
    out = jnp.zeros_like(table)
    for a in range(PEER_TOPK):
        out = jnp.where(sel == a, table[a:a + 1, :], out)
    return out


def _topk_kernel(sc_ref, eidx_ref, gate_ref):
    eidx, gates = [], []
    for h in range(PEER_HEADS):
        s1, i1 = _top16_rows(sc_ref[0, 2 * h], N_KEYS)
        s2, i2 = _top16_rows(sc_ref[0, 2 * h + 1], N_KEYS)
        cand = jnp.concatenate([s1[a:a + 1, :] + s2 for a in range(PEER_TOPK)], axis=0)
        top_s, pos = _top16_rows(cand, PEER_TOPK * PEER_TOPK)
        eidx.append(_pick_rows(i1, pos // PEER_TOPK) * N_KEYS + _pick_rows(i2, pos % PEER_TOPK))
        e = jnp.exp(top_s - top_s[0:1, :])
        gates.append(e / jnp.sum(e, axis=0, keepdims=True))
    eidx_ref[...] = jnp.concatenate(eidx, axis=0).T
    gate_ref[...] = jnp.concatenate(gates, axis=0).T


def _topk(scores):
    b, nhp, nk, l = scores.shape
    tt = min(l, LANES)
    nt = l // tt
    return pl.pallas_call(
        _topk_kernel,
        grid=(b, nt),
        in_specs=[pl.BlockSpec((1, nhp, nk, tt), lambda i, j: (i, 0, 0, j))],
        out_specs=[pl.BlockSpec((tt, PEER_SLOTS), lambda i, j: (i * nt + j, 0)),
                   pl.BlockSpec((tt, PEER_SLOTS), lambda i, j: (i * nt + j, 0))],
        out_shape=[jax.ShapeDtypeStruct((b * l, PEER_SLOTS), jnp.int32),
                   jax.ShapeDtypeStruct((b * l, PEER_SLOTS), F32)],
        compiler_params=_params(("parallel", "parallel")),
        name="topk",
    )(scores)


_SC_ROWS = 32


def _pack_bf16_halves(t):
    half = t.shape[1] // 2
    tb = lax.bitcast_convert_type(t.astype(BF16), jnp.uint16).astype(jnp.uint32)
    return lax.bitcast_convert_type(tb[:, :half] | (tb[:, half:] << 16), jnp.int32)


def _sc_gather2(tab_u, tab_v, idx):
    n = idx.shape[0]
    w = tab_u.shape[1]
    info = plsc.get_sparse_core_info()
    nw = info.num_cores * info.num_subcores
    per_w = n // nw
    nch = per_w // _SC_ROWS
    assert n % (nw * _SC_ROWS * 2) == 0
    mesh = plsc.VectorSubcoreMesh(core_axis_name="c", subcore_axis_name="s")
    rows = pltpu.VMEM((_SC_ROWS, w), tab_u.dtype)
    out = jax.ShapeDtypeStruct((n, w), tab_u.dtype)

    @functools.partial(
        pl.kernel, mesh=mesh, out_type=[out, out],
        scratch_types=[pltpu.VMEM((_SC_ROWS,), jnp.int32)] * 2 + [rows] * 4 + [pltpu.SemaphoreType.DMA] * 8,
    )
    def k(u_hbm, v_hbm, idx_hbm, uo_hbm, vo_hbm, i0, i1, u0, u1, v0, v1, gu0, gu1, gv0, gv1, wu0, wu1, wv0, wv1):
        idx_v, ub, vb = (i0, i1), (u0, u1), (v0, v1)
        gsu, gsv, wsu, wsv = (gu0, gu1), (gv0, gv1), (wu0, wu1), (wv0, wv1)
        wid = lax.axis_index("s") * info.num_cores + lax.axis_index("c")
        base = wid * per_w

        def span(c):
            return pl.ds(pl.multiple_of(base + c * _SC_ROWS, _SC_ROWS), _SC_ROWS)

        def gathers(s):
            return (pltpu.make_async_copy(u_hbm.at[idx_v[s]], ub[s], gsu[s]),
                    pltpu.make_async_copy(v_hbm.at[idx_v[s]], vb[s], gsv[s]))

        def writes(c, s):
            return (pltpu.make_async_copy(ub[s], uo_hbm.at[span(c)], wsu[s]),
                    pltpu.make_async_copy(vb[s], vo_hbm.at[span(c)], wsv[s]))

        def start(copies):
            for cp in copies:
                cp.start()

        def wait(copies):
            for cp in copies:
                cp.wait()

        def fetch(c, s):
            pltpu.sync_copy(idx_hbm.at[span(c)], idx_v[s])
            start(gathers(s))

        fetch(0, 0)

        @pl.loop(0, nch // 2)
        def _(kk):
            c0 = 2 * kk
            wait(gathers(0))
            start(writes(c0, 0))

            @pl.when(kk > 0)
            def _():
                wait(writes(c0 - 1, 1))

            fetch(c0 + 1, 1)
            wait(gathers(1))
            start(writes(c0 + 1, 1))
            wait(writes(c0, 0))

            @pl.when(kk < nch // 2 - 1)
            def _():
                fetch(c0 + 2, 0)

        wait(writes(nch - 1, 1))

    return k(tab_u, tab_v, idx)


_PEER_TOKENS = 16


def _unpack_bf16_halves(w):
    lo = lax.bitcast_convert_type(w << 16, F32)
    hi = lax.bitcast_convert_type(w & jnp.int32(-65536), F32)
    return lo, hi


def _peer_kernel(ug_ref, vg_ref, gate_ref, h2_ref, x1_ref, g2_ref, fn_ref, y_ref):
    half = D_MODEL // 2
    gate_t = gate_ref[...].T
    rows = []
    for t in range(_PEER_TOKENS):
        u_lo, u_hi = _unpack_bf16_halves(ug_ref[t * PEER_SLOTS:(t + 1) * PEER_SLOTS, :])
        pre = jnp.sum(u_lo * h2_ref[t:t + 1, :half] + u_hi * h2_ref[t:t + 1, half:], axis=-1, keepdims=True)
        act = 0.5 * pre * (1.0 + lax.erf(pre * (2.0 ** -0.5)))
        coef = gate_t[:, t:t + 1] * act
        v_lo, v_hi = _unpack_bf16_halves(vg_ref[t * PEER_SLOTS:(t + 1) * PEER_SLOTS, :])
        rows.append(jnp.concatenate([jnp.sum(coef * v_lo, axis=0, keepdims=True),
                                     jnp.sum(coef * v_hi, axis=0, keepdims=True)], axis=-1))
    x2 = x1_ref[...] + g2_ref[0, 5:6, :] * jnp.concatenate(rows, axis=0)
    y_ref[...] = x2 * lax.rsqrt(jnp.mean(x2 * x2, axis=-1, keepdims=True) + EPS) * fn_ref[...]


def _peer_combine(ug, vg, gate, h2, x1, mod_b, final_norm):
    l, d = h2.shape
    tp = _PEER_TOKENS
    return pl.pallas_call(
        _peer_kernel,
        grid=(l // tp,),
        in_specs=[pl.BlockSpec((tp * PEER_SLOTS, d // 2), lambda i: (i, 0)),
                  pl.BlockSpec((tp * PEER_SLOTS, d // 2), lambda i: (i, 0)),
                  pl.BlockSpec((tp, PEER_SLOTS), lambda i: (i, 0)),
                  pl.BlockSpec((tp, d), lambda i: (i, 0)),
                  pl.BlockSpec((tp, d), lambda i: (i, 0)),
                  pl.BlockSpec((1, 6, d), lambda i: (0, 0, 0)),
                  pl.BlockSpec((1, d), lambda i: (0, 0))],
        out_specs=pl.BlockSpec((tp, d), lambda i: (i, 0)),
        out_shape=jax.ShapeDtypeStruct((l, d), F32),
        compiler_params=_params(("parallel",)),
        name="peer",
    )(ug, vg, gate, h2, x1, mod_b, final_norm.reshape(1, d))


def _peer_and_norm(eidx, gate, h2, x1, mod, peer_u, peer_v, final_norm):
    b, l, d = h2.shape
    eidx = eidx.reshape(b, l * PEER_SLOTS)
    gate = gate.reshape(b, l, PEER_SLOTS)

    def gather(i):
        return tuple(_sc_gather2(peer_u, peer_v, eidx[i]))

    def combine(rows, i):
        return _peer_combine(*rows, gate[i], h2[i], x1[i], lax.dynamic_slice_in_dim(mod, i, 1, 0), final_norm)

    def step(even_rows, j):
        i = 2 * j
        odd_rows = gather(i + 1)
        y_even = combine(even_rows, i)
        next_even = gather(jnp.minimum(i + 2, b - 1))
        y_odd = combine(odd_rows, i + 1)
        return next_even, jnp.stack([y_even, y_odd])

    assert b % 2 == 0
    _, ys = lax.scan(step, gather(0), jnp.arange(b // 2))
    return ys.reshape(b, l, d)


def _layer(x, mod, conv0, s0, k_past, v_past, wts, prompt):
    b, l, d = x.shape
    qkv, z, ab, qb, kb, vb, kh, vh = _inproj(x, mod, wts["norm1"], wts["w_in"])
    chunk = CHUNK if prompt else l
    o_a, s_new = _gdn(qkv, z, ab, conv0, s0, wts["conv_w"], wts["alog"], wts["dtb"], wts["onorm"], chunk)
    conv_new = qkv[:, l - (CONV_W - 1):, :]
    if prompt:
        o_b = _attn_prompt(qb, kh, vh, wts["bias_prompt"], wts["lam"], wts["subln"], wts["out_scale"])
    else:
        p = k_past.shape[1]
        bias_past, bias_new = _sample_bias(wts["rel_table"], p, l)
        o_b = _attn_sample(qb, k_past.reshape(b, p, DIFF_WIDTH), v_past.reshape(b, p, DIFF_WIDTH), kb, vb,
                           bias_past, bias_new, wts["lam"], wts["subln"], wts["out_scale"])
    x1, h2, scores = _outproj(o_a, o_b, x, mod, wts["norm2"], wts["w_out"], wts["w_q"], wts["keys"])
    eidx, gate = _topk(scores)
    y = _peer_and_norm(eidx, gate, h2, x1, mod, wts["peer_u"], wts["peer_v"], wts["final_norm"])
    k_new = kb.reshape(1, b, l, DIFF_HEADS, 2 * DIFF_HEAD_DIM)
    v_new = vb.reshape(1, b, l, DIFF_HEADS, DIFF_V_DIM)
    return y, k_new, v_new, s_new[None], conv_new[None]


def kernel(x_prompt, x_sample, c_prompt, c_sample, cache_k, cache_v, state_gdn, state_conv, w_ada, b_ada,
           norm1, norm2, w_in, conv_w, a_log, dt_bias, gdn_onorm, lam_q1, lam_k1, lam_q2, lam_k2, diff_subln,
           w_out, peer_wq, peer_keys, peer_u, peer_v, rel_table, final_norm):
    assert w_ada.shape[0] == 1, "single-layer step"
    bp = x_prompt.shape[0]
    d = D_MODEL
    lam_init = 0.8 - 0.6 * math.exp(-0.3 * 0)
    lam = (jnp.exp(jnp.sum(lam_q1[0] * lam_k1[0])) - jnp.exp(jnp.sum(lam_q2[0] * lam_k2[0])) + lam_init)
    w = w_in[0]
    w_packed = jnp.concatenate(
        [w[:, :_C_AB], jnp.pad(w[:, 2048:2056], ((0, 0), (0, LANES - 2 * GDN_HEADS))), w[:, 2056:]],
        axis=1).astype(BF16)
    wts = dict(
        norm1=norm1[0], norm2=norm2[0], w_in=w_packed, conv_w=conv_w[0],
        alog=jnp.pad(a_log[0], (0, LANES - GDN_HEADS)).reshape(1, LANES),
        dtb=jnp.pad(dt_bias[0], (0, LANES - GDN_HEADS)).reshape(1, LANES),
        onorm=gdn_onorm[0], lam=jnp.full((1, DIFF_V_DIM), lam, F32), subln=diff_subln[0],
        out_scale=1.0 - lam_init, bias_prompt=_prompt_bias_tiles(rel_table), rel_table=rel_table,
        w_out=w_out[0].astype(BF16), w_q=peer_wq[0].astype(BF16),
        keys=peer_keys[0].reshape(2 * PEER_HEADS, N_KEYS, PEER_HALF).astype(BF16),
        peer_u=_pack_bf16_halves(peer_u[0]), peer_v=_pack_bf16_halves(peer_v[0]), final_norm=final_norm)

    mod = _ada(jnp.concatenate([c_prompt, c_sample], axis=0), w_ada[0], b_ada[0]).reshape(-1, 6, d)
    zeros_conv = jnp.zeros((bp, CONV_W - 1, QKV_A), F32)
    zeros_s = jnp.zeros((bp, GDN_HEADS, GDN_HEAD_DIM, GDN_HEAD_DIM), F32)
    yp, kp, vp, sp, cp = _layer(x_prompt, mod[:bp], zeros_conv, zeros_s, None, None, wts, True)
    ys, ks, vs, ss, cs = _layer(x_sample, mod[bp:], state_conv[0], state_gdn[0], cache_k[0], cache_v[0],
                                wts, False)
    return yp, ys, kp, vp, sp, cp, ks, vs, ss, cs
```

```python
import functools
import math

import jax
import jax.numpy as jnp
from jax import lax
from jax.experimental import pallas as pl
from jax.experimental.pallas import tpu as pltpu
from jax.experimental.pallas import tpu_sc as plsc

F32 = jnp.float32
BF16 = jnp.bfloat16
EPS = 1e-6

D_MODEL = 1024
CHUNK = 64
GDN_HEADS = 4
GDN_HEAD_DIM = 128
GDN_WIDTH = GDN_HEADS * GDN_HEAD_DIM
CONV_W = 4
QKV_A = 3 * GDN_WIDTH
DIFF_HEADS = 4
DIFF_HEAD_DIM = 64
DIFF_V_DIM = 128
DIFF_WIDTH = DIFF_HEADS * 2 * DIFF_HEAD_DIM
ATT_BLOCK = 256
N_BUCKETS = 32
REL_MAX_DIST = 128
PEER_HEADS = 8
N_KEYS = 128
PEER_HALF = 128
PEER_TOPK = 16
PEER_SLOTS = PEER_HEADS * PEER_TOPK
LANES = 128
NEG_BIG = -1e30
VMEM_LIMIT = 56 * 1024 * 1024

_C_QKV, _C_Z, _C_AB, _C_QB, _C_KB, _C_VB = 0, 1536, 2048, 2176, 2688, 3200
_C_END = 3712


def _params(sem):
    return pltpu.CompilerParams(dimension_semantics=sem, vmem_limit_bytes=VMEM_LIMIT)


def _dot(a, b, precision=None):
    return jnp.dot(a, b, preferred_element_type=F32, precision=precision)


def _dot_nt(a, b, precision=None):
    return lax.dot_general(a, b, (((1,), (1,)), ((), ())), preferred_element_type=F32, precision=precision)


def _silu(x):
    return x * jax.nn.sigmoid(x)


def _ada_kernel(c_ref, w_ref, b_ref, o_ref):
    a = _silu(c_ref[...]).astype(BF16)
    o_ref[...] = _dot(a, w_ref[...].astype(BF16)) + b_ref[...]


def _ada(c, w_ada, b_ada):
    n, d = c.shape
    cols = w_ada.shape[1]
    tn = 1024
    return pl.pallas_call(
        _ada_kernel,
        grid=(cols // tn,),
        in_specs=[pl.BlockSpec((n, d), lambda j: (0, 0)),
                  pl.BlockSpec((d, tn), lambda j: (0, j)),
                  pl.BlockSpec((1, tn), lambda j: (0, j))],
        out_specs=pl.BlockSpec((n, tn), lambda j: (0, j)),
        out_shape=jax.ShapeDtypeStruct((n, cols), F32),
        compiler_params=_params(("parallel",)),
        name="ada",
    )(c, w_ada, b_ada.reshape(1, cols))


def _modulated_norm(x, gain, shift, scale):
    y = x * lax.rsqrt(jnp.mean(x * x, axis=-1, keepdims=True) + EPS)
    return (y * gain) * (1.0 + scale) + shift


def _inproj_kernel(x_ref, mod_ref, n1_ref, w_ref, qkv_ref, z_ref, ab_ref, qb_ref, kb_ref, vb_ref, kh_ref, vh_ref):
    h = _modulated_norm(x_ref[0], n1_ref[...], mod_ref[0, 0:1, :], mod_ref[0, 1:2, :]).astype(BF16)
    qkv_ref[0] = _dot(h, w_ref[:, _C_QKV:_C_Z])
    z_ref[0] = _dot(h, w_ref[:, _C_Z:_C_AB])
    ab_ref[0] = _dot(h, w_ref[:, _C_AB:_C_QB])
    qb_ref[0] = _dot(h, w_ref[:, _C_QB:_C_KB])
    kb = _dot(h, w_ref[:, _C_KB:_C_VB])
    vb = _dot(h, w_ref[:, _C_VB:_C_END])
    kb_ref[0] = kb
    vb_ref[0] = vb
    kh_ref[0] = kb.astype(BF16)
    vh_ref[0] = vb.astype(BF16)


def _inproj(x, mod, norm1, w_packed):
    b, l, d = x.shape
    tl = min(l, 256)
    widths = (QKV_A, GDN_WIDTH, LANES, DIFF_WIDTH, DIFF_WIDTH, DIFF_WIDTH, DIFF_WIDTH, DIFF_WIDTH)
    dtypes = (F32,) * 6 + (BF16,) * 2
    return pl.pallas_call(
        _inproj_kernel,
        grid=(b, l // tl),
        in_specs=[pl.BlockSpec((1, tl, d), lambda i, j: (i, j, 0)),
                  pl.BlockSpec((1, 6, d), lambda i, j: (i, 0, 0)),
                  pl.BlockSpec((1, d), lambda i, j: (0, 0)),
                  pl.BlockSpec((d, _C_END), lambda i, j: (0, 0))],
        out_specs=[pl.BlockSpec((1, tl, w), lambda i, j: (i, j, 0)) for w in widths],
        out_shape=[jax.ShapeDtypeStruct((b, l, w), dt) for w, dt in zip(widths, dtypes)],
        compiler_params=_params(("parallel", "parallel")),
        name="inproj",
    )(x, mod, norm1.reshape(1, d), w_packed)


_HI = lax.Precision.HIGHEST


def _unit_lower_inverses(mats, n):
    r = lax.broadcasted_iota(jnp.int32, (n, n), 0)
    c = lax.broadcasted_iota(jnp.int32, (n, n), 1)
    eye = (r == c).astype(F32)
    ad = [jnp.where((r // 8) == (c // 8), a, 0.0) for a in mats]
    a2 = [_dot(m, m, _HI) for m in ad]
    a4 = [_dot(m, m, _HI) for m in a2]
    xs = [eye - m for m in ad]
    xs = [x + _dot(x, m, _HI) for x, m in zip(xs, a2)]
    xs = [x + _dot(x, m, _HI) for x, m in zip(xs, a4)]
    bs = 8
    while bs < n:
        off = ((r // (2 * bs)) == (c // (2 * bs))) & ((r // bs) != (c // bs))
        ys = [_dot(jnp.where(off, a, 0.0), x, _HI) for a, x in zip(mats, xs)]
        xs = [x - _dot(x, y, _HI) for x, y in zip(xs, ys)]
        bs *= 2
    return xs


def _mm(a, b):
    return _dot(a.astype(BF16), b.astype(BF16))


def _mm_nt(a, b):
    return _dot_nt(a.astype(BF16), b.astype(BF16))


def _gdn_kernel(qkv_ref, z_ref, ab_ref, conv0_ref, s0_ref, cw_ref, alog_ref, dtb_ref, onorm_ref,
                o_ref, s_ref, xbuf, *, chunk, nch):
    hd = GDN_HEAD_DIM
    rows = chunk * nch

    @pl.when(pl.program_id(1) == 0)
    def _():
        xbuf[5:8, :] = conv0_ref[0]
        s_ref[0] = s0_ref[0]

    x = qkv_ref[0]
    xbuf[8:8 + rows, :] = x
    y = (xbuf[5:5 + rows, :] * cw_ref[0:1, :] + xbuf[6:6 + rows, :] * cw_ref[1:2, :]
         + xbuf[7:7 + rows, :] * cw_ref[2:3, :] + x * cw_ref[3:4, :])
    xbuf[5:8, :] = x[rows - 3:rows, :]
    y = _silu(y)

    ab = ab_ref[0]
    t = ab + dtb_ref[...]
    softplus = jnp.maximum(t, 0.0) + jnp.log(1.0 + jnp.exp(-jnp.abs(t)))
    g = -jnp.exp(alog_ref[...]) * softplus
    beta = jax.nn.sigmoid(ab)

    r = lax.broadcasted_iota(jnp.int32, (chunk, chunk), 0)
    c = lax.broadcasted_iota(jnp.int32, (chunk, chunk), 1)
    lower = r >= c
    tri = lower.astype(F32)

    heads = range(GDN_HEADS)
    pairs = [(ci, h) for ci in range(nch) for h in heads]
    rows_of = {ci: slice(ci * chunk, (ci + 1) * chunk) for ci in range(nch)}
    gc = {ci: _dot(tri, g[rows_of[ci]], _HI) for ci in range(nch)}
    gc_t = {ci: gc[ci].T for ci in range(nch)}
    q, k, vb, kb, decay, egc, g_last = {}, {}, {}, {}, {}, {}, {}
    for ci, h in pairs:
        sl = rows_of[ci]
        qh = y[sl, h * hd:(h + 1) * hd]
        kh = y[sl, GDN_WIDTH + h * hd:GDN_WIDTH + (h + 1) * hd]
        p = ci, h
        q[p] = qh * lax.rsqrt(jnp.sum(qh * qh, axis=-1, keepdims=True) + EPS) * (hd ** -0.5)
        k[p] = kh * lax.rsqrt(jnp.sum(kh * kh, axis=-1, keepdims=True) + EPS)
        gcol = gc[ci][:, h:h + 1]
        bcol = beta[sl, GDN_HEADS + h:GDN_HEADS + h + 1]
        decay[p] = jnp.exp(jnp.where(lower, gcol - gc_t[ci][h:h + 1, :], NEG_BIG))
        kb[p] = k[p] * bcol
        vb[p] = y[sl, 2 * GDN_WIDTH + h * hd:2 * GDN_WIDTH + (h + 1) * hd] * bcol
        egc[p] = jnp.exp(gcol)
        g_last[p] = gcol[chunk - 1:chunk, :]
    kk = {p: _mm_nt(kb[p], k[p]) for p in pairs}
    tinv = dict(zip(pairs, _unit_lower_inverses([jnp.where(r > c, kk[p] * decay[p], 0.0) for p in pairs], chunk)))
    u_v = {p: _dot(tinv[p], vb[p], _HI) for p in pairs}
    w = {p: _dot(tinv[p], kb[p] * egc[p], _HI) for p in pairs}
    qk = {p: _mm_nt(q[p], k[p]) * decay[p] for p in pairs}
    k_dec_t = {p: (k[p] * jnp.exp(g_last[p] - gc[p[0]][:, p[1]:p[1] + 1])).T for p in pairs}

    s = [s_ref[0, h] for h in heads]
    for ci in range(nch):
        ws = [_mm(w[ci, h], s[h]) for h in heads]
        qs = [_mm(q[ci, h] * egc[ci, h], s[h]) for h in heads]
        v_new = [u_v[ci, h] - ws[h] for h in heads]
        o = [qs[h] + _mm(qk[ci, h], v_new[h]) for h in heads]
        s = [s[h] * jnp.exp(g_last[ci, h]) + _mm(k_dec_t[ci, h], v_new[h]) for h in heads]
        for h in heads:
            oh = o[h] * lax.rsqrt(jnp.mean(o[h] * o[h], axis=-1, keepdims=True) + EPS) * onorm_ref[...]
            o_ref[0, rows_of[ci], h * hd:(h + 1) * hd] = oh * _silu(z_ref[0, rows_of[ci], h * hd:(h + 1) * hd])
    for h in heads:
        s_ref[0, h] = s[h]


def _gdn(qkv, z, ab, conv0, s0, conv_w, alog_pad, dtb_pad, onorm, chunk):
    b, l, _ = qkv.shape
    hd = GDN_HEAD_DIM
    nch = 2 if l % (2 * chunk) == 0 else 1
    rows = chunk * nch
    return pl.pallas_call(
        functools.partial(_gdn_kernel, chunk=chunk, nch=nch),
        grid=(b, l // rows),
        in_specs=[pl.BlockSpec((1, rows, QKV_A), lambda i, j: (i, j, 0)),
                  pl.BlockSpec((1, rows, GDN_WIDTH), lambda i, j: (i, j, 0)),
                  pl.BlockSpec((1, rows, LANES), lambda i, j: (i, j, 0)),
                  pl.BlockSpec((1, CONV_W - 1, QKV_A), lambda i, j: (i, 0, 0)),
                  pl.BlockSpec((1, GDN_HEADS, hd, hd), lambda i, j: (i, 0, 0, 0)),
                  pl.BlockSpec((CONV_W, QKV_A), lambda i, j: (0, 0)),
                  pl.BlockSpec((1, LANES), lambda i, j: (0, 0)),
                  pl.BlockSpec((1, LANES), lambda i, j: (0, 0)),
                  pl.BlockSpec((1, hd), lambda i, j: (0, 0))],
        out_specs=[pl.BlockSpec((1, rows, GDN_WIDTH), lambda i, j: (i, j, 0)),
                   pl.BlockSpec((1, GDN_HEADS, hd, hd), lambda i, j: (i, 0, 0, 0))],
        out_shape=[jax.ShapeDtypeStruct((b, l, GDN_WIDTH), F32),
                   jax.ShapeDtypeStruct((b, GDN_HEADS, hd, hd), F32)],
        scratch_shapes=[pltpu.VMEM((8 + rows, QKV_A), F32)],
        compiler_params=_params(("parallel", "arbitrary")),
        name="gdn",
    )(qkv, z, ab, conv0, s0, conv_w, alog_pad, dtb_pad, onorm.reshape(1, hd))


def _rel_bucket(rel):
    nb = N_BUCKETS // 2
    max_exact = nb // 2
    ret = jnp.where(rel > 0, nb, 0)
    n = jnp.abs(rel)
    large = max_exact + (jnp.log(jnp.maximum(n, 1).astype(F32) / max_exact)
                         / math.log(REL_MAX_DIST / max_exact) * (nb - max_exact)).astype(jnp.int32)
    large = jnp.minimum(large, nb - 1)
    return ret + jnp.where(n < max_exact, n, large)


def _diff_finish(o1, o2, lam_ref, subln_ref, out_scale):
    o = o1 - lam_ref[...] * o2
    return o * lax.rsqrt(jnp.mean(o * o, axis=-1, keepdims=True) + EPS) * subln_ref[...] * out_scale


def _attn_prompt_kernel(q_ref, k_ref, v_ref, bias_ref, lam_ref, subln_ref, o_ref, m_ref, l_ref, acc_ref,
                        *, out_scale):
    i = pl.program_id(2)
    tb = ATT_BLOCK
    dh = DIFF_HEAD_DIM
    q = q_ref[0] * (dh ** -0.5)
    lane = lax.broadcasted_iota(jnp.int32, q.shape, 1)
    q2s = jnp.concatenate([jnp.where(lane < dh, q, 0.0), jnp.where(lane >= dh, q, 0.0)], axis=0).astype(BF16)

    def score_tiles(j, tile):
        start = pl.multiple_of(j * tb, tb)
        s = _dot_nt(q2s, k_ref[0, pl.ds(start, tb), :]) + bias_ref[0, tile]
        return [s[:, c:c + LANES] for c in range(0, tb, LANES)]

    def visible_blocks(fn, unroll):
        n_far = jnp.maximum(i - 1, 0)

        @pl.loop(0, n_far // unroll)
        def _(g):
            fn([(unroll * g + u, 0) for u in range(unroll)])

        @pl.loop((n_far // unroll) * unroll, n_far)
        def _(j):
            fn([(j, 0)])

        @pl.when(i > 0)
        def _():
            fn([(i - 1, 1), (i, 2)])

        @pl.when(i == 0)
        def _():
            fn([(i, 2)])

    m_ref[...] = jnp.full(m_ref.shape, NEG_BIG, F32)

    def track_max(blocks):
        tiles = [s for j, tile in blocks for s in score_tiles(j, tile)]
        m_ref[...] = functools.reduce(jnp.maximum, tiles, m_ref[...])

    visible_blocks(track_max, 4)
    m_ref[...] = jnp.broadcast_to(jnp.max(m_ref[...], axis=-1, keepdims=True), m_ref.shape)

    l_ref[...] = jnp.zeros(l_ref.shape, F32)
    acc_ref[...] = jnp.zeros(acc_ref.shape, F32)

    def accumulate(blocks):
        m = m_ref[...]
        l_add, acc_add = [], []
        for j, tile in blocks:
            p = [jnp.exp(s - m) for s in score_tiles(j, tile)]
            l_add.extend(p)
            start = pl.multiple_of(j * tb, tb)
            acc_add.append(_dot(jnp.concatenate(p, axis=-1).astype(BF16), v_ref[0, pl.ds(start, tb), :]))
        l_ref[...] += sum(l_add)
        acc_ref[...] += sum(acc_add)

    visible_blocks(accumulate, 4)
    o = acc_ref[...] / jnp.sum(l_ref[...], axis=-1, keepdims=True)
    o_ref[0] = _diff_finish(o[:tb], o[tb:], lam_ref, subln_ref, out_scale)


def _attn_prompt(qb, kh, vh, bias_tiles, lam_row, subln, out_scale):
    b, l, _ = qb.shape
    tb = ATT_BLOCK
    hw = 2 * DIFF_HEAD_DIM
    return pl.pallas_call(
        functools.partial(_attn_prompt_kernel, out_scale=out_scale),
        grid=(b, DIFF_HEADS, l // tb),
        in_specs=[pl.BlockSpec((1, tb, hw), lambda bi, h, i: (bi, i, h)),
                  pl.BlockSpec((1, l, hw), lambda bi, h, i: (bi, 0, h)),
                  pl.BlockSpec((1, l, DIFF_V_DIM), lambda bi, h, i: (bi, 0, h)),
                  pl.BlockSpec((1, 3, 2 * tb, tb), lambda bi, h, i: (h, 0, 0, 0)),
                  pl.BlockSpec((1, DIFF_V_DIM), lambda bi, h, i: (0, 0)),
                  pl.BlockSpec((1, DIFF_V_DIM), lambda bi, h, i: (0, 0))],
        out_specs=pl.BlockSpec((1, tb, DIFF_V_DIM), lambda bi, h, i: (bi, i, h)),
        out_shape=jax.ShapeDtypeStruct((b, l, DIFF_WIDTH), F32),
        scratch_shapes=[pltpu.VMEM((2 * tb, LANES), F32), pltpu.VMEM((2 * tb, LANES), F32),
                        pltpu.VMEM((2 * tb, DIFF_V_DIM), F32)],
        compiler_params=_params(("parallel", "parallel", "arbitrary")),
        name="attn_prompt",
    )(qb, kh, vh, bias_tiles, lam_row, subln.reshape(1, DIFF_V_DIM))


def _prompt_bias_tiles(rel_table):
    tb = ATT_BLOCK
    qi = jnp.arange(tb)[:, None]
    ki = jnp.arange(tb)[None, :]
    far = jnp.broadcast_to(rel_table[_rel_bucket(jnp.array(-2 * tb))], (tb, tb, DIFF_HEADS))
    prev = rel_table[_rel_bucket(ki - qi - tb)]
    diag = jnp.where(((ki // CHUNK) <= (qi // CHUNK))[..., None], rel_table[_rel_bucket(ki - qi)], NEG_BIG)
    tiles = jnp.moveaxis(jnp.stack([far, prev, diag]).astype(F32), -1, 0)
    return jnp.concatenate([tiles, tiles], axis=2)


def _attn_sample_kernel(q_ref, kp_ref, vp_ref, kn_ref, vn_ref, bp_ref, bn_ref, lam_ref, subln_ref, o_ref,
                        *, out_scale):
    dh = DIFF_HEAD_DIM
    q = q_ref[0] * (dh ** -0.5)
    kp = kp_ref[0].astype(BF16)
    kn = kn_ref[0].astype(BF16)
    vp = vp_ref[0].astype(BF16)
    vn = vn_ref[0].astype(BF16)
    outs = []
    for t in range(2):
        qt = q[:, t * dh:(t + 1) * dh].astype(BF16)
        sp = _dot_nt(qt, kp[:, t * dh:(t + 1) * dh]) + bp_ref[0]
        sn = _dot_nt(qt, kn[:, t * dh:(t + 1) * dh]) + bn_ref[0]
        m = jnp.maximum(jnp.max(sp, axis=-1, keepdims=True), jnp.max(sn, axis=-1, keepdims=True))
        pp = jnp.exp(sp - m)
        pn = jnp.exp(sn - m)
        den = jnp.sum(pp, axis=-1, keepdims=True) + jnp.sum(pn, axis=-1, keepdims=True)
        outs.append((_dot(pp.astype(BF16), vp) + _dot(pn.astype(BF16), vn)) / den)
    o_ref[0] = _diff_finish(outs[0], outs[1], lam_ref, subln_ref, out_scale)


def _attn_sample(qb, k_past, v_past, k_new, v_new, bias_past, bias_new, lam_row, subln, out_scale):
    b, l, _ = qb.shape
    p = k_past.shape[1]
    hw = 2 * DIFF_HEAD_DIM
    return pl.pallas_call(
        functools.partial(_attn_sample_kernel, out_scale=out_scale),
        grid=(b, DIFF_HEADS),
        in_specs=[pl.BlockSpec((1, l, hw), lambda bi, h: (bi, 0, h)),
                  pl.BlockSpec((1, p, hw), lambda bi, h: (bi, 0, h)),
                  pl.BlockSpec((1, p, DIFF_V_DIM), lambda bi, h: (bi, 0, h)),
                  pl.BlockSpec((1, l, hw), lambda bi, h: (bi, 0, h)),
                  pl.BlockSpec((1, l, DIFF_V_DIM), lambda bi, h: (bi, 0, h)),
                  pl.BlockSpec((1, l, p), lambda bi, h: (h, 0, 0)),
                  pl.BlockSpec((1, l, l), lambda bi, h: (h, 0, 0)),
                  pl.BlockSpec((1, DIFF_V_DIM), lambda bi, h: (0, 0)),
                  pl.BlockSpec((1, DIFF_V_DIM), lambda bi, h: (0, 0))],
        out_specs=pl.BlockSpec((1, l, DIFF_V_DIM), lambda bi, h: (bi, 0, h)),
        out_shape=jax.ShapeDtypeStruct((b, l, DIFF_WIDTH), F32),
        compiler_params=_params(("parallel", "parallel")),
        name="attn_sample",
    )(qb, k_past, v_past, k_new, v_new, bias_past, bias_new, lam_row, subln.reshape(1, DIFF_V_DIM))


def _sample_bias(rel_table, p, l):
    rel = jnp.arange(p + l)[None, :] - (p + jnp.arange(l))[:, None]
    bias = jnp.moveaxis(rel_table[_rel_bucket(rel)].astype(F32), -1, 0)
    return bias[:, :, :p], bias[:, :, p:]


def _outproj_kernel(oa_ref, ob_ref, x_ref, mod_ref, n2_ref, wo_ref, wq_ref, keys_ref,
                    x1_ref, h2_ref, sc_ref):
    mixed = jnp.concatenate([oa_ref[0], ob_ref[0]], axis=-1).astype(BF16)
    x1 = x_ref[0] + mod_ref[0, 2:3, :] * _dot(mixed, wo_ref[...])
    x1_ref[0] = x1
    h2 = _modulated_norm(x1, n2_ref[...], mod_ref[0, 3:4, :], mod_ref[0, 4:5, :])
    h2_ref[0] = h2
    qh = _dot(h2.astype(BF16), wq_ref[...]).astype(BF16)
    for hp in range(2 * PEER_HEADS):
        sc_ref[0, hp] = _dot_nt(keys_ref[hp], qh[:, hp * PEER_HALF:(hp + 1) * PEER_HALF])


def _outproj(o_a, o_b, x, mod, norm2, w_out, w_q, keys):
    b, l, d = x.shape
    tl = min(l, 256)
    nhp = 2 * PEER_HEADS
    return pl.pallas_call(
        _outproj_kernel,
        grid=(b, l // tl),
        in_specs=[pl.BlockSpec((1, tl, GDN_WIDTH), lambda i, j: (i, j, 0)),
                  pl.BlockSpec((1, tl, DIFF_WIDTH), lambda i, j: (i, j, 0)),
                  pl.BlockSpec((1, tl, d), lambda i, j: (i, j, 0)),
                  pl.BlockSpec((1, 6, d), lambda i, j: (i, 0, 0)),
                  pl.BlockSpec((1, d), lambda i, j: (0, 0)),
                  pl.BlockSpec((d, d), lambda i, j: (0, 0)),
                  pl.BlockSpec((d, nhp * PEER_HALF), lambda i, j: (0, 0)),
                  pl.BlockSpec((nhp, N_KEYS, PEER_HALF), lambda i, j: (0, 0, 0))],
        out_specs=[pl.BlockSpec((1, tl, d), lambda i, j: (i, j, 0)),
                   pl.BlockSpec((1, tl, d), lambda i, j: (i, j, 0)),
                   pl.BlockSpec((1, nhp, N_KEYS, tl), lambda i, j: (i, 0, 0, j))],
        out_shape=[jax.ShapeDtypeStruct((b, l, d), F32),
                   jax.ShapeDtypeStruct((b, l, d), F32),
                   jax.ShapeDtypeStruct((b, nhp, N_KEYS, l), F32)],
        compiler_params=_params(("parallel", "parallel")),
        name="outproj",
    )(o_a, o_b, x, mod, norm2.reshape(1, d), w_out, w_q, keys)


def _top16_rows(s, n):
    row = lax.broadcasted_iota(jnp.int32, s.shape, 0)
    vals, idxs = [], []
    for _ in range(PEER_TOPK):
        m = jnp.max(s, axis=0, keepdims=True)
        i = jnp.min(jnp.where(s == m, row, n), axis=0, keepdims=True)
        vals.append(m)
        idxs.append(i)
        s = jnp.where(row == i, -jnp.inf, s)
    return jnp.concatenate(vals, axis=0), jnp.concatenate(idxs, axis=0)


def _pick_rows(table, sel):
    out = jnp.zeros_like(table)
    for a in range(PEER_TOPK):
        out = jnp.where(sel == a, table[a:a + 1, :], out)
    return out


def _topk_kernel(sc_ref, eidx_ref, gate_ref):
    eidx, gates = [], []
    for h in range(PEER_HEADS):
        s1, i1 = _top16_rows(sc_ref[0, 2 * h], N_KEYS)
        s2, i2 = _top16_rows(sc_ref[0, 2 * h + 1], N_KEYS)
        cand = jnp.concatenate([s1[a:a + 1, :] + s2 for a in range(PEER_TOPK)], axis=0)
        top_s, pos = _top16_rows(cand, PEER_TOPK * PEER_TOPK)
        eidx.append(_pick_rows(i1, pos // PEER_TOPK) * N_KEYS + _pick_rows(i2, pos % PEER_TOPK))
        e = jnp.exp(top_s - top_s[0:1, :])
        gates.append(e / jnp.sum(e, axis=0, keepdims=True))
    eidx_ref[...] = jnp.concatenate(eidx, axis=0).T
    gate_ref[...] = jnp.concatenate(gates, axis=0).T


def _topk(scores):
    b, nhp, nk, l = scores.shape
    tt = min(l, LANES)
    nt = l // tt
    return pl.pallas_call(
        _topk_kernel,
        grid=(b, nt),
        in_specs=[pl.BlockSpec((1, nhp, nk, tt), lambda i, j: (i, 0, 0, j))],
        out_specs=[pl.BlockSpec((tt, PEER_SLOTS), lambda i, j: (i * nt + j, 0)),
                   pl.BlockSpec((tt, PEER_SLOTS), lambda i, j: (i * nt + j, 0))],
        out_shape=[jax.ShapeDtypeStruct((b * l, PEER_SLOTS), jnp.int32),
                   jax.ShapeDtypeStruct((b * l, PEER_SLOTS), F32)],
        compiler_params=_params(("parallel", "parallel")),
        name="topk",
    )(scores)


_SC_ROWS = 32


def _pack_bf16_halves(t):
    half = t.shape[1] // 2
    tb = lax.bitcast_convert_type(t.astype(BF16), jnp.uint16).astype(jnp.uint32)
    return lax.bitcast_convert_type(tb[:, :half] | (tb[:, half:] << 16), jnp.int32)


def _sc_gather2(tab_u, tab_v, idx):
    n = idx.shape[0]
    w = tab_u.shape[1]
    info = plsc.get_sparse_core_info()
    nw = info.num_cores * info.num_subcores
    per_w = n // nw
    nch = per_w // _SC_ROWS
    assert n % (nw * _SC_ROWS * 2) == 0
    mesh = plsc.VectorSubcoreMesh(core_axis_name="c", subcore_axis_name="s")
    rows = pltpu.VMEM((_SC_ROWS, w), tab_u.dtype)
    out = jax.ShapeDtypeStruct((n, w), tab_u.dtype)

    @functools.partial(
        pl.kernel, mesh=mesh, out_type=[out, out],
        scratch_types=[pltpu.VMEM((_SC_ROWS,), jnp.int32)] * 2 + [rows] * 4 + [pltpu.SemaphoreType.DMA] * 8,
    )
    def k(u_hbm, v_hbm, idx_hbm, uo_hbm, vo_hbm, i0, i1, u0, u1, v0, v1, gu0, gu1, gv0, gv1, wu0, wu1, wv0, wv1):
        idx_v, ub, vb = (i0, i1), (u0, u1), (v0, v1)
        gsu, gsv, wsu, wsv = (gu0, gu1), (gv0, gv1), (wu0, wu1), (wv0, wv1)
        wid = lax.axis_index("s") * info.num_cores + lax.axis_index("c")
        base = wid * per_w

        def span(c):
            return pl.ds(pl.multiple_of(base + c * _SC_ROWS, _SC_ROWS), _SC_ROWS)

        def gathers(s):
            return (pltpu.make_async_copy(u_hbm.at[idx_v[s]], ub[s], gsu[s]),
                    pltpu.make_async_copy(v_hbm.at[idx_v[s]], vb[s], gsv[s]))

        def writes(c, s):
            return (pltpu.make_async_copy(ub[s], uo_hbm.at[span(c)], wsu[s]),
                    pltpu.make_async_copy(vb[s], vo_hbm.at[span(c)], wsv[s]))

        def start(copies):
            for cp in copies:
                cp.start()

        def wait(copies):
            for cp in copies:
                cp.wait()

        def fetch(c, s):
            pltpu.sync_copy(idx_hbm.at[span(c)], idx_v[s])
            start(gathers(s))

        fetch(0, 0)

        @pl.loop(0, nch // 2)
        def _(kk):
            c0 = 2 * kk
            wait(gathers(0))
            start(writes(c0, 0))

            @pl.when(kk > 0)
            def _():
                wait(writes(c0 - 1, 1))

            fetch(c0 + 1, 1)
            wait(gathers(1))
            start(writes(c0 + 1, 1))
            wait(writes(c0, 0))

            @pl.when(kk < nch // 2 - 1)
            def _():
                fetch(c0 + 2, 0)

        wait(writes(nch - 1, 1))

    return k(tab_u, tab_v, idx)


_PEER_TOKENS = 16


def _unpack_bf16_halves(w):
    lo = lax.bitcast_convert_type(w << 16, F32)
    hi = lax.bitcast_convert_type(w & jnp.int32(-65536), F32)
    return lo, hi


def _peer_kernel(ug_ref, vg_ref, gate_ref, h2_ref, x1_ref, g2_ref, fn_ref, y_ref):
    half = D_MODEL // 2
    gate_t = gate_ref[...].T
    rows = []
    for t in range(_PEER_TOKENS):
        u_lo, u_hi = _unpack_bf16_halves(ug_ref[t * PEER_SLOTS:(t + 1) * PEER_SLOTS, :])
        pre = jnp.sum(u_lo * h2_ref[t:t + 1, :half] + u_hi * h2_ref[t:t + 1, half:], axis=-1, keepdims=True)
        act = 0.5 * pre * (1.0 + lax.erf(pre * (2.0 ** -0.5)))
        coef = gate_t[:, t:t + 1] * act
        v_lo, v_hi = _unpack_bf16_halves(vg_ref[t * PEER_SLOTS:(t + 1) * PEER_SLOTS, :])
        rows.append(jnp.concatenate([jnp.sum(coef * v_lo, axis=0, keepdims=True),
                                     jnp.sum(coef * v_hi, axis=0, keepdims=True)], axis=-1))
    x2 = x1_ref[...] + g2_ref[0, 5:6, :] * jnp.concatenate(rows, axis=0)
    y_ref[...] = x2 * lax.rsqrt(jnp.mean(x2 * x2, axis=-1, keepdims=True) + EPS) * fn_ref[...]


def _peer_combine(ug, vg, gate, h2, x1, mod, final_norm):
    b, l, d = h2.shape
    tp = _PEER_TOKENS
    steps_per_row = l // tp
    return pl.pallas_call(
        _peer_kernel,
        grid=(b * steps_per_row,),
        in_specs=[pl.BlockSpec((tp * PEER_SLOTS, d // 2), lambda i: (i, 0)),
                  pl.BlockSpec((tp * PEER_SLOTS, d // 2), lambda i: (i, 0)),
                  pl.BlockSpec((tp, PEER_SLOTS), lambda i: (i, 0)),
                  pl.BlockSpec((tp, d), lambda i: (i, 0)),
                  pl.BlockSpec((tp, d), lambda i: (i, 0)),
                  pl.BlockSpec((1, 6, d), lambda i: (i // steps_per_row, 0, 0)),
                  pl.BlockSpec((1, d), lambda i: (0, 0))],
        out_specs=pl.BlockSpec((tp, d), lambda i: (i, 0)),
        out_shape=jax.ShapeDtypeStruct((b * l, d), F32),
        compiler_params=_params(("parallel",)),
        name="peer",
    )(ug, vg, gate, h2.reshape(b * l, d), x1.reshape(b * l, d), mod, final_norm.reshape(1, d)).reshape(b, l, d)


def _front(x, mod, conv0, s0, k_past, v_past, wts, prompt):
    b, l, d = x.shape
    qkv, z, ab, qb, kb, vb, kh, vh = _inproj(x, mod, wts["norm1"], wts["w_in"])
    chunk = CHUNK if prompt else l
    o_a, s_new = _gdn(qkv, z, ab, conv0, s0, wts["conv_w"], wts["alog"], wts["dtb"], wts["onorm"], chunk)
    conv_new = qkv[:, l - (CONV_W - 1):, :]
    if prompt:
        o_b = _attn_prompt(qb, kh, vh, wts["bias_prompt"], wts["lam"], wts["subln"], wts["out_scale"])
    else:
        p = k_past.shape[1]
        bias_past, bias_new = _sample_bias(wts["rel_table"], p, l)
        o_b = _attn_sample(qb, k_past.reshape(b, p, DIFF_WIDTH), v_past.reshape(b, p, DIFF_WIDTH), kb, vb,
                           bias_past, bias_new, wts["lam"], wts["subln"], wts["out_scale"])
    x1, h2, scores = _outproj(o_a, o_b, x, mod, wts["norm2"], wts["w_out"], wts["w_q"], wts["keys"])
    eidx, gate = _topk(scores)
    ug, vg = _sc_gather2(wts["peer_u"], wts["peer_v"], eidx.reshape(-1))
    return (ug, vg, gate, h2, x1, mod), (kb, vb, s_new, conv_new)


def _back(pending, wts):
    return _peer_combine(*pending, wts["final_norm"])


def _prompt_rows(x, mod, wts):
    b, l, d = x.shape
    assert b % 2 == 0 and b >= 4
    conv0 = jnp.zeros((1, CONV_W - 1, QKV_A), F32)
    s0 = jnp.zeros((1, GDN_HEADS, GDN_HEAD_DIM, GDN_HEAD_DIM), F32)

    def front(i):
        xi = lax.dynamic_slice_in_dim(x, i, 1, 0)
        mi = lax.dynamic_slice_in_dim(mod, i, 1, 0)
        return _front(xi, mi, conv0, s0, None, None, wts, True)

    def step(pend_a, j):
        pend_b, new_b = front(2 * j + 1)
        y_a = _back(pend_a, wts)
        pend_a2, new_a2 = front(2 * j + 2)
        y_b = _back(pend_b, wts)
        return pend_a2, (y_a, y_b, new_b, new_a2)

    pend_0, new_0 = front(0)
    pend_a, (y_a, y_b, new_b, new_a2) = lax.scan(step, pend_0, jnp.arange(b // 2 - 1))
    pend_b, new_last = front(b - 1)
    y_tail = [_back(pend_a, wts), _back(pend_b, wts)]

    def rows(first, pairs, last):
        inter = jnp.stack([pairs[0][:, 0], pairs[1][:, 0]], axis=1)
        inter = inter.reshape((-1,) + inter.shape[2:])
        return jnp.concatenate(first + [inter] + last, axis=0)

    y = rows([], (y_a, y_b), y_tail)
    new = [rows([new_0[t]], (new_b[t], new_a2[t]), [new_last[t]]) for t in range(4)]
    return y, new


def _cache_entries(new, b, l):
    kb, vb, s_new, conv_new = new
    return (kb.reshape(1, b, l, DIFF_HEADS, 2 * DIFF_HEAD_DIM), vb.reshape(1, b, l, DIFF_HEADS, DIFF_V_DIM),
            s_new[None], conv_new[None])


def kernel(x_prompt, x_sample, c_prompt, c_sample, cache_k, cache_v, state_gdn, state_conv, w_ada, b_ada,
           norm1, norm2, w_in, conv_w, a_log, dt_bias, gdn_onorm, lam_q1, lam_k1, lam_q2, lam_k2, diff_subln,
           w_out, peer_wq, peer_keys, peer_u, peer_v, rel_table, final_norm):
    assert w_ada.shape[0] == 1, "single-layer step"
    bp = x_prompt.shape[0]
    d = D_MODEL
    lam_init = 0.8 - 0.6 * math.exp(-0.3 * 0)
    lam = (jnp.exp(jnp.sum(lam_q1[0] * lam_k1[0])) - jnp.exp(jnp.sum(lam_q2[0] * lam_k2[0])) + lam_init)
    w = w_in[0]
    w_packed = jnp.concatenate(
        [w[:, :_C_AB], jnp.pad(w[:, 2048:2056], ((0, 0), (0, LANES - 2 * GDN_HEADS))), w[:, 2056:]],
        axis=1).astype(BF16)
    wts = dict(
        norm1=norm1[0], norm2=norm2[0], w_in=w_packed, conv_w=conv_w[0],
        alog=jnp.pad(a_log[0], (0, LANES - GDN_HEADS)).reshape(1, LANES),
        dtb=jnp.pad(dt_bias[0], (0, LANES - GDN_HEADS)).reshape(1, LANES),
        onorm=gdn_onorm[0], lam=jnp.full((1, DIFF_V_DIM), lam, F32), subln=diff_subln[0],
        out_scale=1.0 - lam_init, bias_prompt=_prompt_bias_tiles(rel_table), rel_table=rel_table,
        w_out=w_out[0].astype(BF16), w_q=peer_wq[0].astype(BF16),
        keys=peer_keys[0].reshape(2 * PEER_HEADS, N_KEYS, PEER_HALF).astype(BF16),
        peer_u=_pack_bf16_halves(peer_u[0]), peer_v=_pack_bf16_halves(peer_v[0]), final_norm=final_norm)

    mod = _ada(jnp.concatenate([c_prompt, c_sample], axis=0), w_ada[0], b_ada[0]).reshape(-1, 6, d)
    pend_s, new_s = _front(x_sample, mod[bp:], state_conv[0], state_gdn[0], cache_k[0], cache_v[0], wts, False)
    yp, new_p = _prompt_rows(x_prompt, mod[:bp], wts)
    ys = _back(pend_s, wts)
    kp, vp, sp, cp = _cache_entries(new_p, *x_prompt.shape[:2])
    ks, vs, ss, cs = _cache_entries(new_s, *x_sample.shape[:2])
    return yp, ys, kp, vp, sp, cp, ks, vs, ss, cs
```

```python
import functools
import math

import jax
import jax.numpy as jnp
from jax import lax
from jax.experimental import pallas as pl
from jax.experimental.pallas import tpu as pltpu
from jax.experimental.pallas import tpu_sc as plsc

F32 = jnp.float32
BF16 = jnp.bfloat16
EPS = 1e-6

D_MODEL = 1024
CHUNK = 64
GDN_HEADS = 4
GDN_HEAD_DIM = 128
GDN_WIDTH = GDN_HEADS * GDN_HEAD_DIM
CONV_W = 4
QKV_A = 3 * GDN_WIDTH
DIFF_HEADS = 4
DIFF_HEAD_DIM = 64
DIFF_V_DIM = 128
DIFF_WIDTH = DIFF_HEADS * 2 * DIFF_HEAD_DIM
ATT_BLOCK = 256
N_BUCKETS = 32
REL_MAX_DIST = 128
PEER_HEADS = 8
N_KEYS = 128
PEER_HALF = 128
PEER_TOPK = 16
PEER_SLOTS = PEER_HEADS * PEER_TOPK
LANES = 128
NEG_BIG = -1e30
VMEM_LIMIT = 56 * 1024 * 1024

_C_QKV, _C_Z, _C_AB, _C_QB, _C_KB, _C_VB = 0, 1536, 2048, 2176, 2688, 3200
_C_END = 3712


def _params(sem):
    return pltpu.CompilerParams(dimension_semantics=sem, vmem_limit_bytes=VMEM_LIMIT)


def _dot(a, b, precision=None):
    return jnp.dot(a, b, preferred_element_type=F32, precision=precision)


def _dot_nt(a, b, precision=None):
    return lax.dot_general(a, b, (((1,), (1,)), ((), ())), preferred_element_type=F32, precision=precision)


def _silu(x):
    return x * jax.nn.sigmoid(x)


def _ada_kernel(c_ref, w_ref, b_ref, o_ref):
    a = _silu(c_ref[...]).astype(BF16)
    o_ref[...] = _dot(a, w_ref[...].astype(BF16)) + b_ref[...]


def _ada(c, w_ada, b_ada):
    n, d = c.shape
    cols = w_ada.shape[1]
    tn = 1024
    return pl.pallas_call(
        _ada_kernel,
        grid=(cols // tn,),
        in_specs=[pl.BlockSpec((n, d), lambda j: (0, 0)),
                  pl.BlockSpec((d, tn), lambda j: (0, j)),
                  pl.BlockSpec((1, tn), lambda j: (0, j))],
        out_specs=pl.BlockSpec((n, tn), lambda j: (0, j)),
        out_shape=jax.ShapeDtypeStruct((n, cols), F32),
        compiler_params=_params(("parallel",)),
        name="ada",
    )(c, w_ada, b_ada.reshape(1, cols))


def _modulated_norm(x, gain, shift, scale):
    y = x * lax.rsqrt(jnp.mean(x * x, axis=-1, keepdims=True) + EPS)
    return (y * gain) * (1.0 + scale) + shift


def _inproj_kernel(x_ref, mod_ref, n1_ref, w_ref, qkv_ref, z_ref, ab_ref, qb_ref, kb_ref, vb_ref, kh_ref, vh_ref):
    h = _modulated_norm(x_ref[0], n1_ref[...], mod_ref[0, 0:1, :], mod_ref[0, 1:2, :]).astype(BF16)
    qkv_ref[0] = _dot(h, w_ref[:, _C_QKV:_C_Z])
    z_ref[0] = _dot(h, w_ref[:, _C_Z:_C_AB])
    ab_ref[0] = _dot(h, w_ref[:, _C_AB:_C_QB])
    qb_ref[0] = _dot(h, w_ref[:, _C_QB:_C_KB])
    kb = _dot(h, w_ref[:, _C_KB:_C_VB])
    vb = _dot(h, w_ref[:, _C_VB:_C_END])
    kb_ref[0] = kb
    vb_ref[0] = vb
    kh_ref[0] = kb.astype(BF16)
    vh_ref[0] = vb.astype(BF16)


def _inproj(x, mod, norm1, w_packed):
    b, l, d = x.shape
    tl = min(l, 256)
    widths = (QKV_A, GDN_WIDTH, LANES, DIFF_WIDTH, DIFF_WIDTH, DIFF_WIDTH, DIFF_WIDTH, DIFF_WIDTH)
    dtypes = (F32,) * 6 + (BF16,) * 2
    return pl.pallas_call(
        _inproj_kernel,
        grid=(b, l // tl),
        in_specs=[pl.BlockSpec((1, tl, d), lambda i, j: (i, j, 0)),
                  pl.BlockSpec((1, 6, d), lambda i, j: (i, 0, 0)),
                  pl.BlockSpec((1, d), lambda i, j: (0, 0)),
                  pl.BlockSpec((d, _C_END), lambda i, j: (0, 0))],
        out_specs=[pl.BlockSpec((1, tl, w), lambda i, j: (i, j, 0)) for w in widths],
        out_shape=[jax.ShapeDtypeStruct((b, l, w), dt) for w, dt in zip(widths, dtypes)],
        compiler_params=_params(("parallel", "parallel")),
        name="inproj",
    )(x, mod, norm1.reshape(1, d), w_packed)


_HI = lax.Precision.HIGHEST


def _unit_lower_inverses(mats, n):
    r = lax.broadcasted_iota(jnp.int32, (n, n), 0)
    c = lax.broadcasted_iota(jnp.int32, (n, n), 1)
    eye = (r == c).astype(F32)
    ad = [jnp.where((r // 8) == (c // 8), a, 0.0) for a in mats]
    a2 = [_dot(m, m, _HI) for m in ad]
    a4 = [_dot(m, m, _HI) for m in a2]
    xs = [eye - m for m in ad]
    xs = [x + _dot(x, m, _HI) for x, m in zip(xs, a2)]
    xs = [x + _dot(x, m, _HI) for x, m in zip(xs, a4)]
    bs = 8
    while bs < n:
        off = ((r // (2 * bs)) == (c // (2 * bs))) & ((r // bs) != (c // bs))
        ys = [_dot(jnp.where(off, a, 0.0), x, _HI) for a, x in zip(mats, xs)]
        xs = [x - _dot(x, y, _HI) for x, y in zip(xs, ys)]
        bs *= 2
    return xs


def _mm(a, b):
    return _dot(a.astype(BF16), b.astype(BF16))


def _mm_nt(a, b):
    return _dot_nt(a.astype(BF16), b.astype(BF16))


def _gdn_kernel(qkv_ref, z_ref, ab_ref, conv0_ref, s0_ref, cw_ref, alog_ref, dtb_ref, onorm_ref,
                o_ref, s_ref, xbuf, *, chunk, nch):
    hd = GDN_HEAD_DIM
    rows = chunk * nch

    @pl.when(pl.program_id(1) == 0)
    def _():
        xbuf[5:8, :] = conv0_ref[0]
        s_ref[0] = s0_ref[0]

    x = qkv_ref[0]
    xbuf[8:8 + rows, :] = x
    y = (xbuf[5:5 + rows, :] * cw_ref[0:1, :] + xbuf[6:6 + rows, :] * cw_ref[1:2, :]
         + xbuf[7:7 + rows, :] * cw_ref[2:3, :] + x * cw_ref[3:4, :])
    xbuf[5:8, :] = x[rows - 3:rows, :]
    y = _silu(y)

    ab = ab_ref[0]
    t = ab + dtb_ref[...]
    softplus = jnp.maximum(t, 0.0) + jnp.log(1.0 + jnp.exp(-jnp.abs(t)))
    g = -jnp.exp(alog_ref[...]) * softplus
    beta = jax.nn.sigmoid(ab)

    r = lax.broadcasted_iota(jnp.int32, (chunk, chunk), 0)
    c = lax.broadcasted_iota(jnp.int32, (chunk, chunk), 1)
    lower = r >= c
    tri = lower.astype(F32)

    heads = range(GDN_HEADS)
    pairs = [(ci, h) for ci in range(nch) for h in heads]
    rows_of = {ci: slice(ci * chunk, (ci + 1) * chunk) for ci in range(nch)}
    gc = {ci: _dot(tri, g[rows_of[ci]], _HI) for ci in range(nch)}
    gc_t = {ci: gc[ci].T for ci in range(nch)}
    q, k, vb, kb, decay, egc, g_last = {}, {}, {}, {}, {}, {}, {}
    for ci, h in pairs:
        sl = rows_of[ci]
        qh = y[sl, h * hd:(h + 1) * hd]
        kh = y[sl, GDN_WIDTH + h * hd:GDN_WIDTH + (h + 1) * hd]
        p = ci, h
        q[p] = qh * lax.rsqrt(jnp.sum(qh * qh, axis=-1, keepdims=True) + EPS) * (hd ** -0.5)
        k[p] = kh * lax.rsqrt(jnp.sum(kh * kh, axis=-1, keepdims=True) + EPS)
        gcol = gc[ci][:, h:h + 1]
        bcol = beta[sl, GDN_HEADS + h:GDN_HEADS + h + 1]
        decay[p] = jnp.exp(jnp.where(lower, gcol - gc_t[ci][h:h + 1, :], NEG_BIG))
        kb[p] = k[p] * bcol
        vb[p] = y[sl, 2 * GDN_WIDTH + h * hd:2 * GDN_WIDTH + (h + 1) * hd] * bcol
        egc[p] = jnp.exp(gcol)
        g_last[p] = gcol[chunk - 1:chunk, :]
    kk = {p: _mm_nt(kb[p], k[p]) for p in pairs}
    tinv = dict(zip(pairs, _unit_lower_inverses([jnp.where(r > c, kk[p] * decay[p], 0.0) for p in pairs], chunk)))
    u_v = {p: _dot(tinv[p], vb[p], _HI) for p in pairs}
    w = {p: _dot(tinv[p], kb[p] * egc[p], _HI) for p in pairs}
    qk = {p: _mm_nt(q[p], k[p]) * decay[p] for p in pairs}
    k_dec_t = {p: (k[p] * jnp.exp(g_last[p] - gc[p[0]][:, p[1]:p[1] + 1])).T for p in pairs}

    s = [s_ref[0, h] for h in heads]
    for ci in range(nch):
        ws = [_mm(w[ci, h], s[h]) for h in heads]
        qs = [_mm(q[ci, h] * egc[ci, h], s[h]) for h in heads]
        v_new = [u_v[ci, h] - ws[h] for h in heads]
        o = [qs[h] + _mm(qk[ci, h], v_new[h]) for h in heads]
        s = [s[h] * jnp.exp(g_last[ci, h]) + _mm(k_dec_t[ci, h], v_new[h]) for h in heads]
        for h in heads:
            oh = o[h] * lax.rsqrt(jnp.mean(o[h] * o[h], axis=-1, keepdims=True) + EPS) * onorm_ref[...]
            o_ref[0, rows_of[ci], h * hd:(h + 1) * hd] = oh * _silu(z_ref[0, rows_of[ci], h * hd:(h + 1) * hd])
    for h in heads:
        s_ref[0, h] = s[h]


def _gdn(qkv, z, ab, conv0, s0, conv_w, alog_pad, dtb_pad, onorm, chunk):
    b, l, _ = qkv.shape
    hd = GDN_HEAD_DIM
    nch = 2 if l % (2 * chunk) == 0 else 1
    rows = chunk * nch
    return pl.pallas_call(
        functools.partial(_gdn_kernel, chunk=chunk, nch=nch),
        grid=(b, l // rows),
        in_specs=[pl.BlockSpec((1, rows, QKV_A), lambda i, j: (i, j, 0)),
                  pl.BlockSpec((1, rows, GDN_WIDTH), lambda i, j: (i, j, 0)),
                  pl.BlockSpec((1, rows, LANES), lambda i, j: (i, j, 0)),
                  pl.BlockSpec((1, CONV_W - 1, QKV_A), lambda i, j: (i, 0, 0)),
                  pl.BlockSpec((1, GDN_HEADS, hd, hd), lambda i, j: (i, 0, 0, 0)),
                  pl.BlockSpec((CONV_W, QKV_A), lambda i, j: (0, 0)),
                  pl.BlockSpec((1, LANES), lambda i, j: (0, 0)),
                  pl.BlockSpec((1, LANES), lambda i, j: (0, 0)),
                  pl.BlockSpec((1, hd), lambda i, j: (0, 0))],
        out_specs=[pl.BlockSpec((1, rows, GDN_WIDTH), lambda i, j: (i, j, 0)),
                   pl.BlockSpec((1, GDN_HEADS, hd, hd), lambda i, j: (i, 0, 0, 0))],
        out_shape=[jax.ShapeDtypeStruct((b, l, GDN_WIDTH), F32),
                   jax.ShapeDtypeStruct((b, GDN_HEADS, hd, hd), F32)],
        scratch_shapes=[pltpu.VMEM((8 + rows, QKV_A), F32)],
        compiler_params=_params(("parallel", "arbitrary")),
        name="gdn",
    )(qkv, z, ab, conv0, s0, conv_w, alog_pad, dtb_pad, onorm.reshape(1, hd))


def _rel_bucket(rel):
    nb = N_BUCKETS // 2
    max_exact = nb // 2
    ret = jnp.where(rel > 0, nb, 0)
    n = jnp.abs(rel)
    large = max_exact + (jnp.log(jnp.maximum(n, 1).astype(F32) / max_exact)
                         / math.log(REL_MAX_DIST / max_exact) * (nb - max_exact)).astype(jnp.int32)
    large = jnp.minimum(large, nb - 1)
    return ret + jnp.where(n < max_exact, n, large)


def _diff_finish(o1, o2, lam_ref, subln_ref, out_scale):
    o = o1 - lam_ref[...] * o2
    return o * lax.rsqrt(jnp.mean(o * o, axis=-1, keepdims=True) + EPS) * subln_ref[...] * out_scale


def _attn_prompt_kernel(q_ref, k_ref, v_ref, bias_ref, lam_ref, subln_ref, o_ref, m_ref, l_ref, acc_ref,
                        *, out_scale):
    i = pl.program_id(2)
    tb = ATT_BLOCK
    dh = DIFF_HEAD_DIM
    q = q_ref[0] * (dh ** -0.5)
    lane = lax.broadcasted_iota(jnp.int32, q.shape, 1)
    q2s = jnp.concatenate([jnp.where(lane < dh, q, 0.0), jnp.where(lane >= dh, q, 0.0)], axis=0).astype(BF16)

    def score_tiles(j, tile):
        start = pl.multiple_of(j * tb, tb)
        s = _dot_nt(q2s, k_ref[0, pl.ds(start, tb), :]) + bias_ref[0, tile]
        return [s[:, c:c + LANES] for c in range(0, tb, LANES)]

    def visible_blocks(fn, unroll):
        n_far = jnp.maximum(i - 1, 0)

        @pl.loop(0, n_far // unroll)
        def _(g):
            fn([(unroll * g + u, 0) for u in range(unroll)])

        @pl.loop((n_far // unroll) * unroll, n_far)
        def _(j):
            fn([(j, 0)])

        @pl.when(i > 0)
        def _():
            fn([(i - 1, 1), (i, 2)])

        @pl.when(i == 0)
        def _():
            fn([(i, 2)])

    m_ref[...] = jnp.full(m_ref.shape, NEG_BIG, F32)

    def track_max(blocks):
        tiles = [s for j, tile in blocks for s in score_tiles(j, tile)]
        m_ref[...] = functools.reduce(jnp.maximum, tiles, m_ref[...])

    visible_blocks(track_max, 4)
    m_ref[...] = jnp.broadcast_to(jnp.max(m_ref[...], axis=-1, keepdims=True), m_ref.shape)

    l_ref[...] = jnp.zeros(l_ref.shape, F32)
    acc_ref[...] = jnp.zeros(acc_ref.shape, F32)

    def accumulate(blocks):
        m = m_ref[...]
        l_add, acc_add = [], []
        for j, tile in blocks:
            p = [jnp.exp(s - m) for s in score_tiles(j, tile)]
            l_add.extend(p)
            start = pl.multiple_of(j * tb, tb)
            acc_add.append(_dot(jnp.concatenate(p, axis=-1).astype(BF16), v_ref[0, pl.ds(start, tb), :]))
        l_ref[...] += sum(l_add)
        acc_ref[...] += sum(acc_add)

    visible_blocks(accumulate, 4)
    o = acc_ref[...] / jnp.sum(l_ref[...], axis=-1, keepdims=True)
    o_ref[0] = _diff_finish(o[:tb], o[tb:], lam_ref, subln_ref, out_scale)


def _attn_prompt(qb, kh, vh, bias_tiles, lam_row, subln, out_scale):
    b, l, _ = qb.shape
    tb = ATT_BLOCK
    hw = 2 * DIFF_HEAD_DIM
    return pl.pallas_call(
        functools.partial(_attn_prompt_kernel, out_scale=out_scale),
        grid=(b, DIFF_HEADS, l // tb),
        in_specs=[pl.BlockSpec((1, tb, hw), lambda bi, h, i: (bi, i, h)),
                  pl.BlockSpec((1, l, hw), lambda bi, h, i: (bi, 0, h)),
                  pl.BlockSpec((1, l, DIFF_V_DIM), lambda bi, h, i: (bi, 0, h)),
                  pl.BlockSpec((1, 3, 2 * tb, tb), lambda bi, h, i: (h, 0, 0, 0)),
                  pl.BlockSpec((1, DIFF_V_DIM), lambda bi, h, i: (0, 0)),
                  pl.BlockSpec((1, DIFF_V_DIM), lambda bi, h, i: (0, 0))],
        out_specs=pl.BlockSpec((1, tb, DIFF_V_DIM), lambda bi, h, i: (bi, i, h)),
        out_shape=jax.ShapeDtypeStruct((b, l, DIFF_WIDTH), F32),
        scratch_shapes=[pltpu.VMEM((2 * tb, LANES), F32), pltpu.VMEM((2 * tb, LANES), F32),
                        pltpu.VMEM((2 * tb, DIFF_V_DIM), F32)],
        compiler_params=_params(("parallel", "parallel", "arbitrary")),
        name="attn_prompt",
    )(qb, kh, vh, bias_tiles, lam_row, subln.reshape(1, DIFF_V_DIM))


def _prompt_bias_tiles(rel_table):
    tb = ATT_BLOCK
    qi = jnp.arange(tb)[:, None]
    ki = jnp.arange(tb)[None, :]
    far = jnp.broadcast_to(rel_table[_rel_bucket(jnp.array(-2 * tb))], (tb, tb, DIFF_HEADS))
    prev = rel_table[_rel_bucket(ki - qi - tb)]
    diag = jnp.where(((ki // CHUNK) <= (qi // CHUNK))[..., None], rel_table[_rel_bucket(ki - qi)], NEG_BIG)
    tiles = jnp.moveaxis(jnp.stack([far, prev, diag]).astype(F32), -1, 0)
    return jnp.concatenate([tiles, tiles], axis=2)


def _attn_sample_kernel(q_ref, kp_ref, vp_ref, kn_ref, vn_ref, bp_ref, bn_ref, lam_ref, subln_ref, o_ref,
                        *, out_scale):
    dh = DIFF_HEAD_DIM
    q = q_ref[0] * (dh ** -0.5)
    kp = kp_ref[0].astype(BF16)
    kn = kn_ref[0].astype(BF16)
    vp = vp_ref[0].astype(BF16)
    vn = vn_ref[0].astype(BF16)
    outs = []
    for t in range(2):
        qt = q[:, t * dh:(t + 1) * dh].astype(BF16)
        sp = _dot_nt(qt, kp[:, t * dh:(t + 1) * dh]) + bp_ref[0]
        sn = _dot_nt(qt, kn[:, t * dh:(t + 1) * dh]) + bn_ref[0]
        m = jnp.maximum(jnp.max(sp, axis=-1, keepdims=True), jnp.max(sn, axis=-1, keepdims=True))
        pp = jnp.exp(sp - m)
        pn = jnp.exp(sn - m)
        den = jnp.sum(pp, axis=-1, keepdims=True) + jnp.sum(pn, axis=-1, keepdims=True)
        outs.append((_dot(pp.astype(BF16), vp) + _dot(pn.astype(BF16), vn)) / den)
    o_ref[0] = _diff_finish(outs[0], outs[1], lam_ref, subln_ref, out_scale)


def _attn_sample(qb, k_past, v_past, k_new, v_new, bias_past, bias_new, lam_row, subln, out_scale):
    b, l, _ = qb.shape
    p = k_past.shape[1]
    hw = 2 * DIFF_HEAD_DIM
    return pl.pallas_call(
        functools.partial(_attn_sample_kernel, out_scale=out_scale),
        grid=(b, DIFF_HEADS),
        in_specs=[pl.BlockSpec((1, l, hw), lambda bi, h: (bi, 0, h)),
                  pl.BlockSpec((1, p, hw), lambda bi, h: (bi, 0, h)),
                  pl.BlockSpec((1, p, DIFF_V_DIM), lambda bi, h: (bi, 0, h)),
                  pl.BlockSpec((1, l, hw), lambda bi, h: (bi, 0, h)),
                  pl.BlockSpec((1, l, DIFF_V_DIM), lambda bi, h: (bi, 0, h)),
                  pl.BlockSpec((1, l, p), lambda bi, h: (h, 0, 0)),
                  pl.BlockSpec((1, l, l), lambda bi, h: (h, 0, 0)),
                  pl.BlockSpec((1, DIFF_V_DIM), lambda bi, h: (0, 0)),
                  pl.BlockSpec((1, DIFF_V_DIM), lambda bi, h: (0, 0))],
        out_specs=pl.BlockSpec((1, l, DIFF_V_DIM), lambda bi, h: (bi, 0, h)),
        out_shape=jax.ShapeDtypeStruct((b, l, DIFF_WIDTH), F32),
        compiler_params=_params(("parallel", "parallel")),
        name="attn_sample",
    )(qb, k_past, v_past, k_new, v_new, bias_past, bias_new, lam_row, subln.reshape(1, DIFF_V_DIM))


def _sample_bias(rel_table, p, l):
    rel = jnp.arange(p + l)[None, :] - (p + jnp.arange(l))[:, None]
    bias = jnp.moveaxis(rel_table[_rel_bucket(rel)].astype(F32), -1, 0)
    return bias[:, :, :p], bias[:, :, p:]


def _outproj_kernel(oa_ref, ob_ref, x_ref, mod_ref, n2_ref, wo_ref, wq_ref, keys_ref,
                    x1_ref, h2_ref, sc_ref):
    mixed = jnp.concatenate([oa_ref[0], ob_ref[0]], axis=-1).astype(BF16)
    x1 = x_ref[0] + mod_ref[0, 2:3, :] * _dot(mixed, wo_ref[...])
    x1_ref[0] = x1
    h2 = _modulated_norm(x1, n2_ref[...], mod_ref[0, 3:4, :], mod_ref[0, 4:5, :])
    h2_ref[0] = h2
    qh = _dot(h2.astype(BF16), wq_ref[...]).astype(BF16)
    for hp in range(2 * PEER_HEADS):
        sc_ref[0, hp] = _dot_nt(keys_ref[hp], qh[:, hp * PEER_HALF:(hp + 1) * PEER_HALF])


def _outproj(o_a, o_b, x, mod, norm2, w_out, w_q, keys):
    b, l, d = x.shape
    tl = min(l, 256)
    nhp = 2 * PEER_HEADS
    return pl.pallas_call(
        _outproj_kernel,
        grid=(b, l // tl),
        in_specs=[pl.BlockSpec((1, tl, GDN_WIDTH), lambda i, j: (i, j, 0)),
                  pl.BlockSpec((1, tl, DIFF_WIDTH), lambda i, j: (i, j, 0)),
                  pl.BlockSpec((1, tl, d), lambda i, j: (i, j, 0)),
                  pl.BlockSpec((1, 6, d), lambda i, j: (i, 0, 0)),
                  pl.BlockSpec((1, d), lambda i, j: (0, 0)),
                  pl.BlockSpec((d, d), lambda i, j: (0, 0)),
                  pl.BlockSpec((d, nhp * PEER_HALF), lambda i, j: (0, 0)),
                  pl.BlockSpec((nhp, N_KEYS, PEER_HALF), lambda i, j: (0, 0, 0))],
        out_specs=[pl.BlockSpec((1, tl, d), lambda i, j: (i, j, 0)),
                   pl.BlockSpec((1, tl, d), lambda i, j: (i, j, 0)),
                   pl.BlockSpec((1, nhp, N_KEYS, tl), lambda i, j: (i, 0, 0, j))],
        out_shape=[jax.ShapeDtypeStruct((b, l, d), F32),
                   jax.ShapeDtypeStruct((b, l, d), F32),
                   jax.ShapeDtypeStruct((b, nhp, N_KEYS, l), F32)],
        compiler_params=_params(("parallel", "parallel")),
        name="outproj",
    )(o_a, o_b, x, mod, norm2.reshape(1, d), w_out, w_q, keys)


def _top16_rows(s, n):
    row = lax.broadcasted_iota(jnp.int32, s.shape, 0)
    vals, idxs = [], []
    for _ in range(PEER_TOPK):
        m = jnp.max(s, axis=0, keepdims=True)
        i = jnp.min(jnp.where(s == m, row, n), axis=0, keepdims=True)
        vals.append(m)
        idxs.append(i)
        s = jnp.where(row == i, -jnp.inf, s)
    return jnp.concatenate(vals, axis=0), jnp.concatenate(idxs, axis=0)


def _pick_rows(table, sel):
    out = jnp.zeros_like(table)
    for a in range(PEER_TOPK):
        out = jnp.where(sel == a, table[a:a + 1, :], out)
    return out


def _topk_kernel(sc_ref, eidx_ref, gate_ref):
    eidx, gates = [], []
    for h in range(PEER_HEADS):
        s1, i1 = _top16_rows(sc_ref[0, 2 * h], N_KEYS)
        s2, i2 = _top16_rows(sc_ref[0, 2 * h + 1], N_KEYS)
        cand = jnp.concatenate([s1[a:a + 1, :] + s2 for a in range(PEER_TOPK)], axis=0)
        top_s, pos = _top16_rows(cand, PEER_TOPK * PEER_TOPK)
        eidx.append(_pick_rows(i1, pos // PEER_TOPK) * N_KEYS + _pick_rows(i2, pos % PEER_TOPK))
        e = jnp.exp(top_s - top_s[0:1, :])
        gates.append(e / jnp.sum(e, axis=0, keepdims=True))
    eidx_ref[...] = jnp.concatenate(eidx, axis=0).T
    gate_ref[...] = jnp.concatenate(gates, axis=0).T


def _topk(scores):
    b, nhp, nk, l = scores.shape
    tt = min(l, LANES)
    nt = l // tt
    return pl.pallas_call(
        _topk_kernel,
        grid=(b, nt),
        in_specs=[pl.BlockSpec((1, nhp, nk, tt), lambda i, j: (i, 0, 0, j))],
        out_specs=[pl.BlockSpec((tt, PEER_SLOTS), lambda i, j: (i * nt + j, 0)),
                   pl.BlockSpec((tt, PEER_SLOTS), lambda i, j: (i * nt + j, 0))],
        out_shape=[jax.ShapeDtypeStruct((b * l, PEER_SLOTS), jnp.int32),
                   jax.ShapeDtypeStruct((b * l, PEER_SLOTS), F32)],
        compiler_params=_params(("parallel", "parallel")),
        name="topk",
    )(scores)


_SC_ROWS = 32


def _pack_bf16_halves(t):
    half = t.shape[1] // 2
    tb = lax.bitcast_convert_type(t.astype(BF16), jnp.uint16).astype(jnp.uint32)
    return lax.bitcast_convert_type(tb[:, :half] | (tb[:, half:] << 16), jnp.int32)


def _sc_gather2(tab_u, tab_v, idx):
    n = idx.shape[0]
    w = tab_u.shape[1]
    info = plsc.get_sparse_core_info()
    nw = info.num_cores * info.num_subcores
    per_w = n // nw
    nch = per_w // _SC_ROWS
    assert n % (nw * _SC_ROWS * 2) == 0
    mesh = plsc.VectorSubcoreMesh(core_axis_name="c", subcore_axis_name="s")
    rows = pltpu.VMEM((_SC_ROWS, w), tab_u.dtype)
    out = jax.ShapeDtypeStruct((n, w), tab_u.dtype)

    @functools.partial(
        pl.kernel, mesh=mesh, out_type=[out, out],
        scratch_types=[pltpu.VMEM((_SC_ROWS,), jnp.int32)] * 2 + [rows] * 4 + [pltpu.SemaphoreType.DMA] * 8,
    )
    def k(u_hbm, v_hbm, idx_hbm, uo_hbm, vo_hbm, i0, i1, u0, u1, v0, v1, gu0, gu1, gv0, gv1, wu0, wu1, wv0, wv1):
        idx_v, ub, vb = (i0, i1), (u0, u1), (v0, v1)
        gsu, gsv, wsu, wsv = (gu0, gu1), (gv0, gv1), (wu0, wu1), (wv0, wv1)
        wid = lax.axis_index("s") * info.num_cores + lax.axis_index("c")
        base = wid * per_w

        def span(c):
            return pl.ds(pl.multiple_of(base + c * _SC_ROWS, _SC_ROWS), _SC_ROWS)

        def gathers(s):
            return (pltpu.make_async_copy(u_hbm.at[idx_v[s]], ub[s], gsu[s]),
                    pltpu.make_async_copy(v_hbm.at[idx_v[s]], vb[s], gsv[s]))

        def writes(c, s):
            return (pltpu.make_async_copy(ub[s], uo_hbm.at[span(c)], wsu[s]),
                    pltpu.make_async_copy(vb[s], vo_hbm.at[span(c)], wsv[s]))

        def start(copies):
            for cp in copies:
                cp.start()

        def wait(copies):
            for cp in copies:
                cp.wait()

        def fetch(c, s):
            pltpu.sync_copy(idx_hbm.at[span(c)], idx_v[s])
            start(gathers(s))

        fetch(0, 0)

        @pl.loop(0, nch // 2)
        def _(kk):
            c0 = 2 * kk
            wait(gathers(0))
            start(writes(c0, 0))

            @pl.when(kk > 0)
            def _():
                wait(writes(c0 - 1, 1))

            fetch(c0 + 1, 1)
            wait(gathers(1))
            start(writes(c0 + 1, 1))
            wait(writes(c0, 0))

            @pl.when(kk < nch // 2 - 1)
            def _():
                fetch(c0 + 2, 0)

        wait(writes(nch - 1, 1))

    return k(tab_u, tab_v, idx)


_PEER_TOKENS = 16


def _unpack_bf16_halves(w):
    lo = lax.bitcast_convert_type(w << 16, F32)
    hi = lax.bitcast_convert_type(w & jnp.int32(-65536), F32)
    return lo, hi


def _peer_kernel(ug_ref, vg_ref, gate_ref, h2_ref, x1_ref, g2_ref, fn_ref, y_ref):
    half = D_MODEL // 2
    gate_t = gate_ref[...].T
    rows = []
    for t in range(_PEER_TOKENS):
        u_lo, u_hi = _unpack_bf16_halves(ug_ref[t * PEER_SLOTS:(t + 1) * PEER_SLOTS, :])
        pre = jnp.sum(u_lo * h2_ref[t:t + 1, :half] + u_hi * h2_ref[t:t + 1, half:], axis=-1, keepdims=True)
        act = 0.5 * pre * (1.0 + lax.erf(pre * (2.0 ** -0.5)))
        coef = gate_t[:, t:t + 1] * act
        v_lo, v_hi = _unpack_bf16_halves(vg_ref[t * PEER_SLOTS:(t + 1) * PEER_SLOTS, :])
        rows.append(jnp.concatenate([jnp.sum(coef * v_lo, axis=0, keepdims=True),
                                     jnp.sum(coef * v_hi, axis=0, keepdims=True)], axis=-1))
    x2 = x1_ref[...] + g2_ref[0, 5:6, :] * jnp.concatenate(rows, axis=0)
    y_ref[...] = x2 * lax.rsqrt(jnp.mean(x2 * x2, axis=-1, keepdims=True) + EPS) * fn_ref[...]


def _peer_combine(ug, vg, gate, h2, x1, mod, final_norm):
    b, l, d = h2.shape
    tp = _PEER_TOKENS
    steps_per_row = l // tp
    return pl.pallas_call(
        _peer_kernel,
        grid=(b * steps_per_row,),
        in_specs=[pl.BlockSpec((tp * PEER_SLOTS, d // 2), lambda i: (i, 0)),
                  pl.BlockSpec((tp * PEER_SLOTS, d // 2), lambda i: (i, 0)),
                  pl.BlockSpec((tp, PEER_SLOTS), lambda i: (i, 0)),
                  pl.BlockSpec((tp, d), lambda i: (i, 0)),
                  pl.BlockSpec((tp, d), lambda i: (i, 0)),
                  pl.BlockSpec((1, 6, d), lambda i: (i // steps_per_row, 0, 0)),
                  pl.BlockSpec((1, d), lambda i: (0, 0))],
        out_specs=pl.BlockSpec((tp, d), lambda i: (i, 0)),
        out_shape=jax.ShapeDtypeStruct((b * l, d), F32),
        compiler_params=_params(("parallel",)),
        name="peer",
    )(ug, vg, gate, h2.reshape(b * l, d), x1.reshape(b * l, d), mod, final_norm.reshape(1, d)).reshape(b, l, d)


def _front(x, mod, conv0, s0, k_past, v_past, wts, prompt):
    b, l, d = x.shape
    qkv, z, ab, qb, kb, vb, kh, vh = _inproj(x, mod, wts["norm1"], wts["w_in"])
    chunk = CHUNK if prompt else l
    o_a, s_new = _gdn(qkv, z, ab, conv0, s0, wts["conv_w"], wts["alog"], wts["dtb"], wts["onorm"], chunk)
    conv_new = qkv[:, l - (CONV_W - 1):, :]
    if prompt:
        o_b = _attn_prompt(qb, kh, vh, wts["bias_prompt"], wts["lam"], wts["subln"], wts["out_scale"])
    else:
        p = k_past.shape[1]
        bias_past, bias_new = _sample_bias(wts["rel_table"], p, l)
        o_b = _attn_sample(qb, k_past.reshape(b, p, DIFF_WIDTH), v_past.reshape(b, p, DIFF_WIDTH), kb, vb,
                           bias_past, bias_new, wts["lam"], wts["subln"], wts["out_scale"])
    x1, h2, scores = _outproj(o_a, o_b, x, mod, wts["norm2"], wts["w_out"], wts["w_q"], wts["keys"])
    eidx, gate = _topk(scores)
    ug, vg = _sc_gather2(wts["peer_u"], wts["peer_v"], eidx.reshape(-1))
    return (ug, vg, gate, h2, x1, mod), (kb, vb, s_new, conv_new)


def _back(pending, wts):
    return _peer_combine(*pending, wts["final_norm"])


def _prompt_rows(x, mod, wts):
    b, l, d = x.shape
    per = 3
    n_steps = (b - 2) // per
    assert n_steps >= 1
    conv0 = jnp.zeros((1, CONV_W - 1, QKV_A), F32)
    s0 = jnp.zeros((1, GDN_HEADS, GDN_HEAD_DIM, GDN_HEAD_DIM), F32)

    def front(i):
        xi = lax.dynamic_slice_in_dim(x, i, 1, 0)
        mi = lax.dynamic_slice_in_dim(mod, i, 1, 0)
        return _front(xi, mi, conv0, s0, None, None, wts, True)

    def step(carry, j):
        pend_a, pend_b = carry
        r = per * j
        pend_c, new_c = front(r + 2)
        y_a = _back(pend_a, wts)
        pend_a2, new_a2 = front(r + 3)
        y_b = _back(pend_b, wts)
        pend_b2, new_b2 = front(r + 4)
        y_c = _back(pend_c, wts)
        return (pend_a2, pend_b2), ((y_a, y_b, y_c), (new_c, new_a2, new_b2))

    pend_0, new_0 = front(0)
    pend_1, new_1 = front(1)
    pending, (y_steps, new_steps) = lax.scan(step, (pend_0, pend_1), jnp.arange(n_steps))
    pending, y_tail, new_tail = list(pending), [], []
    for i in range(2 + per * n_steps, b):
        pend, new = front(i)
        new_tail.append(new)
        y_tail.append(_back(pending.pop(0), wts))
        pending.append(pend)
    y_tail += [_back(pend, wts) for pend in pending]

    def rows(first, stacks, last):
        inter = jnp.stack([s[:, 0] for s in stacks], axis=1)
        inter = inter.reshape((-1,) + inter.shape[2:])
        return jnp.concatenate(first + [inter] + last, axis=0)

    y = rows([], y_steps, y_tail)
    new = [rows([new_0[t], new_1[t]], [n[t] for n in new_steps], [n[t] for n in new_tail]) for t in range(4)]
    return y, new


def _cache_entries(new, b, l):
    kb, vb, s_new, conv_new = new
    return (kb.reshape(1, b, l, DIFF_HEADS, 2 * DIFF_HEAD_DIM), vb.reshape(1, b, l, DIFF_HEADS, DIFF_V_DIM),
            s_new[None], conv_new[None])


def kernel(x_prompt, x_sample, c_prompt, c_sample, cache_k, cache_v, state_gdn, state_conv, w_ada, b_ada,
           norm1, norm2, w_in, conv_w, a_log, dt_bias, gdn_onorm, lam_q1, lam_k1, lam_q2, lam_k2, diff_subln,
           w_out, peer_wq, peer_keys, peer_u, peer_v, rel_table, final_norm):
    assert w_ada.shape[0] == 1, "single-layer step"
    bp = x_prompt.shape[0]
    d = D_MODEL
    lam_init = 0.8 - 0.6 * math.exp(-0.3 * 0)
    lam = (jnp.exp(jnp.sum(lam_q1[0] * lam_k1[0])) - jnp.exp(jnp.sum(lam_q2[0] * lam_k2[0])) + lam_init)
    w = w_in[0]
    w_packed = jnp.concatenate(
        [w[:, :_C_AB], jnp.pad(w[:, 2048:2056], ((0, 0), (0, LANES - 2 * GDN_HEADS))), w[:, 2056:]],
        axis=1).astype(BF16)
    wts = dict(
        norm1=norm1[0], norm2=norm2[0], w_in=w_packed, conv_w=conv_w[0],
        alog=jnp.pad(a_log[0], (0, LANES - GDN_HEADS)).reshape(1, LANES),
        dtb=jnp.pad(dt_bias[0], (0, LANES - GDN_HEADS)).reshape(1, LANES),
        onorm=gdn_onorm[0], lam=jnp.full((1, DIFF_V_DIM), lam, F32), subln=diff_subln[0],
        out_scale=1.0 - lam_init, bias_prompt=_prompt_bias_tiles(rel_table), rel_table=rel_table,
        w_out=w_out[0].astype(BF16), w_q=peer_wq[0].astype(BF16),
        keys=peer_keys[0].reshape(2 * PEER_HEADS, N_KEYS, PEER_HALF).astype(BF16),
        peer_u=_pack_bf16_halves(peer_u[0]), peer_v=_pack_bf16_halves(peer_v[0]), final_norm=final_norm)

    mod = _ada(jnp.concatenate([c_prompt, c_sample], axis=0), w_ada[0], b_ada[0]).reshape(-1, 6, d)
    pend_s, new_s = _front(x_sample, mod[bp:], state_conv[0], state_gdn[0], cache_k[0], cache_v[0], wts, False)
    yp, new_p = _prompt_rows(x_prompt, mod[:bp], wts)
    ys = _back(pend_s, wts)
    kp, vp, sp, cp = _cache_entries(new_p, *x_prompt.shape[:2])
    ks, vs, ss, cs = _cache_entries(new_s, *x_sample.shape[:2])
    return yp, ys, kp, vp, sp, cp, ks, vs, ss, cs
```

```python
import functools
import math

import jax
import jax.numpy as jnp
from jax import lax
from jax.experimental import pallas as pl
from jax.experimental.pallas import tpu as pltpu
from jax.experimental.pallas import tpu_sc as plsc

F32 = jnp.float32
BF16 = jnp.bfloat16
EPS = 1e-6

D_MODEL = 1024
CHUNK = 64
GDN_HEADS = 4
GDN_HEAD_DIM = 128
GDN_WIDTH = GDN_HEADS * GDN_HEAD_DIM
CONV_W = 4
QKV_A = 3 * GDN_WIDTH
DIFF_HEADS = 4
DIFF_HEAD_DIM = 64
DIFF_V_DIM = 128
DIFF_WIDTH = DIFF_HEADS * 2 * DIFF_HEAD_DIM
ATT_BLOCK = 256
N_BUCKETS = 32
REL_MAX_DIST = 128
PEER_HEADS = 8
N_KEYS = 128
PEER_HALF = 128
PEER_TOPK = 16
PEER_SLOTS = PEER_HEADS * PEER_TOPK
LANES = 128
NEG_BIG = -1e30
VMEM_LIMIT = 56 * 1024 * 1024

_C_QKV, _C_Z, _C_AB, _C_QB, _C_KB, _C_VB = 0, 1536, 2048, 2176, 2688, 3200
_C_END = 3712


def _params(sem):
    return pltpu.CompilerParams(dimension_semantics=sem, vmem_limit_bytes=VMEM_LIMIT)


def _dot(a, b, precision=None):
    return jnp.dot(a, b, preferred_element_type=F32, precision=precision)


def _dot_nt(a, b, precision=None):
    return lax.dot_general(a, b, (((1,), (1,)), ((), ())), preferred_element_type=F32, precision=precision)


def _silu(x):
    return x * jax.nn.sigmoid(x)


def _ada_kernel(c_ref, w_ref, b_ref, o_ref):
    a = _silu(c_ref[...]).astype(BF16)
    o_ref[...] = _dot(a, w_ref[...].astype(BF16)) + b_ref[...]


def _ada(c, w_ada, b_ada):
    n, d = c.shape
    cols = w_ada.shape[1]
    tn = 1024
    return pl.pallas_call(
        _ada_kernel,
        grid=(cols // tn,),
        in_specs=[pl.BlockSpec((n, d), lambda j: (0, 0)),
                  pl.BlockSpec((d, tn), lambda j: (0, j)),
                  pl.BlockSpec((1, tn), lambda j: (0, j))],
        out_specs=pl.BlockSpec((n, tn), lambda j: (0, j)),
        out_shape=jax.ShapeDtypeStruct((n, cols), F32),
        compiler_params=_params(("parallel",)),
        name="ada",
    )(c, w_ada, b_ada.reshape(1, cols))


def _modulated_norm(x, gain, shift, scale):
    y = x * lax.rsqrt(jnp.mean(x * x, axis=-1, keepdims=True) + EPS)
    return (y * gain) * (1.0 + scale) + shift


def _inproj_kernel(x_ref, mod_ref, n1_ref, w_ref, qkv_ref, z_ref, ab_ref, qb_ref, kb_ref, vb_ref, kh_ref, vh_ref):
    h = _modulated_norm(x_ref[0], n1_ref[...], mod_ref[0, 0:1, :], mod_ref[0, 1:2, :]).astype(BF16)
    qkv_ref[0] = _dot(h, w_ref[:, _C_QKV:_C_Z])
    z_ref[0] = _dot(h, w_ref[:, _C_Z:_C_AB])
    ab_ref[0] = _dot(h, w_ref[:, _C_AB:_C_QB])
    qb_ref[0] = _dot(h, w_ref[:, _C_QB:_C_KB])
    kb = _dot(h, w_ref[:, _C_KB:_C_VB])
    vb = _dot(h, w_ref[:, _C_VB:_C_END])
    kb_ref[0] = kb
    vb_ref[0] = vb
    kh_ref[0] = kb.astype(BF16)
    vh_ref[0] = vb.astype(BF16)


def _inproj(x, mod, norm1, w_packed):
    b, l, d = x.shape
    tl = min(l, 256)
    widths = (QKV_A, GDN_WIDTH, LANES, DIFF_WIDTH, DIFF_WIDTH, DIFF_WIDTH, DIFF_WIDTH, DIFF_WIDTH)
    dtypes = (F32,) * 6 + (BF16,) * 2
    return pl.pallas_call(
        _inproj_kernel,
        grid=(b, l // tl),
        in_specs=[pl.BlockSpec((1, tl, d), lambda i, j: (i, j, 0)),
                  pl.BlockSpec((1, 6, d), lambda i, j: (i, 0, 0)),
                  pl.BlockSpec((1, d), lambda i, j: (0, 0)),
                  pl.BlockSpec((d, _C_END), lambda i, j: (0, 0))],
        out_specs=[pl.BlockSpec((1, tl, w), lambda i, j: (i, j, 0)) for w in widths],
        out_shape=[jax.ShapeDtypeStruct((b, l, w), dt) for w, dt in zip(widths, dtypes)],
        compiler_params=_params(("parallel", "parallel")),
        name="inproj",
    )(x, mod, norm1.reshape(1, d), w_packed)


_HI = lax.Precision.HIGHEST


def _unit_lower_inverses(mats, n):
    r = lax.broadcasted_iota(jnp.int32, (n, n), 0)
    c = lax.broadcasted_iota(jnp.int32, (n, n), 1)
    eye = (r == c).astype(F32)
    ad = [jnp.where((r // 8) == (c // 8), a, 0.0) for a in mats]
    a2 = [_dot(m, m, _HI) for m in ad]
    a4 = [_dot(m, m, _HI) for m in a2]
    xs = [eye - m for m in ad]
    xs = [x + _dot(x, m, _HI) for x, m in zip(xs, a2)]
    xs = [x + _dot(x, m, _HI) for x, m in zip(xs, a4)]
    bs = 8
    while bs < n:
        off = ((r // (2 * bs)) == (c // (2 * bs))) & ((r // bs) != (c // bs))
        ys = [_dot(jnp.where(off, a, 0.0), x, _HI) for a, x in zip(mats, xs)]
        xs = [x - _dot(x, y, _HI) for x, y in zip(xs, ys)]
        bs *= 2
    return xs


def _mm(a, b):
    return _dot(a.astype(BF16), b.astype(BF16))


def _mm_nt(a, b):
    return _dot_nt(a.astype(BF16), b.astype(BF16))


def _gdn_kernel(qkv_ref, z_ref, ab_ref, conv0_ref, s0_ref, cw_ref, alog_ref, dtb_ref, onorm_ref,
                o_ref, s_ref, xbuf, *, chunk, nch):
    hd = GDN_HEAD_DIM
    rows = chunk * nch

    @pl.when(pl.program_id(1) == 0)
    def _():
        xbuf[5:8, :] = conv0_ref[0]
        s_ref[0] = s0_ref[0]

    x = qkv_ref[0]
    xbuf[8:8 + rows, :] = x
    y = (xbuf[5:5 + rows, :] * cw_ref[0:1, :] + xbuf[6:6 + rows, :] * cw_ref[1:2, :]
         + xbuf[7:7 + rows, :] * cw_ref[2:3, :] + x * cw_ref[3:4, :])
    xbuf[5:8, :] = x[rows - 3:rows, :]
    y = _silu(y)

    ab = ab_ref[0]
    t = ab + dtb_ref[...]
    softplus = jnp.maximum(t, 0.0) + jnp.log(1.0 + jnp.exp(-jnp.abs(t)))
    g = -jnp.exp(alog_ref[...]) * softplus
    beta = jax.nn.sigmoid(ab)

    r = lax.broadcasted_iota(jnp.int32, (chunk, chunk), 0)
    c = lax.broadcasted_iota(jnp.int32, (chunk, chunk), 1)
    lower = r >= c
    tri = lower.astype(F32)

    heads = range(GDN_HEADS)
    pairs = [(ci, h) for ci in range(nch) for h in heads]
    rows_of = {ci: slice(ci * chunk, (ci + 1) * chunk) for ci in range(nch)}
    gc = {ci: _dot(tri, g[rows_of[ci]], _HI) for ci in range(nch)}
    gc_t = {ci: gc[ci].T for ci in range(nch)}
    q, k, vb, kb, decay, egc, g_last = {}, {}, {}, {}, {}, {}, {}
    for ci, h in pairs:
        sl = rows_of[ci]
        qh = y[sl, h * hd:(h + 1) * hd]
        kh = y[sl, GDN_WIDTH + h * hd:GDN_WIDTH + (h + 1) * hd]
        p = ci, h
        q[p] = qh * lax.rsqrt(jnp.sum(qh * qh, axis=-1, keepdims=True) + EPS) * (hd ** -0.5)
        k[p] = kh * lax.rsqrt(jnp.sum(kh * kh, axis=-1, keepdims=True) + EPS)
        gcol = gc[ci][:, h:h + 1]
        bcol = beta[sl, GDN_HEADS + h:GDN_HEADS + h + 1]
        decay[p] = jnp.exp(jnp.where(lower, gcol - gc_t[ci][h:h + 1, :], NEG_BIG))
        kb[p] = k[p] * bcol
        vb[p] = y[sl, 2 * GDN_WIDTH + h * hd:2 * GDN_WIDTH + (h + 1) * hd] * bcol
        egc[p] = jnp.exp(gcol)
        g_last[p] = gcol[chunk - 1:chunk, :]
    kk = {p: _mm_nt(kb[p], k[p]) for p in pairs}
    tinv = dict(zip(pairs, _unit_lower_inverses([jnp.where(r > c, kk[p] * decay[p], 0.0) for p in pairs], chunk)))
    u_v = {p: _dot(tinv[p], vb[p], _HI) for p in pairs}
    w = {p: _dot(tinv[p], kb[p] * egc[p], _HI) for p in pairs}
    qk = {p: _mm_nt(q[p], k[p]) * decay[p] for p in pairs}
    k_dec_t = {p: (k[p] * jnp.exp(g_last[p] - gc[p[0]][:, p[1]:p[1] + 1])).T for p in pairs}

    s = [s_ref[0, h] for h in heads]
    for ci in range(nch):
        ws = [_mm(w[ci, h], s[h]) for h in heads]
        qs = [_mm(q[ci, h] * egc[ci, h], s[h]) for h in heads]
        v_new = [u_v[ci, h] - ws[h] for h in heads]
        o = [qs[h] + _mm(qk[ci, h], v_new[h]) for h in heads]
        s = [s[h] * jnp.exp(g_last[ci, h]) + _mm(k_dec_t[ci, h], v_new[h]) for h in heads]
        for h in heads:
            oh = o[h] * lax.rsqrt(jnp.mean(o[h] * o[h], axis=-1, keepdims=True) + EPS) * onorm_ref[...]
            o_ref[0, rows_of[ci], h * hd:(h + 1) * hd] = oh * _silu(z_ref[0, rows_of[ci], h * hd:(h + 1) * hd])
    for h in heads:
        s_ref[0, h] = s[h]


def _gdn(qkv, z, ab, conv0, s0, conv_w, alog_pad, dtb_pad, onorm, chunk):
    b, l, _ = qkv.shape
    hd = GDN_HEAD_DIM
    nch = 2 if l % (2 * chunk) == 0 else 1
    rows = chunk * nch
    return pl.pallas_call(
        functools.partial(_gdn_kernel, chunk=chunk, nch=nch),
        grid=(b, l // rows),
        in_specs=[pl.BlockSpec((1, rows, QKV_A), lambda i, j: (i, j, 0)),
                  pl.BlockSpec((1, rows, GDN_WIDTH), lambda i, j: (i, j, 0)),
                  pl.BlockSpec((1, rows, LANES), lambda i, j: (i, j, 0)),
                  pl.BlockSpec((1, CONV_W - 1, QKV_A), lambda i, j: (i, 0, 0)),
                  pl.BlockSpec((1, GDN_HEADS, hd, hd), lambda i, j: (i, 0, 0, 0)),
                  pl.BlockSpec((CONV_W, QKV_A), lambda i, j: (0, 0)),
                  pl.BlockSpec((1, LANES), lambda i, j: (0, 0)),
                  pl.BlockSpec((1, LANES), lambda i, j: (0, 0)),
                  pl.BlockSpec((1, hd), lambda i, j: (0, 0))],
        out_specs=[pl.BlockSpec((1, rows, GDN_WIDTH), lambda i, j: (i, j, 0)),
                   pl.BlockSpec((1, GDN_HEADS, hd, hd), lambda i, j: (i, 0, 0, 0))],
        out_shape=[jax.ShapeDtypeStruct((b, l, GDN_WIDTH), F32),
                   jax.ShapeDtypeStruct((b, GDN_HEADS, hd, hd), F32)],
        scratch_shapes=[pltpu.VMEM((8 + rows, QKV_A), F32)],
        compiler_params=_params(("parallel", "arbitrary")),
        name="gdn",
    )(qkv, z, ab, conv0, s0, conv_w, alog_pad, dtb_pad, onorm.reshape(1, hd))


def _rel_bucket(rel):
    nb = N_BUCKETS // 2
    max_exact = nb // 2
    ret = jnp.where(rel > 0, nb, 0)
    n = jnp.abs(rel)
    large = max_exact + (jnp.log(jnp.maximum(n, 1).astype(F32) / max_exact)
                         / math.log(REL_MAX_DIST / max_exact) * (nb - max_exact)).astype(jnp.int32)
    large = jnp.minimum(large, nb - 1)
    return ret + jnp.where(n < max_exact, n, large)


def _diff_finish(o1, o2, lam_ref, subln_ref, out_scale):
    o = o1 - lam_ref[...] * o2
    return o * lax.rsqrt(jnp.mean(o * o, axis=-1, keepdims=True) + EPS) * subln_ref[...] * out_scale


def _attn_prompt_kernel(q_ref, k_ref, v_ref, bias_ref, lam_ref, subln_ref, o_ref, m_ref, l_ref, acc_ref,
                        *, out_scale):
    i = pl.program_id(2)
    tb = ATT_BLOCK
    dh = DIFF_HEAD_DIM
    q = q_ref[0] * (dh ** -0.5)
    lane = lax.broadcasted_iota(jnp.int32, q.shape, 1)
    q2s = jnp.concatenate([jnp.where(lane < dh, q, 0.0), jnp.where(lane >= dh, q, 0.0)], axis=0).astype(BF16)

    def score_tiles(j, tile):
        start = pl.multiple_of(j * tb, tb)
        s = _dot_nt(q2s, k_ref[0, pl.ds(start, tb), :]) + bias_ref[0, tile]
        return [s[:, c:c + LANES] for c in range(0, tb, LANES)]

    def visible_blocks(fn, unroll):
        n_far = jnp.maximum(i - 1, 0)

        @pl.loop(0, n_far // unroll)
        def _(g):
            fn([(unroll * g + u, 0) for u in range(unroll)])

        @pl.loop((n_far // unroll) * unroll, n_far)
        def _(j):
            fn([(j, 0)])

        @pl.when(i > 0)
        def _():
            fn([(i - 1, 1), (i, 2)])

        @pl.when(i == 0)
        def _():
            fn([(i, 2)])

    m_ref[...] = jnp.full(m_ref.shape, NEG_BIG, F32)

    def track_max(blocks):
        tiles = [s for j, tile in blocks for s in score_tiles(j, tile)]
        m_ref[...] = functools.reduce(jnp.maximum, tiles, m_ref[...])

    visible_blocks(track_max, 4)
    m_ref[...] = jnp.broadcast_to(jnp.max(m_ref[...], axis=-1, keepdims=True), m_ref.shape)

    l_ref[...] = jnp.zeros(l_ref.shape, F32)
    acc_ref[...] = jnp.zeros(acc_ref.shape, F32)

    def accumulate(blocks):
        m = m_ref[...]
        l_add, acc_add = [], []
        for j, tile in blocks:
            p = [jnp.exp(s - m) for s in score_tiles(j, tile)]
            l_add.extend(p)
            start = pl.multiple_of(j * tb, tb)
            acc_add.append(_dot(jnp.concatenate(p, axis=-1).astype(BF16), v_ref[0, pl.ds(start, tb), :]))
        l_ref[...] += sum(l_add)
        acc_ref[...] += sum(acc_add)

    visible_blocks(accumulate, 4)
    o = acc_ref[...] / jnp.sum(l_ref[...], axis=-1, keepdims=True)
    o_ref[0] = _diff_finish(o[:tb], o[tb:], lam_ref, subln_ref, out_scale)


def _attn_prompt(qb, kh, vh, bias_tiles, lam_row, subln, out_scale):
    b, l, _ = qb.shape
    tb = ATT_BLOCK
    hw = 2 * DIFF_HEAD_DIM
    return pl.pallas_call(
        functools.partial(_attn_prompt_kernel, out_scale=out_scale),
        grid=(b, DIFF_HEADS, l // tb),
        in_specs=[pl.BlockSpec((1, tb, hw), lambda bi, h, i: (bi, i, h)),
                  pl.BlockSpec((1, l, hw), lambda bi, h, i: (bi, 0, h)),
                  pl.BlockSpec((1, l, DIFF_V_DIM), lambda bi, h, i: (bi, 0, h)),
                  pl.BlockSpec((1, 3, 2 * tb, tb), lambda bi, h, i: (h, 0, 0, 0)),
                  pl.BlockSpec((1, DIFF_V_DIM), lambda bi, h, i: (0, 0)),
                  pl.BlockSpec((1, DIFF_V_DIM), lambda bi, h, i: (0, 0))],
        out_specs=pl.BlockSpec((1, tb, DIFF_V_DIM), lambda bi, h, i: (bi, i, h)),
        out_shape=jax.ShapeDtypeStruct((b, l, DIFF_WIDTH), F32),
        scratch_shapes=[pltpu.VMEM((2 * tb, LANES), F32), pltpu.VMEM((2 * tb, LANES), F32),
                        pltpu.VMEM((2 * tb, DIFF_V_DIM), F32)],
        compiler_params=_params(("parallel", "parallel", "arbitrary")),
        name="attn_prompt",
    )(qb, kh, vh, bias_tiles, lam_row, subln.reshape(1, DIFF_V_DIM))


def _prompt_bias_tiles(rel_table):
    tb = ATT_BLOCK
    qi = jnp.arange(tb)[:, None]
    ki = jnp.arange(tb)[None, :]
    far = jnp.broadcast_to(rel_table[_rel_bucket(jnp.array(-2 * tb))], (tb, tb, DIFF_HEADS))
    prev = rel_table[_rel_bucket(ki - qi - tb)]
    diag = jnp.where(((ki // CHUNK) <= (qi // CHUNK))[..., None], rel_table[_rel_bucket(ki - qi)], NEG_BIG)
    tiles = jnp.moveaxis(jnp.stack([far, prev, diag]).astype(F32), -1, 0)
    return jnp.concatenate([tiles, tiles], axis=2)


def _attn_sample_kernel(q_ref, kp_ref, vp_ref, kn_ref, vn_ref, bp_ref, bn_ref, lam_ref, subln_ref, o_ref,
                        *, out_scale):
    dh = DIFF_HEAD_DIM
    q = q_ref[0] * (dh ** -0.5)
    kp = kp_ref[0].astype(BF16)
    kn = kn_ref[0].astype(BF16)
    vp = vp_ref[0].astype(BF16)
    vn = vn_ref[0].astype(BF16)
    outs = []
    for t in range(2):
        qt = q[:, t * dh:(t + 1) * dh].astype(BF16)
        sp = _dot_nt(qt, kp[:, t * dh:(t + 1) * dh]) + bp_ref[0]
        sn = _dot_nt(qt, kn[:, t * dh:(t + 1) * dh]) + bn_ref[0]
        m = jnp.maximum(jnp.max(sp, axis=-1, keepdims=True), jnp.max(sn, axis=-1, keepdims=True))
        pp = jnp.exp(sp - m)
        pn = jnp.exp(sn - m)
        den = jnp.sum(pp, axis=-1, keepdims=True) + jnp.sum(pn, axis=-1, keepdims=True)
        outs.append((_dot(pp.astype(BF16), vp) + _dot(pn.astype(BF16), vn)) / den)
    o_ref[0] = _diff_finish(outs[0], outs[1], lam_ref, subln_ref, out_scale)


def _attn_sample(qb, k_past, v_past, k_new, v_new, bias_past, bias_new, lam_row, subln, out_scale):
    b, l, _ = qb.shape
    p = k_past.shape[1]
    hw = 2 * DIFF_HEAD_DIM
    return pl.pallas_call(
        functools.partial(_attn_sample_kernel, out_scale=out_scale),
        grid=(b, DIFF_HEADS),
        in_specs=[pl.BlockSpec((1, l, hw), lambda bi, h: (bi, 0, h)),
                  pl.BlockSpec((1, p, hw), lambda bi, h: (bi, 0, h)),
                  pl.BlockSpec((1, p, DIFF_V_DIM), lambda bi, h: (bi, 0, h)),
                  pl.BlockSpec((1, l, hw), lambda bi, h: (bi, 0, h)),
                  pl.BlockSpec((1, l, DIFF_V_DIM), lambda bi, h: (bi, 0, h)),
                  pl.BlockSpec((1, l, p), lambda bi, h: (h, 0, 0)),
                  pl.BlockSpec((1, l, l), lambda bi, h: (h, 0, 0)),
                  pl.BlockSpec((1, DIFF_V_DIM), lambda bi, h: (0, 0)),
                  pl.BlockSpec((1, DIFF_V_DIM), lambda bi, h: (0, 0))],
        out_specs=pl.BlockSpec((1, l, DIFF_V_DIM), lambda bi, h: (bi, 0, h)),
        out_shape=jax.ShapeDtypeStruct((b, l, DIFF_WIDTH), F32),
        compiler_params=_params(("parallel", "parallel")),
        name="attn_sample",
    )(qb, k_past, v_past, k_new, v_new, bias_past, bias_new, lam_row, subln.reshape(1, DIFF_V_DIM))


def _sample_bias(rel_table, p, l):
    rel = jnp.arange(-(p + l - 1), l)
    by_rel = rel_table[_rel_bucket(rel)].astype(F32).T
    bias = jnp.stack([lax.slice_in_dim(by_rel, l - 1 - i, p + 2 * l - 1 - i, axis=1) for i in range(l)], axis=1)
    return bias[:, :, :p], bias[:, :, p:]


def _outproj_kernel(oa_ref, ob_ref, x_ref, mod_ref, n2_ref, wo_ref, wq_ref, keys_ref,
                    x1_ref, h2_ref, sc_ref):
    mixed = jnp.concatenate([oa_ref[0], ob_ref[0]], axis=-1).astype(BF16)
    x1 = x_ref[0] + mod_ref[0, 2:3, :] * _dot(mixed, wo_ref[...])
    x1_ref[0] = x1
    h2 = _modulated_norm(x1, n2_ref[...], mod_ref[0, 3:4, :], mod_ref[0, 4:5, :])
    h2_ref[0] = h2
    qh = _dot(h2.astype(BF16), wq_ref[...]).astype(BF16)
    for hp in range(2 * PEER_HEADS):
        sc_ref[0, hp] = _dot_nt(keys_ref[hp], qh[:, hp * PEER_HALF:(hp + 1) * PEER_HALF])


def _outproj(o_a, o_b, x, mod, norm2, w_out, w_q, keys):
    b, l, d = x.shape
    tl = min(l, 256)
    nhp = 2 * PEER_HEADS
    return pl.pallas_call(
        _outproj_kernel,
        grid=(b, l // tl),
        in_specs=[pl.BlockSpec((1, tl, GDN_WIDTH), lambda i, j: (i, j, 0)),
                  pl.BlockSpec((1, tl, DIFF_WIDTH), lambda i, j: (i, j, 0)),
                  pl.BlockSpec((1, tl, d), lambda i, j: (i, j, 0)),
                  pl.BlockSpec((1, 6, d), lambda i, j: (i, 0, 0)),
                  pl.BlockSpec((1, d), lambda i, j: (0, 0)),
                  pl.BlockSpec((d, d), lambda i, j: (0, 0)),
                  pl.BlockSpec((d, nhp * PEER_HALF), lambda i, j: (0, 0)),
                  pl.BlockSpec((nhp, N_KEYS, PEER_HALF), lambda i, j: (0, 0, 0))],
        out_specs=[pl.BlockSpec((1, tl, d), lambda i, j: (i, j, 0)),
                   pl.BlockSpec((1, tl, d), lambda i, j: (i, j, 0)),
                   pl.BlockSpec((1, nhp, N_KEYS, tl), lambda i, j: (i, 0, 0, j))],
        out_shape=[jax.ShapeDtypeStruct((b, l, d), F32),
                   jax.ShapeDtypeStruct((b, l, d), F32),
                   jax.ShapeDtypeStruct((b, nhp, N_KEYS, l), F32)],
        compiler_params=_params(("parallel", "parallel")),
        name="outproj",
    )(o_a, o_b, x, mod, norm2.reshape(1, d), w_out, w_q, keys)


def _top16_rows(s, ids, n):
    vals, idxs = [], []
    for _ in range(PEER_TOPK):
        m = jnp.max(s, axis=0, keepdims=True)
        i = jnp.min(jnp.where(s == m, ids, n), axis=0, keepdims=True)
        vals.append(m)
        idxs.append(i)
        s = jnp.where(ids == i, -jnp.inf, s)
    return jnp.concatenate(vals, axis=0), jnp.concatenate(idxs, axis=0)


def _pair_candidates(s1, s2):
    t = s1.shape[1]
    sub16 = lax.broadcasted_iota(jnp.int32, (PEER_TOPK, t), 0)
    sub8 = sub16[:8]
    cand = [s1[0:1] + s2] + [s1[a:a + 1] + s2[:8] for a in range(1, 8)] + [s1[8:] + s2[0:1]]
    pos = [sub16] + [a * PEER_TOPK + sub8 for a in range(1, 8)] + [(8 + sub8) * PEER_TOPK]
    return jnp.concatenate(cand, axis=0), jnp.concatenate(pos, axis=0)


def _pick_rows(table, sel):
    out = jnp.zeros_like(table)
    for a in range(PEER_TOPK):
        out = jnp.where(sel == a, table[a:a + 1, :], out)
    return out


def _topk_kernel(sc_ref, eidx_ref, gate_ref):
    eidx, gates = [], []
    key_ids = lax.broadcasted_iota(jnp.int32, sc_ref.shape[2:], 0)
    for h in range(PEER_HEADS):
        s1, i1 = _top16_rows(sc_ref[0, 2 * h], key_ids, N_KEYS)
        s2, i2 = _top16_rows(sc_ref[0, 2 * h + 1], key_ids, N_KEYS)
        cand, cand_pos = _pair_candidates(s1, s2)
        top_s, pos = _top16_rows(cand, cand_pos, PEER_TOPK * PEER_TOPK)
        eidx.append(_pick_rows(i1, pos // PEER_TOPK) * N_KEYS + _pick_rows(i2, pos % PEER_TOPK))
        e = jnp.exp(top_s - top_s[0:1, :])
        gates.append(e / jnp.sum(e, axis=0, keepdims=True))
    eidx_ref[...] = jnp.concatenate(eidx, axis=0).T
    gate_ref[...] = jnp.concatenate(gates, axis=0).T


def _topk(scores):
    b, nhp, nk, l = scores.shape
    tt = min(l, LANES)
    nt = l // tt
    return pl.pallas_call(
        _topk_kernel,
        grid=(b, nt),
        in_specs=[pl.BlockSpec((1, nhp, nk, tt), lambda i, j: (i, 0, 0, j))],
        out_specs=[pl.BlockSpec((tt, PEER_SLOTS), lambda i, j: (i * nt + j, 0)),
                   pl.BlockSpec((tt, PEER_SLOTS), lambda i, j: (i * nt + j, 0))],
        out_shape=[jax.ShapeDtypeStruct((b * l, PEER_SLOTS), jnp.int32),
                   jax.ShapeDtypeStruct((b * l, PEER_SLOTS), F32)],
        compiler_params=_params(("parallel", "parallel")),
        name="topk",
    )(scores)


_SC_ROWS = 32


def _pack_kernel(t_ref, o_ref):
    half = t_ref.shape[1] // 2
    bits = lax.bitcast_convert_type(t_ref[...].astype(BF16).astype(F32), jnp.int32)
    o_ref[...] = lax.shift_right_logical(bits[:, :half], 16) | bits[:, half:]


def _pack_bf16_halves(t):
    v, d = t.shape
    tv = 512
    return pl.pallas_call(
        _pack_kernel,
        grid=(v // tv,),
        in_specs=[pl.BlockSpec((tv, d), lambda i: (i, 0))],
        out_specs=pl.BlockSpec((tv, d // 2), lambda i: (i, 0)),
        out_shape=jax.ShapeDtypeStruct((v, d // 2), jnp.int32),
        compiler_params=_params(("parallel",)),
        name="pack",
    )(t)


def _sc_gather2(tab_u, tab_v, idx):
    n = idx.shape[0]
    w = tab_u.shape[1]
    info = plsc.get_sparse_core_info()
    nw = info.num_cores * info.num_subcores
    per_w = n // nw
    nch = per_w // _SC_ROWS
    assert n % (nw * _SC_ROWS * 2) == 0
    mesh = plsc.VectorSubcoreMesh(core_axis_name="c", subcore_axis_name="s")
    rows = pltpu.VMEM((_SC_ROWS, w), tab_u.dtype)
    out = jax.ShapeDtypeStruct((n, w), tab_u.dtype)

    @functools.partial(
        pl.kernel, mesh=mesh, out_type=[out, out],
        scratch_types=[pltpu.VMEM((_SC_ROWS,), jnp.int32)] * 2 + [rows] * 4 + [pltpu.SemaphoreType.DMA] * 8,
    )
    def k(u_hbm, v_hbm, idx_hbm, uo_hbm, vo_hbm, i0, i1, u0, u1, v0, v1, gu0, gu1, gv0, gv1, wu0, wu1, wv0, wv1):
        idx_v, ub, vb = (i0, i1), (u0, u1), (v0, v1)
        gsu, gsv, wsu, wsv = (gu0, gu1), (gv0, gv1), (wu0, wu1), (wv0, wv1)
        wid = lax.axis_index("s") * info.num_cores + lax.axis_index("c")
        base = wid * per_w

        def span(c):
            return pl.ds(pl.multiple_of(base + c * _SC_ROWS, _SC_ROWS), _SC_ROWS)

        def gathers(s):
            return (pltpu.make_async_copy(u_hbm.at[idx_v[s]], ub[s], gsu[s]),
                    pltpu.make_async_copy(v_hbm.at[idx_v[s]], vb[s], gsv[s]))

        def writes(c, s):
            return (pltpu.make_async_copy(ub[s], uo_hbm.at[span(c)], wsu[s]),
                    pltpu.make_async_copy(vb[s], vo_hbm.at[span(c)], wsv[s]))

        def start(copies):
            for cp in copies:
                cp.start()

        def wait(copies):
            for cp in copies:
                cp.wait()

        def fetch(c, s):
            pltpu.sync_copy(idx_hbm.at[span(c)], idx_v[s])
            start(gathers(s))

        fetch(0, 0)

        @pl.loop(0, nch // 2)
        def _(kk):
            c0 = 2 * kk
            wait(gathers(0))
            start(writes(c0, 0))

            @pl.when(kk > 0)
            def _():
                wait(writes(c0 - 1, 1))

            fetch(c0 + 1, 1)
            wait(gathers(1))
            start(writes(c0 + 1, 1))
            wait(writes(c0, 0))

            @pl.when(kk < nch // 2 - 1)
            def _():
                fetch(c0 + 2, 0)

        wait(writes(nch - 1, 1))

    return k(tab_u, tab_v, idx)


_PEER_TOKENS = 16


def _unpack_bf16_halves(w):
    lo = lax.bitcast_convert_type(w << 16, F32)
    hi = lax.bitcast_convert_type(w & jnp.int32(-65536), F32)
    return lo, hi


def _peer_kernel(ug_ref, vg_ref, gate_ref, h2_ref, x1_ref, g2_ref, fn_ref, y_ref):
    half = D_MODEL // 2
    gate_t = gate_ref[...].T
    rows = []
    for t in range(_PEER_TOKENS):
        u_lo, u_hi = _unpack_bf16_halves(ug_ref[t * PEER_SLOTS:(t + 1) * PEER_SLOTS, :])
        pre = jnp.sum(u_lo * h2_ref[t:t + 1, :half] + u_hi * h2_ref[t:t + 1, half:], axis=-1, keepdims=True)
        act = 0.5 * pre * (1.0 + lax.erf(pre * (2.0 ** -0.5)))
        coef = gate_t[:, t:t + 1] * act
        v_lo, v_hi = _unpack_bf16_halves(vg_ref[t * PEER_SLOTS:(t + 1) * PEER_SLOTS, :])
        rows.append(jnp.concatenate([jnp.sum(coef * v_lo, axis=0, keepdims=True),
                                     jnp.sum(coef * v_hi, axis=0, keepdims=True)], axis=-1))
    x2 = x1_ref[...] + g2_ref[0, 5:6, :] * jnp.concatenate(rows, axis=0)
    y_ref[...] = x2 * lax.rsqrt(jnp.mean(x2 * x2, axis=-1, keepdims=True) + EPS) * fn_ref[...]


def _peer_combine(ug, vg, gate, h2, x1, mod, final_norm):
    b, l, d = h2.shape
    tp = _PEER_TOKENS
    steps_per_row = l // tp
    return pl.pallas_call(
        _peer_kernel,
        grid=(b * steps_per_row,),
        in_specs=[pl.BlockSpec((tp * PEER_SLOTS, d // 2), lambda i: (i, 0)),
                  pl.BlockSpec((tp * PEER_SLOTS, d // 2), lambda i: (i, 0)),
                  pl.BlockSpec((tp, PEER_SLOTS), lambda i: (i, 0)),
                  pl.BlockSpec((tp, d), lambda i: (i, 0)),
                  pl.BlockSpec((tp, d), lambda i: (i, 0)),
                  pl.BlockSpec((1, 6, d), lambda i: (i // steps_per_row, 0, 0)),
                  pl.BlockSpec((1, d), lambda i: (0, 0))],
        out_specs=pl.BlockSpec((tp, d), lambda i: (i, 0)),
        out_shape=jax.ShapeDtypeStruct((b * l, d), F32),
        compiler_params=_params(("parallel",)),
        name="peer",
    )(ug, vg, gate, h2.reshape(b * l, d), x1.reshape(b * l, d), mod, final_norm.reshape(1, d)).reshape(b, l, d)


def _front(x, mod, conv0, s0, k_past, v_past, wts, prompt):
    b, l, d = x.shape
    qkv, z, ab, qb, kb, vb, kh, vh = _inproj(x, mod, wts["norm1"], wts["w_in"])
    chunk = CHUNK if prompt else l
    o_a, s_new = _gdn(qkv, z, ab, conv0, s0, wts["conv_w"], wts["alog"], wts["dtb"], wts["onorm"], chunk)
    conv_new = qkv[:, l - (CONV_W - 1):, :]
    if prompt:
        o_b = _attn_prompt(qb, kh, vh, wts["bias_prompt"], wts["lam"], wts["subln"], wts["out_scale"])
    else:
        p = k_past.shape[1]
        bias_past, bias_new = _sample_bias(wts["rel_table"], p, l)
        o_b = _attn_sample(qb, k_past.reshape(b, p, DIFF_WIDTH), v_past.reshape(b, p, DIFF_WIDTH), kb, vb,
                           bias_past, bias_new, wts["lam"], wts["subln"], wts["out_scale"])
    x1, h2, scores = _outproj(o_a, o_b, x, mod, wts["norm2"], wts["w_out"], wts["w_q"], wts["keys"])
    eidx, gate = _topk(scores)
    ug, vg = _sc_gather2(wts["peer_u"], wts["peer_v"], eidx.reshape(-1))
    return (ug, vg, gate, h2, x1, mod), (kb, vb, s_new, conv_new)


def _back(pending, wts):
    return _peer_combine(*pending, wts["final_norm"])


def _prompt_rows(x, mod, wts):
    b, l, d = x.shape
    per = 3
    n_steps = (b - 2) // per
    assert n_steps >= 1
    conv0 = jnp.zeros((1, CONV_W - 1, QKV_A), F32)
    s0 = jnp.zeros((1, GDN_HEADS, GDN_HEAD_DIM, GDN_HEAD_DIM), F32)

    def after(value, dep):
        return lax.optimization_barrier((value, dep))[0]

    def front(i, dep):
        xi = after(lax.dynamic_slice_in_dim(x, i, 1, 0), dep)
        mi = lax.dynamic_slice_in_dim(mod, i, 1, 0)
        return _front(xi, mi, conv0, s0, None, None, wts, True)

    def back(pend, dep):
        ug, vg, gate, h2, x1, mi = pend
        return _back((ug, vg, after(gate, dep), h2, x1, mi), wts)

    def step(carry, j):
        pend_a, pend_b, last_y = carry
        r = per * j
        pend_c, new_c = front(r + 2, last_y)
        y_a = back(pend_a, pend_c[2])
        pend_a2, new_a2 = front(r + 3, y_a)
        y_b = back(pend_b, pend_a2[2])
        pend_b2, new_b2 = front(r + 4, y_b)
        y_c = back(pend_c, pend_b2[2])
        return (pend_a2, pend_b2, y_c), ((y_a, y_b, y_c), (new_c, new_a2, new_b2))

    pend_0, new_0 = front(0, mod)
    pend_1, new_1 = front(1, pend_0[2])
    (pend_a, pend_b, last_y), (y_steps, new_steps) = lax.scan(
        step, (pend_0, pend_1, jnp.zeros((1, l, d), F32)), jnp.arange(n_steps))
    pending, y_tail, new_tail = [pend_a, pend_b], [], []
    for i in range(2 + per * n_steps, b):
        pend, new = front(i, last_y)
        new_tail.append(new)
        last_y = back(pending.pop(0), pend[2])
        y_tail.append(last_y)
        pending.append(pend)
    for pend in pending:
        last_y = back(pend, last_y)
        y_tail.append(last_y)

    def rows(first, stacks, last):
        inter = jnp.stack([s[:, 0] for s in stacks], axis=1)
        inter = inter.reshape((-1,) + inter.shape[2:])
        return jnp.concatenate(first + [inter] + last, axis=0)

    y = rows([], y_steps, y_tail)
    new = [rows([new_0[t], new_1[t]], [n[t] for n in new_steps], [n[t] for n in new_tail]) for t in range(4)]
    return y, new


def _cache_entries(new, b, l):
    kb, vb, s_new, conv_new = new
    return (kb.reshape(1, b, l, DIFF_HEADS, 2 * DIFF_HEAD_DIM), vb.reshape(1, b, l, DIFF_HEADS, DIFF_V_DIM),
            s_new[None], conv_new[None])


def kernel(x_prompt, x_sample, c_prompt, c_sample, cache_k, cache_v, state_gdn, state_conv, w_ada, b_ada,
           norm1, norm2, w_in, conv_w, a_log, dt_bias, gdn_onorm, lam_q1, lam_k1, lam_q2, lam_k2, diff_subln,
           w_out, peer_wq, peer_keys, peer_u, peer_v, rel_table, final_norm):
    assert w_ada.shape[0] == 1, "single-layer step"
    bp = x_prompt.shape[0]
    d = D_MODEL
    lam_init = 0.8 - 0.6 * math.exp(-0.3 * 0)
    lam = (jnp.exp(jnp.sum(lam_q1[0] * lam_k1[0])) - jnp.exp(jnp.sum(lam_q2[0] * lam_k2[0])) + lam_init)
    w = w_in[0]
    w_packed = jnp.concatenate(
        [w[:, :_C_AB], jnp.pad(w[:, 2048:2056], ((0, 0), (0, LANES - 2 * GDN_HEADS))), w[:, 2056:]],
        axis=1).astype(BF16)
    wts = dict(
        norm1=norm1[0], norm2=norm2[0], w_in=w_packed, conv_w=conv_w[0],
        alog=jnp.pad(a_log[0], (0, LANES - GDN_HEADS)).reshape(1, LANES),
        dtb=jnp.pad(dt_bias[0], (0, LANES - GDN_HEADS)).reshape(1, LANES),
        onorm=gdn_onorm[0], lam=jnp.full((1, DIFF_V_DIM), lam, F32), subln=diff_subln[0],
        out_scale=1.0 - lam_init, bias_prompt=_prompt_bias_tiles(rel_table), rel_table=rel_table,
        w_out=w_out[0].astype(BF16), w_q=peer_wq[0].astype(BF16),
        keys=peer_keys[0].reshape(2 * PEER_HEADS, N_KEYS, PEER_HALF).astype(BF16),
        peer_u=_pack_bf16_halves(peer_u[0]), peer_v=_pack_bf16_halves(peer_v[0]), final_norm=final_norm)

    mod = _ada(jnp.concatenate([c_prompt, c_sample], axis=0), w_ada[0], b_ada[0]).reshape(-1, 6, d)
    pend_s, new_s = _front(x_sample, mod[bp:], state_conv[0], state_gdn[0], cache_k[0], cache_v[0], wts, False)
    yp, new_p = _prompt_rows(x_prompt, mod[:bp], wts)
    ys = _back(pend_s, wts)
    kp, vp, sp, cp = _cache_entries(new_p, *x_prompt.shape[:2])
    ks, vs, ss, cs = _cache_entries(new_s, *x_sample.shape[:2])
    return yp, ys, kp, vp, sp, cp, ks, vs, ss, cs
```

```python
import functools
import math

import jax
import jax.numpy as jnp
from jax import lax
from jax.experimental import pallas as pl
from jax.experimental.pallas import tpu as pltpu
from jax.experimental.pallas import tpu_sc as plsc

F32 = jnp.float32
BF16 = jnp.bfloat16
EPS = 1e-6

D_MODEL = 1024
CHUNK = 64
GDN_HEADS = 4
GDN_HEAD_DIM = 128
GDN_WIDTH = GDN_HEADS * GDN_HEAD_DIM
CONV_W = 4
QKV_A = 3 * GDN_WIDTH
DIFF_HEADS = 4
DIFF_HEAD_DIM = 64
DIFF_V_DIM = 128
DIFF_WIDTH = DIFF_HEADS * 2 * DIFF_HEAD_DIM
ATT_BLOCK = 256
N_BUCKETS = 32
REL_MAX_DIST = 128
PEER_HEADS = 8
N_KEYS = 128
PEER_HALF = 128
PEER_TOPK = 16
PEER_SLOTS = PEER_HEADS * PEER_TOPK
LANES = 128
NEG_BIG = -1e30
VMEM_LIMIT = 56 * 1024 * 1024

_C_QKV, _C_Z, _C_AB, _C_QB, _C_KB, _C_VB = 0, 1536, 2048, 2176, 2688, 3200
_C_END = 3712


def _params(sem):
    return pltpu.CompilerParams(dimension_semantics=sem, vmem_limit_bytes=VMEM_LIMIT)


def _dot(a, b, precision=None):
    return jnp.dot(a, b, preferred_element_type=F32, precision=precision)


def _dot_nt(a, b, precision=None):
    return lax.dot_general(a, b, (((1,), (1,)), ((), ())), preferred_element_type=F32, precision=precision)


def _silu(x):
    return x * jax.nn.sigmoid(x)


def _ada_kernel(c_ref, w_ref, b_ref, o_ref):
    a = _silu(c_ref[...]).astype(BF16)
    o_ref[...] = _dot(a, w_ref[...].astype(BF16)) + b_ref[...]


def _ada(c, w_ada, b_ada):
    n, d = c.shape
    cols = w_ada.shape[1]
    tn = 1024
    return pl.pallas_call(
        _ada_kernel,
        grid=(cols // tn,),
        in_specs=[pl.BlockSpec((n, d), lambda j: (0, 0)),
                  pl.BlockSpec((d, tn), lambda j: (0, j)),
                  pl.BlockSpec((1, tn), lambda j: (0, j))],
        out_specs=pl.BlockSpec((n, tn), lambda j: (0, j)),
        out_shape=jax.ShapeDtypeStruct((n, cols), F32),
        compiler_params=_params(("parallel",)),
        name="ada",
    )(c, w_ada, b_ada.reshape(1, cols))


def _modulated_norm(x, gain, shift, scale):
    y = x * lax.rsqrt(jnp.mean(x * x, axis=-1, keepdims=True) + EPS)
    return (y * gain) * (1.0 + scale) + shift


def _inproj_kernel(x_ref, mod_ref, n1_ref, w_ref, qkv_ref, z_ref, ab_ref, qb_ref, kb_ref, vb_ref, kh_ref, vh_ref):
    h = _modulated_norm(x_ref[0], n1_ref[...], mod_ref[0, 0:1, :], mod_ref[0, 1:2, :]).astype(BF16)
    qkv_ref[0] = _dot(h, w_ref[:, _C_QKV:_C_Z])
    z_ref[0] = _dot(h, w_ref[:, _C_Z:_C_AB])
    ab_ref[0] = _dot(h, w_ref[:, _C_AB:_C_QB])
    qb_ref[0] = _dot(h, w_ref[:, _C_QB:_C_KB])
    kb = _dot(h, w_ref[:, _C_KB:_C_VB])
    vb = _dot(h, w_ref[:, _C_VB:_C_END])
    kb_ref[0] = kb
    vb_ref[0] = vb
    kh_ref[0] = kb.astype(BF16)
    vh_ref[0] = vb.astype(BF16)


def _inproj(x, mod, norm1, w_packed):
    b, l, d = x.shape
    tl = min(l, 256)
    widths = (QKV_A, GDN_WIDTH, LANES, DIFF_WIDTH, DIFF_WIDTH, DIFF_WIDTH, DIFF_WIDTH, DIFF_WIDTH)
    dtypes = (F32,) * 6 + (BF16,) * 2
    return pl.pallas_call(
        _inproj_kernel,
        grid=(b, l // tl),
        in_specs=[pl.BlockSpec((1, tl, d), lambda i, j: (i, j, 0)),
                  pl.BlockSpec((1, 6, d), lambda i, j: (i, 0, 0)),
                  pl.BlockSpec((1, d), lambda i, j: (0, 0)),
                  pl.BlockSpec((d, _C_END), lambda i, j: (0, 0))],
        out_specs=[pl.BlockSpec((1, tl, w), lambda i, j: (i, j, 0)) for w in widths],
        out_shape=[jax.ShapeDtypeStruct((b, l, w), dt) for w, dt in zip(widths, dtypes)],
        compiler_params=_params(("parallel", "parallel")),
        name="inproj",
    )(x, mod, norm1.reshape(1, d), w_packed)


_HI = lax.Precision.HIGHEST


def _unit_lower_inverses(mats, n):
    r = lax.broadcasted_iota(jnp.int32, (n, n), 0)
    c = lax.broadcasted_iota(jnp.int32, (n, n), 1)
    eye = (r == c).astype(F32)
    ad = [jnp.where((r // 8) == (c // 8), a, 0.0) for a in mats]
    a2 = [_dot(m, m, _HI) for m in ad]
    a4 = [_dot(m, m, _HI) for m in a2]
    xs = [eye - m for m in ad]
    xs = [x + _dot(x, m, _HI) for x, m in zip(xs, a2)]
    xs = [x + _dot(x, m, _HI) for x, m in zip(xs, a4)]
    bs = 8
    while bs < n:
        off = ((r // (2 * bs)) == (c // (2 * bs))) & ((r // bs) != (c // bs))
        ys = [_dot(jnp.where(off, a, 0.0), x, _HI) for a, x in zip(mats, xs)]
        xs = [x - _dot(x, y, _HI) for x, y in zip(xs, ys)]
        bs *= 2
    return xs


def _mm(a, b):
    return _dot(a.astype(BF16), b.astype(BF16))


def _mm_nt(a, b):
    return _dot_nt(a.astype(BF16), b.astype(BF16))


def _gdn_kernel(qkv_ref, z_ref, ab_ref, conv0_ref, s0_ref, cw_ref, alog_ref, dtb_ref, onorm_ref,
                o_ref, s_ref, xbuf, *, chunk, nch):
    hd = GDN_HEAD_DIM
    rows = chunk * nch

    @pl.when(pl.program_id(1) == 0)
    def _():
        xbuf[5:8, :] = conv0_ref[0]
        s_ref[0] = s0_ref[0]

    x = qkv_ref[0]
    xbuf[8:8 + rows, :] = x
    y = (xbuf[5:5 + rows, :] * cw_ref[0:1, :] + xbuf[6:6 + rows, :] * cw_ref[1:2, :]
         + xbuf[7:7 + rows, :] * cw_ref[2:3, :] + x * cw_ref[3:4, :])
    xbuf[5:8, :] = x[rows - 3:rows, :]
    y = _silu(y)

    ab = ab_ref[0]
    t = ab + dtb_ref[...]
    softplus = jnp.maximum(t, 0.0) + jnp.log(1.0 + jnp.exp(-jnp.abs(t)))
    g = -jnp.exp(alog_ref[...]) * softplus
    beta = jax.nn.sigmoid(ab)

    r = lax.broadcasted_iota(jnp.int32, (chunk, chunk), 0)
    c = lax.broadcasted_iota(jnp.int32, (chunk, chunk), 1)
    lower = r >= c
    tri = lower.astype(F32)

    heads = range(GDN_HEADS)
    pairs = [(ci, h) for ci in range(nch) for h in heads]
    rows_of = {ci: slice(ci * chunk, (ci + 1) * chunk) for ci in range(nch)}
    gc = {ci: _dot(tri, g[rows_of[ci]], _HI) for ci in range(nch)}
    gc_t = {ci: gc[ci].T for ci in range(nch)}
    q, k, vb, kb, decay, egc, g_last = {}, {}, {}, {}, {}, {}, {}
    for ci, h in pairs:
        sl = rows_of[ci]
        qh = y[sl, h * hd:(h + 1) * hd]
        kh = y[sl, GDN_WIDTH + h * hd:GDN_WIDTH + (h + 1) * hd]
        p = ci, h
        q[p] = qh * lax.rsqrt(jnp.sum(qh * qh, axis=-1, keepdims=True) + EPS) * (hd ** -0.5)
        k[p] = kh * lax.rsqrt(jnp.sum(kh * kh, axis=-1, keepdims=True) + EPS)
        gcol = gc[ci][:, h:h + 1]
        bcol = beta[sl, GDN_HEADS + h:GDN_HEADS + h + 1]
        decay[p] = jnp.exp(jnp.where(lower, gcol - gc_t[ci][h:h + 1, :], NEG_BIG))
        kb[p] = k[p] * bcol
        vb[p] = y[sl, 2 * GDN_WIDTH + h * hd:2 * GDN_WIDTH + (h + 1) * hd] * bcol
        egc[p] = jnp.exp(gcol)
        g_last[p] = gcol[chunk - 1:chunk, :]
    kk = {p: _mm_nt(kb[p], k[p]) for p in pairs}
    tinv = dict(zip(pairs, _unit_lower_inverses([jnp.where(r > c, kk[p] * decay[p], 0.0) for p in pairs], chunk)))
    u_v = {p: _dot(tinv[p], vb[p], _HI) for p in pairs}
    w = {p: _dot(tinv[p], kb[p] * egc[p], _HI) for p in pairs}
    qk = {p: _mm_nt(q[p], k[p]) * decay[p] for p in pairs}
    k_dec_t = {p: (k[p] * jnp.exp(g_last[p] - gc[p[0]][:, p[1]:p[1] + 1])).T for p in pairs}

    s = [s_ref[0, h] for h in heads]
    for ci in range(nch):
        ws = [_mm(w[ci, h], s[h]) for h in heads]
        qs = [_mm(q[ci, h] * egc[ci, h], s[h]) for h in heads]
        v_new = [u_v[ci, h] - ws[h] for h in heads]
        o = [qs[h] + _mm(qk[ci, h], v_new[h]) for h in heads]
        s = [s[h] * jnp.exp(g_last[ci, h]) + _mm(k_dec_t[ci, h], v_new[h]) for h in heads]
        for h in heads:
            oh = o[h] * lax.rsqrt(jnp.mean(o[h] * o[h], axis=-1, keepdims=True) + EPS) * onorm_ref[...]
            o_ref[0, rows_of[ci], h * hd:(h + 1) * hd] = oh * _silu(z_ref[0, rows_of[ci], h * hd:(h + 1) * hd])
    for h in heads:
        s_ref[0, h] = s[h]


def _gdn(qkv, z, ab, conv0, s0, conv_w, alog_pad, dtb_pad, onorm, chunk):
    b, l, _ = qkv.shape
    hd = GDN_HEAD_DIM
    nch = 2 if l % (2 * chunk) == 0 else 1
    rows = chunk * nch
    return pl.pallas_call(
        functools.partial(_gdn_kernel, chunk=chunk, nch=nch),
        grid=(b, l // rows),
        in_specs=[pl.BlockSpec((1, rows, QKV_A), lambda i, j: (i, j, 0)),
                  pl.BlockSpec((1, rows, GDN_WIDTH), lambda i, j: (i, j, 0)),
                  pl.BlockSpec((1, rows, LANES), lambda i, j: (i, j, 0)),
                  pl.BlockSpec((1, CONV_W - 1, QKV_A), lambda i, j: (i, 0, 0)),
                  pl.BlockSpec((1, GDN_HEADS, hd, hd), lambda i, j: (i, 0, 0, 0)),
                  pl.BlockSpec((CONV_W, QKV_A), lambda i, j: (0, 0)),
                  pl.BlockSpec((1, LANES), lambda i, j: (0, 0)),
                  pl.BlockSpec((1, LANES), lambda i, j: (0, 0)),
                  pl.BlockSpec((1, hd), lambda i, j: (0, 0))],
        out_specs=[pl.BlockSpec((1, rows, GDN_WIDTH), lambda i, j: (i, j, 0)),
                   pl.BlockSpec((1, GDN_HEADS, hd, hd), lambda i, j: (i, 0, 0, 0))],
        out_shape=[jax.ShapeDtypeStruct((b, l, GDN_WIDTH), F32),
                   jax.ShapeDtypeStruct((b, GDN_HEADS, hd, hd), F32)],
        scratch_shapes=[pltpu.VMEM((8 + rows, QKV_A), F32)],
        compiler_params=_params(("parallel", "arbitrary")),
        name="gdn",
    )(qkv, z, ab, conv0, s0, conv_w, alog_pad, dtb_pad, onorm.reshape(1, hd))


def _rel_bucket(rel):
    nb = N_BUCKETS // 2
    max_exact = nb // 2
    ret = jnp.where(rel > 0, nb, 0)
    n = jnp.abs(rel)
    large = max_exact + (jnp.log(jnp.maximum(n, 1).astype(F32) / max_exact)
                         / math.log(REL_MAX_DIST / max_exact) * (nb - max_exact)).astype(jnp.int32)
    large = jnp.minimum(large, nb - 1)
    return ret + jnp.where(n < max_exact, n, large)


def _diff_finish(o1, o2, lam_ref, subln_ref, out_scale):
    o = o1 - lam_ref[...] * o2
    return o * lax.rsqrt(jnp.mean(o * o, axis=-1, keepdims=True) + EPS) * subln_ref[...] * out_scale


def _attn_prompt_kernel(q_ref, k_ref, v_ref, bias_ref, lam_ref, subln_ref, o_ref, m_ref, l_ref, acc_ref,
                        *, out_scale):
    i = pl.program_id(2)
    tb = ATT_BLOCK
    dh = DIFF_HEAD_DIM
    q = q_ref[0] * (dh ** -0.5)
    lane = lax.broadcasted_iota(jnp.int32, q.shape, 1)
    q2s = jnp.concatenate([jnp.where(lane < dh, q, 0.0), jnp.where(lane >= dh, q, 0.0)], axis=0).astype(BF16)

    def score_tiles(j, tile):
        start = pl.multiple_of(j * tb, tb)
        s = _dot_nt(q2s, k_ref[0, pl.ds(start, tb), :]) + bias_ref[0, tile]
        return [s[:, c:c + LANES] for c in range(0, tb, LANES)]

    def visible_blocks(fn, unroll):
        n_far = jnp.maximum(i - 1, 0)

        @pl.loop(0, n_far // unroll)
        def _(g):
            fn([(unroll * g + u, 0) for u in range(unroll)])

        @pl.loop((n_far // unroll) * unroll, n_far)
        def _(j):
            fn([(j, 0)])

        @pl.when(i > 0)
        def _():
            fn([(i - 1, 1), (i, 2)])

        @pl.when(i == 0)
        def _():
            fn([(i, 2)])

    m_ref[...] = jnp.full(m_ref.shape, NEG_BIG, F32)

    def track_max(blocks):
        tiles = [s for j, tile in blocks for s in score_tiles(j, tile)]
        m_ref[...] = functools.reduce(jnp.maximum, tiles, m_ref[...])

    visible_blocks(track_max, 4)
    m_ref[...] = jnp.broadcast_to(jnp.max(m_ref[...], axis=-1, keepdims=True), m_ref.shape)

    l_ref[...] = jnp.zeros(l_ref.shape, F32)
    acc_ref[...] = jnp.zeros(acc_ref.shape, F32)

    def accumulate(blocks):
        m = m_ref[...]
        l_add, acc_add = [], []
        for j, tile in blocks:
            p = [jnp.exp(s - m) for s in score_tiles(j, tile)]
            l_add.extend(p)
            start = pl.multiple_of(j * tb, tb)
            acc_add.append(_dot(jnp.concatenate(p, axis=-1).astype(BF16), v_ref[0, pl.ds(start, tb), :]))
        l_ref[...] += sum(l_add)
        acc_ref[...] += sum(acc_add)

    visible_blocks(accumulate, 4)
    o = acc_ref[...] / jnp.sum(l_ref[...], axis=-1, keepdims=True)
    o_ref[0] = _diff_finish(o[:tb], o[tb:], lam_ref, subln_ref, out_scale)


def _attn_prompt(qb, kh, vh, bias_tiles, lam_row, subln, out_scale):
    b, l, _ = qb.shape
    tb = ATT_BLOCK
    hw = 2 * DIFF_HEAD_DIM
    return pl.pallas_call(
        functools.partial(_attn_prompt_kernel, out_scale=out_scale),
        grid=(b, DIFF_HEADS, l // tb),
        in_specs=[pl.BlockSpec((1, tb, hw), lambda bi, h, i: (bi, i, h)),
                  pl.BlockSpec((1, l, hw), lambda bi, h, i: (bi, 0, h)),
                  pl.BlockSpec((1, l, DIFF_V_DIM), lambda bi, h, i: (bi, 0, h)),
                  pl.BlockSpec((1, 3, 2 * tb, tb), lambda bi, h, i: (h, 0, 0, 0)),
                  pl.BlockSpec((1, DIFF_V_DIM), lambda bi, h, i: (0, 0)),
                  pl.BlockSpec((1, DIFF_V_DIM), lambda bi, h, i: (0, 0))],
        out_specs=pl.BlockSpec((1, tb, DIFF_V_DIM), lambda bi, h, i: (bi, i, h)),
        out_shape=jax.ShapeDtypeStruct((b, l, DIFF_WIDTH), F32),
        scratch_shapes=[pltpu.VMEM((2 * tb, LANES), F32), pltpu.VMEM((2 * tb, LANES), F32),
                        pltpu.VMEM((2 * tb, DIFF_V_DIM), F32)],
        compiler_params=_params(("parallel", "parallel", "arbitrary")),
        name="attn_prompt",
    )(qb, kh, vh, bias_tiles, lam_row, subln.reshape(1, DIFF_V_DIM))


def _prompt_bias_tiles(rel_table):
    tb = ATT_BLOCK
    qi = jnp.arange(tb)[:, None]
    ki = jnp.arange(tb)[None, :]
    far = jnp.broadcast_to(rel_table[_rel_bucket(jnp.array(-2 * tb))], (tb, tb, DIFF_HEADS))
    prev = rel_table[_rel_bucket(ki - qi - tb)]
    diag = jnp.where(((ki // CHUNK) <= (qi // CHUNK))[..., None], rel_table[_rel_bucket(ki - qi)], NEG_BIG)
    tiles = jnp.moveaxis(jnp.stack([far, prev, diag]).astype(F32), -1, 0)
    return jnp.concatenate([tiles, tiles], axis=2)


def _attn_sample_kernel(q_ref, kp_ref, vp_ref, kn_ref, vn_ref, bp_ref, bn_ref, lam_ref, subln_ref, o_ref,
                        *, out_scale):
    dh = DIFF_HEAD_DIM
    q = q_ref[0] * (dh ** -0.5)
    kp = kp_ref[0].astype(BF16)
    kn = kn_ref[0].astype(BF16)
    vp = vp_ref[0].astype(BF16)
    vn = vn_ref[0].astype(BF16)
    outs = []
    for t in range(2):
        qt = q[:, t * dh:(t + 1) * dh].astype(BF16)
        sp = _dot_nt(qt, kp[:, t * dh:(t + 1) * dh]) + bp_ref[0]
        sn = _dot_nt(qt, kn[:, t * dh:(t + 1) * dh]) + bn_ref[0]
        m = jnp.maximum(jnp.max(sp, axis=-1, keepdims=True), jnp.max(sn, axis=-1, keepdims=True))
        pp = jnp.exp(sp - m)
        pn = jnp.exp(sn - m)
        den = jnp.sum(pp, axis=-1, keepdims=True) + jnp.sum(pn, axis=-1, keepdims=True)
        outs.append((_dot(pp.astype(BF16), vp) + _dot(pn.astype(BF16), vn)) / den)
    o_ref[0] = _diff_finish(outs[0], outs[1], lam_ref, subln_ref, out_scale)


def _attn_sample(qb, k_past, v_past, k_new, v_new, bias_past, bias_new, lam_row, subln, out_scale):
    b, l, _ = qb.shape
    p = k_past.shape[1]
    hw = 2 * DIFF_HEAD_DIM
    return pl.pallas_call(
        functools.partial(_attn_sample_kernel, out_scale=out_scale),
        grid=(b, DIFF_HEADS),
        in_specs=[pl.BlockSpec((1, l, hw), lambda bi, h: (bi, 0, h)),
                  pl.BlockSpec((1, p, hw), lambda bi, h: (bi, 0, h)),
                  pl.BlockSpec((1, p, DIFF_V_DIM), lambda bi, h: (bi, 0, h)),
                  pl.BlockSpec((1, l, hw), lambda bi, h: (bi, 0, h)),
                  pl.BlockSpec((1, l, DIFF_V_DIM), lambda bi, h: (bi, 0, h)),
                  pl.BlockSpec((1, l, p), lambda bi, h: (h, 0, 0)),
                  pl.BlockSpec((1, l, l), lambda bi, h: (h, 0, 0)),
                  pl.BlockSpec((1, DIFF_V_DIM), lambda bi, h: (0, 0)),
                  pl.BlockSpec((1, DIFF_V_DIM), lambda bi, h: (0, 0))],
        out_specs=pl.BlockSpec((1, l, DIFF_V_DIM), lambda bi, h: (bi, 0, h)),
        out_shape=jax.ShapeDtypeStruct((b, l, DIFF_WIDTH), F32),
        compiler_params=_params(("parallel", "parallel")),
        name="attn_sample",
    )(qb, k_past, v_past, k_new, v_new, bias_past, bias_new, lam_row, subln.reshape(1, DIFF_V_DIM))


def _sample_bias(rel_table, p, l):
    rel = jnp.arange(-(p + l - 1), l)
    by_rel = rel_table[_rel_bucket(rel)].astype(F32).T
    bias = jnp.stack([lax.slice_in_dim(by_rel, l - 1 - i, p + 2 * l - 1 - i, axis=1) for i in range(l)], axis=1)
    return bias[:, :, :p], bias[:, :, p:]


def _outproj_kernel(oa_ref, ob_ref, x_ref, mod_ref, n2_ref, wo_ref, wq_ref, keys_ref,
                    x1_ref, h2_ref, sc_ref):
    mixed = jnp.concatenate([oa_ref[0], ob_ref[0]], axis=-1).astype(BF16)
    x1 = x_ref[0] + mod_ref[0, 2:3, :] * _dot(mixed, wo_ref[...])
    x1_ref[0] = x1
    h2 = _modulated_norm(x1, n2_ref[...], mod_ref[0, 3:4, :], mod_ref[0, 4:5, :])
    h2_ref[0] = h2
    qh = _dot(h2.astype(BF16), wq_ref[...]).astype(BF16)
    for hp in range(2 * PEER_HEADS):
        sc_ref[0, hp] = _dot_nt(keys_ref[hp], qh[:, hp * PEER_HALF:(hp + 1) * PEER_HALF])


def _outproj(o_a, o_b, x, mod, norm2, w_out, w_q, keys):
    b, l, d = x.shape
    tl = min(l, 256)
    nhp = 2 * PEER_HEADS
    return pl.pallas_call(
        _outproj_kernel,
        grid=(b, l // tl),
        in_specs=[pl.BlockSpec((1, tl, GDN_WIDTH), lambda i, j: (i, j, 0)),
                  pl.BlockSpec((1, tl, DIFF_WIDTH), lambda i, j: (i, j, 0)),
                  pl.BlockSpec((1, tl, d), lambda i, j: (i, j, 0)),
                  pl.BlockSpec((1, 6, d), lambda i, j: (i, 0, 0)),
                  pl.BlockSpec((1, d), lambda i, j: (0, 0)),
                  pl.BlockSpec((d, d), lambda i, j: (0, 0)),
                  pl.BlockSpec((d, nhp * PEER_HALF), lambda i, j: (0, 0)),
                  pl.BlockSpec((nhp, N_KEYS, PEER_HALF), lambda i, j: (0, 0, 0))],
        out_specs=[pl.BlockSpec((1, tl, d), lambda i, j: (i, j, 0)),
                   pl.BlockSpec((1, tl, d), lambda i, j: (i, j, 0)),
                   pl.BlockSpec((1, nhp, N_KEYS, tl), lambda i, j: (i, 0, 0, j))],
        out_shape=[jax.ShapeDtypeStruct((b, l, d), F32),
                   jax.ShapeDtypeStruct((b, l, d), F32),
                   jax.ShapeDtypeStruct((b, nhp, N_KEYS, l), F32)],
        compiler_params=_params(("parallel", "parallel")),
        name="outproj",
    )(o_a, o_b, x, mod, norm2.reshape(1, d), w_out, w_q, keys)


def _top16_rows(s, ids, n):
    vals, idxs = [], []
    for _ in range(PEER_TOPK):
        m = jnp.max(s, axis=0, keepdims=True)
        i = jnp.min(jnp.where(s == m, ids, n), axis=0, keepdims=True)
        vals.append(m)
        idxs.append(i)
        s = jnp.where(ids == i, -jnp.inf, s)
    return jnp.concatenate(vals, axis=0), jnp.concatenate(idxs, axis=0)


def _pair_candidates(s1, s2):
    t = s1.shape[1]
    sub16 = lax.broadcasted_iota(jnp.int32, (PEER_TOPK, t), 0)
    sub8 = sub16[:8]
    cand = [s1[0:1] + s2] + [s1[a:a + 1] + s2[:8] for a in range(1, 8)] + [s1[8:] + s2[0:1]]
    pos = [sub16] + [a * PEER_TOPK + sub8 for a in range(1, 8)] + [(8 + sub8) * PEER_TOPK]
    return jnp.concatenate(cand, axis=0), jnp.concatenate(pos, axis=0)


def _pick_rows(table, sel):
    out = jnp.zeros_like(table)
    for a in range(PEER_TOPK):
        out = jnp.where(sel == a, table[a:a + 1, :], out)
    return out


def _topk_kernel(sc_ref, eidx_ref, gate_ref):
    eidx, gates = [], []
    key_ids = lax.broadcasted_iota(jnp.int32, sc_ref.shape[2:], 0)
    for h in range(PEER_HEADS):
        s1, i1 = _top16_rows(sc_ref[0, 2 * h], key_ids, N_KEYS)
        s2, i2 = _top16_rows(sc_ref[0, 2 * h + 1], key_ids, N_KEYS)
        cand, cand_pos = _pair_candidates(s1, s2)
        top_s, pos = _top16_rows(cand, cand_pos, PEER_TOPK * PEER_TOPK)
        eidx.append(_pick_rows(i1, pos // PEER_TOPK) * N_KEYS + _pick_rows(i2, pos % PEER_TOPK))
        e = jnp.exp(top_s - top_s[0:1, :])
        gates.append(e / jnp.sum(e, axis=0, keepdims=True))
    eidx_ref[...] = jnp.concatenate(eidx, axis=0).T
    gate_ref[...] = jnp.concatenate(gates, axis=0).T


def _topk(scores):
    b, nhp, nk, l = scores.shape
    tt = min(l, LANES)
    nt = l // tt
    return pl.pallas_call(
        _topk_kernel,
        grid=(b, nt),
        in_specs=[pl.BlockSpec((1, nhp, nk, tt), lambda i, j: (i, 0, 0, j))],
        out_specs=[pl.BlockSpec((tt, PEER_SLOTS), lambda i, j: (i * nt + j, 0)),
                   pl.BlockSpec((tt, PEER_SLOTS), lambda i, j: (i * nt + j, 0))],
        out_shape=[jax.ShapeDtypeStruct((b * l, PEER_SLOTS), jnp.int32),
                   jax.ShapeDtypeStruct((b * l, PEER_SLOTS), F32)],
        compiler_params=_params(("parallel", "parallel")),
        name="topk",
    )(scores)


_SC_ROWS = 32


def _pack_kernel(t_ref, o_ref):
    half = t_ref.shape[1] // 2
    bits = lax.bitcast_convert_type(t_ref[...].astype(BF16).astype(F32), jnp.int32)
    o_ref[...] = lax.shift_right_logical(bits[:, :half], 16) | bits[:, half:]


def _pack_bf16_halves(t):
    v, d = t.shape
    tv = 512
    return pl.pallas_call(
        _pack_kernel,
        grid=(v // tv,),
        in_specs=[pl.BlockSpec((tv, d), lambda i: (i, 0))],
        out_specs=pl.BlockSpec((tv, d // 2), lambda i: (i, 0)),
        out_shape=jax.ShapeDtypeStruct((v, d // 2), jnp.int32),
        compiler_params=_params(("parallel",)),
        name="pack",
    )(t)


def _sc_gather2(tab_u, tab_v, idx):
    n = idx.shape[0]
    w = tab_u.shape[1]
    info = plsc.get_sparse_core_info()
    nw = info.num_cores * info.num_subcores
    per_w = n // nw
    nch = per_w // _SC_ROWS
    assert n % (nw * _SC_ROWS * 2) == 0
    mesh = plsc.VectorSubcoreMesh(core_axis_name="c", subcore_axis_name="s")
    rows = pltpu.VMEM((_SC_ROWS, w), tab_u.dtype)
    out = jax.ShapeDtypeStruct((n, w), tab_u.dtype)

    @functools.partial(
        pl.kernel, mesh=mesh, out_type=[out, out],
        scratch_types=[pltpu.VMEM((_SC_ROWS,), jnp.int32)] * 2 + [rows] * 4 + [pltpu.SemaphoreType.DMA] * 8,
    )
    def k(u_hbm, v_hbm, idx_hbm, uo_hbm, vo_hbm, i0, i1, u0, u1, v0, v1, gu0, gu1, gv0, gv1, wu0, wu1, wv0, wv1):
        idx_v, ub, vb = (i0, i1), (u0, u1), (v0, v1)
        gsu, gsv, wsu, wsv = (gu0, gu1), (gv0, gv1), (wu0, wu1), (wv0, wv1)
        wid = lax.axis_index("s") * info.num_cores + lax.axis_index("c")
        base = wid * per_w

        def span(c):
            return pl.ds(pl.multiple_of(base + c * _SC_ROWS, _SC_ROWS), _SC_ROWS)

        def gathers(s):
            return (pltpu.make_async_copy(u_hbm.at[idx_v[s]], ub[s], gsu[s]),
                    pltpu.make_async_copy(v_hbm.at[idx_v[s]], vb[s], gsv[s]))

        def writes(c, s):
            return (pltpu.make_async_copy(ub[s], uo_hbm.at[span(c)], wsu[s]),
                    pltpu.make_async_copy(vb[s], vo_hbm.at[span(c)], wsv[s]))

        def start(copies):
            for cp in copies:
                cp.start()

        def wait(copies):
            for cp in copies:
                cp.wait()

        def fetch(c, s):
            pltpu.sync_copy(idx_hbm.at[span(c)], idx_v[s])
            start(gathers(s))

        fetch(0, 0)

        @pl.loop(0, nch // 2)
        def _(kk):
            c0 = 2 * kk
            wait(gathers(0))
            start(writes(c0, 0))

            @pl.when(kk > 0)
            def _():
                wait(writes(c0 - 1, 1))

            fetch(c0 + 1, 1)
            wait(gathers(1))
            start(writes(c0 + 1, 1))
            wait(writes(c0, 0))

            @pl.when(kk < nch // 2 - 1)
            def _():
                fetch(c0 + 2, 0)

        wait(writes(nch - 1, 1))

    return k(tab_u, tab_v, idx)


_PEER_TOKENS = 16


def _unpack_bf16_halves(w):
    lo = lax.bitcast_convert_type(w << 16, F32)
    hi = lax.bitcast_convert_type(w & jnp.int32(-65536), F32)
    return lo, hi


def _peer_kernel(ug_ref, vg_ref, gate_ref, h2_ref, x1_ref, g2_ref, fn_ref, y_ref):
    half = D_MODEL // 2
    gate_t = gate_ref[...].T
    rows = []
    for t in range(_PEER_TOKENS):
        u_lo, u_hi = _unpack_bf16_halves(ug_ref[t * PEER_SLOTS:(t + 1) * PEER_SLOTS, :])
        pre = jnp.sum(u_lo * h2_ref[t:t + 1, :half] + u_hi * h2_ref[t:t + 1, half:], axis=-1, keepdims=True)
        act = 0.5 * pre * (1.0 + lax.erf(pre * (2.0 ** -0.5)))
        coef = gate_t[:, t:t + 1] * act
        v_lo, v_hi = _unpack_bf16_halves(vg_ref[t * PEER_SLOTS:(t + 1) * PEER_SLOTS, :])
        rows.append(jnp.concatenate([jnp.sum(coef * v_lo, axis=0, keepdims=True),
                                     jnp.sum(coef * v_hi, axis=0, keepdims=True)], axis=-1))
    x2 = x1_ref[...] + g2_ref[0, 5:6, :] * jnp.concatenate(rows, axis=0)
    y_ref[...] = x2 * lax.rsqrt(jnp.mean(x2 * x2, axis=-1, keepdims=True) + EPS) * fn_ref[...]


def _peer_combine(ug, vg, gate, h2, x1, mod, final_norm):
    b, l, d = h2.shape
    tp = _PEER_TOKENS
    steps_per_row = l // tp
    return pl.pallas_call(
        _peer_kernel,
        grid=(b * steps_per_row,),
        in_specs=[pl.BlockSpec((tp * PEER_SLOTS, d // 2), lambda i: (i, 0)),
                  pl.BlockSpec((tp * PEER_SLOTS, d // 2), lambda i: (i, 0)),
                  pl.BlockSpec((tp, PEER_SLOTS), lambda i: (i, 0)),
                  pl.BlockSpec((tp, d), lambda i: (i, 0)),
                  pl.BlockSpec((tp, d), lambda i: (i, 0)),
                  pl.BlockSpec((1, 6, d), lambda i: (i // steps_per_row, 0, 0)),
                  pl.BlockSpec((1, d), lambda i: (0, 0))],
        out_specs=pl.BlockSpec((tp, d), lambda i: (i, 0)),
        out_shape=jax.ShapeDtypeStruct((b * l, d), F32),
        compiler_params=_params(("parallel",)),
        name="peer",
    )(ug, vg, gate, h2.reshape(b * l, d), x1.reshape(b * l, d), mod, final_norm.reshape(1, d)).reshape(b, l, d)


def _front(x, mod, conv0, s0, k_past, v_past, wts, prompt):
    b, l, d = x.shape
    qkv, z, ab, qb, kb, vb, kh, vh = _inproj(x, mod, wts["norm1"], wts["w_in"])
    chunk = CHUNK if prompt else l
    o_a, s_new = _gdn(qkv, z, ab, conv0, s0, wts["conv_w"], wts["alog"], wts["dtb"], wts["onorm"], chunk)
    conv_new = qkv[:, l - (CONV_W - 1):, :]
    if prompt:
        o_b = _attn_prompt(qb, kh, vh, wts["bias_prompt"], wts["lam"], wts["subln"], wts["out_scale"])
    else:
        p = k_past.shape[1]
        bias_past, bias_new = _sample_bias(wts["rel_table"], p, l)
        o_b = _attn_sample(qb, k_past.reshape(b, p, DIFF_WIDTH), v_past.reshape(b, p, DIFF_WIDTH), kb, vb,
                           bias_past, bias_new, wts["lam"], wts["subln"], wts["out_scale"])
    x1, h2, scores = _outproj(o_a, o_b, x, mod, wts["norm2"], wts["w_out"], wts["w_q"], wts["keys"])
    eidx, gate = _topk(scores)
    ug, vg = _sc_gather2(wts["peer_u"], wts["peer_v"], eidx.reshape(-1))
    return (ug, vg, gate, h2, x1, mod), (kb, vb, s_new, conv_new)


def _back(pending, wts):
    return _peer_combine(*pending, wts["final_norm"])


def _prompt_rows(x, mod, wts):
    b, l, d = x.shape
    ahead = 2
    conv0 = jnp.zeros((1, CONV_W - 1, QKV_A), F32)
    s0 = jnp.zeros((1, GDN_HEADS, GDN_HEAD_DIM, GDN_HEAD_DIM), F32)

    def after(value, dep):
        return lax.optimization_barrier((value, dep))[0]

    def front(i, dep):
        xi = after(lax.dynamic_slice_in_dim(x, i, 1, 0), dep)
        mi = lax.dynamic_slice_in_dim(mod, i, 1, 0)
        return _front(xi, mi, conv0, s0, None, None, wts, True)

    def back(pend, dep):
        ug, vg, gate, h2, x1, mi = pend
        return _back((ug, vg, after(gate, dep), h2, x1, mi), wts)

    pending, news, ys, dep = {}, [], [], mod
    for t in range(b + ahead):
        if t < b:
            pending[t], new = front(t, dep)
            news.append(new)
            dep = pending[t][2]
        if t >= ahead:
            dep = back(pending.pop(t - ahead), dep)
            ys.append(dep)
    return jnp.concatenate(ys, axis=0), [jnp.concatenate([n[k] for n in news], axis=0) for k in range(4)]


def _cache_entries(new, b, l):
    kb, vb, s_new, conv_new = new
    return (kb.reshape(1, b, l, DIFF_HEADS, 2 * DIFF_HEAD_DIM), vb.reshape(1, b, l, DIFF_HEADS, DIFF_V_DIM),
            s_new[None], conv_new[None])


def kernel(x_prompt, x_sample, c_prompt, c_sample, cache_k, cache_v, state_gdn, state_conv, w_ada, b_ada,
           norm1, norm2, w_in, conv_w, a_log, dt_bias, gdn_onorm, lam_q1, lam_k1, lam_q2, lam_k2, diff_subln,
           w_out, peer_wq, peer_keys, peer_u, peer_v, rel_table, final_norm):
    assert w_ada.shape[0] == 1, "single-layer step"
    bp = x_prompt.shape[0]
    d = D_MODEL
    lam_init = 0.8 - 0.6 * math.exp(-0.3 * 0)
    lam = (jnp.exp(jnp.sum(lam_q1[0] * lam_k1[0])) - jnp.exp(jnp.sum(lam_q2[0] * lam_k2[0])) + lam_init)
    w = w_in[0]
    w_packed = jnp.concatenate(
        [w[:, :_C_AB], jnp.pad(w[:, 2048:2056], ((0, 0), (0, LANES - 2 * GDN_HEADS))), w[:, 2056:]],
        axis=1).astype(BF16)
    wts = dict(
        norm1=norm1[0], norm2=norm2[0], w_in=w_packed, conv_w=conv_w[0],
        alog=jnp.pad(a_log[0], (0, LANES - GDN_HEADS)).reshape(1, LANES),
        dtb=jnp.pad(dt_bias[0], (0, LANES - GDN_HEADS)).reshape(1, LANES),
        onorm=gdn_onorm[0], lam=jnp.full((1, DIFF_V_DIM), lam, F32), subln=diff_subln[0],
        out_scale=1.0 - lam_init, bias_prompt=_prompt_bias_tiles(rel_table), rel_table=rel_table,
        w_out=w_out[0].astype(BF16), w_q=peer_wq[0].astype(BF16),
        keys=peer_keys[0].reshape(2 * PEER_HEADS, N_KEYS, PEER_HALF).astype(BF16),
        peer_u=_pack_bf16_halves(peer_u[0]), peer_v=_pack_bf16_halves(peer_v[0]), final_norm=final_norm)

    mod = _ada(jnp.concatenate([c_prompt, c_sample], axis=0), w_ada[0], b_ada[0]).reshape(-1, 6, d)
    pend_s, new_s = _front(x_sample, mod[bp:], state_conv[0], state_gdn[0], cache_k[0], cache_v[0], wts, False)
    yp, new_p = _prompt_rows(x_prompt, mod[:bp], wts)
    ys = _back(pend_s, wts)
    kp, vp, sp, cp = _cache_entries(new_p, *x_prompt.shape[:2])
    ks, vs, ss, cs = _cache_entries(new_s, *x_sample.shape[:2])
    return yp, ys, kp, vp, sp, cp, ks, vs, ss, cs
```

```python
import functools
import math

import jax
import jax.numpy as jnp
from jax import lax
from jax.experimental import pallas as pl
from jax.experimental.pallas import tpu as pltpu
from jax.experimental.pallas import tpu_sc as plsc

F32 = jnp.float32
BF16 = jnp.bfloat16
EPS = 1e-6

D_MODEL = 1024
CHUNK = 64
GDN_HEADS = 4
GDN_HEAD_DIM = 128
GDN_WIDTH = GDN_HEADS * GDN_HEAD_DIM
CONV_W = 4
QKV_A = 3 * GDN_WIDTH
DIFF_HEADS = 4
DIFF_HEAD_DIM = 64
DIFF_V_DIM = 128
DIFF_WIDTH = DIFF_HEADS * 2 * DIFF_HEAD_DIM
ATT_BLOCK = 256
N_BUCKETS = 32
REL_MAX_DIST = 128
PEER_HEADS = 8
N_KEYS = 128
PEER_HALF = 128
PEER_TOPK = 16
PEER_SLOTS = PEER_HEADS * PEER_TOPK
LANES = 128
NEG_BIG = -1e30
VMEM_LIMIT = 56 * 1024 * 1024

_C_QKV, _C_Z, _C_AB, _C_QB, _C_KB, _C_VB = 0, 1536, 2048, 2176, 2688, 3200
_C_END = 3712


def _params(sem):
    return pltpu.CompilerParams(dimension_semantics=sem, vmem_limit_bytes=VMEM_LIMIT)


def _dot(a, b, precision=None):
    return jnp.dot(a, b, preferred_element_type=F32, precision=precision)


def _dot_nt(a, b, precision=None):
    return lax.dot_general(a, b, (((1,), (1,)), ((), ())), preferred_element_type=F32, precision=precision)


def _silu(x):
    return x * jax.nn.sigmoid(x)


def _ada_kernel(c_ref, w_ref, b_ref, o_ref):
    a = _silu(c_ref[...]).astype(BF16)
    o_ref[...] = _dot(a, w_ref[...].astype(BF16)) + b_ref[...]


def _ada(c, w_ada, b_ada):
    n, d = c.shape
    cols = w_ada.shape[1]
    tn = 1024
    return pl.pallas_call(
        _ada_kernel,
        grid=(cols // tn,),
        in_specs=[pl.BlockSpec((n, d), lambda j: (0, 0)),
                  pl.BlockSpec((d, tn), lambda j: (0, j)),
                  pl.BlockSpec((1, tn), lambda j: (0, j))],
        out_specs=pl.BlockSpec((n, tn), lambda j: (0, j)),
        out_shape=jax.ShapeDtypeStruct((n, cols), F32),
        compiler_params=_params(("parallel",)),
        name="ada",
    )(c, w_ada, b_ada.reshape(1, cols))


def _modulated_norm(x, gain, shift, scale):
    y = x * lax.rsqrt(jnp.mean(x * x, axis=-1, keepdims=True) + EPS)
    return (y * gain) * (1.0 + scale) + shift


def _inproj_kernel(x_ref, mod_ref, n1_ref, w_ref, qkv_ref, z_ref, ab_ref, qb_ref, kb_ref, vb_ref, kh_ref, vh_ref):
    h = _modulated_norm(x_ref[0], n1_ref[...], mod_ref[0, 0:1, :], mod_ref[0, 1:2, :]).astype(BF16)
    qkv_ref[0] = _dot(h, w_ref[:, _C_QKV:_C_Z])
    z_ref[0] = _dot(h, w_ref[:, _C_Z:_C_AB])
    ab_ref[0] = _dot(h, w_ref[:, _C_AB:_C_QB])
    qb_ref[0] = _dot(h, w_ref[:, _C_QB:_C_KB])
    kb = _dot(h, w_ref[:, _C_KB:_C_VB])
    vb = _dot(h, w_ref[:, _C_VB:_C_END])
    kb_ref[0] = kb
    vb_ref[0] = vb
    kh_ref[0] = kb.astype(BF16)
    vh_ref[0] = vb.astype(BF16)


def _inproj(x, mod, norm1, w_packed):
    b, l, d = x.shape
    tl = min(l, 256)
    widths = (QKV_A, GDN_WIDTH, LANES, DIFF_WIDTH, DIFF_WIDTH, DIFF_WIDTH, DIFF_WIDTH, DIFF_WIDTH)
    dtypes = (F32,) * 6 + (BF16,) * 2
    return pl.pallas_call(
        _inproj_kernel,
        grid=(b, l // tl),
        in_specs=[pl.BlockSpec((1, tl, d), lambda i, j: (i, j, 0)),
                  pl.BlockSpec((1, 6, d), lambda i, j: (i, 0, 0)),
                  pl.BlockSpec((1, d), lambda i, j: (0, 0)),
                  pl.BlockSpec((d, _C_END), lambda i, j: (0, 0))],
        out_specs=[pl.BlockSpec((1, tl, w), lambda i, j: (i, j, 0)) for w in widths],
        out_shape=[jax.ShapeDtypeStruct((b, l, w), dt) for w, dt in zip(widths, dtypes)],
        compiler_params=_params(("parallel", "parallel")),
        name="inproj",
    )(x, mod, norm1.reshape(1, d), w_packed)


_HI = lax.Precision.HIGHEST


def _unit_lower_inverses(mats, n):
    r = lax.broadcasted_iota(jnp.int32, (n, n), 0)
    c = lax.broadcasted_iota(jnp.int32, (n, n), 1)
    eye = (r == c).astype(F32)
    ad = [jnp.where((r // 8) == (c // 8), a, 0.0) for a in mats]
    a2 = [_dot(m, m, _HI) for m in ad]
    a4 = [_dot(m, m, _HI) for m in a2]
    xs = [eye - m for m in ad]
    xs = [x + _dot(x, m, _HI) for x, m in zip(xs, a2)]
    xs = [x + _dot(x, m, _HI) for x, m in zip(xs, a4)]
    bs = 8
    while bs < n:
        off = ((r // (2 * bs)) == (c // (2 * bs))) & ((r // bs) != (c // bs))
        ys = [_dot(jnp.where(off, a, 0.0), x, _HI) for a, x in zip(mats, xs)]
        xs = [x - _dot(x, y, _HI) for x, y in zip(xs, ys)]
        bs *= 2
    return xs


def _mm(a, b):
    return _dot(a.astype(BF16), b.astype(BF16))


def _mm_nt(a, b):
    return _dot_nt(a.astype(BF16), b.astype(BF16))


def _gdn_kernel(qkv_ref, z_ref, ab_ref, conv0_ref, s0_ref, cw_ref, alog_ref, dtb_ref, onorm_ref,
                o_ref, s_ref, xbuf, *, chunk, nch):
    hd = GDN_HEAD_DIM
    rows = chunk * nch

    @pl.when(pl.program_id(1) == 0)
    def _():
        xbuf[5:8, :] = conv0_ref[0]
        s_ref[0] = s0_ref[0]

    x = qkv_ref[0]
    xbuf[8:8 + rows, :] = x
    y = (xbuf[5:5 + rows, :] * cw_ref[0:1, :] + xbuf[6:6 + rows, :] * cw_ref[1:2, :]
         + xbuf[7:7 + rows, :] * cw_ref[2:3, :] + x * cw_ref[3:4, :])
    xbuf[5:8, :] = x[rows - 3:rows, :]
    y = _silu(y)

    ab = ab_ref[0]
    t = ab + dtb_ref[...]
    softplus = jnp.maximum(t, 0.0) + jnp.log(1.0 + jnp.exp(-jnp.abs(t)))
    g = -jnp.exp(alog_ref[...]) * softplus
    beta = jax.nn.sigmoid(ab)

    r = lax.broadcasted_iota(jnp.int32, (chunk, chunk), 0)
    c = lax.broadcasted_iota(jnp.int32, (chunk, chunk), 1)
    lower = r >= c
    tri = lower.astype(F32)

    heads = range(GDN_HEADS)
    pairs = [(ci, h) for ci in range(nch) for h in heads]
    rows_of = {ci: slice(ci * chunk, (ci + 1) * chunk) for ci in range(nch)}
    gc = {ci: _dot(tri, g[rows_of[ci]], _HI) for ci in range(nch)}
    gc_t = {ci: gc[ci].T for ci in range(nch)}
    q, k, vb, kb, decay, egc, g_last = {}, {}, {}, {}, {}, {}, {}
    for ci, h in pairs:
        sl = rows_of[ci]
        qh = y[sl, h * hd:(h + 1) * hd]
        kh = y[sl, GDN_WIDTH + h * hd:GDN_WIDTH + (h + 1) * hd]
        p = ci, h
        q[p] = qh * lax.rsqrt(jnp.sum(qh * qh, axis=-1, keepdims=True) + EPS) * (hd ** -0.5)
        k[p] = kh * lax.rsqrt(jnp.sum(kh * kh, axis=-1, keepdims=True) + EPS)
        gcol = gc[ci][:, h:h + 1]
        bcol = beta[sl, GDN_HEADS + h:GDN_HEADS + h + 1]
        decay[p] = jnp.exp(jnp.where(lower, gcol - gc_t[ci][h:h + 1, :], NEG_BIG))
        kb[p] = k[p] * bcol
        vb[p] = y[sl, 2 * GDN_WIDTH + h * hd:2 * GDN_WIDTH + (h + 1) * hd] * bcol
        egc[p] = jnp.exp(gcol)
        g_last[p] = gcol[chunk - 1:chunk, :]
    kk = {p: _mm_nt(kb[p], k[p]) for p in pairs}
    tinv = dict(zip(pairs, _unit_lower_inverses([jnp.where(r > c, kk[p] * decay[p], 0.0) for p in pairs], chunk)))
    u_v = {p: _dot(tinv[p], vb[p], _HI) for p in pairs}
    w = {p: _dot(tinv[p], kb[p] * egc[p], _HI) for p in pairs}
    qk = {p: _mm_nt(q[p], k[p]) * decay[p] for p in pairs}
    k_dec_t = {p: (k[p] * jnp.exp(g_last[p] - gc[p[0]][:, p[1]:p[1] + 1])).T for p in pairs}

    s = [s_ref[0, h] for h in heads]
    for ci in range(nch):
        ws = [_mm(w[ci, h], s[h]) for h in heads]
        qs = [_mm(q[ci, h] * egc[ci, h], s[h]) for h in heads]
        v_new = [u_v[ci, h] - ws[h] for h in heads]
        o = [qs[h] + _mm(qk[ci, h], v_new[h]) for h in heads]
        s = [s[h] * jnp.exp(g_last[ci, h]) + _mm(k_dec_t[ci, h], v_new[h]) for h in heads]
        for h in heads:
            oh = o[h] * lax.rsqrt(jnp.mean(o[h] * o[h], axis=-1, keepdims=True) + EPS) * onorm_ref[...]
            o_ref[0, rows_of[ci], h * hd:(h + 1) * hd] = oh * _silu(z_ref[0, rows_of[ci], h * hd:(h + 1) * hd])
    for h in heads:
        s_ref[0, h] = s[h]


def _gdn(qkv, z, ab, conv0, s0, conv_w, alog_pad, dtb_pad, onorm, chunk):
    b, l, _ = qkv.shape
    hd = GDN_HEAD_DIM
    nch = 2 if l % (2 * chunk) == 0 else 1
    rows = chunk * nch
    return pl.pallas_call(
        functools.partial(_gdn_kernel, chunk=chunk, nch=nch),
        grid=(b, l // rows),
        in_specs=[pl.BlockSpec((1, rows, QKV_A), lambda i, j: (i, j, 0)),
                  pl.BlockSpec((1, rows, GDN_WIDTH), lambda i, j: (i, j, 0)),
                  pl.BlockSpec((1, rows, LANES), lambda i, j: (i, j, 0)),
                  pl.BlockSpec((1, CONV_W - 1, QKV_A), lambda i, j: (i, 0, 0)),
                  pl.BlockSpec((1, GDN_HEADS, hd, hd), lambda i, j: (i, 0, 0, 0)),
                  pl.BlockSpec((CONV_W, QKV_A), lambda i, j: (0, 0)),
                  pl.BlockSpec((1, LANES), lambda i, j: (0, 0)),
                  pl.BlockSpec((1, LANES), lambda i, j: (0, 0)),
                  pl.BlockSpec((1, hd), lambda i, j: (0, 0))],
        out_specs=[pl.BlockSpec((1, rows, GDN_WIDTH), lambda i, j: (i, j, 0)),
                   pl.BlockSpec((1, GDN_HEADS, hd, hd), lambda i, j: (i, 0, 0, 0))],
        out_shape=[jax.ShapeDtypeStruct((b, l, GDN_WIDTH), F32),
                   jax.ShapeDtypeStruct((b, GDN_HEADS, hd, hd), F32)],
        scratch_shapes=[pltpu.VMEM((8 + rows, QKV_A), F32)],
        compiler_params=_params(("parallel", "arbitrary")),
        name="gdn",
    )(qkv, z, ab, conv0, s0, conv_w, alog_pad, dtb_pad, onorm.reshape(1, hd))


def _rel_bucket(rel):
    nb = N_BUCKETS // 2
    max_exact = nb // 2
    ret = jnp.where(rel > 0, nb, 0)
    n = jnp.abs(rel)
    large = max_exact + (jnp.log(jnp.maximum(n, 1).astype(F32) / max_exact)
                         / math.log(REL_MAX_DIST / max_exact) * (nb - max_exact)).astype(jnp.int32)
    large = jnp.minimum(large, nb - 1)
    return ret + jnp.where(n < max_exact, n, large)


def _diff_finish(o1, o2, lam_ref, subln_ref, out_scale):
    o = o1 - lam_ref[...] * o2
    return o * lax.rsqrt(jnp.mean(o * o, axis=-1, keepdims=True) + EPS) * subln_ref[...] * out_scale


def _attn_prompt_kernel(q_ref, k_ref, v_ref, bias_ref, lam_ref, subln_ref, o_ref, m_ref, l_ref, acc_ref,
                        *, out_scale):
    i = pl.program_id(2)
    tb = ATT_BLOCK
    dh = DIFF_HEAD_DIM
    q = q_ref[0] * (dh ** -0.5)
    lane = lax.broadcasted_iota(jnp.int32, q.shape, 1)
    q2s = jnp.concatenate([jnp.where(lane < dh, q, 0.0), jnp.where(lane >= dh, q, 0.0)], axis=0).astype(BF16)

    def score_tiles(j, tile):
        start = pl.multiple_of(j * tb, tb)
        s = _dot_nt(q2s, k_ref[0, pl.ds(start, tb), :]) + bias_ref[0, tile]
        return [s[:, c:c + LANES] for c in range(0, tb, LANES)]

    def visible_blocks(fn, unroll):
        n_far = jnp.maximum(i - 1, 0)

        @pl.loop(0, n_far // unroll)
        def _(g):
            fn([(unroll * g + u, 0) for u in range(unroll)])

        @pl.loop((n_far // unroll) * unroll, n_far)
        def _(j):
            fn([(j, 0)])

        @pl.when(i > 0)
        def _():
            fn([(i - 1, 1), (i, 2)])

        @pl.when(i == 0)
        def _():
            fn([(i, 2)])

    m_ref[...] = jnp.full(m_ref.shape, NEG_BIG, F32)

    def track_max(blocks):
        tiles = [s for j, tile in blocks for s in score_tiles(j, tile)]
        m_ref[...] = functools.reduce(jnp.maximum, tiles, m_ref[...])

    visible_blocks(track_max, 4)
    m_ref[...] = jnp.broadcast_to(jnp.max(m_ref[...], axis=-1, keepdims=True), m_ref.shape)

    l_ref[...] = jnp.zeros(l_ref.shape, F32)
    acc_ref[...] = jnp.zeros(acc_ref.shape, F32)

    def accumulate(blocks):
        m = m_ref[...]
        l_add, acc_add = [], []
        for j, tile in blocks:
            p = [jnp.exp(s - m) for s in score_tiles(j, tile)]
            l_add.extend(p)
            start = pl.multiple_of(j * tb, tb)
            acc_add.append(_dot(jnp.concatenate(p, axis=-1).astype(BF16), v_ref[0, pl.ds(start, tb), :]))
        l_ref[...] += sum(l_add)
        acc_ref[...] += sum(acc_add)

    visible_blocks(accumulate, 4)
    o = acc_ref[...] / jnp.sum(l_ref[...], axis=-1, keepdims=True)
    o_ref[0] = _diff_finish(o[:tb], o[tb:], lam_ref, subln_ref, out_scale)


def _attn_prompt(qb, kh, vh, bias_tiles, lam_row, subln, out_scale):
    b, l, _ = qb.shape
    tb = ATT_BLOCK
    hw = 2 * DIFF_HEAD_DIM
    return pl.pallas_call(
        functools.partial(_attn_prompt_kernel, out_scale=out_scale),
        grid=(b, DIFF_HEADS, l // tb),
        in_specs=[pl.BlockSpec((1, tb, hw), lambda bi, h, i: (bi, i, h)),
                  pl.BlockSpec((1, l, hw), lambda bi, h, i: (bi, 0, h)),
                  pl.BlockSpec((1, l, DIFF_V_DIM), lambda bi, h, i: (bi, 0, h)),
                  pl.BlockSpec((1, 3, 2 * tb, tb), lambda bi, h, i: (h, 0, 0, 0)),
                  pl.BlockSpec((1, DIFF_V_DIM), lambda bi, h, i: (0, 0)),
                  pl.BlockSpec((1, DIFF_V_DIM), lambda bi, h, i: (0, 0))],
        out_specs=pl.BlockSpec((1, tb, DIFF_V_DIM), lambda bi, h, i: (bi, i, h)),
        out_shape=jax.ShapeDtypeStruct((b, l, DIFF_WIDTH), F32),
        scratch_shapes=[pltpu.VMEM((2 * tb, LANES), F32), pltpu.VMEM((2 * tb, LANES), F32),
                        pltpu.VMEM((2 * tb, DIFF_V_DIM), F32)],
        compiler_params=_params(("parallel", "parallel", "arbitrary")),
        name="attn_prompt",
    )(qb, kh, vh, bias_tiles, lam_row, subln.reshape(1, DIFF_V_DIM))


def _prompt_bias_tiles(rel_table):
    tb = ATT_BLOCK
    qi = jnp.arange(tb)[:, None]
    ki = jnp.arange(tb)[None, :]
    far = jnp.broadcast_to(rel_table[_rel_bucket(jnp.array(-2 * tb))], (tb, tb, DIFF_HEADS))
    prev = rel_table[_rel_bucket(ki - qi - tb)]
    diag = jnp.where(((ki // CHUNK) <= (qi // CHUNK))[..., None], rel_table[_rel_bucket(ki - qi)], NEG_BIG)
    tiles = jnp.moveaxis(jnp.stack([far, prev, diag]).astype(F32), -1, 0)
    return jnp.concatenate([tiles, tiles], axis=2)


def _attn_sample_kernel(q_ref, kp_ref, vp_ref, kn_ref, vn_ref, bp_ref, bn_ref, lam_ref, subln_ref, o_ref,
                        *, out_scale):
    dh = DIFF_HEAD_DIM
    q = q_ref[0] * (dh ** -0.5)
    kp = kp_ref[0].astype(BF16)
    kn = kn_ref[0].astype(BF16)
    vp = vp_ref[0].astype(BF16)
    vn = vn_ref[0].astype(BF16)
    outs = []
    for t in range(2):
        qt = q[:, t * dh:(t + 1) * dh].astype(BF16)
        sp = _dot_nt(qt, kp[:, t * dh:(t + 1) * dh]) + bp_ref[0]
        sn = _dot_nt(qt, kn[:, t * dh:(t + 1) * dh]) + bn_ref[0]
        m = jnp.maximum(jnp.max(sp, axis=-1, keepdims=True), jnp.max(sn, axis=-1, keepdims=True))
        pp = jnp.exp(sp - m)
        pn = jnp.exp(sn - m)
        den = jnp.sum(pp, axis=-1, keepdims=True) + jnp.sum(pn, axis=-1, keepdims=True)
        outs.append((_dot(pp.astype(BF16), vp) + _dot(pn.astype(BF16), vn)) / den)
    o_ref[0] = _diff_finish(outs[0], outs[1], lam_ref, subln_ref, out_scale)


def _attn_sample(qb, k_past, v_past, k_new, v_new, bias_past, bias_new, lam_row, subln, out_scale):
    b, l, _ = qb.shape
    p = k_past.shape[1]
    hw = 2 * DIFF_HEAD_DIM
    return pl.pallas_call(
        functools.partial(_attn_sample_kernel, out_scale=out_scale),
        grid=(b, DIFF_HEADS),
        in_specs=[pl.BlockSpec((1, l, hw), lambda bi, h: (bi, 0, h)),
                  pl.BlockSpec((1, p, hw), lambda bi, h: (bi, 0, h)),
                  pl.BlockSpec((1, p, DIFF_V_DIM), lambda bi, h: (bi, 0, h)),
                  pl.BlockSpec((1, l, hw), lambda bi, h: (bi, 0, h)),
                  pl.BlockSpec((1, l, DIFF_V_DIM), lambda bi, h: (bi, 0, h)),
                  pl.BlockSpec((1, l, p), lambda bi, h: (h, 0, 0)),
                  pl.BlockSpec((1, l, l), lambda bi, h: (h, 0, 0)),
                  pl.BlockSpec((1, DIFF_V_DIM), lambda bi, h: (0, 0)),
                  pl.BlockSpec((1, DIFF_V_DIM), lambda bi, h: (0, 0))],
        out_specs=pl.BlockSpec((1, l, DIFF_V_DIM), lambda bi, h: (bi, 0, h)),
        out_shape=jax.ShapeDtypeStruct((b, l, DIFF_WIDTH), F32),
        compiler_params=_params(("parallel", "parallel")),
        name="attn_sample",
    )(qb, k_past, v_past, k_new, v_new, bias_past, bias_new, lam_row, subln.reshape(1, DIFF_V_DIM))


def _sample_bias(rel_table, p, l):
    rel = jnp.arange(-(p + l - 1), l)
    by_rel = rel_table[_rel_bucket(rel)].astype(F32).T
    bias = jnp.stack([lax.slice_in_dim(by_rel, l - 1 - i, p + 2 * l - 1 - i, axis=1) for i in range(l)], axis=1)
    return bias[:, :, :p], bias[:, :, p:]


def _outproj_kernel(oa_ref, ob_ref, x_ref, mod_ref, n2_ref, wo_ref, wq_ref, keys_ref,
                    x1_ref, h2_ref, sc_ref):
    mixed = jnp.concatenate([oa_ref[0], ob_ref[0]], axis=-1).astype(BF16)
    x1 = x_ref[0] + mod_ref[0, 2:3, :] * _dot(mixed, wo_ref[...])
    x1_ref[0] = x1
    h2 = _modulated_norm(x1, n2_ref[...], mod_ref[0, 3:4, :], mod_ref[0, 4:5, :])
    h2_ref[0] = h2
    qh = _dot(h2.astype(BF16), wq_ref[...]).astype(BF16)
    for hp in range(2 * PEER_HEADS):
        sc_ref[0, hp] = _dot_nt(keys_ref[hp], qh[:, hp * PEER_HALF:(hp + 1) * PEER_HALF])


def _outproj(o_a, o_b, x, mod, norm2, w_out, w_q, keys):
    b, l, d = x.shape
    tl = min(l, 256)
    nhp = 2 * PEER_HEADS
    return pl.pallas_call(
        _outproj_kernel,
        grid=(b, l // tl),
        in_specs=[pl.BlockSpec((1, tl, GDN_WIDTH), lambda i, j: (i, j, 0)),
                  pl.BlockSpec((1, tl, DIFF_WIDTH), lambda i, j: (i, j, 0)),
                  pl.BlockSpec((1, tl, d), lambda i, j: (i, j, 0)),
                  pl.BlockSpec((1, 6, d), lambda i, j: (i, 0, 0)),
                  pl.BlockSpec((1, d), lambda i, j: (0, 0)),
                  pl.BlockSpec((d, d), lambda i, j: (0, 0)),
                  pl.BlockSpec((d, nhp * PEER_HALF), lambda i, j: (0, 0)),
                  pl.BlockSpec((nhp, N_KEYS, PEER_HALF), lambda i, j: (0, 0, 0))],
        out_specs=[pl.BlockSpec((1, tl, d), lambda i, j: (i, j, 0)),
                   pl.BlockSpec((1, tl, d), lambda i, j: (i, j, 0)),
                   pl.BlockSpec((1, nhp, N_KEYS, tl), lambda i, j: (i, 0, 0, j))],
        out_shape=[jax.ShapeDtypeStruct((b, l, d), F32),
                   jax.ShapeDtypeStruct((b, l, d), F32),
                   jax.ShapeDtypeStruct((b, nhp, N_KEYS, l), F32)],
        compiler_params=_params(("parallel", "parallel")),
        name="outproj",
    )(o_a, o_b, x, mod, norm2.reshape(1, d), w_out, w_q, keys)


def _top16_rows(s, ids, n):
    vals, idxs = [], []
    for _ in range(PEER_TOPK):
        m = jnp.max(s, axis=0, keepdims=True)
        i = jnp.min(jnp.where(s == m, ids, n), axis=0, keepdims=True)
        vals.append(m)
        idxs.append(i)
        s = jnp.where(ids == i, -jnp.inf, s)
    return jnp.concatenate(vals, axis=0), jnp.concatenate(idxs, axis=0)


def _pair_candidates(s1, s2):
    t = s1.shape[1]
    sub16 = lax.broadcasted_iota(jnp.int32, (PEER_TOPK, t), 0)
    sub8 = sub16[:8]
    cand = [s1[0:1] + s2] + [s1[a:a + 1] + s2[:8] for a in range(1, 8)] + [s1[8:] + s2[0:1]]
    pos = [sub16] + [a * PEER_TOPK + sub8 for a in range(1, 8)] + [(8 + sub8) * PEER_TOPK]
    return jnp.concatenate(cand, axis=0), jnp.concatenate(pos, axis=0)


def _pick_rows(table, sel):
    out = jnp.zeros_like(table)
    for a in range(PEER_TOPK):
        out = jnp.where(sel == a, table[a:a + 1, :], out)
    return out


def _topk_kernel(sc_ref, eidx_ref, gate_ref):
    eidx, gates = [], []
    key_ids = lax.broadcasted_iota(jnp.int32, sc_ref.shape[2:], 0)
    for h in range(PEER_HEADS):
        s1, i1 = _top16_rows(sc_ref[0, 2 * h], key_ids, N_KEYS)
        s2, i2 = _top16_rows(sc_ref[0, 2 * h + 1], key_ids, N_KEYS)
        cand, cand_pos = _pair_candidates(s1, s2)
        top_s, pos = _top16_rows(cand, cand_pos, PEER_TOPK * PEER_TOPK)
        eidx.append(_pick_rows(i1, pos // PEER_TOPK) * N_KEYS + _pick_rows(i2, pos % PEER_TOPK))
        e = jnp.exp(top_s - top_s[0:1, :])
        gates.append(e / jnp.sum(e, axis=0, keepdims=True))
    eidx_ref[...] = jnp.concatenate(eidx, axis=0).T
    gate_ref[...] = jnp.concatenate(gates, axis=0).T


def _topk(scores):
    b, nhp, nk, l = scores.shape
    tt = min(l, LANES)
    nt = l // tt
    return pl.pallas_call(
        _topk_kernel,
        grid=(b, nt),
        in_specs=[pl.BlockSpec((1, nhp, nk, tt), lambda i, j: (i, 0, 0, j))],
        out_specs=[pl.BlockSpec((tt, PEER_SLOTS), lambda i, j: (i * nt + j, 0)),
                   pl.BlockSpec((tt, PEER_SLOTS), lambda i, j: (i * nt + j, 0))],
        out_shape=[jax.ShapeDtypeStruct((b * l, PEER_SLOTS), jnp.int32),
                   jax.ShapeDtypeStruct((b * l, PEER_SLOTS), F32)],
        compiler_params=_params(("parallel", "parallel")),
        name="topk",
    )(scores)


_SC_ROWS = 16
_SC_SLOTS = 4


def _pack_kernel(t_ref, o_ref):
    half = t_ref.shape[2] // 2
    bits = lax.bitcast_convert_type(t_ref[0].astype(BF16).astype(F32), jnp.int32)
    o_ref[...] = lax.shift_right_logical(bits[:, :half], 16) | bits[:, half:]


def _pack_bf16_halves(t):
    _, v, d = t.shape
    tv = 512
    return pl.pallas_call(
        _pack_kernel,
        grid=(v // tv,),
        in_specs=[pl.BlockSpec((1, tv, d), lambda i: (0, i, 0))],
        out_specs=pl.BlockSpec((tv, d // 2), lambda i: (i, 0)),
        out_shape=jax.ShapeDtypeStruct((v, d // 2), jnp.int32),
        compiler_params=_params(("parallel",)),
        name="pack",
    )(t)


def _sc_gather2(tab_u, tab_v, idx):
    n = idx.shape[0]
    w = tab_u.shape[1]
    info = plsc.get_sparse_core_info()
    nw = info.num_cores * info.num_subcores
    per_w = n // nw
    ns, nr = _SC_SLOTS, _SC_ROWS
    ngroups = per_w // (ns * nr)
    assert n % (nw * ns * nr) == 0
    mesh = plsc.VectorSubcoreMesh(core_axis_name="c", subcore_axis_name="s")
    rows = pltpu.VMEM((nr, w), tab_u.dtype)
    out = jax.ShapeDtypeStruct((n, w), tab_u.dtype)

    @functools.partial(
        pl.kernel, mesh=mesh, out_type=[out, out],
        scratch_types=[pltpu.VMEM((per_w,), jnp.int32)] + [rows] * (2 * ns) + [pltpu.SemaphoreType.DMA] * (4 * ns),
    )
    def k(u_hbm, v_hbm, idx_hbm, uo_hbm, vo_hbm, idx_v, *scratch):
        ub, vb = scratch[:ns], scratch[ns:2 * ns]
        sems = scratch[2 * ns:]
        gsu, gsv, wsu, wsv = sems[:ns], sems[ns:2 * ns], sems[2 * ns:3 * ns], sems[3 * ns:]
        wid = lax.axis_index("s") * info.num_cores + lax.axis_index("c")
        base = wid * per_w
        pltpu.sync_copy(idx_hbm.at[pl.ds(pl.multiple_of(base, nr), per_w)], idx_v)

        def gathers(c, s):
            picks = idx_v.at[pl.ds(pl.multiple_of(c * nr, nr), nr)]
            return (pltpu.make_async_copy(u_hbm.at[picks], ub[s], gsu[s]),
                    pltpu.make_async_copy(v_hbm.at[picks], vb[s], gsv[s]))

        def writes(c, s):
            span = pl.ds(pl.multiple_of(base + c * nr, nr), nr)
            return (pltpu.make_async_copy(ub[s], uo_hbm.at[span], wsu[s]),
                    pltpu.make_async_copy(vb[s], vo_hbm.at[span], wsv[s]))

        def start(copies):
            for cp in copies:
                cp.start()

        def wait(copies):
            for cp in copies:
                cp.wait()

        for s in range(ns):
            start(gathers(s, s))

        @pl.loop(0, ngroups)
        def _(g):
            for s in range(ns):
                wait(gathers(g * ns + s, s))
                start(writes(g * ns + s, s))
            for s in range(ns):
                wait(writes(g * ns + s, s))

                @pl.when(g < ngroups - 1)
                def _():
                    start(gathers((g + 1) * ns + s, s))

    return k(tab_u, tab_v, idx)


_PEER_TOKENS = 16


def _unpack_bf16_halves(w):
    lo = lax.bitcast_convert_type(w << 16, F32)
    hi = lax.bitcast_convert_type(w & jnp.int32(-65536), F32)
    return lo, hi


def _peer_kernel(ug_ref, vg_ref, gate_ref, h2_ref, x1_ref, g2_ref, fn_ref, y_ref):
    half = D_MODEL // 2
    gate_t = gate_ref[...].T
    rows = []
    for t in range(_PEER_TOKENS):
        u_lo, u_hi = _unpack_bf16_halves(ug_ref[t * PEER_SLOTS:(t + 1) * PEER_SLOTS, :])
        pre = jnp.sum(u_lo * h2_ref[t:t + 1, :half] + u_hi * h2_ref[t:t + 1, half:], axis=-1, keepdims=True)
        act = 0.5 * pre * (1.0 + lax.erf(pre * (2.0 ** -0.5)))
        coef = gate_t[:, t:t + 1] * act
        v_lo, v_hi = _unpack_bf16_halves(vg_ref[t * PEER_SLOTS:(t + 1) * PEER_SLOTS, :])
        rows.append(jnp.concatenate([jnp.sum(coef * v_lo, axis=0, keepdims=True),
                                     jnp.sum(coef * v_hi, axis=0, keepdims=True)], axis=-1))
    x2 = x1_ref[...] + g2_ref[0, 5:6, :] * jnp.concatenate(rows, axis=0)
    y_ref[...] = x2 * lax.rsqrt(jnp.mean(x2 * x2, axis=-1, keepdims=True) + EPS) * fn_ref[...]


def _peer_combine(ug, vg, gate, h2, x1, mod, final_norm):
    b, l, d = h2.shape
    tp = _PEER_TOKENS
    steps_per_row = l // tp
    return pl.pallas_call(
        _peer_kernel,
        grid=(b * steps_per_row,),
        in_specs=[pl.BlockSpec((tp * PEER_SLOTS, d // 2), lambda i: (i, 0)),
                  pl.BlockSpec((tp * PEER_SLOTS, d // 2), lambda i: (i, 0)),
                  pl.BlockSpec((tp, PEER_SLOTS), lambda i: (i, 0)),
                  pl.BlockSpec((tp, d), lambda i: (i, 0)),
                  pl.BlockSpec((tp, d), lambda i: (i, 0)),
                  pl.BlockSpec((1, 6, d), lambda i: (i // steps_per_row, 0, 0)),
                  pl.BlockSpec((1, d), lambda i: (0, 0))],
        out_specs=pl.BlockSpec((tp, d), lambda i: (i, 0)),
        out_shape=jax.ShapeDtypeStruct((b * l, d), F32),
        compiler_params=_params(("parallel",)),
        name="peer",
    )(ug, vg, gate, h2.reshape(b * l, d), x1.reshape(b * l, d), mod, final_norm.reshape(1, d)).reshape(b, l, d)


def _front(x, mod, conv0, s0, k_past, v_past, wts, prompt):
    b, l, d = x.shape
    qkv, z, ab, qb, kb, vb, kh, vh = _inproj(x, mod, wts["norm1"], wts["w_in"])
    chunk = CHUNK if prompt else l
    o_a, s_new = _gdn(qkv, z, ab, conv0, s0, wts["conv_w"], wts["alog"], wts["dtb"], wts["onorm"], chunk)
    conv_new = qkv[:, l - (CONV_W - 1):, :]
    if prompt:
        o_b = _attn_prompt(qb, kh, vh, wts["bias_prompt"], wts["lam"], wts["subln"], wts["out_scale"])
    else:
        p = k_past.shape[1]
        bias_past, bias_new = _sample_bias(wts["rel_table"], p, l)
        o_b = _attn_sample(qb, k_past.reshape(b, p, DIFF_WIDTH), v_past.reshape(b, p, DIFF_WIDTH), kb, vb,
                           bias_past, bias_new, wts["lam"], wts["subln"], wts["out_scale"])
    x1, h2, scores = _outproj(o_a, o_b, x, mod, wts["norm2"], wts["w_out"], wts["w_q"], wts["keys"])
    eidx, gate = _topk(scores)
    ug, vg = _sc_gather2(wts["peer_u"], wts["peer_v"], eidx.reshape(-1))
    return (ug, vg, gate, h2, x1, mod), (kb, vb, s_new, conv_new)


def _back(pending, wts):
    return _peer_combine(*pending, wts["final_norm"])


def _prompt_rows(x, mod, wts):
    b, l, d = x.shape
    ahead = 2
    conv0 = jnp.zeros((1, CONV_W - 1, QKV_A), F32)
    s0 = jnp.zeros((1, GDN_HEADS, GDN_HEAD_DIM, GDN_HEAD_DIM), F32)

    def tie(a, b_):
        return lax.optimization_barrier((a, b_))

    pending, news, ys = {}, [], []
    for t in range(b + ahead):
        if t < b:
            xi = lax.dynamic_slice_in_dim(x, t, 1, 0)
            if ys:
                xi, ys[-1] = tie(xi, ys[-1])
            pend, new = _front(xi, lax.dynamic_slice_in_dim(mod, t, 1, 0), conv0, s0, None, None, wts, True)
            pending[t] = list(pend)
            news.append(new)
        if t >= ahead:
            pend = pending.pop(t - ahead)
            if t < b:
                pend[2], pending[t][2] = tie(pend[2], pending[t][2])
            ys.append(_back(pend, wts))
    return jnp.concatenate(ys, axis=0), [jnp.concatenate([n[k] for n in news], axis=0) for k in range(4)]


def _cache_entries(new, b, l):
    kb, vb, s_new, conv_new = new
    return (kb.reshape(1, b, l, DIFF_HEADS, 2 * DIFF_HEAD_DIM), vb.reshape(1, b, l, DIFF_HEADS, DIFF_V_DIM),
            s_new[None], conv_new[None])


def kernel(x_prompt, x_sample, c_prompt, c_sample, cache_k, cache_v, state_gdn, state_conv, w_ada, b_ada,
           norm1, norm2, w_in, conv_w, a_log, dt_bias, gdn_onorm, lam_q1, lam_k1, lam_q2, lam_k2, diff_subln,
           w_out, peer_wq, peer_keys, peer_u, peer_v, rel_table, final_norm):
    assert w_ada.shape[0] == 1, "single-layer step"
    bp = x_prompt.shape[0]
    d = D_MODEL
    lam_init = 0.8 - 0.6 * math.exp(-0.3 * 0)
    lam = (jnp.exp(jnp.sum(lam_q1[0] * lam_k1[0])) - jnp.exp(jnp.sum(lam_q2[0] * lam_k2[0])) + lam_init)
    w = w_in[0]
    w_packed = jnp.concatenate(
        [w[:, :_C_AB], jnp.pad(w[:, 2048:2056], ((0, 0), (0, LANES - 2 * GDN_HEADS))), w[:, 2056:]],
        axis=1).astype(BF16)
    wts = dict(
        norm1=norm1[0], norm2=norm2[0], w_in=w_packed, conv_w=conv_w[0],
        alog=jnp.pad(a_log[0], (0, LANES - GDN_HEADS)).reshape(1, LANES),
        dtb=jnp.pad(dt_bias[0], (0, LANES - GDN_HEADS)).reshape(1, LANES),
        onorm=gdn_onorm[0], lam=jnp.full((1, DIFF_V_DIM), lam, F32), subln=diff_subln[0],
        out_scale=1.0 - lam_init, bias_prompt=_prompt_bias_tiles(rel_table), rel_table=rel_table,
        w_out=w_out[0].astype(BF16), w_q=peer_wq[0].astype(BF16),
        keys=peer_keys[0].reshape(2 * PEER_HEADS, N_KEYS, PEER_HALF).astype(BF16),
        peer_u=_pack_bf16_halves(peer_u), peer_v=_pack_bf16_halves(peer_v), final_norm=final_norm)

    mod = _ada(jnp.concatenate([c_prompt, c_sample], axis=0), w_ada[0], b_ada[0]).reshape(-1, 6, d)
    pend_s, new_s = _front(x_sample, mod[bp:], state_conv[0], state_gdn[0], cache_k[0], cache_v[0], wts, False)
    yp, new_p = _prompt_rows(x_prompt, mod[:bp], wts)
    ys = _back(pend_s, wts)
    kp, vp, sp, cp = _cache_entries(new_p, *x_prompt.shape[:2])
    ks, vs, ss, cs = _cache_entries(new_s, *x_sample.shape[:2])
    return yp, ys, kp, vp, sp, cp, ks, vs, ss, cs
```

```python
import functools
import math

import jax
import jax.numpy as jnp
from jax import lax
from jax.experimental import pallas as pl
from jax.experimental.pallas import tpu as pltpu
from jax.experimental.pallas import tpu_sc as plsc

F32 = jnp.float32
BF16 = jnp.bfloat16
EPS = 1e-6

D_MODEL = 1024
CHUNK = 64
GDN_HEADS = 4
GDN_HEAD_DIM = 128
GDN_WIDTH = GDN_HEADS * GDN_HEAD_DIM
CONV_W = 4
QKV_A = 3 * GDN_WIDTH
DIFF_HEADS = 4
DIFF_HEAD_DIM = 64
DIFF_V_DIM = 128
DIFF_WIDTH = DIFF_HEADS * 2 * DIFF_HEAD_DIM
ATT_BLOCK = 256
N_BUCKETS = 32
REL_MAX_DIST = 128
PEER_HEADS = 8
N_KEYS = 128
PEER_HALF = 128
PEER_TOPK = 16
PEER_SLOTS = PEER_HEADS * PEER_TOPK
LANES = 128
NEG_BIG = -1e30
VMEM_LIMIT = 56 * 1024 * 1024

_C_QKV, _C_Z, _C_AB, _C_QB, _C_KB, _C_VB = 0, 1536, 2048, 2176, 2688, 3200
_C_END = 3712


def _params(sem):
    return pltpu.CompilerParams(dimension_semantics=sem, vmem_limit_bytes=VMEM_LIMIT)


def _dot(a, b, precision=None):
    return jnp.dot(a, b, preferred_element_type=F32, precision=precision)


def _dot_nt(a, b, precision=None):
    return lax.dot_general(a, b, (((1,), (1,)), ((), ())), preferred_element_type=F32, precision=precision)


def _silu(x):
    return x * jax.nn.sigmoid(x)


def _ada_kernel(c_ref, w_ref, b_ref, o_ref):
    a = _silu(c_ref[...]).astype(BF16)
    o_ref[...] = _dot(a, w_ref[...].astype(BF16)) + b_ref[...]


def _ada(c, w_ada, b_ada):
    n, d = c.shape
    cols = w_ada.shape[1]
    tn = 1024
    return pl.pallas_call(
        _ada_kernel,
        grid=(cols // tn,),
        in_specs=[pl.BlockSpec((n, d), lambda j: (0, 0)),
                  pl.BlockSpec((d, tn), lambda j: (0, j)),
                  pl.BlockSpec((1, tn), lambda j: (0, j))],
        out_specs=pl.BlockSpec((n, tn), lambda j: (0, j)),
        out_shape=jax.ShapeDtypeStruct((n, cols), F32),
        compiler_params=_params(("parallel",)),
        name="ada",
    )(c, w_ada, b_ada.reshape(1, cols))


def _modulated_norm(x, gain, shift, scale):
    y = x * lax.rsqrt(jnp.mean(x * x, axis=-1, keepdims=True) + EPS)
    return (y * gain) * (1.0 + scale) + shift


def _inproj_kernel(x_ref, mod_ref, n1_ref, w_ref, qkv_ref, z_ref, ab_ref, qb_ref, kb_ref, vb_ref, kh_ref, vh_ref):
    h = _modulated_norm(x_ref[0], n1_ref[...], mod_ref[0, 0:1, :], mod_ref[0, 1:2, :]).astype(BF16)
    qkv_ref[0] = _dot(h, w_ref[:, _C_QKV:_C_Z])
    z_ref[0] = _dot(h, w_ref[:, _C_Z:_C_AB])
    ab_ref[0] = _dot(h, w_ref[:, _C_AB:_C_QB])
    qb_ref[0] = _dot(h, w_ref[:, _C_QB:_C_KB])
    kb = _dot(h, w_ref[:, _C_KB:_C_VB])
    vb = _dot(h, w_ref[:, _C_VB:_C_END])
    kb_ref[0] = kb
    vb_ref[0] = vb
    kh_ref[0] = kb.astype(BF16)
    vh_ref[0] = vb.astype(BF16)


def _inproj(x, mod, norm1, w_packed):
    b, l, d = x.shape
    tl = min(l, 256)
    widths = (QKV_A, GDN_WIDTH, LANES, DIFF_WIDTH, DIFF_WIDTH, DIFF_WIDTH, DIFF_WIDTH, DIFF_WIDTH)
    dtypes = (F32,) * 6 + (BF16,) * 2
    return pl.pallas_call(
        _inproj_kernel,
        grid=(b, l // tl),
        in_specs=[pl.BlockSpec((1, tl, d), lambda i, j: (i, j, 0)),
                  pl.BlockSpec((1, 6, d), lambda i, j: (i, 0, 0)),
                  pl.BlockSpec((1, d), lambda i, j: (0, 0)),
                  pl.BlockSpec((d, _C_END), lambda i, j: (0, 0))],
        out_specs=[pl.BlockSpec((1, tl, w), lambda i, j: (i, j, 0)) for w in widths],
        out_shape=[jax.ShapeDtypeStruct((b, l, w), dt) for w, dt in zip(widths, dtypes)],
        compiler_params=_params(("parallel", "parallel")),
        name="inproj",
    )(x, mod, norm1.reshape(1, d), w_packed)


_HI = lax.Precision.HIGHEST


def _unit_lower_inverses(mats, n):
    r = lax.broadcasted_iota(jnp.int32, (n, n), 0)
    c = lax.broadcasted_iota(jnp.int32, (n, n), 1)
    eye = (r == c).astype(F32)
    ad = [jnp.where((r // 8) == (c // 8), a, 0.0) for a in mats]
    a2 = [_dot(m, m, _HI) for m in ad]
    a4 = [_dot(m, m, _HI) for m in a2]
    xs = [eye - m for m in ad]
    xs = [x + _dot(x, m, _HI) for x, m in zip(xs, a2)]
    xs = [x + _dot(x, m, _HI) for x, m in zip(xs, a4)]
    bs = 8
    while bs < n:
        off = ((r // (2 * bs)) == (c // (2 * bs))) & ((r // bs) != (c // bs))
        ys = [_dot(jnp.where(off, a, 0.0), x, _HI) for a, x in zip(mats, xs)]
        xs = [x - _dot(x, y, _HI) for x, y in zip(xs, ys)]
        bs *= 2
    return xs


def _mm(a, b):
    return _dot(a.astype(BF16), b.astype(BF16))


def _mm_nt(a, b):
    return _dot_nt(a.astype(BF16), b.astype(BF16))


def _gdn_kernel(qkv_ref, z_ref, ab_ref, conv0_ref, s0_ref, cw_ref, alog_ref, dtb_ref, onorm_ref,
                o_ref, s_ref, xbuf, *, chunk, nch):
    hd = GDN_HEAD_DIM
    rows = chunk * nch

    @pl.when(pl.program_id(1) == 0)
    def _():
        xbuf[5:8, :] = conv0_ref[0]
        s_ref[0] = s0_ref[0]

    x = qkv_ref[0]
    xbuf[8:8 + rows, :] = x
    y = (xbuf[5:5 + rows, :] * cw_ref[0:1, :] + xbuf[6:6 + rows, :] * cw_ref[1:2, :]
         + xbuf[7:7 + rows, :] * cw_ref[2:3, :] + x * cw_ref[3:4, :])
    xbuf[5:8, :] = x[rows - 3:rows, :]
    y = _silu(y)

    ab = ab_ref[0]
    t = ab + dtb_ref[...]
    softplus = jnp.maximum(t, 0.0) + jnp.log(1.0 + jnp.exp(-jnp.abs(t)))
    g = -jnp.exp(alog_ref[...]) * softplus
    beta = jax.nn.sigmoid(ab)

    r = lax.broadcasted_iota(jnp.int32, (chunk, chunk), 0)
    c = lax.broadcasted_iota(jnp.int32, (chunk, chunk), 1)
    lower = r >= c
    tri = lower.astype(F32)

    heads = range(GDN_HEADS)
    pairs = [(ci, h) for ci in range(nch) for h in heads]
    rows_of = {ci: slice(ci * chunk, (ci + 1) * chunk) for ci in range(nch)}
    gc = {ci: _dot(tri, g[rows_of[ci]], _HI) for ci in range(nch)}
    gc_t = {ci: gc[ci].T for ci in range(nch)}
    q, k, vb, kb, decay, egc, g_last = {}, {}, {}, {}, {}, {}, {}
    for ci, h in pairs:
        sl = rows_of[ci]
        qh = y[sl, h * hd:(h + 1) * hd]
        kh = y[sl, GDN_WIDTH + h * hd:GDN_WIDTH + (h + 1) * hd]
        p = ci, h
        q[p] = qh * lax.rsqrt(jnp.sum(qh * qh, axis=-1, keepdims=True) + EPS) * (hd ** -0.5)
        k[p] = kh * lax.rsqrt(jnp.sum(kh * kh, axis=-1, keepdims=True) + EPS)
        gcol = gc[ci][:, h:h + 1]
        bcol = beta[sl, GDN_HEADS + h:GDN_HEADS + h + 1]
        decay[p] = jnp.exp(jnp.where(lower, gcol - gc_t[ci][h:h + 1, :], NEG_BIG))
        kb[p] = k[p] * bcol
        vb[p] = y[sl, 2 * GDN_WIDTH + h * hd:2 * GDN_WIDTH + (h + 1) * hd] * bcol
        egc[p] = jnp.exp(gcol)
        g_last[p] = gcol[chunk - 1:chunk, :]
    kk = {p: _mm_nt(kb[p], k[p]) for p in pairs}
    tinv = dict(zip(pairs, _unit_lower_inverses([jnp.where(r > c, kk[p] * decay[p], 0.0) for p in pairs], chunk)))
    u_v = {p: _dot(tinv[p], vb[p], _HI) for p in pairs}
    w = {p: _dot(tinv[p], kb[p] * egc[p], _HI) for p in pairs}
    qk = {p: _mm_nt(q[p], k[p]) * decay[p] for p in pairs}
    k_dec_t = {p: (k[p] * jnp.exp(g_last[p] - gc[p[0]][:, p[1]:p[1] + 1])).T for p in pairs}

    s = [s_ref[0, h] for h in heads]
    for ci in range(nch):
        ws = [_mm(w[ci, h], s[h]) for h in heads]
        qs = [_mm(q[ci, h] * egc[ci, h], s[h]) for h in heads]
        v_new = [u_v[ci, h] - ws[h] for h in heads]
        o = [qs[h] + _mm(qk[ci, h], v_new[h]) for h in heads]
        s = [s[h] * jnp.exp(g_last[ci, h]) + _mm(k_dec_t[ci, h], v_new[h]) for h in heads]
        for h in heads:
            oh = o[h] * lax.rsqrt(jnp.mean(o[h] * o[h], axis=-1, keepdims=True) + EPS) * onorm_ref[...]
            o_ref[0, rows_of[ci], h * hd:(h + 1) * hd] = oh * _silu(z_ref[0, rows_of[ci], h * hd:(h + 1) * hd])
    for h in heads:
        s_ref[0, h] = s[h]


def _gdn(qkv, z, ab, conv0, s0, conv_w, alog_pad, dtb_pad, onorm, chunk):
    b, l, _ = qkv.shape
    hd = GDN_HEAD_DIM
    nch = 2 if l % (2 * chunk) == 0 else 1
    rows = chunk * nch
    return pl.pallas_call(
        functools.partial(_gdn_kernel, chunk=chunk, nch=nch),
        grid=(b, l // rows),
        in_specs=[pl.BlockSpec((1, rows, QKV_A), lambda i, j: (i, j, 0)),
                  pl.BlockSpec((1, rows, GDN_WIDTH), lambda i, j: (i, j, 0)),
                  pl.BlockSpec((1, rows, LANES), lambda i, j: (i, j, 0)),
                  pl.BlockSpec((1, CONV_W - 1, QKV_A), lambda i, j: (i, 0, 0)),
                  pl.BlockSpec((1, GDN_HEADS, hd, hd), lambda i, j: (i, 0, 0, 0)),
                  pl.BlockSpec((CONV_W, QKV_A), lambda i, j: (0, 0)),
                  pl.BlockSpec((1, LANES), lambda i, j: (0, 0)),
                  pl.BlockSpec((1, LANES), lambda i, j: (0, 0)),
                  pl.BlockSpec((1, hd), lambda i, j: (0, 0))],
        out_specs=[pl.BlockSpec((1, rows, GDN_WIDTH), lambda i, j: (i, j, 0)),
                   pl.BlockSpec((1, GDN_HEADS, hd, hd), lambda i, j: (i, 0, 0, 0))],
        out_shape=[jax.ShapeDtypeStruct((b, l, GDN_WIDTH), F32),
                   jax.ShapeDtypeStruct((b, GDN_HEADS, hd, hd), F32)],
        scratch_shapes=[pltpu.VMEM((8 + rows, QKV_A), F32)],
        compiler_params=_params(("parallel", "arbitrary")),
        name="gdn",
    )(qkv, z, ab, conv0, s0, conv_w, alog_pad, dtb_pad, onorm.reshape(1, hd))


def _rel_bucket(rel):
    nb = N_BUCKETS // 2
    max_exact = nb // 2
    ret = jnp.where(rel > 0, nb, 0)
    n = jnp.abs(rel)
    large = max_exact + (jnp.log(jnp.maximum(n, 1).astype(F32) / max_exact)
                         / math.log(REL_MAX_DIST / max_exact) * (nb - max_exact)).astype(jnp.int32)
    large = jnp.minimum(large, nb - 1)
    return ret + jnp.where(n < max_exact, n, large)


def _diff_finish(o1, o2, lam_ref, subln_ref, out_scale):
    o = o1 - lam_ref[...] * o2
    return o * lax.rsqrt(jnp.mean(o * o, axis=-1, keepdims=True) + EPS) * subln_ref[...] * out_scale


def _attn_prompt_kernel(q_ref, k_ref, v_ref, bias_ref, lam_ref, subln_ref, o_ref, m_ref, l_ref, acc_ref,
                        *, out_scale):
    i = pl.program_id(2)
    tb = ATT_BLOCK
    dh = DIFF_HEAD_DIM
    q = q_ref[0] * (dh ** -0.5)
    lane = lax.broadcasted_iota(jnp.int32, q.shape, 1)
    q2s = jnp.concatenate([jnp.where(lane < dh, q, 0.0), jnp.where(lane >= dh, q, 0.0)], axis=0).astype(BF16)

    def score_tiles(j, tile):
        start = pl.multiple_of(j * tb, tb)
        s = _dot_nt(q2s, k_ref[0, pl.ds(start, tb), :]) + bias_ref[0, tile]
        return [s[:, c:c + LANES] for c in range(0, tb, LANES)]

    def visible_blocks(fn, unroll):
        n_far = jnp.maximum(i - 1, 0)

        @pl.loop(0, n_far // unroll)
        def _(g):
            fn([(unroll * g + u, 0) for u in range(unroll)])

        @pl.loop((n_far // unroll) * unroll, n_far)
        def _(j):
            fn([(j, 0)])

        @pl.when(i > 0)
        def _():
            fn([(i - 1, 1), (i, 2)])

        @pl.when(i == 0)
        def _():
            fn([(i, 2)])

    m_ref[...] = jnp.full(m_ref.shape, NEG_BIG, F32)

    def track_max(blocks):
        tiles = [s for j, tile in blocks for s in score_tiles(j, tile)]
        m_ref[...] = functools.reduce(jnp.maximum, tiles, m_ref[...])

    visible_blocks(track_max, 4)
    m_ref[...] = jnp.broadcast_to(jnp.max(m_ref[...], axis=-1, keepdims=True), m_ref.shape)

    l_ref[...] = jnp.zeros(l_ref.shape, F32)
    acc_ref[...] = jnp.zeros(acc_ref.shape, F32)

    def accumulate(blocks):
        m = m_ref[...]
        l_add, acc_add = [], []
        for j, tile in blocks:
            p = [jnp.exp(s - m) for s in score_tiles(j, tile)]
            l_add.extend(p)
            start = pl.multiple_of(j * tb, tb)
            acc_add.append(_dot(jnp.concatenate(p, axis=-1).astype(BF16), v_ref[0, pl.ds(start, tb), :]))
        l_ref[...] += sum(l_add)
        acc_ref[...] += sum(acc_add)

    visible_blocks(accumulate, 4)
    o = acc_ref[...] / jnp.sum(l_ref[...], axis=-1, keepdims=True)
    o_ref[0] = _diff_finish(o[:tb], o[tb:], lam_ref, subln_ref, out_scale)


def _attn_prompt(qb, kh, vh, bias_tiles, lam_row, subln, out_scale):
    b, l, _ = qb.shape
    tb = ATT_BLOCK
    hw = 2 * DIFF_HEAD_DIM
    return pl.pallas_call(
        functools.partial(_attn_prompt_kernel, out_scale=out_scale),
        grid=(b, DIFF_HEADS, l // tb),
        in_specs=[pl.BlockSpec((1, tb, hw), lambda bi, h, i: (bi, i, h)),
                  pl.BlockSpec((1, l, hw), lambda bi, h, i: (bi, 0, h)),
                  pl.BlockSpec((1, l, DIFF_V_DIM), lambda bi, h, i: (bi, 0, h)),
                  pl.BlockSpec((1, 3, 2 * tb, tb), lambda bi, h, i: (h, 0, 0, 0)),
                  pl.BlockSpec((1, DIFF_V_DIM), lambda bi, h, i: (0, 0)),
                  pl.BlockSpec((1, DIFF_V_DIM), lambda bi, h, i: (0, 0))],
        out_specs=pl.BlockSpec((1, tb, DIFF_V_DIM), lambda bi, h, i: (bi, i, h)),
        out_shape=jax.ShapeDtypeStruct((b, l, DIFF_WIDTH), F32),
        scratch_shapes=[pltpu.VMEM((2 * tb, LANES), F32), pltpu.VMEM((2 * tb, LANES), F32),
                        pltpu.VMEM((2 * tb, DIFF_V_DIM), F32)],
        compiler_params=_params(("parallel", "parallel", "arbitrary")),
        name="attn_prompt",
    )(qb, kh, vh, bias_tiles, lam_row, subln.reshape(1, DIFF_V_DIM))


def _prompt_bias_tiles(rel_table):
    tb = ATT_BLOCK
    qi = jnp.arange(tb)[:, None]
    ki = jnp.arange(tb)[None, :]
    far = jnp.broadcast_to(rel_table[_rel_bucket(jnp.array(-2 * tb))], (tb, tb, DIFF_HEADS))
    prev = rel_table[_rel_bucket(ki - qi - tb)]
    diag = jnp.where(((ki // CHUNK) <= (qi // CHUNK))[..., None], rel_table[_rel_bucket(ki - qi)], NEG_BIG)
    tiles = jnp.moveaxis(jnp.stack([far, prev, diag]).astype(F32), -1, 0)
    return jnp.concatenate([tiles, tiles], axis=2)


def _attn_sample_kernel(q_ref, kp_ref, vp_ref, kn_ref, vn_ref, bp_ref, bn_ref, lam_ref, subln_ref, o_ref,
                        *, out_scale):
    dh = DIFF_HEAD_DIM
    q = q_ref[0] * (dh ** -0.5)
    kp = kp_ref[0].astype(BF16)
    kn = kn_ref[0].astype(BF16)
    vp = vp_ref[0].astype(BF16)
    vn = vn_ref[0].astype(BF16)
    outs = []
    for t in range(2):
        qt = q[:, t * dh:(t + 1) * dh].astype(BF16)
        sp = _dot_nt(qt, kp[:, t * dh:(t + 1) * dh]) + bp_ref[0]
        sn = _dot_nt(qt, kn[:, t * dh:(t + 1) * dh]) + bn_ref[0]
        m = jnp.maximum(jnp.max(sp, axis=-1, keepdims=True), jnp.max(sn, axis=-1, keepdims=True))
        pp = jnp.exp(sp - m)
        pn = jnp.exp(sn - m)
        den = jnp.sum(pp, axis=-1, keepdims=True) + jnp.sum(pn, axis=-1, keepdims=True)
        outs.append((_dot(pp.astype(BF16), vp) + _dot(pn.astype(BF16), vn)) / den)
    o_ref[0] = _diff_finish(outs[0], outs[1], lam_ref, subln_ref, out_scale)


def _attn_sample(qb, k_past, v_past, k_new, v_new, bias_past, bias_new, lam_row, subln, out_scale):
    b, l, _ = qb.shape
    p = k_past.shape[1]
    hw = 2 * DIFF_HEAD_DIM
    return pl.pallas_call(
        functools.partial(_attn_sample_kernel, out_scale=out_scale),
        grid=(b, DIFF_HEADS),
        in_specs=[pl.BlockSpec((1, l, hw), lambda bi, h: (bi, 0, h)),
                  pl.BlockSpec((1, p, hw), lambda bi, h: (bi, 0, h)),
                  pl.BlockSpec((1, p, DIFF_V_DIM), lambda bi, h: (bi, 0, h)),
                  pl.BlockSpec((1, l, hw), lambda bi, h: (bi, 0, h)),
                  pl.BlockSpec((1, l, DIFF_V_DIM), lambda bi, h: (bi, 0, h)),
                  pl.BlockSpec((1, l, p), lambda bi, h: (h, 0, 0)),
                  pl.BlockSpec((1, l, l), lambda bi, h: (h, 0, 0)),
                  pl.BlockSpec((1, DIFF_V_DIM), lambda bi, h: (0, 0)),
                  pl.BlockSpec((1, DIFF_V_DIM), lambda bi, h: (0, 0))],
        out_specs=pl.BlockSpec((1, l, DIFF_V_DIM), lambda bi, h: (bi, 0, h)),
        out_shape=jax.ShapeDtypeStruct((b, l, DIFF_WIDTH), F32),
        compiler_params=_params(("parallel", "parallel")),
        name="attn_sample",
    )(qb, k_past, v_past, k_new, v_new, bias_past, bias_new, lam_row, subln.reshape(1, DIFF_V_DIM))


def _sample_bias(rel_table, p, l):
    rel = jnp.arange(-(p + l - 1), l)
    by_rel = rel_table[_rel_bucket(rel)].astype(F32).T
    bias = jnp.stack([lax.slice_in_dim(by_rel, l - 1 - i, p + 2 * l - 1 - i, axis=1) for i in range(l)], axis=1)
    return bias[:, :, :p], bias[:, :, p:]


def _outproj_kernel(oa_ref, ob_ref, x_ref, mod_ref, n2_ref, wo_ref, wq_ref, keys_ref,
                    x1_ref, h2_ref, sc_ref):
    mixed = jnp.concatenate([oa_ref[0], ob_ref[0]], axis=-1).astype(BF16)
    x1 = x_ref[0] + mod_ref[0, 2:3, :] * _dot(mixed, wo_ref[...])
    x1_ref[0] = x1
    h2 = _modulated_norm(x1, n2_ref[...], mod_ref[0, 3:4, :], mod_ref[0, 4:5, :])
    h2_ref[0] = h2
    qh = _dot(h2.astype(BF16), wq_ref[...]).astype(BF16)
    for hp in range(2 * PEER_HEADS):
        sc_ref[0, hp] = _dot_nt(keys_ref[hp], qh[:, hp * PEER_HALF:(hp + 1) * PEER_HALF])


def _outproj(o_a, o_b, x, mod, norm2, w_out, w_q, keys):
    b, l, d = x.shape
    tl = min(l, 256)
    nhp = 2 * PEER_HEADS
    return pl.pallas_call(
        _outproj_kernel,
        grid=(b, l // tl),
        in_specs=[pl.BlockSpec((1, tl, GDN_WIDTH), lambda i, j: (i, j, 0)),
                  pl.BlockSpec((1, tl, DIFF_WIDTH), lambda i, j: (i, j, 0)),
                  pl.BlockSpec((1, tl, d), lambda i, j: (i, j, 0)),
                  pl.BlockSpec((1, 6, d), lambda i, j: (i, 0, 0)),
                  pl.BlockSpec((1, d), lambda i, j: (0, 0)),
                  pl.BlockSpec((d, d), lambda i, j: (0, 0)),
                  pl.BlockSpec((d, nhp * PEER_HALF), lambda i, j: (0, 0)),
                  pl.BlockSpec((nhp, N_KEYS, PEER_HALF), lambda i, j: (0, 0, 0))],
        out_specs=[pl.BlockSpec((1, tl, d), lambda i, j: (i, j, 0)),
                   pl.BlockSpec((1, tl, d), lambda i, j: (i, j, 0)),
                   pl.BlockSpec((1, nhp, N_KEYS, tl), lambda i, j: (i, 0, 0, j))],
        out_shape=[jax.ShapeDtypeStruct((b, l, d), F32),
                   jax.ShapeDtypeStruct((b, l, d), F32),
                   jax.ShapeDtypeStruct((b, nhp, N_KEYS, l), F32)],
        compiler_params=_params(("parallel", "parallel")),
        name="outproj",
    )(o_a, o_b, x, mod, norm2.reshape(1, d), w_out, w_q, keys)


def _top16_rows(s, ids, n):
    vals, idxs = [], []
    for _ in range(PEER_TOPK):
        m = jnp.max(s, axis=0, keepdims=True)
        i = jnp.min(jnp.where(s == m, ids, n), axis=0, keepdims=True)
        vals.append(m)
        idxs.append(i)
        s = jnp.where(ids == i, -jnp.inf, s)
    return jnp.concatenate(vals, axis=0), jnp.concatenate(idxs, axis=0)


def _pair_candidates(s1, s2):
    t = s1.shape[1]
    sub16 = lax.broadcasted_iota(jnp.int32, (PEER_TOPK, t), 0)
    sub8 = sub16[:8]
    cand = [s1[0:1] + s2] + [s1[a:a + 1] + s2[:8] for a in range(1, 8)] + [s1[8:] + s2[0:1]]
    pos = [sub16] + [a * PEER_TOPK + sub8 for a in range(1, 8)] + [(8 + sub8) * PEER_TOPK]
    return jnp.concatenate(cand, axis=0), jnp.concatenate(pos, axis=0)


def _pick_rows(table, sel):
    out = jnp.zeros_like(table)
    for a in range(PEER_TOPK):
        out = jnp.where(sel == a, table[a:a + 1, :], out)
    return out


def _topk_kernel(sc_ref, eidx_ref, gate_ref):
    eidx, gates = [], []
    key_ids = lax.broadcasted_iota(jnp.int32, sc_ref.shape[2:], 0)
    for h in range(PEER_HEADS):
        s1, i1 = _top16_rows(sc_ref[0, 2 * h], key_ids, N_KEYS)
        s2, i2 = _top16_rows(sc_ref[0, 2 * h + 1], key_ids, N_KEYS)
        cand, cand_pos = _pair_candidates(s1, s2)
        top_s, pos = _top16_rows(cand, cand_pos, PEER_TOPK * PEER_TOPK)
        eidx.append(_pick_rows(i1, pos // PEER_TOPK) * N_KEYS + _pick_rows(i2, pos % PEER_TOPK))
        e = jnp.exp(top_s - top_s[0:1, :])
        gates.append(e / jnp.sum(e, axis=0, keepdims=True))
    eidx_ref[...] = jnp.concatenate(eidx, axis=0).T
    gate_ref[...] = jnp.concatenate(gates, axis=0).T


def _topk(scores):
    b, nhp, nk, l = scores.shape
    tt = min(l, LANES)
    nt = l // tt
    return pl.pallas_call(
        _topk_kernel,
        grid=(b, nt),
        in_specs=[pl.BlockSpec((1, nhp, nk, tt), lambda i, j: (i, 0, 0, j))],
        out_specs=[pl.BlockSpec((tt, PEER_SLOTS), lambda i, j: (i * nt + j, 0)),
                   pl.BlockSpec((tt, PEER_SLOTS), lambda i, j: (i * nt + j, 0))],
        out_shape=[jax.ShapeDtypeStruct((b * l, PEER_SLOTS), jnp.int32),
                   jax.ShapeDtypeStruct((b * l, PEER_SLOTS), F32)],
        compiler_params=_params(("parallel", "parallel")),
        name="topk",
    )(scores)


_SC_ROWS = 32
_SC_SLOTS = 4


def _pack_kernel(t_ref, o_ref):
    half = t_ref.shape[2] // 2
    bits = lax.bitcast_convert_type(t_ref[0].astype(BF16).astype(F32), jnp.int32)
    o_ref[...] = lax.shift_right_logical(bits[:, :half], 16) | bits[:, half:]


def _pack_bf16_halves(t):
    _, v, d = t.shape
    tv = 512
    return pl.pallas_call(
        _pack_kernel,
        grid=(v // tv,),
        in_specs=[pl.BlockSpec((1, tv, d), lambda i: (0, i, 0))],
        out_specs=pl.BlockSpec((tv, d // 2), lambda i: (i, 0)),
        out_shape=jax.ShapeDtypeStruct((v, d // 2), jnp.int32),
        compiler_params=_params(("parallel",)),
        name="pack",
    )(t)


def _sc_mesh():
    info = plsc.get_sparse_core_info()
    mesh = plsc.VectorSubcoreMesh(core_axis_name="c", subcore_axis_name="s")
    return mesh, info.num_cores, info.num_cores * info.num_subcores


def _sc_gather(tab, idx):
    n = idx.shape[0]
    w = tab.shape[1]
    mesh, ncores, nw = _sc_mesh()
    per_w = n // nw
    ns, nr = _SC_SLOTS, _SC_ROWS
    ngroups = per_w // (ns * nr)
    assert n % (nw * ns * nr) == 0

    @functools.partial(
        pl.kernel, mesh=mesh, out_type=jax.ShapeDtypeStruct((n, w), tab.dtype),
        scratch_types=([pltpu.VMEM((per_w,), jnp.int32)] + [pltpu.VMEM((nr, w), tab.dtype)] * ns
                       + [pltpu.SemaphoreType.DMA] * (2 * ns)),
    )
    def k(tab_hbm, idx_hbm, out_hbm, idx_v, *scratch):
        bufs, gsem, wsem = scratch[:ns], scratch[ns:2 * ns], scratch[2 * ns:]
        base = (lax.axis_index("s") * ncores + lax.axis_index("c")) * per_w
        pltpu.sync_copy(idx_hbm.at[pl.ds(pl.multiple_of(base, nr), per_w)], idx_v)

        def gather(c, s):
            picks = idx_v.at[pl.ds(pl.multiple_of(c * nr, nr), nr)]
            return pltpu.make_async_copy(tab_hbm.at[picks], bufs[s], gsem[s])

        def write(c, s):
            span = pl.ds(pl.multiple_of(base + c * nr, nr), nr)
            return pltpu.make_async_copy(bufs[s], out_hbm.at[span], wsem[s])

        for s in range(ns):
            gather(s, s).start()

        @pl.loop(0, ngroups)
        def _(g):
            for s in range(ns):
                gather(g * ns + s, s).wait()
                write(g * ns + s, s).start()
            for s in range(ns):
                write(g * ns + s, s).wait()

                @pl.when(g < ngroups - 1)
                def _():
                    gather((g + 1) * ns + s, s).start()

    return k(tab, idx)


_ACC_ROWS = 64
_ACC_TOKENS = 8


def _sc_accumulate(tab, idx, coef):
    n = idx.shape[0]
    w = tab.shape[1]
    d = 2 * w
    lanes = 16
    mesh, ncores, nw = _sc_mesh()
    ntok = n // PEER_SLOTS
    tok_w = ntok // nw
    per_w = tok_w * PEER_SLOTS
    nchunks = per_w // _ACC_ROWS
    assert ntok % (nw * _ACC_TOKENS) == 0 and PEER_SLOTS == 2 * _ACC_ROWS and w % (4 * lanes) == 0

    @functools.partial(
        pl.kernel, mesh=mesh, out_type=jax.ShapeDtypeStruct((ntok, d), F32),
        scratch_types=[pltpu.VMEM((per_w,), jnp.int32), pltpu.VMEM((per_w,), F32),
                       pltpu.VMEM((_ACC_ROWS, w), tab.dtype), pltpu.VMEM((_ACC_ROWS, w), tab.dtype),
                       pltpu.VMEM((_ACC_TOKENS, d), F32), pltpu.SemaphoreType.DMA, pltpu.SemaphoreType.DMA],
        compiler_params=pltpu.CompilerParams(needs_layout_passes=False),
    )
    def k(tab_hbm, idx_hbm, coef_hbm, out_hbm, idx_v, coef_v, rows0, rows1, acc_v, sem0, sem1):
        rows, sems = (rows0, rows1), (sem0, sem1)
        wid = lax.axis_index("s") * ncores + lax.axis_index("c")
        base = pl.multiple_of(wid * per_w, _ACC_ROWS)
        pltpu.sync_copy(idx_hbm.at[pl.ds(base, per_w)], idx_v)
        pltpu.sync_copy(coef_hbm.at[pl.ds(base, per_w)], coef_v)

        def gather(c, s):
            picks = idx_v.at[pl.ds(pl.multiple_of(c * _ACC_ROWS, _ACC_ROWS), _ACC_ROWS)]
            return pltpu.make_async_copy(tab_hbm.at[picks], rows[s], sems[s])

        def accumulate(rows_ref, c, arow):
            nrow, ncol = 4, 4

            @pl.loop(0, _ACC_ROWS // nrow)
            def _(q):
                r0 = q * nrow
                cvec = [plsc.load_gather(coef_v, [jnp.full((lanes,), c * _ACC_ROWS + r0 + i, jnp.int32)])
                        for i in range(nrow)]
                for col0 in range(0, w, ncol * lanes):
                    cols = [col0 + j * lanes for j in range(ncol)]
                    words = [[rows_ref[r0 + i, pl.ds(col, lanes)] for i in range(nrow)] for col in cols]
                    sums = []
                    for wds in words:
                        lo = [cvec[i] * lax.bitcast_convert_type(wds[i] << 16, F32) for i in range(nrow)]
                        hi = [cvec[i] * lax.bitcast_convert_type(wds[i] & jnp.int32(-65536), F32)
                              for i in range(nrow)]
                        sums.append(((lo[0] + lo[1]) + (lo[2] + lo[3]), (hi[0] + hi[1]) + (hi[2] + hi[3])))
                    for col, (lo, hi) in zip(cols, sums):
                        plsc.addupdate(acc_v.at[arow, pl.ds(col, lanes)], lo)
                        plsc.addupdate(acc_v.at[arow, pl.ds(w + col, lanes)], hi)

        gather(0, 0).start()

        @pl.loop(0, tok_w)
        def _(t):
            arow = t % _ACC_TOKENS
            for col in range(0, d, lanes):
                acc_v[arow, pl.ds(col, lanes)] = jnp.zeros((lanes,), F32)
            for half in range(2):
                c = 2 * t + half

                @pl.when(c + 1 < nchunks)
                def _():
                    gather(c + 1, 1 - half).start()

                gather(c, half).wait()
                accumulate(rows[half], c, arow)

            @pl.when(arow == _ACC_TOKENS - 1)
            def _():
                first_tok = pl.multiple_of(wid * tok_w + t - (_ACC_TOKENS - 1), _ACC_TOKENS)
                pltpu.sync_copy(acc_v, out_hbm.at[pl.ds(first_tok, _ACC_TOKENS)])

    return k(tab, idx, coef)


_PEER_TOKENS = 16


def _unpack_bf16_halves(w):
    lo = lax.bitcast_convert_type(w << 16, F32)
    hi = lax.bitcast_convert_type(w & jnp.int32(-65536), F32)
    return lo, hi


def _coef_kernel(ug_ref, gate_ref, h2_ref, coef_ref):
    half = D_MODEL // 2
    gate_t = gate_ref[...].T
    cols = []
    for t in range(_PEER_TOKENS):
        u_lo, u_hi = _unpack_bf16_halves(ug_ref[t * PEER_SLOTS:(t + 1) * PEER_SLOTS, :])
        pre = jnp.sum(u_lo * h2_ref[t:t + 1, :half] + u_hi * h2_ref[t:t + 1, half:], axis=-1, keepdims=True)
        act = 0.5 * pre * (1.0 + lax.erf(pre * (2.0 ** -0.5)))
        cols.append(gate_t[:, t:t + 1] * act)
    coef_ref[...] = jnp.concatenate(cols, axis=1).T


def _peer_coef(ug, gate, h2):
    n, d = h2.shape
    tp = _PEER_TOKENS
    return pl.pallas_call(
        _coef_kernel,
        grid=(n // tp,),
        in_specs=[pl.BlockSpec((tp * PEER_SLOTS, d // 2), lambda i: (i, 0)),
                  pl.BlockSpec((tp, PEER_SLOTS), lambda i: (i, 0)),
                  pl.BlockSpec((tp, d), lambda i: (i, 0))],
        out_specs=pl.BlockSpec((tp, PEER_SLOTS), lambda i: (i, 0)),
        out_shape=jax.ShapeDtypeStruct((n, PEER_SLOTS), F32),
        compiler_params=_params(("parallel",)),
        name="peer_coef",
    )(ug, gate, h2)


def _final_kernel(acc_ref, x1_ref, g2_ref, fn_ref, y_ref):
    x2 = x1_ref[0] + g2_ref[0, 5:6, :] * acc_ref[0]
    y_ref[0] = x2 * lax.rsqrt(jnp.mean(x2 * x2, axis=-1, keepdims=True) + EPS) * fn_ref[...]


def _peer_final(acc, x1, mod, final_norm):
    b, l, d = x1.shape
    tl = min(l, 512)
    return pl.pallas_call(
        _final_kernel,
        grid=(b, l // tl),
        in_specs=[pl.BlockSpec((1, tl, d), lambda i, j: (i, j, 0)),
                  pl.BlockSpec((1, tl, d), lambda i, j: (i, j, 0)),
                  pl.BlockSpec((1, 6, d), lambda i, j: (i, 0, 0)),
                  pl.BlockSpec((1, d), lambda i, j: (0, 0))],
        out_specs=pl.BlockSpec((1, tl, d), lambda i, j: (i, j, 0)),
        out_shape=jax.ShapeDtypeStruct((b, l, d), F32),
        compiler_params=_params(("parallel", "parallel")),
        name="peer_final",
    )(acc, x1, mod, final_norm.reshape(1, d))


def _front(x, mod, conv0, s0, k_past, v_past, wts, prompt):
    b, l, d = x.shape
    qkv, z, ab, qb, kb, vb, kh, vh = _inproj(x, mod, wts["norm1"], wts["w_in"])
    chunk = CHUNK if prompt else l
    o_a, s_new = _gdn(qkv, z, ab, conv0, s0, wts["conv_w"], wts["alog"], wts["dtb"], wts["onorm"], chunk)
    conv_new = qkv[:, l - (CONV_W - 1):, :]
    if prompt:
        o_b = _attn_prompt(qb, kh, vh, wts["bias_prompt"], wts["lam"], wts["subln"], wts["out_scale"])
    else:
        p = k_past.shape[1]
        bias_past, bias_new = _sample_bias(wts["rel_table"], p, l)
        o_b = _attn_sample(qb, k_past.reshape(b, p, DIFF_WIDTH), v_past.reshape(b, p, DIFF_WIDTH), kb, vb,
                           bias_past, bias_new, wts["lam"], wts["subln"], wts["out_scale"])
    x1, h2, scores = _outproj(o_a, o_b, x, mod, wts["norm2"], wts["w_out"], wts["w_q"], wts["keys"])
    eidx, gate = _topk(scores)
    eidx = eidx.reshape(-1)
    ug = _sc_gather(wts["peer_u"], eidx)
    return (ug, eidx, gate, h2, x1, mod), (kb, vb, s_new, conv_new)


def _back(pending, wts):
    ug, eidx, gate, h2, x1, mod = pending
    b, l, d = h2.shape
    coef = _peer_coef(ug, gate, h2.reshape(b * l, d))
    acc = _sc_accumulate(wts["peer_v"], eidx, coef.reshape(-1))
    return _peer_final(acc.reshape(b, l, d), x1, mod, wts["final_norm"])


def _prompt_rows(x, mod, wts):
    b, l, d = x.shape
    conv0 = jnp.zeros((1, CONV_W - 1, QKV_A), F32)
    s0 = jnp.zeros((1, GDN_HEADS, GDN_HEAD_DIM, GDN_HEAD_DIM), F32)
    news, ys = [], []
    for t in range(b):
        pend, new = _front(lax.dynamic_slice_in_dim(x, t, 1, 0), lax.dynamic_slice_in_dim(mod, t, 1, 0),
                           conv0, s0, None, None, wts, True)
        news.append(new)
        ys.append(_back(pend, wts))
    return jnp.concatenate(ys, axis=0), [jnp.concatenate([n[k] for n in news], axis=0) for k in range(4)]


def _cache_entries(new, b, l):
    kb, vb, s_new, conv_new = new
    return (kb.reshape(1, b, l, DIFF_HEADS, 2 * DIFF_HEAD_DIM), vb.reshape(1, b, l, DIFF_HEADS, DIFF_V_DIM),
            s_new[None], conv_new[None])


def kernel(x_prompt, x_sample, c_prompt, c_sample, cache_k, cache_v, state_gdn, state_conv, w_ada, b_ada,
           norm1, norm2, w_in, conv_w, a_log, dt_bias, gdn_onorm, lam_q1, lam_k1, lam_q2, lam_k2, diff_subln,
           w_out, peer_wq, peer_keys, peer_u, peer_v, rel_table, final_norm):
    assert w_ada.shape[0] == 1, "single-layer step"
    bp = x_prompt.shape[0]
    d = D_MODEL
    lam_init = 0.8 - 0.6 * math.exp(-0.3 * 0)
    lam = (jnp.exp(jnp.sum(lam_q1[0] * lam_k1[0])) - jnp.exp(jnp.sum(lam_q2[0] * lam_k2[0])) + lam_init)
    w = w_in[0]
    w_packed = jnp.concatenate(
        [w[:, :_C_AB], jnp.pad(w[:, 2048:2056], ((0, 0), (0, LANES - 2 * GDN_HEADS))), w[:, 2056:]],
        axis=1).astype(BF16)
    wts = dict(
        norm1=norm1[0], norm2=norm2[0], w_in=w_packed, conv_w=conv_w[0],
        alog=jnp.pad(a_log[0], (0, LANES - GDN_HEADS)).reshape(1, LANES),
        dtb=jnp.pad(dt_bias[0], (0, LANES - GDN_HEADS)).reshape(1, LANES),
        onorm=gdn_onorm[0], lam=jnp.full((1, DIFF_V_DIM), lam, F32), subln=diff_subln[0],
        out_scale=1.0 - lam_init, bias_prompt=_prompt_bias_tiles(rel_table), rel_table=rel_table,
        w_out=w_out[0].astype(BF16), w_q=peer_wq[0].astype(BF16),
        keys=peer_keys[0].reshape(2 * PEER_HEADS, N_KEYS, PEER_HALF).astype(BF16),
        peer_u=_pack_bf16_halves(peer_u), peer_v=_pack_bf16_halves(peer_v), final_norm=final_norm)

    mod = _ada(jnp.concatenate([c_prompt, c_sample], axis=0), w_ada[0], b_ada[0]).reshape(-1, 6, d)
    pend_s, new_s = _front(x_sample, mod[bp:], state_conv[0], state_gdn[0], cache_k[0], cache_v[0], wts, False)
    yp, new_p = _prompt_rows(x_prompt, mod[:bp], wts)
    ys = _back(pend_s, wts)
    kp, vp, sp, cp = _cache_entries(new_p, *x_prompt.shape[:2])
    ks, vs, ss, cs = _cache_entries(new_s, *x_sample.shape[:2])
    return yp, ys, kp, vp, sp, cp, ks, vs, ss, cs
```

```python
import functools
import math

import jax
import jax.numpy as jnp
from jax import lax
from jax.experimental import pallas as pl
from jax.experimental.pallas import tpu as pltpu
from jax.experimental.pallas import tpu_sc as plsc

F32 = jnp.float32
BF16 = jnp.bfloat16
EPS = 1e-6

D_MODEL = 1024
CHUNK = 64
GDN_HEADS = 4
GDN_HEAD_DIM = 128
GDN_WIDTH = GDN_HEADS * GDN_HEAD_DIM
CONV_W = 4
QKV_A = 3 * GDN_WIDTH
DIFF_HEADS = 4
DIFF_HEAD_DIM = 64
DIFF_V_DIM = 128
DIFF_WIDTH = DIFF_HEADS * 2 * DIFF_HEAD_DIM
ATT_BLOCK = 256
N_BUCKETS = 32
REL_MAX_DIST = 128
PEER_HEADS = 8
N_KEYS = 128
PEER_HALF = 128
PEER_TOPK = 16
PEER_SLOTS = PEER_HEADS * PEER_TOPK
LANES = 128
NEG_BIG = -1e30
VMEM_LIMIT = 56 * 1024 * 1024

_C_QKV, _C_Z, _C_AB, _C_QB, _C_KB, _C_VB = 0, 1536, 2048, 2176, 2688, 3200
_C_END = 3712


def _params(sem):
    return pltpu.CompilerParams(dimension_semantics=sem, vmem_limit_bytes=VMEM_LIMIT)


def _dot(a, b, precision=None):
    return jnp.dot(a, b, preferred_element_type=F32, precision=precision)


def _dot_nt(a, b, precision=None):
    return lax.dot_general(a, b, (((1,), (1,)), ((), ())), preferred_element_type=F32, precision=precision)


def _silu(x):
    return x * jax.nn.sigmoid(x)


def _ada_kernel(c_ref, w_ref, b_ref, o_ref):
    a = _silu(c_ref[...]).astype(BF16)
    o_ref[...] = _dot(a, w_ref[...].astype(BF16)) + b_ref[...]


def _ada(c, w_ada, b_ada):
    n, d = c.shape
    cols = w_ada.shape[1]
    tn = 1024
    return pl.pallas_call(
        _ada_kernel,
        grid=(cols // tn,),
        in_specs=[pl.BlockSpec((n, d), lambda j: (0, 0)),
                  pl.BlockSpec((d, tn), lambda j: (0, j)),
                  pl.BlockSpec((1, tn), lambda j: (0, j))],
        out_specs=pl.BlockSpec((n, tn), lambda j: (0, j)),
        out_shape=jax.ShapeDtypeStruct((n, cols), F32),
        compiler_params=_params(("parallel",)),
        name="ada",
    )(c, w_ada, b_ada.reshape(1, cols))


def _modulated_norm(x, gain, shift, scale):
    y = x * lax.rsqrt(jnp.mean(x * x, axis=-1, keepdims=True) + EPS)
    return (y * gain) * (1.0 + scale) + shift


def _inproj_kernel(x_ref, mod_ref, n1_ref, w_ref, qkv_ref, z_ref, ab_ref, qb_ref, kb_ref, vb_ref, kh_ref, vh_ref):
    h = _modulated_norm(x_ref[0], n1_ref[...], mod_ref[0, 0:1, :], mod_ref[0, 1:2, :]).astype(BF16)
    qkv_ref[0] = _dot(h, w_ref[:, _C_QKV:_C_Z])
    z_ref[0] = _dot(h, w_ref[:, _C_Z:_C_AB])
    ab_ref[0] = _dot(h, w_ref[:, _C_AB:_C_QB])
    qb_ref[0] = _dot(h, w_ref[:, _C_QB:_C_KB])
    kb = _dot(h, w_ref[:, _C_KB:_C_VB])
    vb = _dot(h, w_ref[:, _C_VB:_C_END])
    kb_ref[0] = kb
    vb_ref[0] = vb
    kh_ref[0] = kb.astype(BF16)
    vh_ref[0] = vb.astype(BF16)


def _inproj(x, mod, norm1, w_packed):
    b, l, d = x.shape
    tl = min(l, 256)
    widths = (QKV_A, GDN_WIDTH, LANES, DIFF_WIDTH, DIFF_WIDTH, DIFF_WIDTH, DIFF_WIDTH, DIFF_WIDTH)
    dtypes = (F32,) * 6 + (BF16,) * 2
    return pl.pallas_call(
        _inproj_kernel,
        grid=(b, l // tl),
        in_specs=[pl.BlockSpec((1, tl, d), lambda i, j: (i, j, 0)),
                  pl.BlockSpec((1, 6, d), lambda i, j: (i, 0, 0)),
                  pl.BlockSpec((1, d), lambda i, j: (0, 0)),
                  pl.BlockSpec((d, _C_END), lambda i, j: (0, 0))],
        out_specs=[pl.BlockSpec((1, tl, w), lambda i, j: (i, j, 0)) for w in widths],
        out_shape=[jax.ShapeDtypeStruct((b, l, w), dt) for w, dt in zip(widths, dtypes)],
        compiler_params=_params(("parallel", "parallel")),
        name="inproj",
    )(x, mod, norm1.reshape(1, d), w_packed)


_HI = lax.Precision.HIGHEST


def _unit_lower_inverses(mats, n):
    r = lax.broadcasted_iota(jnp.int32, (n, n), 0)
    c = lax.broadcasted_iota(jnp.int32, (n, n), 1)
    eye = (r == c).astype(F32)
    ad = [jnp.where((r // 8) == (c // 8), a, 0.0) for a in mats]
    a2 = [_dot(m, m, _HI) for m in ad]
    a4 = [_dot(m, m, _HI) for m in a2]
    xs = [eye - m for m in ad]
    xs = [x + _dot(x, m, _HI) for x, m in zip(xs, a2)]
    xs = [x + _dot(x, m, _HI) for x, m in zip(xs, a4)]
    bs = 8
    while bs < n:
        off = ((r // (2 * bs)) == (c // (2 * bs))) & ((r // bs) != (c // bs))
        ys = [_dot(jnp.where(off, a, 0.0), x, _HI) for a, x in zip(mats, xs)]
        xs = [x - _dot(x, y, _HI) for x, y in zip(xs, ys)]
        bs *= 2
    return xs


def _mm(a, b):
    return _dot(a.astype(BF16), b.astype(BF16))


def _mm_nt(a, b):
    return _dot_nt(a.astype(BF16), b.astype(BF16))


def _gdn_kernel(qkv_ref, z_ref, ab_ref, conv0_ref, s0_ref, cw_ref, alog_ref, dtb_ref, onorm_ref,
                o_ref, s_ref, xbuf, *, chunk, nch):
    hd = GDN_HEAD_DIM
    rows = chunk * nch

    @pl.when(pl.program_id(1) == 0)
    def _():
        xbuf[5:8, :] = conv0_ref[0]
        s_ref[0] = s0_ref[0]

    x = qkv_ref[0]
    xbuf[8:8 + rows, :] = x
    y = (xbuf[5:5 + rows, :] * cw_ref[0:1, :] + xbuf[6:6 + rows, :] * cw_ref[1:2, :]
         + xbuf[7:7 + rows, :] * cw_ref[2:3, :] + x * cw_ref[3:4, :])
    xbuf[5:8, :] = x[rows - 3:rows, :]
    y = _silu(y)

    ab = ab_ref[0]
    t = ab + dtb_ref[...]
    softplus = jnp.maximum(t, 0.0) + jnp.log(1.0 + jnp.exp(-jnp.abs(t)))
    g = -jnp.exp(alog_ref[...]) * softplus
    beta = jax.nn.sigmoid(ab)

    r = lax.broadcasted_iota(jnp.int32, (chunk, chunk), 0)
    c = lax.broadcasted_iota(jnp.int32, (chunk, chunk), 1)
    lower = r >= c
    tri = lower.astype(F32)

    heads = range(GDN_HEADS)
    pairs = [(ci, h) for ci in range(nch) for h in heads]
    rows_of = {ci: slice(ci * chunk, (ci + 1) * chunk) for ci in range(nch)}
    gc = {ci: _dot(tri, g[rows_of[ci]], _HI) for ci in range(nch)}
    gc_t = {ci: gc[ci].T for ci in range(nch)}
    q, k, vb, kb, decay, egc, g_last = {}, {}, {}, {}, {}, {}, {}
    for ci, h in pairs:
        sl = rows_of[ci]
        qh = y[sl, h * hd:(h + 1) * hd]
        kh = y[sl, GDN_WIDTH + h * hd:GDN_WIDTH + (h + 1) * hd]
        p = ci, h
        q[p] = qh * lax.rsqrt(jnp.sum(qh * qh, axis=-1, keepdims=True) + EPS) * (hd ** -0.5)
        k[p] = kh * lax.rsqrt(jnp.sum(kh * kh, axis=-1, keepdims=True) + EPS)
        gcol = gc[ci][:, h:h + 1]
        bcol = beta[sl, GDN_HEADS + h:GDN_HEADS + h + 1]
        decay[p] = jnp.exp(jnp.where(lower, gcol - gc_t[ci][h:h + 1, :], NEG_BIG))
        kb[p] = k[p] * bcol
        vb[p] = y[sl, 2 * GDN_WIDTH + h * hd:2 * GDN_WIDTH + (h + 1) * hd] * bcol
        egc[p] = jnp.exp(gcol)
        g_last[p] = gcol[chunk - 1:chunk, :]
    kk = {p: _mm_nt(kb[p], k[p]) for p in pairs}
    tinv = dict(zip(pairs, _unit_lower_inverses([jnp.where(r > c, kk[p] * decay[p], 0.0) for p in pairs], chunk)))
    u_v = {p: _dot(tinv[p], vb[p], _HI) for p in pairs}
    w = {p: _dot(tinv[p], kb[p] * egc[p], _HI) for p in pairs}
    qk = {p: _mm_nt(q[p], k[p]) * decay[p] for p in pairs}
    k_dec_t = {p: (k[p] * jnp.exp(g_last[p] - gc[p[0]][:, p[1]:p[1] + 1])).T for p in pairs}

    s = [s_ref[0, h] for h in heads]
    for ci in range(nch):
        ws = [_mm(w[ci, h], s[h]) for h in heads]
        qs = [_mm(q[ci, h] * egc[ci, h], s[h]) for h in heads]
        v_new = [u_v[ci, h] - ws[h] for h in heads]
        o = [qs[h] + _mm(qk[ci, h], v_new[h]) for h in heads]
        s = [s[h] * jnp.exp(g_last[ci, h]) + _mm(k_dec_t[ci, h], v_new[h]) for h in heads]
        for h in heads:
            oh = o[h] * lax.rsqrt(jnp.mean(o[h] * o[h], axis=-1, keepdims=True) + EPS) * onorm_ref[...]
            o_ref[0, rows_of[ci], h * hd:(h + 1) * hd] = oh * _silu(z_ref[0, rows_of[ci], h * hd:(h + 1) * hd])
    for h in heads:
        s_ref[0, h] = s[h]


def _gdn(qkv, z, ab, conv0, s0, conv_w, alog_pad, dtb_pad, onorm, chunk):
    b, l, _ = qkv.shape
    hd = GDN_HEAD_DIM
    nch = 2 if l % (2 * chunk) == 0 else 1
    rows = chunk * nch
    return pl.pallas_call(
        functools.partial(_gdn_kernel, chunk=chunk, nch=nch),
        grid=(b, l // rows),
        in_specs=[pl.BlockSpec((1, rows, QKV_A), lambda i, j: (i, j, 0)),
                  pl.BlockSpec((1, rows, GDN_WIDTH), lambda i, j: (i, j, 0)),
                  pl.BlockSpec((1, rows, LANES), lambda i, j: (i, j, 0)),
                  pl.BlockSpec((1, CONV_W - 1, QKV_A), lambda i, j: (i, 0, 0)),
                  pl.BlockSpec((1, GDN_HEADS, hd, hd), lambda i, j: (i, 0, 0, 0)),
                  pl.BlockSpec((CONV_W, QKV_A), lambda i, j: (0, 0)),
                  pl.BlockSpec((1, LANES), lambda i, j: (0, 0)),
                  pl.BlockSpec((1, LANES), lambda i, j: (0, 0)),
                  pl.BlockSpec((1, hd), lambda i, j: (0, 0))],
        out_specs=[pl.BlockSpec((1, rows, GDN_WIDTH), lambda i, j: (i, j, 0)),
                   pl.BlockSpec((1, GDN_HEADS, hd, hd), lambda i, j: (i, 0, 0, 0))],
        out_shape=[jax.ShapeDtypeStruct((b, l, GDN_WIDTH), F32),
                   jax.ShapeDtypeStruct((b, GDN_HEADS, hd, hd), F32)],
        scratch_shapes=[pltpu.VMEM((8 + rows, QKV_A), F32)],
        compiler_params=_params(("parallel", "arbitrary")),
        name="gdn",
    )(qkv, z, ab, conv0, s0, conv_w, alog_pad, dtb_pad, onorm.reshape(1, hd))


def _rel_bucket(rel):
    nb = N_BUCKETS // 2
    max_exact = nb // 2
    ret = jnp.where(rel > 0, nb, 0)
    n = jnp.abs(rel)
    large = max_exact + (jnp.log(jnp.maximum(n, 1).astype(F32) / max_exact)
                         / math.log(REL_MAX_DIST / max_exact) * (nb - max_exact)).astype(jnp.int32)
    large = jnp.minimum(large, nb - 1)
    return ret + jnp.where(n < max_exact, n, large)


def _diff_finish(o1, o2, lam_ref, subln_ref, out_scale):
    o = o1 - lam_ref[...] * o2
    return o * lax.rsqrt(jnp.mean(o * o, axis=-1, keepdims=True) + EPS) * subln_ref[...] * out_scale


def _attn_prompt_kernel(q_ref, k_ref, v_ref, bias_ref, lam_ref, subln_ref, o_ref, m_ref, l_ref, acc_ref,
                        *, out_scale):
    i = pl.program_id(2)
    tb = ATT_BLOCK
    dh = DIFF_HEAD_DIM
    q = q_ref[0] * (dh ** -0.5)
    lane = lax.broadcasted_iota(jnp.int32, q.shape, 1)
    q2s = jnp.concatenate([jnp.where(lane < dh, q, 0.0), jnp.where(lane >= dh, q, 0.0)], axis=0).astype(BF16)

    def score_tiles(j, tile):
        start = pl.multiple_of(j * tb, tb)
        s = _dot_nt(q2s, k_ref[0, pl.ds(start, tb), :]) + bias_ref[0, tile]
        return [s[:, c:c + LANES] for c in range(0, tb, LANES)]

    def visible_blocks(fn, unroll):
        n_far = jnp.maximum(i - 1, 0)

        @pl.loop(0, n_far // unroll)
        def _(g):
            fn([(unroll * g + u, 0) for u in range(unroll)])

        @pl.loop((n_far // unroll) * unroll, n_far)
        def _(j):
            fn([(j, 0)])

        @pl.when(i > 0)
        def _():
            fn([(i - 1, 1), (i, 2)])

        @pl.when(i == 0)
        def _():
            fn([(i, 2)])

    m_ref[...] = jnp.full(m_ref.shape, NEG_BIG, F32)

    def track_max(blocks):
        tiles = [s for j, tile in blocks for s in score_tiles(j, tile)]
        m_ref[...] = functools.reduce(jnp.maximum, tiles, m_ref[...])

    visible_blocks(track_max, 4)
    m_ref[...] = jnp.broadcast_to(jnp.max(m_ref[...], axis=-1, keepdims=True), m_ref.shape)

    l_ref[...] = jnp.zeros(l_ref.shape, F32)
    acc_ref[...] = jnp.zeros(acc_ref.shape, F32)

    def accumulate(blocks):
        m = m_ref[...]
        l_add, acc_add = [], []
        for j, tile in blocks:
            p = [jnp.exp(s - m) for s in score_tiles(j, tile)]
            l_add.extend(p)
            start = pl.multiple_of(j * tb, tb)
            acc_add.append(_dot(jnp.concatenate(p, axis=-1).astype(BF16), v_ref[0, pl.ds(start, tb), :]))
        l_ref[...] += sum(l_add)
        acc_ref[...] += sum(acc_add)

    visible_blocks(accumulate, 4)
    o = acc_ref[...] / jnp.sum(l_ref[...], axis=-1, keepdims=True)
    o_ref[0] = _diff_finish(o[:tb], o[tb:], lam_ref, subln_ref, out_scale)


def _attn_prompt(qb, kh, vh, bias_tiles, lam_row, subln, out_scale):
    b, l, _ = qb.shape
    tb = ATT_BLOCK
    hw = 2 * DIFF_HEAD_DIM
    return pl.pallas_call(
        functools.partial(_attn_prompt_kernel, out_scale=out_scale),
        grid=(b, DIFF_HEADS, l // tb),
        in_specs=[pl.BlockSpec((1, tb, hw), lambda bi, h, i: (bi, i, h)),
                  pl.BlockSpec((1, l, hw), lambda bi, h, i: (bi, 0, h)),
                  pl.BlockSpec((1, l, DIFF_V_DIM), lambda bi, h, i: (bi, 0, h)),
                  pl.BlockSpec((1, 3, 2 * tb, tb), lambda bi, h, i: (h, 0, 0, 0)),
                  pl.BlockSpec((1, DIFF_V_DIM), lambda bi, h, i: (0, 0)),
                  pl.BlockSpec((1, DIFF_V_DIM), lambda bi, h, i: (0, 0))],
        out_specs=pl.BlockSpec((1, tb, DIFF_V_DIM), lambda bi, h, i: (bi, i, h)),
        out_shape=jax.ShapeDtypeStruct((b, l, DIFF_WIDTH), F32),
        scratch_shapes=[pltpu.VMEM((2 * tb, LANES), F32), pltpu.VMEM((2 * tb, LANES), F32),
                        pltpu.VMEM((2 * tb, DIFF_V_DIM), F32)],
        compiler_params=_params(("parallel", "parallel", "arbitrary")),
        name="attn_prompt",
    )(qb, kh, vh, bias_tiles, lam_row, subln.reshape(1, DIFF_V_DIM))


def _prompt_bias_tiles(rel_table):
    tb = ATT_BLOCK
    qi = jnp.arange(tb)[:, None]
    ki = jnp.arange(tb)[None, :]
    far = jnp.broadcast_to(rel_table[_rel_bucket(jnp.array(-2 * tb))], (tb, tb, DIFF_HEADS))
    prev = rel_table[_rel_bucket(ki - qi - tb)]
    diag = jnp.where(((ki // CHUNK) <= (qi // CHUNK))[..., None], rel_table[_rel_bucket(ki - qi)], NEG_BIG)
    tiles = jnp.moveaxis(jnp.stack([far, prev, diag]).astype(F32), -1, 0)
    return jnp.concatenate([tiles, tiles], axis=2)


def _attn_sample_kernel(q_ref, kp_ref, vp_ref, kn_ref, vn_ref, bp_ref, bn_ref, lam_ref, subln_ref, o_ref,
                        *, out_scale):
    dh = DIFF_HEAD_DIM
    q = q_ref[0] * (dh ** -0.5)
    kp = kp_ref[0].astype(BF16)
    kn = kn_ref[0].astype(BF16)
    vp = vp_ref[0].astype(BF16)
    vn = vn_ref[0].astype(BF16)
    outs = []
    for t in range(2):
        qt = q[:, t * dh:(t + 1) * dh].astype(BF16)
        sp = _dot_nt(qt, kp[:, t * dh:(t + 1) * dh]) + bp_ref[0]
        sn = _dot_nt(qt, kn[:, t * dh:(t + 1) * dh]) + bn_ref[0]
        m = jnp.maximum(jnp.max(sp, axis=-1, keepdims=True), jnp.max(sn, axis=-1, keepdims=True))
        pp = jnp.exp(sp - m)
        pn = jnp.exp(sn - m)
        den = jnp.sum(pp, axis=-1, keepdims=True) + jnp.sum(pn, axis=-1, keepdims=True)
        outs.append((_dot(pp.astype(BF16), vp) + _dot(pn.astype(BF16), vn)) / den)
    o_ref[0] = _diff_finish(outs[0], outs[1], lam_ref, subln_ref, out_scale)


def _attn_sample(qb, k_past, v_past, k_new, v_new, bias_past, bias_new, lam_row, subln, out_scale):
    b, l, _ = qb.shape
    p = k_past.shape[1]
    hw = 2 * DIFF_HEAD_DIM
    return pl.pallas_call(
        functools.partial(_attn_sample_kernel, out_scale=out_scale),
        grid=(b, DIFF_HEADS),
        in_specs=[pl.BlockSpec((1, l, hw), lambda bi, h: (bi, 0, h)),
                  pl.BlockSpec((1, p, hw), lambda bi, h: (bi, 0, h)),
                  pl.BlockSpec((1, p, DIFF_V_DIM), lambda bi, h: (bi, 0, h)),
                  pl.BlockSpec((1, l, hw), lambda bi, h: (bi, 0, h)),
                  pl.BlockSpec((1, l, DIFF_V_DIM), lambda bi, h: (bi, 0, h)),
                  pl.BlockSpec((1, l, p), lambda bi, h: (h, 0, 0)),
                  pl.BlockSpec((1, l, l), lambda bi, h: (h, 0, 0)),
                  pl.BlockSpec((1, DIFF_V_DIM), lambda bi, h: (0, 0)),
                  pl.BlockSpec((1, DIFF_V_DIM), lambda bi, h: (0, 0))],
        out_specs=pl.BlockSpec((1, l, DIFF_V_DIM), lambda bi, h: (bi, 0, h)),
        out_shape=jax.ShapeDtypeStruct((b, l, DIFF_WIDTH), F32),
        compiler_params=_params(("parallel", "parallel")),
        name="attn_sample",
    )(qb, k_past, v_past, k_new, v_new, bias_past, bias_new, lam_row, subln.reshape(1, DIFF_V_DIM))


def _sample_bias(rel_table, p, l):
    rel = jnp.arange(-(p + l - 1), l)
    by_rel = rel_table[_rel_bucket(rel)].astype(F32).T
    bias = jnp.stack([lax.slice_in_dim(by_rel, l - 1 - i, p + 2 * l - 1 - i, axis=1) for i in range(l)], axis=1)
    return bias[:, :, :p], bias[:, :, p:]


def _outproj_kernel(oa_ref, ob_ref, x_ref, mod_ref, n2_ref, wo_ref, wq_ref, keys_ref,
                    x1_ref, h2_ref, sc_ref):
    mixed = jnp.concatenate([oa_ref[0], ob_ref[0]], axis=-1).astype(BF16)
    x1 = x_ref[0] + mod_ref[0, 2:3, :] * _dot(mixed, wo_ref[...])
    x1_ref[0] = x1
    h2 = _modulated_norm(x1, n2_ref[...], mod_ref[0, 3:4, :], mod_ref[0, 4:5, :])
    h2_ref[0] = h2
    qh = _dot(h2.astype(BF16), wq_ref[...]).astype(BF16)
    for hp in range(2 * PEER_HEADS):
        sc_ref[0, hp] = _dot_nt(keys_ref[hp], qh[:, hp * PEER_HALF:(hp + 1) * PEER_HALF])


def _outproj(o_a, o_b, x, mod, norm2, w_out, w_q, keys):
    b, l, d = x.shape
    tl = min(l, 256)
    nhp = 2 * PEER_HEADS
    return pl.pallas_call(
        _outproj_kernel,
        grid=(b, l // tl),
        in_specs=[pl.BlockSpec((1, tl, GDN_WIDTH), lambda i, j: (i, j, 0)),
                  pl.BlockSpec((1, tl, DIFF_WIDTH), lambda i, j: (i, j, 0)),
                  pl.BlockSpec((1, tl, d), lambda i, j: (i, j, 0)),
                  pl.BlockSpec((1, 6, d), lambda i, j: (i, 0, 0)),
                  pl.BlockSpec((1, d), lambda i, j: (0, 0)),
                  pl.BlockSpec((d, d), lambda i, j: (0, 0)),
                  pl.BlockSpec((d, nhp * PEER_HALF), lambda i, j: (0, 0)),
                  pl.BlockSpec((nhp, N_KEYS, PEER_HALF), lambda i, j: (0, 0, 0))],
        out_specs=[pl.BlockSpec((1, tl, d), lambda i, j: (i, j, 0)),
                   pl.BlockSpec((1, tl, d), lambda i, j: (i, j, 0)),
                   pl.BlockSpec((1, nhp, N_KEYS, tl), lambda i, j: (i, 0, 0, j))],
        out_shape=[jax.ShapeDtypeStruct((b, l, d), F32),
                   jax.ShapeDtypeStruct((b, l, d), F32),
                   jax.ShapeDtypeStruct((b, nhp, N_KEYS, l), F32)],
        compiler_params=_params(("parallel", "parallel")),
        name="outproj",
    )(o_a, o_b, x, mod, norm2.reshape(1, d), w_out, w_q, keys)


def _top16_rows(s, ids, n):
    vals, idxs = [], []
    for _ in range(PEER_TOPK):
        m = jnp.max(s, axis=0, keepdims=True)
        i = jnp.min(jnp.where(s == m, ids, n), axis=0, keepdims=True)
        vals.append(m)
        idxs.append(i)
        s = jnp.where(ids == i, -jnp.inf, s)
    return jnp.concatenate(vals, axis=0), jnp.concatenate(idxs, axis=0)


def _pair_candidates(s1, s2):
    t = s1.shape[1]
    sub16 = lax.broadcasted_iota(jnp.int32, (PEER_TOPK, t), 0)
    sub8 = sub16[:8]
    cand = [s1[0:1] + s2] + [s1[a:a + 1] + s2[:8] for a in range(1, 8)] + [s1[8:] + s2[0:1]]
    pos = [sub16] + [a * PEER_TOPK + sub8 for a in range(1, 8)] + [(8 + sub8) * PEER_TOPK]
    return jnp.concatenate(cand, axis=0), jnp.concatenate(pos, axis=0)


def _pick_rows(table, sel):
    out = jnp.zeros_like(table)
    for a in range(PEER_TOPK):
        out = jnp.where(sel == a, table[a:a + 1, :], out)
    return out


def _topk_kernel(sc_ref, eidx_ref, gate_ref):
    eidx, gates = [], []
    key_ids = lax.broadcasted_iota(jnp.int32, sc_ref.shape[2:], 0)
    for h in range(PEER_HEADS):
        s1, i1 = _top16_rows(sc_ref[0, 2 * h], key_ids, N_KEYS)
        s2, i2 = _top16_rows(sc_ref[0, 2 * h + 1], key_ids, N_KEYS)
        cand, cand_pos = _pair_candidates(s1, s2)
        top_s, pos = _top16_rows(cand, cand_pos, PEER_TOPK * PEER_TOPK)
        eidx.append(_pick_rows(i1, pos // PEER_TOPK) * N_KEYS + _pick_rows(i2, pos % PEER_TOPK))
        e = jnp.exp(top_s - top_s[0:1, :])
        gates.append(e / jnp.sum(e, axis=0, keepdims=True))
    eidx_ref[...] = jnp.concatenate(eidx, axis=0).T
    gate_ref[...] = jnp.concatenate(gates, axis=0).T


def _topk(scores):
    b, nhp, nk, l = scores.shape
    tt = min(l, LANES)
    nt = l // tt
    return pl.pallas_call(
        _topk_kernel,
        grid=(b, nt),
        in_specs=[pl.BlockSpec((1, nhp, nk, tt), lambda i, j: (i, 0, 0, j))],
        out_specs=[pl.BlockSpec((tt, PEER_SLOTS), lambda i, j: (i * nt + j, 0)),
                   pl.BlockSpec((tt, PEER_SLOTS), lambda i, j: (i * nt + j, 0))],
        out_shape=[jax.ShapeDtypeStruct((b * l, PEER_SLOTS), jnp.int32),
                   jax.ShapeDtypeStruct((b * l, PEER_SLOTS), F32)],
        compiler_params=_params(("parallel", "parallel")),
        name="topk",
    )(scores)


_SC_ROWS = 32
_SC_SLOTS = 4


def _pack_kernel(t_ref, o_ref):
    half = t_ref.shape[2] // 2
    bits = lax.bitcast_convert_type(t_ref[0].astype(BF16).astype(F32), jnp.int32)
    o_ref[...] = lax.shift_right_logical(bits[:, :half], 16) | bits[:, half:]


def _pack_bf16_halves(t):
    _, v, d = t.shape
    tv = 512
    return pl.pallas_call(
        _pack_kernel,
        grid=(v // tv,),
        in_specs=[pl.BlockSpec((1, tv, d), lambda i: (0, i, 0))],
        out_specs=pl.BlockSpec((tv, d // 2), lambda i: (i, 0)),
        out_shape=jax.ShapeDtypeStruct((v, d // 2), jnp.int32),
        compiler_params=_params(("parallel",)),
        name="pack",
    )(t)


def _sc_mesh():
    info = plsc.get_sparse_core_info()
    mesh = plsc.VectorSubcoreMesh(core_axis_name="c", subcore_axis_name="s")
    return mesh, info.num_cores, info.num_cores * info.num_subcores


def _sc_gather(tab, idx):
    n = idx.shape[0]
    w = tab.shape[1]
    mesh, ncores, nw = _sc_mesh()
    per_w = n // nw
    ns, nr = _SC_SLOTS, _SC_ROWS
    ngroups = per_w // (ns * nr)
    assert n % (nw * ns * nr) == 0

    @functools.partial(
        pl.kernel, mesh=mesh, out_type=jax.ShapeDtypeStruct((n, w), tab.dtype),
        scratch_types=([pltpu.VMEM((per_w,), jnp.int32)] + [pltpu.VMEM((nr, w), tab.dtype)] * ns
                       + [pltpu.SemaphoreType.DMA] * (2 * ns)),
    )
    def k(tab_hbm, idx_hbm, out_hbm, idx_v, *scratch):
        bufs, gsem, wsem = scratch[:ns], scratch[ns:2 * ns], scratch[2 * ns:]
        base = (lax.axis_index("s") * ncores + lax.axis_index("c")) * per_w
        pltpu.sync_copy(idx_hbm.at[pl.ds(pl.multiple_of(base, nr), per_w)], idx_v)

        def gather(c, s):
            picks = idx_v.at[pl.ds(pl.multiple_of(c * nr, nr), nr)]
            return pltpu.make_async_copy(tab_hbm.at[picks], bufs[s], gsem[s])

        def write(c, s):
            span = pl.ds(pl.multiple_of(base + c * nr, nr), nr)
            return pltpu.make_async_copy(bufs[s], out_hbm.at[span], wsem[s])

        for s in range(ns):
            gather(s, s).start()

        @pl.loop(0, ngroups)
        def _(g):
            for s in range(ns):
                gather(g * ns + s, s).wait()
                write(g * ns + s, s).start()
            for s in range(ns):
                write(g * ns + s, s).wait()

                @pl.when(g < ngroups - 1)
                def _():
                    gather((g + 1) * ns + s, s).start()

    return k(tab, idx)


_ACC_ROWS = 64
_ACC_TOKENS = 8


def _sc_accumulate(tab, idx, coef):
    n = idx.shape[0]
    w = tab.shape[1]
    d = 2 * w
    lanes = 16
    mesh, ncores, nw = _sc_mesh()
    ntok = n // PEER_SLOTS
    tok_w = ntok // nw
    per_w = tok_w * PEER_SLOTS
    nchunks = per_w // _ACC_ROWS
    assert ntok % (nw * _ACC_TOKENS) == 0 and PEER_SLOTS == 2 * _ACC_ROWS and w % (4 * lanes) == 0

    @functools.partial(
        pl.kernel, mesh=mesh, out_type=jax.ShapeDtypeStruct((ntok, d), F32),
        scratch_types=[pltpu.VMEM((per_w,), jnp.int32), pltpu.VMEM((per_w,), F32),
                       pltpu.VMEM((_ACC_ROWS, w), tab.dtype), pltpu.VMEM((_ACC_ROWS, w), tab.dtype),
                       pltpu.VMEM((_ACC_TOKENS, d), F32), pltpu.SemaphoreType.DMA, pltpu.SemaphoreType.DMA],
        compiler_params=pltpu.CompilerParams(needs_layout_passes=False),
    )
    def k(tab_hbm, idx_hbm, coef_hbm, out_hbm, idx_v, coef_v, rows0, rows1, acc_v, sem0, sem1):
        rows, sems = (rows0, rows1), (sem0, sem1)
        wid = lax.axis_index("s") * ncores + lax.axis_index("c")
        base = pl.multiple_of(wid * per_w, _ACC_ROWS)
        pltpu.sync_copy(idx_hbm.at[pl.ds(base, per_w)], idx_v)
        pltpu.sync_copy(coef_hbm.at[pl.ds(base, per_w)], coef_v)

        def gather(c, s):
            picks = idx_v.at[pl.ds(pl.multiple_of(c * _ACC_ROWS, _ACC_ROWS), _ACC_ROWS)]
            return pltpu.make_async_copy(tab_hbm.at[picks], rows[s], sems[s])

        def accumulate(rows_ref, c, arow):
            nrow, ncol = 4, 4

            @pl.loop(0, _ACC_ROWS // nrow)
            def _(q):
                r0 = q * nrow
                cvec = [plsc.load_gather(coef_v, [jnp.full((lanes,), c * _ACC_ROWS + r0 + i, jnp.int32)])
                        for i in range(nrow)]
                groups = [[col0 + j * lanes for j in range(ncol)] for col0 in range(0, w, ncol * lanes)]

                def load(cols):
                    return [[rows_ref[r0 + i, pl.ds(col, lanes)] for i in range(nrow)] for col in cols]

                ahead = load(groups[0])
                for g, cols in enumerate(groups):
                    words = ahead
                    if g + 1 < len(groups):
                        ahead = load(groups[g + 1])
                    sums = []
                    for wds in words:
                        lo = [cvec[i] * lax.bitcast_convert_type(wds[i] << 16, F32) for i in range(nrow)]
                        hi = [cvec[i] * lax.bitcast_convert_type(wds[i] & jnp.int32(-65536), F32)
                              for i in range(nrow)]
                        sums.append(((lo[0] + lo[1]) + (lo[2] + lo[3]), (hi[0] + hi[1]) + (hi[2] + hi[3])))
                    for col, (lo, hi) in zip(cols, sums):
                        plsc.addupdate(acc_v.at[arow, pl.ds(col, lanes)], lo)
                        plsc.addupdate(acc_v.at[arow, pl.ds(w + col, lanes)], hi)

        gather(0, 0).start()

        @pl.loop(0, tok_w)
        def _(t):
            arow = t % _ACC_TOKENS
            for col in range(0, d, lanes):
                acc_v[arow, pl.ds(col, lanes)] = jnp.zeros((lanes,), F32)
            for half in range(2):
                c = 2 * t + half

                @pl.when(c + 1 < nchunks)
                def _():
                    gather(c + 1, 1 - half).start()

                gather(c, half).wait()
                accumulate(rows[half], c, arow)

            @pl.when(arow == _ACC_TOKENS - 1)
            def _():
                first_tok = pl.multiple_of(wid * tok_w + t - (_ACC_TOKENS - 1), _ACC_TOKENS)
                pltpu.sync_copy(acc_v, out_hbm.at[pl.ds(first_tok, _ACC_TOKENS)])

    return k(tab, idx, coef)


_PEER_TOKENS = 16


def _unpack_bf16_halves(w):
    lo = lax.bitcast_convert_type(w << 16, F32)
    hi = lax.bitcast_convert_type(w & jnp.int32(-65536), F32)
    return lo, hi


def _coef_kernel(ug_ref, gate_ref, h2_ref, coef_ref):
    half = D_MODEL // 2
    gate_t = gate_ref[...].T
    cols = []
    for t in range(_PEER_TOKENS):
        u_lo, u_hi = _unpack_bf16_halves(ug_ref[t * PEER_SLOTS:(t + 1) * PEER_SLOTS, :])
        pre = jnp.sum(u_lo * h2_ref[t:t + 1, :half] + u_hi * h2_ref[t:t + 1, half:], axis=-1, keepdims=True)
        act = 0.5 * pre * (1.0 + lax.erf(pre * (2.0 ** -0.5)))
        cols.append(gate_t[:, t:t + 1] * act)
    coef_ref[...] = jnp.concatenate(cols, axis=1).T


def _peer_coef(ug, gate, h2):
    n, d = h2.shape
    tp = _PEER_TOKENS
    return pl.pallas_call(
        _coef_kernel,
        grid=(n // tp,),
        in_specs=[pl.BlockSpec((tp * PEER_SLOTS, d // 2), lambda i: (i, 0)),
                  pl.BlockSpec((tp, PEER_SLOTS), lambda i: (i, 0)),
                  pl.BlockSpec((tp, d), lambda i: (i, 0))],
        out_specs=pl.BlockSpec((tp, PEER_SLOTS), lambda i: (i, 0)),
        out_shape=jax.ShapeDtypeStruct((n, PEER_SLOTS), F32),
        compiler_params=_params(("parallel",)),
        name="peer_coef",
    )(ug, gate, h2)


def _final_kernel(acc_ref, x1_ref, g2_ref, fn_ref, y_ref):
    x2 = x1_ref[0] + g2_ref[0, 5:6, :] * acc_ref[0]
    y_ref[0] = x2 * lax.rsqrt(jnp.mean(x2 * x2, axis=-1, keepdims=True) + EPS) * fn_ref[...]


def _peer_final(acc, x1, mod, final_norm):
    b, l, d = x1.shape
    tl = min(l, 512)
    return pl.pallas_call(
        _final_kernel,
        grid=(b, l // tl),
        in_specs=[pl.BlockSpec((1, tl, d), lambda i, j: (i, j, 0)),
                  pl.BlockSpec((1, tl, d), lambda i, j: (i, j, 0)),
                  pl.BlockSpec((1, 6, d), lambda i, j: (i, 0, 0)),
                  pl.BlockSpec((1, d), lambda i, j: (0, 0))],
        out_specs=pl.BlockSpec((1, tl, d), lambda i, j: (i, j, 0)),
        out_shape=jax.ShapeDtypeStruct((b, l, d), F32),
        compiler_params=_params(("parallel", "parallel")),
        name="peer_final",
    )(acc, x1, mod, final_norm.reshape(1, d))


def _front(x, mod, conv0, s0, k_past, v_past, wts, prompt):
    b, l, d = x.shape
    qkv, z, ab, qb, kb, vb, kh, vh = _inproj(x, mod, wts["norm1"], wts["w_in"])
    chunk = CHUNK if prompt else l
    o_a, s_new = _gdn(qkv, z, ab, conv0, s0, wts["conv_w"], wts["alog"], wts["dtb"], wts["onorm"], chunk)
    conv_new = qkv[:, l - (CONV_W - 1):, :]
    if prompt:
        o_b = _attn_prompt(qb, kh, vh, wts["bias_prompt"], wts["lam"], wts["subln"], wts["out_scale"])
    else:
        p = k_past.shape[1]
        bias_past, bias_new = _sample_bias(wts["rel_table"], p, l)
        o_b = _attn_sample(qb, k_past.reshape(b, p, DIFF_WIDTH), v_past.reshape(b, p, DIFF_WIDTH), kb, vb,
                           bias_past, bias_new, wts["lam"], wts["subln"], wts["out_scale"])
    x1, h2, scores = _outproj(o_a, o_b, x, mod, wts["norm2"], wts["w_out"], wts["w_q"], wts["keys"])
    eidx, gate = _topk(scores)
    eidx = eidx.reshape(-1)
    ug = _sc_gather(wts["peer_u"], eidx)
    return (ug, eidx, gate, h2, x1, mod), (kb, vb, s_new, conv_new)


def _back(pending, wts):
    ug, eidx, gate, h2, x1, mod = pending
    b, l, d = h2.shape
    coef = _peer_coef(ug, gate, h2.reshape(b * l, d))
    acc = _sc_accumulate(wts["peer_v"], eidx, coef.reshape(-1))
    return _peer_final(acc.reshape(b, l, d), x1, mod, wts["final_norm"])


def _prompt_rows(x, mod, wts, sample_front):
    b, l, d = x.shape
    conv0 = jnp.zeros((1, CONV_W - 1, QKV_A), F32)
    s0 = jnp.zeros((1, GDN_HEADS, GDN_HEAD_DIM, GDN_HEAD_DIM), F32)
    news, ys, sample = [], [], None
    for t in range(b):
        xi = lax.dynamic_slice_in_dim(x, t, 1, 0)
        if t == b // 2:
            pend_s, new_s = sample
            xi, gate_s = lax.optimization_barrier((xi, pend_s[2]))
            sample = (pend_s[:2] + (gate_s,) + pend_s[3:], new_s)
        pend, new = _front(xi, lax.dynamic_slice_in_dim(mod, t, 1, 0), conv0, s0, None, None, wts, True)
        if t == b // 4:
            gate, sample = sample_front(pend[2])
            pend = pend[:2] + (gate,) + pend[3:]
        news.append(new)
        ys.append(_back(pend, wts))
    y = jnp.concatenate(ys, axis=0)
    return y, [jnp.concatenate([n[k] for n in news], axis=0) for k in range(4)], sample


def _cache_entries(new, b, l):
    kb, vb, s_new, conv_new = new
    return (kb.reshape(1, b, l, DIFF_HEADS, 2 * DIFF_HEAD_DIM), vb.reshape(1, b, l, DIFF_HEADS, DIFF_V_DIM),
            s_new[None], conv_new[None])


def kernel(x_prompt, x_sample, c_prompt, c_sample, cache_k, cache_v, state_gdn, state_conv, w_ada, b_ada,
           norm1, norm2, w_in, conv_w, a_log, dt_bias, gdn_onorm, lam_q1, lam_k1, lam_q2, lam_k2, diff_subln,
           w_out, peer_wq, peer_keys, peer_u, peer_v, rel_table, final_norm):
    assert w_ada.shape[0] == 1, "single-layer step"
    bp = x_prompt.shape[0]
    d = D_MODEL
    lam_init = 0.8 - 0.6 * math.exp(-0.3 * 0)
    lam = (jnp.exp(jnp.sum(lam_q1[0] * lam_k1[0])) - jnp.exp(jnp.sum(lam_q2[0] * lam_k2[0])) + lam_init)
    w = w_in[0]
    w_packed = jnp.concatenate(
        [w[:, :_C_AB], jnp.pad(w[:, 2048:2056], ((0, 0), (0, LANES - 2 * GDN_HEADS))), w[:, 2056:]],
        axis=1).astype(BF16)
    wts = dict(
        norm1=norm1[0], norm2=norm2[0], w_in=w_packed, conv_w=conv_w[0],
        alog=jnp.pad(a_log[0], (0, LANES - GDN_HEADS)).reshape(1, LANES),
        dtb=jnp.pad(dt_bias[0], (0, LANES - GDN_HEADS)).reshape(1, LANES),
        onorm=gdn_onorm[0], lam=jnp.full((1, DIFF_V_DIM), lam, F32), subln=diff_subln[0],
        out_scale=1.0 - lam_init, bias_prompt=_prompt_bias_tiles(rel_table), rel_table=rel_table,
        w_out=w_out[0].astype(BF16), w_q=peer_wq[0].astype(BF16),
        keys=peer_keys[0].reshape(2 * PEER_HEADS, N_KEYS, PEER_HALF).astype(BF16),
        peer_u=_pack_bf16_halves(peer_u), peer_v=_pack_bf16_halves(peer_v), final_norm=final_norm)

    mod = _ada(jnp.concatenate([c_prompt, c_sample], axis=0), w_ada[0], b_ada[0]).reshape(-1, 6, d)
    def sample_front(dep):
        xs, dep = lax.optimization_barrier((x_sample, dep))
        return dep, _front(xs, mod[bp:], state_conv[0], state_gdn[0], cache_k[0], cache_v[0], wts, False)

    yp, new_p, (pend_s, new_s) = _prompt_rows(x_prompt, mod[:bp], wts, sample_front)
    ys = _back(pend_s, wts)
    kp, vp, sp, cp = _cache_entries(new_p, *x_prompt.shape[:2])
    ks, vs, ss, cs = _cache_entries(new_s, *x_sample.shape[:2])
    return yp, ys, kp, vp, sp, cp, ks, vs, ss, cs
```

```python
import functools
import math

import jax
import jax.numpy as jnp
from jax import lax
from jax.experimental import pallas as pl
from jax.experimental.pallas import tpu as pltpu
from jax.experimental.pallas import tpu_sc as plsc

F32 = jnp.float32
BF16 = jnp.bfloat16
EPS = 1e-6

D_MODEL = 1024
CHUNK = 64
GDN_HEADS = 4
GDN_HEAD_DIM = 128
GDN_WIDTH = GDN_HEADS * GDN_HEAD_DIM
CONV_W = 4
QKV_A = 3 * GDN_WIDTH
DIFF_HEADS = 4
DIFF_HEAD_DIM = 64
DIFF_V_DIM = 128
DIFF_WIDTH = DIFF_HEADS * 2 * DIFF_HEAD_DIM
ATT_BLOCK = 256
N_BUCKETS = 32
REL_MAX_DIST = 128
PEER_HEADS = 8
N_KEYS = 128
PEER_HALF = 128
PEER_TOPK = 16
PEER_SLOTS = PEER_HEADS * PEER_TOPK
LANES = 128
NEG_BIG = -1e30
VMEM_LIMIT = 56 * 1024 * 1024

_C_QKV, _C_Z, _C_AB, _C_QB, _C_KB, _C_VB = 0, 1536, 2048, 2176, 2688, 3200
_C_END = 3712


def _params(sem):
    return pltpu.CompilerParams(dimension_semantics=sem, vmem_limit_bytes=VMEM_LIMIT)


def _dot(a, b, precision=None):
    return jnp.dot(a, b, preferred_element_type=F32, precision=precision)


def _dot_nt(a, b, precision=None):
    return lax.dot_general(a, b, (((1,), (1,)), ((), ())), preferred_element_type=F32, precision=precision)


def _silu(x):
    return x * jax.nn.sigmoid(x)


def _ada_kernel(c_ref, w_ref, b_ref, o_ref):
    a = _silu(c_ref[...]).astype(BF16)
    o_ref[...] = _dot(a, w_ref[...].astype(BF16)) + b_ref[...]


def _ada(c, w_ada, b_ada):
    n, d = c.shape
    cols = w_ada.shape[1]
    tn = 1024
    return pl.pallas_call(
        _ada_kernel,
        grid=(cols // tn,),
        in_specs=[pl.BlockSpec((n, d), lambda j: (0, 0)),
                  pl.BlockSpec((d, tn), lambda j: (0, j)),
                  pl.BlockSpec((1, tn), lambda j: (0, j))],
        out_specs=pl.BlockSpec((n, tn), lambda j: (0, j)),
        out_shape=jax.ShapeDtypeStruct((n, cols), F32),
        compiler_params=_params(("parallel",)),
        name="ada",
    )(c, w_ada, b_ada.reshape(1, cols))


def _modulated_norm(x, gain, shift, scale):
    y = x * lax.rsqrt(jnp.mean(x * x, axis=-1, keepdims=True) + EPS)
    return (y * gain) * (1.0 + scale) + shift


def _inproj_kernel(x_ref, mod_ref, n1_ref, w_ref, qkv_ref, z_ref, ab_ref, qb_ref, kb_ref, vb_ref, kh_ref, vh_ref):
    h = _modulated_norm(x_ref[0], n1_ref[...], mod_ref[0, 0:1, :], mod_ref[0, 1:2, :]).astype(BF16)
    qkv_ref[0] = _dot(h, w_ref[:, _C_QKV:_C_Z])
    z_ref[0] = _dot(h, w_ref[:, _C_Z:_C_AB])
    ab_ref[0] = _dot(h, w_ref[:, _C_AB:_C_QB])
    qb_ref[0] = _dot(h, w_ref[:, _C_QB:_C_KB])
    kb = _dot(h, w_ref[:, _C_KB:_C_VB])
    vb = _dot(h, w_ref[:, _C_VB:_C_END])
    kb_ref[0] = kb
    vb_ref[0] = vb
    kh_ref[0] = kb.astype(BF16)
    vh_ref[0] = vb.astype(BF16)


def _inproj(x, mod, norm1, w_packed):
    b, l, d = x.shape
    tl = min(l, 256)
    widths = (QKV_A, GDN_WIDTH, LANES, DIFF_WIDTH, DIFF_WIDTH, DIFF_WIDTH, DIFF_WIDTH, DIFF_WIDTH)
    dtypes = (F32,) * 6 + (BF16,) * 2
    return pl.pallas_call(
        _inproj_kernel,
        grid=(b, l // tl),
        in_specs=[pl.BlockSpec((1, tl, d), lambda i, j: (i, j, 0)),
                  pl.BlockSpec((1, 6, d), lambda i, j: (i, 0, 0)),
                  pl.BlockSpec((1, d), lambda i, j: (0, 0)),
                  pl.BlockSpec((d, _C_END), lambda i, j: (0, 0))],
        out_specs=[pl.BlockSpec((1, tl, w), lambda i, j: (i, j, 0)) for w in widths],
        out_shape=[jax.ShapeDtypeStruct((b, l, w), dt) for w, dt in zip(widths, dtypes)],
        compiler_params=_params(("parallel", "parallel")),
        name="inproj",
    )(x, mod, norm1.reshape(1, d), w_packed)


_HI = lax.Precision.HIGHEST


def _unit_lower_inverses(mats, n):
    r = lax.broadcasted_iota(jnp.int32, (n, n), 0)
    c = lax.broadcasted_iota(jnp.int32, (n, n), 1)
    eye = (r == c).astype(F32)
    ad = [jnp.where((r // 8) == (c // 8), a, 0.0) for a in mats]
    a2 = [_dot(m, m, _HI) for m in ad]
    a4 = [_dot(m, m, _HI) for m in a2]
    xs = [eye - m for m in ad]
    xs = [x + _dot(x, m, _HI) for x, m in zip(xs, a2)]
    xs = [x + _dot(x, m, _HI) for x, m in zip(xs, a4)]
    bs = 8
    while bs < n:
        off = ((r // (2 * bs)) == (c // (2 * bs))) & ((r // bs) != (c // bs))
        ys = [_dot(jnp.where(off, a, 0.0), x, _HI) for a, x in zip(mats, xs)]
        xs = [x - _dot(x, y, _HI) for x, y in zip(xs, ys)]
        bs *= 2
    return xs


def _mm(a, b):
    return _dot(a.astype(BF16), b.astype(BF16))


def _mm_nt(a, b):
    return _dot_nt(a.astype(BF16), b.astype(BF16))


def _gdn_kernel(qkv_ref, z_ref, ab_ref, conv0_ref, s0_ref, cw_ref, alog_ref, dtb_ref, onorm_ref,
                o_ref, s_ref, xbuf, *, chunk, nch):
    hd = GDN_HEAD_DIM
    rows = chunk * nch

    @pl.when(pl.program_id(1) == 0)
    def _():
        xbuf[5:8, :] = conv0_ref[0]
        s_ref[0] = s0_ref[0]

    x = qkv_ref[0]
    xbuf[8:8 + rows, :] = x
    y = (xbuf[5:5 + rows, :] * cw_ref[0:1, :] + xbuf[6:6 + rows, :] * cw_ref[1:2, :]
         + xbuf[7:7 + rows, :] * cw_ref[2:3, :] + x * cw_ref[3:4, :])
    xbuf[5:8, :] = x[rows - 3:rows, :]
    y = _silu(y)

    ab = ab_ref[0]
    t = ab + dtb_ref[...]
    softplus = jnp.maximum(t, 0.0) + jnp.log(1.0 + jnp.exp(-jnp.abs(t)))
    g = -jnp.exp(alog_ref[...]) * softplus
    beta = jax.nn.sigmoid(ab)

    r = lax.broadcasted_iota(jnp.int32, (chunk, chunk), 0)
    c = lax.broadcasted_iota(jnp.int32, (chunk, chunk), 1)
    lower = r >= c
    tri = lower.astype(F32)

    heads = range(GDN_HEADS)
    pairs = [(ci, h) for ci in range(nch) for h in heads]
    rows_of = {ci: slice(ci * chunk, (ci + 1) * chunk) for ci in range(nch)}
    gc = {ci: _dot(tri, g[rows_of[ci]], _HI) for ci in range(nch)}
    gc_t = {ci: gc[ci].T for ci in range(nch)}
    q, k, vb, kb, decay, egc, g_last = {}, {}, {}, {}, {}, {}, {}
    for ci, h in pairs:
        sl = rows_of[ci]
        qh = y[sl, h * hd:(h + 1) * hd]
        kh = y[sl, GDN_WIDTH + h * hd:GDN_WIDTH + (h + 1) * hd]
        p = ci, h
        q[p] = qh * lax.rsqrt(jnp.sum(qh * qh, axis=-1, keepdims=True) + EPS) * (hd ** -0.5)
        k[p] = kh * lax.rsqrt(jnp.sum(kh * kh, axis=-1, keepdims=True) + EPS)
        gcol = gc[ci][:, h:h + 1]
        bcol = beta[sl, GDN_HEADS + h:GDN_HEADS + h + 1]
        decay[p] = jnp.exp(jnp.where(lower, gcol - gc_t[ci][h:h + 1, :], NEG_BIG))
        kb[p] = k[p] * bcol
        vb[p] = y[sl, 2 * GDN_WIDTH + h * hd:2 * GDN_WIDTH + (h + 1) * hd] * bcol
        egc[p] = jnp.exp(gcol)
        g_last[p] = gcol[chunk - 1:chunk, :]
    kk = {p: _mm_nt(kb[p], k[p]) for p in pairs}
    tinv = dict(zip(pairs, _unit_lower_inverses([jnp.where(r > c, kk[p] * decay[p], 0.0) for p in pairs], chunk)))
    u_v = {p: _dot(tinv[p], vb[p], _HI) for p in pairs}
    w = {p: _dot(tinv[p], kb[p] * egc[p], _HI) for p in pairs}
    qk = {p: _mm_nt(q[p], k[p]) * decay[p] for p in pairs}
    k_dec_t = {p: (k[p] * jnp.exp(g_last[p] - gc[p[0]][:, p[1]:p[1] + 1])).T for p in pairs}

    s = [s_ref[0, h] for h in heads]
    for ci in range(nch):
        ws = [_mm(w[ci, h], s[h]) for h in heads]
        qs = [_mm(q[ci, h] * egc[ci, h], s[h]) for h in heads]
        v_new = [u_v[ci, h] - ws[h] for h in heads]
        o = [qs[h] + _mm(qk[ci, h], v_new[h]) for h in heads]
        s = [s[h] * jnp.exp(g_last[ci, h]) + _mm(k_dec_t[ci, h], v_new[h]) for h in heads]
        for h in heads:
            oh = o[h] * lax.rsqrt(jnp.mean(o[h] * o[h], axis=-1, keepdims=True) + EPS) * onorm_ref[...]
            o_ref[0, rows_of[ci], h * hd:(h + 1) * hd] = oh * _silu(z_ref[0, rows_of[ci], h * hd:(h + 1) * hd])
    for h in heads:
        s_ref[0, h] = s[h]


def _gdn(qkv, z, ab, conv0, s0, conv_w, alog_pad, dtb_pad, onorm, chunk):
    b, l, _ = qkv.shape
    hd = GDN_HEAD_DIM
    nch = 2 if l % (2 * chunk) == 0 else 1
    rows = chunk * nch
    return pl.pallas_call(
        functools.partial(_gdn_kernel, chunk=chunk, nch=nch),
        grid=(b, l // rows),
        in_specs=[pl.BlockSpec((1, rows, QKV_A), lambda i, j: (i, j, 0)),
                  pl.BlockSpec((1, rows, GDN_WIDTH), lambda i, j: (i, j, 0)),
                  pl.BlockSpec((1, rows, LANES), lambda i, j: (i, j, 0)),
                  pl.BlockSpec((1, CONV_W - 1, QKV_A), lambda i, j: (i, 0, 0)),
                  pl.BlockSpec((1, GDN_HEADS, hd, hd), lambda i, j: (i, 0, 0, 0)),
                  pl.BlockSpec((CONV_W, QKV_A), lambda i, j: (0, 0)),
                  pl.BlockSpec((1, LANES), lambda i, j: (0, 0)),
                  pl.BlockSpec((1, LANES), lambda i, j: (0, 0)),
                  pl.BlockSpec((1, hd), lambda i, j: (0, 0))],
        out_specs=[pl.BlockSpec((1, rows, GDN_WIDTH), lambda i, j: (i, j, 0)),
                   pl.BlockSpec((1, GDN_HEADS, hd, hd), lambda i, j: (i, 0, 0, 0))],
        out_shape=[jax.ShapeDtypeStruct((b, l, GDN_WIDTH), F32),
                   jax.ShapeDtypeStruct((b, GDN_HEADS, hd, hd), F32)],
        scratch_shapes=[pltpu.VMEM((8 + rows, QKV_A), F32)],
        compiler_params=_params(("parallel", "arbitrary")),
        name="gdn",
    )(qkv, z, ab, conv0, s0, conv_w, alog_pad, dtb_pad, onorm.reshape(1, hd))


def _rel_bucket(rel):
    nb = N_BUCKETS // 2
    max_exact = nb // 2
    ret = jnp.where(rel > 0, nb, 0)
    n = jnp.abs(rel)
    large = max_exact + (jnp.log(jnp.maximum(n, 1).astype(F32) / max_exact)
                         / math.log(REL_MAX_DIST / max_exact) * (nb - max_exact)).astype(jnp.int32)
    large = jnp.minimum(large, nb - 1)
    return ret + jnp.where(n < max_exact, n, large)


def _diff_finish(o1, o2, lam_ref, subln_ref, out_scale):
    o = o1 - lam_ref[...] * o2
    return o * lax.rsqrt(jnp.mean(o * o, axis=-1, keepdims=True) + EPS) * subln_ref[...] * out_scale


def _attn_prompt_kernel(q_ref, k_ref, v_ref, bias_ref, lam_ref, subln_ref, o_ref, m_ref, l_ref, acc_ref,
                        *, out_scale):
    i = pl.program_id(2)
    tb = ATT_BLOCK
    dh = DIFF_HEAD_DIM
    q = q_ref[0] * (dh ** -0.5)
    lane = lax.broadcasted_iota(jnp.int32, q.shape, 1)
    q2s = jnp.concatenate([jnp.where(lane < dh, q, 0.0), jnp.where(lane >= dh, q, 0.0)], axis=0).astype(BF16)

    def score_tiles(j, tile):
        start = pl.multiple_of(j * tb, tb)
        s = _dot_nt(q2s, k_ref[0, pl.ds(start, tb), :]) + bias_ref[0, tile]
        return [s[:, c:c + LANES] for c in range(0, tb, LANES)]

    def visible_blocks(fn, unroll):
        n_far = jnp.maximum(i - 1, 0)

        @pl.loop(0, n_far // unroll)
        def _(g):
            fn([(unroll * g + u, 0) for u in range(unroll)])

        @pl.loop((n_far // unroll) * unroll, n_far)
        def _(j):
            fn([(j, 0)])

        @pl.when(i > 0)
        def _():
            fn([(i - 1, 1), (i, 2)])

        @pl.when(i == 0)
        def _():
            fn([(i, 2)])

    m_ref[...] = jnp.full(m_ref.shape, NEG_BIG, F32)

    def track_max(blocks):
        tiles = [s for j, tile in blocks for s in score_tiles(j, tile)]
        m_ref[...] = functools.reduce(jnp.maximum, tiles, m_ref[...])

    visible_blocks(track_max, 4)
    m_ref[...] = jnp.broadcast_to(jnp.max(m_ref[...], axis=-1, keepdims=True), m_ref.shape)

    l_ref[...] = jnp.zeros(l_ref.shape, F32)
    acc_ref[...] = jnp.zeros(acc_ref.shape, F32)

    def accumulate(blocks):
        m = m_ref[...]
        l_add, acc_add = [], []
        for j, tile in blocks:
            p = [jnp.exp(s - m) for s in score_tiles(j, tile)]
            l_add.extend(p)
            start = pl.multiple_of(j * tb, tb)
            acc_add.append(_dot(jnp.concatenate(p, axis=-1).astype(BF16), v_ref[0, pl.ds(start, tb), :]))
        l_ref[...] += sum(l_add)
        acc_ref[...] += sum(acc_add)

    visible_blocks(accumulate, 4)
    o = acc_ref[...] / jnp.sum(l_ref[...], axis=-1, keepdims=True)
    o_ref[0] = _diff_finish(o[:tb], o[tb:], lam_ref, subln_ref, out_scale)


def _attn_prompt(qb, kh, vh, bias_tiles, lam_row, subln, out_scale):
    b, l, _ = qb.shape
    tb = ATT_BLOCK
    hw = 2 * DIFF_HEAD_DIM
    return pl.pallas_call(
        functools.partial(_attn_prompt_kernel, out_scale=out_scale),
        grid=(b, DIFF_HEADS, l // tb),
        in_specs=[pl.BlockSpec((1, tb, hw), lambda bi, h, i: (bi, i, h)),
                  pl.BlockSpec((1, l, hw), lambda bi, h, i: (bi, 0, h)),
                  pl.BlockSpec((1, l, DIFF_V_DIM), lambda bi, h, i: (bi, 0, h)),
                  pl.BlockSpec((1, 3, 2 * tb, tb), lambda bi, h, i: (h, 0, 0, 0)),
                  pl.BlockSpec((1, DIFF_V_DIM), lambda bi, h, i: (0, 0)),
                  pl.BlockSpec((1, DIFF_V_DIM), lambda bi, h, i: (0, 0))],
        out_specs=pl.BlockSpec((1, tb, DIFF_V_DIM), lambda bi, h, i: (bi, i, h)),
        out_shape=jax.ShapeDtypeStruct((b, l, DIFF_WIDTH), F32),
        scratch_shapes=[pltpu.VMEM((2 * tb, LANES), F32), pltpu.VMEM((2 * tb, LANES), F32),
                        pltpu.VMEM((2 * tb, DIFF_V_DIM), F32)],
        compiler_params=_params(("parallel", "parallel", "arbitrary")),
        name="attn_prompt",
    )(qb, kh, vh, bias_tiles, lam_row, subln.reshape(1, DIFF_V_DIM))


def _prompt_bias_tiles(rel_table):
    tb = ATT_BLOCK
    qi = jnp.arange(tb)[:, None]
    ki = jnp.arange(tb)[None, :]
    far = jnp.broadcast_to(rel_table[_rel_bucket(jnp.array(-2 * tb))], (tb, tb, DIFF_HEADS))
    prev = rel_table[_rel_bucket(ki - qi - tb)]
    diag = jnp.where(((ki // CHUNK) <= (qi // CHUNK))[..., None], rel_table[_rel_bucket(ki - qi)], NEG_BIG)
    tiles = jnp.moveaxis(jnp.stack([far, prev, diag]).astype(F32), -1, 0)
    return jnp.concatenate([tiles, tiles], axis=2)


def _attn_sample_kernel(q_ref, kp_ref, vp_ref, kn_ref, vn_ref, bp_ref, bn_ref, lam_ref, subln_ref, o_ref,
                        *, out_scale):
    dh = DIFF_HEAD_DIM
    q = q_ref[0] * (dh ** -0.5)
    kp = kp_ref[0].astype(BF16)
    kn = kn_ref[0].astype(BF16)
    vp = vp_ref[0].astype(BF16)
    vn = vn_ref[0].astype(BF16)
    outs = []
    for t in range(2):
        qt = q[:, t * dh:(t + 1) * dh].astype(BF16)
        sp = _dot_nt(qt, kp[:, t * dh:(t + 1) * dh]) + bp_ref[0]
        sn = _dot_nt(qt, kn[:, t * dh:(t + 1) * dh]) + bn_ref[0]
        m = jnp.maximum(jnp.max(sp, axis=-1, keepdims=True), jnp.max(sn, axis=-1, keepdims=True))
        pp = jnp.exp(sp - m)
        pn = jnp.exp(sn - m)
        den = jnp.sum(pp, axis=-1, keepdims=True) + jnp.sum(pn, axis=-1, keepdims=True)
        outs.append((_dot(pp.astype(BF16), vp) + _dot(pn.astype(BF16), vn)) / den)
    o_ref[0] = _diff_finish(outs[0], outs[1], lam_ref, subln_ref, out_scale)


def _attn_sample(qb, k_past, v_past, k_new, v_new, bias_past, bias_new, lam_row, subln, out_scale):
    b, l, _ = qb.shape
    p = k_past.shape[1]
    hw = 2 * DIFF_HEAD_DIM
    return pl.pallas_call(
        functools.partial(_attn_sample_kernel, out_scale=out_scale),
        grid=(b, DIFF_HEADS),
        in_specs=[pl.BlockSpec((1, l, hw), lambda bi, h: (bi, 0, h)),
                  pl.BlockSpec((1, p, hw), lambda bi, h: (bi, 0, h)),
                  pl.BlockSpec((1, p, DIFF_V_DIM), lambda bi, h: (bi, 0, h)),
                  pl.BlockSpec((1, l, hw), lambda bi, h: (bi, 0, h)),
                  pl.BlockSpec((1, l, DIFF_V_DIM), lambda bi, h: (bi, 0, h)),
                  pl.BlockSpec((1, l, p), lambda bi, h: (h, 0, 0)),
                  pl.BlockSpec((1, l, l), lambda bi, h: (h, 0, 0)),
                  pl.BlockSpec((1, DIFF_V_DIM), lambda bi, h: (0, 0)),
                  pl.BlockSpec((1, DIFF_V_DIM), lambda bi, h: (0, 0))],
        out_specs=pl.BlockSpec((1, l, DIFF_V_DIM), lambda bi, h: (bi, 0, h)),
        out_shape=jax.ShapeDtypeStruct((b, l, DIFF_WIDTH), F32),
        compiler_params=_params(("parallel", "parallel")),
        name="attn_sample",
    )(qb, k_past, v_past, k_new, v_new, bias_past, bias_new, lam_row, subln.reshape(1, DIFF_V_DIM))


def _sample_bias(rel_table, p, l):
    rel = jnp.arange(-(p + l - 1), l)
    by_rel = rel_table[_rel_bucket(rel)].astype(F32).T
    bias = jnp.stack([lax.slice_in_dim(by_rel, l - 1 - i, p + 2 * l - 1 - i, axis=1) for i in range(l)], axis=1)
    return bias[:, :, :p], bias[:, :, p:]


def _outproj_kernel(oa_ref, ob_ref, x_ref, mod_ref, n2_ref, wo_ref, wq_ref, keys_ref,
                    x1_ref, h2_ref, sc_ref):
    mixed = jnp.concatenate([oa_ref[0], ob_ref[0]], axis=-1).astype(BF16)
    x1 = x_ref[0] + mod_ref[0, 2:3, :] * _dot(mixed, wo_ref[...])
    x1_ref[0] = x1
    h2 = _modulated_norm(x1, n2_ref[...], mod_ref[0, 3:4, :], mod_ref[0, 4:5, :])
    h2_ref[0] = h2
    qh = _dot(h2.astype(BF16), wq_ref[...]).astype(BF16)
    for hp in range(2 * PEER_HEADS):
        sc_ref[0, hp] = _dot_nt(keys_ref[hp], qh[:, hp * PEER_HALF:(hp + 1) * PEER_HALF])


def _outproj(o_a, o_b, x, mod, norm2, w_out, w_q, keys):
    b, l, d = x.shape
    tl = min(l, 256)
    nhp = 2 * PEER_HEADS
    return pl.pallas_call(
        _outproj_kernel,
        grid=(b, l // tl),
        in_specs=[pl.BlockSpec((1, tl, GDN_WIDTH), lambda i, j: (i, j, 0)),
                  pl.BlockSpec((1, tl, DIFF_WIDTH), lambda i, j: (i, j, 0)),
                  pl.BlockSpec((1, tl, d), lambda i, j: (i, j, 0)),
                  pl.BlockSpec((1, 6, d), lambda i, j: (i, 0, 0)),
                  pl.BlockSpec((1, d), lambda i, j: (0, 0)),
                  pl.BlockSpec((d, d), lambda i, j: (0, 0)),
                  pl.BlockSpec((d, nhp * PEER_HALF), lambda i, j: (0, 0)),
                  pl.BlockSpec((nhp, N_KEYS, PEER_HALF), lambda i, j: (0, 0, 0))],
        out_specs=[pl.BlockSpec((1, tl, d), lambda i, j: (i, j, 0)),
                   pl.BlockSpec((1, tl, d), lambda i, j: (i, j, 0)),
                   pl.BlockSpec((1, nhp, N_KEYS, tl), lambda i, j: (i, 0, 0, j))],
        out_shape=[jax.ShapeDtypeStruct((b, l, d), F32),
                   jax.ShapeDtypeStruct((b, l, d), F32),
                   jax.ShapeDtypeStruct((b, nhp, N_KEYS, l), F32)],
        compiler_params=_params(("parallel", "parallel")),
        name="outproj",
    )(o_a, o_b, x, mod, norm2.reshape(1, d), w_out, w_q, keys)


def _top16_rows(s, ids, n):
    vals, idxs = [], []
    for _ in range(PEER_TOPK):
        m = jnp.max(s, axis=0, keepdims=True)
        i = jnp.min(jnp.where(s == m, ids, n), axis=0, keepdims=True)
        vals.append(m)
        idxs.append(i)
        s = jnp.where(ids == i, -jnp.inf, s)
    return jnp.concatenate(vals, axis=0), jnp.concatenate(idxs, axis=0)


def _pair_candidates(s1, s2):
    t = s1.shape[1]
    sub16 = lax.broadcasted_iota(jnp.int32, (PEER_TOPK, t), 0)
    sub8 = sub16[:8]
    cand = [s1[0:1] + s2] + [s1[a:a + 1] + s2[:8] for a in range(1, 8)] + [s1[8:] + s2[0:1]]
    pos = [sub16] + [a * PEER_TOPK + sub8 for a in range(1, 8)] + [(8 + sub8) * PEER_TOPK]
    return jnp.concatenate(cand, axis=0), jnp.concatenate(pos, axis=0)


def _pick_rows(table, sel):
    out = jnp.zeros_like(table)
    for a in range(PEER_TOPK):
        out = jnp.where(sel == a, table[a:a + 1, :], out)
    return out


def _topk_kernel(sc_ref, eidx_ref, gate_ref):
    eidx, gates = [], []
    key_ids = lax.broadcasted_iota(jnp.int32, sc_ref.shape[2:], 0)
    for h in range(PEER_HEADS):
        s1, i1 = _top16_rows(sc_ref[0, 2 * h], key_ids, N_KEYS)
        s2, i2 = _top16_rows(sc_ref[0, 2 * h + 1], key_ids, N_KEYS)
        cand, cand_pos = _pair_candidates(s1, s2)
        top_s, pos = _top16_rows(cand, cand_pos, PEER_TOPK * PEER_TOPK)
        eidx.append(_pick_rows(i1, pos // PEER_TOPK) * N_KEYS + _pick_rows(i2, pos % PEER_TOPK))
        e = jnp.exp(top_s - top_s[0:1, :])
        gates.append(e / jnp.sum(e, axis=0, keepdims=True))
    eidx_ref[...] = jnp.concatenate(eidx, axis=0).T
    gate_ref[...] = jnp.concatenate(gates, axis=0).T


def _topk(scores):
    b, nhp, nk, l = scores.shape
    tt = min(l, LANES)
    nt = l // tt
    return pl.pallas_call(
        _topk_kernel,
        grid=(b, nt),
        in_specs=[pl.BlockSpec((1, nhp, nk, tt), lambda i, j: (i, 0, 0, j))],
        out_specs=[pl.BlockSpec((tt, PEER_SLOTS), lambda i, j: (i * nt + j, 0)),
                   pl.BlockSpec((tt, PEER_SLOTS), lambda i, j: (i * nt + j, 0))],
        out_shape=[jax.ShapeDtypeStruct((b * l, PEER_SLOTS), jnp.int32),
                   jax.ShapeDtypeStruct((b * l, PEER_SLOTS), F32)],
        compiler_params=_params(("parallel", "parallel")),
        name="topk",
    )(scores)


_SC_ROWS = 16
_SC_SLOTS = 8


def _pack_kernel(t_ref, o_ref):
    half = t_ref.shape[2] // 2
    bits = lax.bitcast_convert_type(t_ref[0].astype(BF16).astype(F32), jnp.int32)
    o_ref[...] = lax.shift_right_logical(bits[:, :half], 16) | bits[:, half:]


def _pack_bf16_halves(t):
    _, v, d = t.shape
    tv = 512
    return pl.pallas_call(
        _pack_kernel,
        grid=(v // tv,),
        in_specs=[pl.BlockSpec((1, tv, d), lambda i: (0, i, 0))],
        out_specs=pl.BlockSpec((tv, d // 2), lambda i: (i, 0)),
        out_shape=jax.ShapeDtypeStruct((v, d // 2), jnp.int32),
        compiler_params=_params(("parallel",)),
        name="pack",
    )(t)


def _sc_mesh():
    info = plsc.get_sparse_core_info()
    mesh = plsc.VectorSubcoreMesh(core_axis_name="c", subcore_axis_name="s")
    return mesh, info.num_cores, info.num_cores * info.num_subcores


def _sc_gather(tab, idx):
    n = idx.shape[0]
    w = tab.shape[1]
    mesh, ncores, nw = _sc_mesh()
    per_w = n // nw
    ns, nr = _SC_SLOTS, _SC_ROWS
    ngroups = per_w // (ns * nr)
    assert n % (nw * ns * nr) == 0

    @functools.partial(
        pl.kernel, mesh=mesh, out_type=jax.ShapeDtypeStruct((n, w), tab.dtype),
        scratch_types=([pltpu.VMEM((per_w,), jnp.int32)] + [pltpu.VMEM((nr, w), tab.dtype)] * ns
                       + [pltpu.SemaphoreType.DMA] * (2 * ns)),
    )
    def k(tab_hbm, idx_hbm, out_hbm, idx_v, *scratch):
        bufs, gsem, wsem = scratch[:ns], scratch[ns:2 * ns], scratch[2 * ns:]
        base = (lax.axis_index("s") * ncores + lax.axis_index("c")) * per_w
        pltpu.sync_copy(idx_hbm.at[pl.ds(pl.multiple_of(base, nr), per_w)], idx_v)

        def gather(c, s):
            picks = idx_v.at[pl.ds(pl.multiple_of(c * nr, nr), nr)]
            return pltpu.make_async_copy(tab_hbm.at[picks], bufs[s], gsem[s])

        def write(c, s):
            span = pl.ds(pl.multiple_of(base + c * nr, nr), nr)
            return pltpu.make_async_copy(bufs[s], out_hbm.at[span], wsem[s])

        for s in range(ns):
            gather(s, s).start()

        @pl.loop(0, ngroups)
        def _(g):
            for s in range(ns):
                gather(g * ns + s, s).wait()
                write(g * ns + s, s).start()
            for s in range(ns):
                write(g * ns + s, s).wait()

                @pl.when(g < ngroups - 1)
                def _():
                    gather((g + 1) * ns + s, s).start()

    return k(tab, idx)


_ACC_ROWS = 64
_ACC_TOKENS = 8


def _sc_accumulate(tab, idx, coef):
    n = idx.shape[0]
    w = tab.shape[1]
    d = 2 * w
    lanes = 16
    mesh, ncores, nw = _sc_mesh()
    ntok = n // PEER_SLOTS
    tok_w = ntok // nw
    per_w = tok_w * PEER_SLOTS
    nchunks = per_w // _ACC_ROWS
    assert ntok % (nw * _ACC_TOKENS) == 0 and PEER_SLOTS == 2 * _ACC_ROWS and w % (4 * lanes) == 0

    @functools.partial(
        pl.kernel, mesh=mesh, out_type=jax.ShapeDtypeStruct((ntok, d), F32),
        scratch_types=[pltpu.VMEM((per_w,), jnp.int32), pltpu.VMEM((per_w,), F32),
                       pltpu.VMEM((_ACC_ROWS, w), tab.dtype), pltpu.VMEM((_ACC_ROWS, w), tab.dtype),
                       pltpu.VMEM((_ACC_TOKENS, d), F32), pltpu.SemaphoreType.DMA, pltpu.SemaphoreType.DMA],
        compiler_params=pltpu.CompilerParams(needs_layout_passes=False),
    )
    def k(tab_hbm, idx_hbm, coef_hbm, out_hbm, idx_v, coef_v, rows0, rows1, acc_v, sem0, sem1):
        rows, sems = (rows0, rows1), (sem0, sem1)
        wid = lax.axis_index("s") * ncores + lax.axis_index("c")
        base = pl.multiple_of(wid * per_w, _ACC_ROWS)
        pltpu.sync_copy(idx_hbm.at[pl.ds(base, per_w)], idx_v)
        pltpu.sync_copy(coef_hbm.at[pl.ds(base, per_w)], coef_v)

        def gather(c, s):
            picks = idx_v.at[pl.ds(pl.multiple_of(c * _ACC_ROWS, _ACC_ROWS), _ACC_ROWS)]
            return pltpu.make_async_copy(tab_hbm.at[picks], rows[s], sems[s])

        def accumulate(rows_ref, c, arow):
            nrow, ncol = 2, 8

            @pl.loop(0, _ACC_ROWS // nrow)
            def _(q):
                r0 = q * nrow
                cvec = [plsc.load_gather(coef_v, [jnp.full((lanes,), c * _ACC_ROWS + r0 + i, jnp.int32)])
                        for i in range(nrow)]
                groups = [[col0 + j * lanes for j in range(ncol)] for col0 in range(0, w, ncol * lanes)]

                def load(cols):
                    return [[rows_ref[r0 + i, pl.ds(col, lanes)] for i in range(nrow)] for col in cols]

                ahead = load(groups[0])
                for g, cols in enumerate(groups):
                    words = ahead
                    if g + 1 < len(groups):
                        ahead = load(groups[g + 1])
                    sums = []
                    for wds in words:
                        lo = [cvec[i] * lax.bitcast_convert_type(wds[i] << 16, F32) for i in range(nrow)]
                        hi = [cvec[i] * lax.bitcast_convert_type(wds[i] & jnp.int32(-65536), F32)
                              for i in range(nrow)]
                        sums.append((functools.reduce(jnp.add, lo), functools.reduce(jnp.add, hi)))
                    for col, (lo, hi) in zip(cols, sums):
                        plsc.addupdate(acc_v.at[arow, pl.ds(col, lanes)], lo)
                        plsc.addupdate(acc_v.at[arow, pl.ds(w + col, lanes)], hi)

        gather(0, 0).start()

        @pl.loop(0, tok_w)
        def _(t):
            arow = t % _ACC_TOKENS
            for col in range(0, d, lanes):
                acc_v[arow, pl.ds(col, lanes)] = jnp.zeros((lanes,), F32)
            for half in range(2):
                c = 2 * t + half

                @pl.when(c + 1 < nchunks)
                def _():
                    gather(c + 1, 1 - half).start()

                gather(c, half).wait()
                accumulate(rows[half], c, arow)

            @pl.when(arow == _ACC_TOKENS - 1)
            def _():
                first_tok = pl.multiple_of(wid * tok_w + t - (_ACC_TOKENS - 1), _ACC_TOKENS)
                pltpu.sync_copy(acc_v, out_hbm.at[pl.ds(first_tok, _ACC_TOKENS)])

    return k(tab, idx, coef)


_PEER_TOKENS = 16


def _unpack_bf16_halves(w):
    lo = lax.bitcast_convert_type(w << 16, F32)
    hi = lax.bitcast_convert_type(w & jnp.int32(-65536), F32)
    return lo, hi


def _coef_kernel(ug_ref, gate_ref, h2_ref, coef_ref):
    half = D_MODEL // 2
    gate_t = gate_ref[...].T
    cols = []
    for t in range(_PEER_TOKENS):
        u_lo, u_hi = _unpack_bf16_halves(ug_ref[t * PEER_SLOTS:(t + 1) * PEER_SLOTS, :])
        pre = jnp.sum(u_lo * h2_ref[t:t + 1, :half] + u_hi * h2_ref[t:t + 1, half:], axis=-1, keepdims=True)
        act = 0.5 * pre * (1.0 + lax.erf(pre * (2.0 ** -0.5)))
        cols.append(gate_t[:, t:t + 1] * act)
    coef_ref[...] = jnp.concatenate(cols, axis=1).T


def _peer_coef(ug, gate, h2):
    n, d = h2.shape
    tp = _PEER_TOKENS
    return pl.pallas_call(
        _coef_kernel,
        grid=(n // tp,),
        in_specs=[pl.BlockSpec((tp * PEER_SLOTS, d // 2), lambda i: (i, 0)),
                  pl.BlockSpec((tp, PEER_SLOTS), lambda i: (i, 0)),
                  pl.BlockSpec((tp, d), lambda i: (i, 0))],
        out_specs=pl.BlockSpec((tp, PEER_SLOTS), lambda i: (i, 0)),
        out_shape=jax.ShapeDtypeStruct((n, PEER_SLOTS), F32),
        compiler_params=_params(("parallel",)),
        name="peer_coef",
    )(ug, gate, h2)


def _final_kernel(acc_ref, x1_ref, g2_ref, fn_ref, y_ref):
    x2 = x1_ref[0] + g2_ref[0, 5:6, :] * acc_ref[0]
    y_ref[0] = x2 * lax.rsqrt(jnp.mean(x2 * x2, axis=-1, keepdims=True) + EPS) * fn_ref[...]


def _peer_final(acc, x1, mod, final_norm):
    b, l, d = x1.shape
    tl = min(l, 512)
    return pl.pallas_call(
        _final_kernel,
        grid=(b, l // tl),
        in_specs=[pl.BlockSpec((1, tl, d), lambda i, j: (i, j, 0)),
                  pl.BlockSpec((1, tl, d), lambda i, j: (i, j, 0)),
                  pl.BlockSpec((1, 6, d), lambda i, j: (i, 0, 0)),
                  pl.BlockSpec((1, d), lambda i, j: (0, 0))],
        out_specs=pl.BlockSpec((1, tl, d), lambda i, j: (i, j, 0)),
        out_shape=jax.ShapeDtypeStruct((b, l, d), F32),
        compiler_params=_params(("parallel", "parallel")),
        name="peer_final",
    )(acc, x1, mod, final_norm.reshape(1, d))


def _front(x, mod, conv0, s0, k_past, v_past, wts, prompt):
    b, l, d = x.shape
    qkv, z, ab, qb, kb, vb, kh, vh = _inproj(x, mod, wts["norm1"], wts["w_in"])
    chunk = CHUNK if prompt else l
    o_a, s_new = _gdn(qkv, z, ab, conv0, s0, wts["conv_w"], wts["alog"], wts["dtb"], wts["onorm"], chunk)
    conv_new = qkv[:, l - (CONV_W - 1):, :]
    if prompt:
        o_b = _attn_prompt(qb, kh, vh, wts["bias_prompt"], wts["lam"], wts["subln"], wts["out_scale"])
    else:
        p = k_past.shape[1]
        bias_past, bias_new = _sample_bias(wts["rel_table"], p, l)
        o_b = _attn_sample(qb, k_past.reshape(b, p, DIFF_WIDTH), v_past.reshape(b, p, DIFF_WIDTH), kb, vb,
                           bias_past, bias_new, wts["lam"], wts["subln"], wts["out_scale"])
    x1, h2, scores = _outproj(o_a, o_b, x, mod, wts["norm2"], wts["w_out"], wts["w_q"], wts["keys"])
    eidx, gate = _topk(scores)
    eidx = eidx.reshape(-1)
    ug = _sc_gather(wts["peer_u"], eidx)
    return (ug, eidx, gate, h2, x1, mod), (kb, vb, s_new, conv_new)


def _back(pending, wts):
    ug, eidx, gate, h2, x1, mod = pending
    b, l, d = h2.shape
    coef = _peer_coef(ug, gate, h2.reshape(b * l, d))
    acc = _sc_accumulate(wts["peer_v"], eidx, coef.reshape(-1))
    return _peer_final(acc.reshape(b, l, d), x1, mod, wts["final_norm"])


def _prompt_rows(x, mod, wts, sample_front):
    b, l, d = x.shape
    conv0 = jnp.zeros((1, CONV_W - 1, QKV_A), F32)
    s0 = jnp.zeros((1, GDN_HEADS, GDN_HEAD_DIM, GDN_HEAD_DIM), F32)
    news, ys, sample = [], [], None
    for t in range(b):
        pend, new = _front(lax.dynamic_slice_in_dim(x, t, 1, 0), lax.dynamic_slice_in_dim(mod, t, 1, 0),
                           conv0, s0, None, None, wts, True)
        if t == b - 1:
            gate, sample = sample_front(pend[2])
            pend = pend[:2] + (gate,) + pend[3:]
        news.append(new)
        ys.append(_back(pend, wts))
    y = jnp.concatenate(ys, axis=0)
    return y, [jnp.concatenate([n[k] for n in news], axis=0) for k in range(4)], sample


def _cache_entries(new, b, l):
    kb, vb, s_new, conv_new = new
    return (kb.reshape(1, b, l, DIFF_HEADS, 2 * DIFF_HEAD_DIM), vb.reshape(1, b, l, DIFF_HEADS, DIFF_V_DIM),
            s_new[None], conv_new[None])


def kernel(x_prompt, x_sample, c_prompt, c_sample, cache_k, cache_v, state_gdn, state_conv, w_ada, b_ada,
           norm1, norm2, w_in, conv_w, a_log, dt_bias, gdn_onorm, lam_q1, lam_k1, lam_q2, lam_k2, diff_subln,
           w_out, peer_wq, peer_keys, peer_u, peer_v, rel_table, final_norm):
    assert w_ada.shape[0] == 1, "single-layer step"
    bp = x_prompt.shape[0]
    d = D_MODEL
    lam_init = 0.8 - 0.6 * math.exp(-0.3 * 0)
    lam = (jnp.exp(jnp.sum(lam_q1[0] * lam_k1[0])) - jnp.exp(jnp.sum(lam_q2[0] * lam_k2[0])) + lam_init)
    w = w_in[0]
    w_packed = jnp.concatenate(
        [w[:, :_C_AB], jnp.pad(w[:, 2048:2056], ((0, 0), (0, LANES - 2 * GDN_HEADS))), w[:, 2056:]],
        axis=1).astype(BF16)
    wts = dict(
        norm1=norm1[0], norm2=norm2[0], w_in=w_packed, conv_w=conv_w[0],
        alog=jnp.pad(a_log[0], (0, LANES - GDN_HEADS)).reshape(1, LANES),
        dtb=jnp.pad(dt_bias[0], (0, LANES - GDN_HEADS)).reshape(1, LANES),
        onorm=gdn_onorm[0], lam=jnp.full((1, DIFF_V_DIM), lam, F32), subln=diff_subln[0],
        out_scale=1.0 - lam_init, bias_prompt=_prompt_bias_tiles(rel_table), rel_table=rel_table,
        w_out=w_out[0].astype(BF16), w_q=peer_wq[0].astype(BF16),
        keys=peer_keys[0].reshape(2 * PEER_HEADS, N_KEYS, PEER_HALF).astype(BF16),
        peer_u=_pack_bf16_halves(peer_u), peer_v=_pack_bf16_halves(peer_v), final_norm=final_norm)

    mod = _ada(jnp.concatenate([c_prompt, c_sample], axis=0), w_ada[0], b_ada[0]).reshape(-1, 6, d)
    def sample_front(dep):
        xs, dep = lax.optimization_barrier((x_sample, dep))
        return dep, _front(xs, mod[bp:], state_conv[0], state_gdn[0], cache_k[0], cache_v[0], wts, False)

    yp, new_p, (pend_s, new_s) = _prompt_rows(x_prompt, mod[:bp], wts, sample_front)
    ys = _back(pend_s, wts)
    kp, vp, sp, cp = _cache_entries(new_p, *x_prompt.shape[:2])
    ks, vs, ss, cs = _cache_entries(new_s, *x_sample.shape[:2])
    return yp, ys, kp, vp, sp, cp, ks, vs, ss, cs
```

```python
import functools
import math

import jax
import jax.numpy as jnp
from jax import lax
from jax.experimental import pallas as pl
from jax.experimental.pallas import tpu as pltpu
from jax.experimental.pallas import tpu_sc as plsc

F32 = jnp.float32
BF16 = jnp.bfloat16
EPS = 1e-6

D_MODEL = 1024
CHUNK = 64
GDN_HEADS = 4
GDN_HEAD_DIM = 128
GDN_WIDTH = GDN_HEADS * GDN_HEAD_DIM
CONV_W = 4
QKV_A = 3 * GDN_WIDTH
DIFF_HEADS = 4
DIFF_HEAD_DIM = 64
DIFF_V_DIM = 128
DIFF_WIDTH = DIFF_HEADS * 2 * DIFF_HEAD_DIM
ATT_BLOCK = 256
N_BUCKETS = 32
REL_MAX_DIST = 128
PEER_HEADS = 8
N_KEYS = 128
PEER_HALF = 128
PEER_TOPK = 16
PEER_SLOTS = PEER_HEADS * PEER_TOPK
LANES = 128
NEG_BIG = -1e30
VMEM_LIMIT = 56 * 1024 * 1024

_C_QKV, _C_Z, _C_AB, _C_QB, _C_KB, _C_VB = 0, 1536, 2048, 2176, 2688, 3200
_C_END = 3712


def _params(sem):
    return pltpu.CompilerParams(dimension_semantics=sem, vmem_limit_bytes=VMEM_LIMIT)


def _dot(a, b, precision=None):
    return jnp.dot(a, b, preferred_element_type=F32, precision=precision)


def _dot_nt(a, b, precision=None):
    return lax.dot_general(a, b, (((1,), (1,)), ((), ())), preferred_element_type=F32, precision=precision)


def _silu(x):
    return x * jax.nn.sigmoid(x)


def _ada_kernel(c_ref, w_ref, b_ref, o_ref):
    a = _silu(c_ref[...]).astype(BF16)
    o_ref[...] = _dot(a, w_ref[...].astype(BF16)) + b_ref[...]


def _ada(c, w_ada, b_ada):
    n, d = c.shape
    cols = w_ada.shape[1]
    tn = 1024
    return pl.pallas_call(
        _ada_kernel,
        grid=(cols // tn,),
        in_specs=[pl.BlockSpec((n, d), lambda j: (0, 0)),
                  pl.BlockSpec((d, tn), lambda j: (0, j)),
                  pl.BlockSpec((1, tn), lambda j: (0, j))],
        out_specs=pl.BlockSpec((n, tn), lambda j: (0, j)),
        out_shape=jax.ShapeDtypeStruct((n, cols), F32),
        compiler_params=_params(("parallel",)),
        name="ada",
    )(c, w_ada, b_ada.reshape(1, cols))


def _modulated_norm(x, gain, shift, scale):
    y = x * lax.rsqrt(jnp.mean(x * x, axis=-1, keepdims=True) + EPS)
    return (y * gain) * (1.0 + scale) + shift


def _inproj_kernel(x_ref, mod_ref, n1_ref, w_ref, qkv_ref, z_ref, ab_ref, qb_ref, kb_ref, vb_ref, kh_ref, vh_ref):
    h = _modulated_norm(x_ref[0], n1_ref[...], mod_ref[0, 0:1, :], mod_ref[0, 1:2, :]).astype(BF16)
    qkv_ref[0] = _dot(h, w_ref[:, _C_QKV:_C_Z])
    z_ref[0] = _dot(h, w_ref[:, _C_Z:_C_AB])
    ab_ref[0] = _dot(h, w_ref[:, _C_AB:_C_QB])
    qb_ref[0] = _dot(h, w_ref[:, _C_QB:_C_KB])
    kb = _dot(h, w_ref[:, _C_KB:_C_VB])
    vb = _dot(h, w_ref[:, _C_VB:_C_END])
    kb_ref[0] = kb
    vb_ref[0] = vb
    kh_ref[0] = kb.astype(BF16)
    vh_ref[0] = vb.astype(BF16)


def _inproj(x, mod, norm1, w_packed):
    b, l, d = x.shape
    tl = min(l, 256)
    widths = (QKV_A, GDN_WIDTH, LANES, DIFF_WIDTH, DIFF_WIDTH, DIFF_WIDTH, DIFF_WIDTH, DIFF_WIDTH)
    dtypes = (F32,) * 6 + (BF16,) * 2
    return pl.pallas_call(
        _inproj_kernel,
        grid=(b, l // tl),
        in_specs=[pl.BlockSpec((1, tl, d), lambda i, j: (i, j, 0)),
                  pl.BlockSpec((1, 6, d), lambda i, j: (i, 0, 0)),
                  pl.BlockSpec((1, d), lambda i, j: (0, 0)),
                  pl.BlockSpec((d, _C_END), lambda i, j: (0, 0))],
        out_specs=[pl.BlockSpec((1, tl, w), lambda i, j: (i, j, 0)) for w in widths],
        out_shape=[jax.ShapeDtypeStruct((b, l, w), dt) for w, dt in zip(widths, dtypes)],
        compiler_params=_params(("parallel", "parallel")),
        name="inproj",
    )(x, mod, norm1.reshape(1, d), w_packed)


_HI = lax.Precision.HIGHEST


def _unit_lower_inverses(mats, n):
    r = lax.broadcasted_iota(jnp.int32, (n, n), 0)
    c = lax.broadcasted_iota(jnp.int32, (n, n), 1)
    eye = (r == c).astype(F32)
    ad = [jnp.where((r // 8) == (c // 8), a, 0.0) for a in mats]
    a2 = [_dot(m, m, _HI) for m in ad]
    a4 = [_dot(m, m, _HI) for m in a2]
    xs = [eye - m for m in ad]
    xs = [x + _dot(x, m, _HI) for x, m in zip(xs, a2)]
    xs = [x + _dot(x, m, _HI) for x, m in zip(xs, a4)]
    bs = 8
    while bs < n:
        off = ((r // (2 * bs)) == (c // (2 * bs))) & ((r // bs) != (c // bs))
        ys = [_dot(jnp.where(off, a, 0.0), x, _HI) for a, x in zip(mats, xs)]
        xs = [x - _dot(x, y, _HI) for x, y in zip(xs, ys)]
        bs *= 2
    return xs


def _mm(a, b):
    return _dot(a.astype(BF16), b.astype(BF16))


def _mm_nt(a, b):
    return _dot_nt(a.astype(BF16), b.astype(BF16))


def _gdn_kernel(qkv_ref, z_ref, ab_ref, conv0_ref, s0_ref, cw_ref, alog_ref, dtb_ref, onorm_ref,
                o_ref, s_ref, xbuf, *, chunk, nch):
    hd = GDN_HEAD_DIM
    rows = chunk * nch

    @pl.when(pl.program_id(1) == 0)
    def _():
        xbuf[5:8, :] = conv0_ref[0]
        s_ref[0] = s0_ref[0]

    x = qkv_ref[0]
    xbuf[8:8 + rows, :] = x
    y = (xbuf[5:5 + rows, :] * cw_ref[0:1, :] + xbuf[6:6 + rows, :] * cw_ref[1:2, :]
         + xbuf[7:7 + rows, :] * cw_ref[2:3, :] + x * cw_ref[3:4, :])
    xbuf[5:8, :] = x[rows - 3:rows, :]
    y = _silu(y)

    ab = ab_ref[0]
    t = ab + dtb_ref[...]
    softplus = jnp.maximum(t, 0.0) + jnp.log(1.0 + jnp.exp(-jnp.abs(t)))
    g = -jnp.exp(alog_ref[...]) * softplus
    beta = jax.nn.sigmoid(ab)

    r = lax.broadcasted_iota(jnp.int32, (chunk, chunk), 0)
    c = lax.broadcasted_iota(jnp.int32, (chunk, chunk), 1)
    lower = r >= c
    tri = lower.astype(F32)

    heads = range(GDN_HEADS)
    pairs = [(ci, h) for ci in range(nch) for h in heads]
    rows_of = {ci: slice(ci * chunk, (ci + 1) * chunk) for ci in range(nch)}
    gc = {ci: _dot(tri, g[rows_of[ci]], _HI) for ci in range(nch)}
    gc_t = {ci: gc[ci].T for ci in range(nch)}
    q, k, vb, kb, decay, egc, g_last = {}, {}, {}, {}, {}, {}, {}
    for ci, h in pairs:
        sl = rows_of[ci]
        qh = y[sl, h * hd:(h + 1) * hd]
        kh = y[sl, GDN_WIDTH + h * hd:GDN_WIDTH + (h + 1) * hd]
        p = ci, h
        q[p] = qh * lax.rsqrt(jnp.sum(qh * qh, axis=-1, keepdims=True) + EPS) * (hd ** -0.5)
        k[p] = kh * lax.rsqrt(jnp.sum(kh * kh, axis=-1, keepdims=True) + EPS)
        gcol = gc[ci][:, h:h + 1]
        bcol = beta[sl, GDN_HEADS + h:GDN_HEADS + h + 1]
        decay[p] = jnp.exp(jnp.where(lower, gcol - gc_t[ci][h:h + 1, :], NEG_BIG))
        kb[p] = k[p] * bcol
        vb[p] = y[sl, 2 * GDN_WIDTH + h * hd:2 * GDN_WIDTH + (h + 1) * hd] * bcol
        egc[p] = jnp.exp(gcol)
        g_last[p] = gcol[chunk - 1:chunk, :]
    kk = {p: _mm_nt(kb[p], k[p]) for p in pairs}
    tinv = dict(zip(pairs, _unit_lower_inverses([jnp.where(r > c, kk[p] * decay[p], 0.0) for p in pairs], chunk)))
    u_v = {p: _dot(tinv[p], vb[p], _HI) for p in pairs}
    w = {p: _dot(tinv[p], kb[p] * egc[p], _HI) for p in pairs}
    qk = {p: _mm_nt(q[p], k[p]) * decay[p] for p in pairs}
    k_dec_t = {p: (k[p] * jnp.exp(g_last[p] - gc[p[0]][:, p[1]:p[1] + 1])).T for p in pairs}

    s = [s_ref[0, h] for h in heads]
    for ci in range(nch):
        ws = [_mm(w[ci, h], s[h]) for h in heads]
        qs = [_mm(q[ci, h] * egc[ci, h], s[h]) for h in heads]
        v_new = [u_v[ci, h] - ws[h] for h in heads]
        o = [qs[h] + _mm(qk[ci, h], v_new[h]) for h in heads]
        s = [s[h] * jnp.exp(g_last[ci, h]) + _mm(k_dec_t[ci, h], v_new[h]) for h in heads]
        for h in heads:
            oh = o[h] * lax.rsqrt(jnp.mean(o[h] * o[h], axis=-1, keepdims=True) + EPS) * onorm_ref[...]
            o_ref[0, rows_of[ci], h * hd:(h + 1) * hd] = oh * _silu(z_ref[0, rows_of[ci], h * hd:(h + 1) * hd])
    for h in heads:
        s_ref[0, h] = s[h]


def _gdn(qkv, z, ab, conv0, s0, conv_w, alog_pad, dtb_pad, onorm, chunk):
    b, l, _ = qkv.shape
    hd = GDN_HEAD_DIM
    nch = 2 if l % (2 * chunk) == 0 else 1
    rows = chunk * nch
    return pl.pallas_call(
        functools.partial(_gdn_kernel, chunk=chunk, nch=nch),
        grid=(b, l // rows),
        in_specs=[pl.BlockSpec((1, rows, QKV_A), lambda i, j: (i, j, 0)),
                  pl.BlockSpec((1, rows, GDN_WIDTH), lambda i, j: (i, j, 0)),
                  pl.BlockSpec((1, rows, LANES), lambda i, j: (i, j, 0)),
                  pl.BlockSpec((1, CONV_W - 1, QKV_A), lambda i, j: (i, 0, 0)),
                  pl.BlockSpec((1, GDN_HEADS, hd, hd), lambda i, j: (i, 0, 0, 0)),
                  pl.BlockSpec((CONV_W, QKV_A), lambda i, j: (0, 0)),
                  pl.BlockSpec((1, LANES), lambda i, j: (0, 0)),
                  pl.BlockSpec((1, LANES), lambda i, j: (0, 0)),
                  pl.BlockSpec((1, hd), lambda i, j: (0, 0))],
        out_specs=[pl.BlockSpec((1, rows, GDN_WIDTH), lambda i, j: (i, j, 0)),
                   pl.BlockSpec((1, GDN_HEADS, hd, hd), lambda i, j: (i, 0, 0, 0))],
        out_shape=[jax.ShapeDtypeStruct((b, l, GDN_WIDTH), F32),
                   jax.ShapeDtypeStruct((b, GDN_HEADS, hd, hd), F32)],
        scratch_shapes=[pltpu.VMEM((8 + rows, QKV_A), F32)],
        compiler_params=_params(("parallel", "arbitrary")),
        name="gdn",
    )(qkv, z, ab, conv0, s0, conv_w, alog_pad, dtb_pad, onorm.reshape(1, hd))


def _rel_bucket(rel):
    nb = N_BUCKETS // 2
    max_exact = nb // 2
    ret = jnp.where(rel > 0, nb, 0)
    n = jnp.abs(rel)
    large = max_exact + (jnp.log(jnp.maximum(n, 1).astype(F32) / max_exact)
                         / math.log(REL_MAX_DIST / max_exact) * (nb - max_exact)).astype(jnp.int32)
    large = jnp.minimum(large, nb - 1)
    return ret + jnp.where(n < max_exact, n, large)


def _diff_finish(o1, o2, lam_ref, subln_ref, out_scale):
    o = o1 - lam_ref[...] * o2
    return o * lax.rsqrt(jnp.mean(o * o, axis=-1, keepdims=True) + EPS) * subln_ref[...] * out_scale


def _attn_prompt_kernel(q_ref, k_ref, v_ref, bias_ref, lam_ref, subln_ref, o_ref, m_ref, l_ref, acc_ref,
                        *, out_scale):
    i = pl.program_id(2)
    tb = ATT_BLOCK
    dh = DIFF_HEAD_DIM
    q = q_ref[0] * (dh ** -0.5)
    lane = lax.broadcasted_iota(jnp.int32, q.shape, 1)
    q2s = jnp.concatenate([jnp.where(lane < dh, q, 0.0), jnp.where(lane >= dh, q, 0.0)], axis=0).astype(BF16)

    def score_tiles(j, tile):
        start = pl.multiple_of(j * tb, tb)
        s = _dot_nt(q2s, k_ref[0, pl.ds(start, tb), :]) + bias_ref[0, tile]
        return [s[:, c:c + LANES] for c in range(0, tb, LANES)]

    def visible_blocks(fn, unroll):
        n_far = jnp.maximum(i - 1, 0)

        @pl.loop(0, n_far // unroll)
        def _(g):
            fn([(unroll * g + u, 0) for u in range(unroll)])

        @pl.loop((n_far // unroll) * unroll, n_far)
        def _(j):
            fn([(j, 0)])

        @pl.when(i > 0)
        def _():
            fn([(i - 1, 1), (i, 2)])

        @pl.when(i == 0)
        def _():
            fn([(i, 2)])

    m_ref[...] = jnp.full(m_ref.shape, NEG_BIG, F32)

    def track_max(blocks):
        tiles = [s for j, tile in blocks for s in score_tiles(j, tile)]
        m_ref[...] = functools.reduce(jnp.maximum, tiles, m_ref[...])

    visible_blocks(track_max, 4)
    m_ref[...] = jnp.broadcast_to(jnp.max(m_ref[...], axis=-1, keepdims=True), m_ref.shape)

    l_ref[...] = jnp.zeros(l_ref.shape, F32)
    acc_ref[...] = jnp.zeros(acc_ref.shape, F32)

    def accumulate(blocks):
        m = m_ref[...]
        l_add, acc_add = [], []
        for j, tile in blocks:
            p = [jnp.exp(s - m) for s in score_tiles(j, tile)]
            l_add.extend(p)
            start = pl.multiple_of(j * tb, tb)
            acc_add.append(_dot(jnp.concatenate(p, axis=-1).astype(BF16), v_ref[0, pl.ds(start, tb), :]))
        l_ref[...] += sum(l_add)
        acc_ref[...] += sum(acc_add)

    visible_blocks(accumulate, 4)
    o = acc_ref[...] / jnp.sum(l_ref[...], axis=-1, keepdims=True)
    o_ref[0] = _diff_finish(o[:tb], o[tb:], lam_ref, subln_ref, out_scale)


def _attn_prompt(qb, kh, vh, bias_tiles, lam_row, subln, out_scale):
    b, l, _ = qb.shape
    tb = ATT_BLOCK
    hw = 2 * DIFF_HEAD_DIM
    return pl.pallas_call(
        functools.partial(_attn_prompt_kernel, out_scale=out_scale),
        grid=(b, DIFF_HEADS, l // tb),
        in_specs=[pl.BlockSpec((1, tb, hw), lambda bi, h, i: (bi, i, h)),
                  pl.BlockSpec((1, l, hw), lambda bi, h, i: (bi, 0, h)),
                  pl.BlockSpec((1, l, DIFF_V_DIM), lambda bi, h, i: (bi, 0, h)),
                  pl.BlockSpec((1, 3, 2 * tb, tb), lambda bi, h, i: (h, 0, 0, 0)),
                  pl.BlockSpec((1, DIFF_V_DIM), lambda bi, h, i: (0, 0)),
                  pl.BlockSpec((1, DIFF_V_DIM), lambda bi, h, i: (0, 0))],
        out_specs=pl.BlockSpec((1, tb, DIFF_V_DIM), lambda bi, h, i: (bi, i, h)),
        out_shape=jax.ShapeDtypeStruct((b, l, DIFF_WIDTH), F32),
        scratch_shapes=[pltpu.VMEM((2 * tb, LANES), F32), pltpu.VMEM((2 * tb, LANES), F32),
                        pltpu.VMEM((2 * tb, DIFF_V_DIM), F32)],
        compiler_params=_params(("parallel", "parallel", "arbitrary")),
        name="attn_prompt",
    )(qb, kh, vh, bias_tiles, lam_row, subln.reshape(1, DIFF_V_DIM))


def _prompt_bias_tiles(rel_table):
    tb = ATT_BLOCK
    qi = jnp.arange(tb)[:, None]
    ki = jnp.arange(tb)[None, :]
    far = jnp.broadcast_to(rel_table[_rel_bucket(jnp.array(-2 * tb))], (tb, tb, DIFF_HEADS))
    prev = rel_table[_rel_bucket(ki - qi - tb)]
    diag = jnp.where(((ki // CHUNK) <= (qi // CHUNK))[..., None], rel_table[_rel_bucket(ki - qi)], NEG_BIG)
    tiles = jnp.moveaxis(jnp.stack([far, prev, diag]).astype(F32), -1, 0)
    return jnp.concatenate([tiles, tiles], axis=2)


def _attn_sample_kernel(q_ref, kp_ref, vp_ref, kn_ref, vn_ref, bp_ref, bn_ref, lam_ref, subln_ref, o_ref,
                        *, out_scale):
    dh = DIFF_HEAD_DIM
    q = q_ref[0] * (dh ** -0.5)
    kp = kp_ref[0].astype(BF16)
    kn = kn_ref[0].astype(BF16)
    vp = vp_ref[0].astype(BF16)
    vn = vn_ref[0].astype(BF16)
    outs = []
    for t in range(2):
        qt = q[:, t * dh:(t + 1) * dh].astype(BF16)
        sp = _dot_nt(qt, kp[:, t * dh:(t + 1) * dh]) + bp_ref[0]
        sn = _dot_nt(qt, kn[:, t * dh:(t + 1) * dh]) + bn_ref[0]
        m = jnp.maximum(jnp.max(sp, axis=-1, keepdims=True), jnp.max(sn, axis=-1, keepdims=True))
        pp = jnp.exp(sp - m)
        pn = jnp.exp(sn - m)
        den = jnp.sum(pp, axis=-1, keepdims=True) + jnp.sum(pn, axis=-1, keepdims=True)
        outs.append((_dot(pp.astype(BF16), vp) + _dot(pn.astype(BF16), vn)) / den)
    o_ref[0] = _diff_finish(outs[0], outs[1], lam_ref, subln_ref, out_scale)


def _attn_sample(qb, k_past, v_past, k_new, v_new, bias_past, bias_new, lam_row, subln, out_scale):
    b, l, _ = qb.shape
    p = k_past.shape[1]
    hw = 2 * DIFF_HEAD_DIM
    return pl.pallas_call(
        functools.partial(_attn_sample_kernel, out_scale=out_scale),
        grid=(b, DIFF_HEADS),
        in_specs=[pl.BlockSpec((1, l, hw), lambda bi, h: (bi, 0, h)),
                  pl.BlockSpec((1, p, hw), lambda bi, h: (bi, 0, h)),
                  pl.BlockSpec((1, p, DIFF_V_DIM), lambda bi, h: (bi, 0, h)),
                  pl.BlockSpec((1, l, hw), lambda bi, h: (bi, 0, h)),
                  pl.BlockSpec((1, l, DIFF_V_DIM), lambda bi, h: (bi, 0, h)),
                  pl.BlockSpec((1, l, p), lambda bi, h: (h, 0, 0)),
                  pl.BlockSpec((1, l, l), lambda bi, h: (h, 0, 0)),
                  pl.BlockSpec((1, DIFF_V_DIM), lambda bi, h: (0, 0)),
                  pl.BlockSpec((1, DIFF_V_DIM), lambda bi, h: (0, 0))],
        out_specs=pl.BlockSpec((1, l, DIFF_V_DIM), lambda bi, h: (bi, 0, h)),
        out_shape=jax.ShapeDtypeStruct((b, l, DIFF_WIDTH), F32),
        compiler_params=_params(("parallel", "parallel")),
        name="attn_sample",
    )(qb, k_past, v_past, k_new, v_new, bias_past, bias_new, lam_row, subln.reshape(1, DIFF_V_DIM))


def _sample_bias(rel_table, p, l):
    rel = jnp.arange(-(p + l - 1), l)
    by_rel = rel_table[_rel_bucket(rel)].astype(F32).T
    bias = jnp.stack([lax.slice_in_dim(by_rel, l - 1 - i, p + 2 * l - 1 - i, axis=1) for i in range(l)], axis=1)
    return bias[:, :, :p], bias[:, :, p:]


def _outproj_kernel(oa_ref, ob_ref, x_ref, mod_ref, n2_ref, wo_ref, wq_ref, keys_ref,
                    x1_ref, h2_ref, sc_ref):
    mixed = jnp.concatenate([oa_ref[0], ob_ref[0]], axis=-1).astype(BF16)
    x1 = x_ref[0] + mod_ref[0, 2:3, :] * _dot(mixed, wo_ref[...])
    x1_ref[0] = x1
    h2 = _modulated_norm(x1, n2_ref[...], mod_ref[0, 3:4, :], mod_ref[0, 4:5, :])
    h2_ref[0] = h2
    qh = _dot(h2.astype(BF16), wq_ref[...]).astype(BF16)
    for hp in range(2 * PEER_HEADS):
        sc_ref[0, hp] = _dot_nt(keys_ref[hp], qh[:, hp * PEER_HALF:(hp + 1) * PEER_HALF])


def _outproj(o_a, o_b, x, mod, norm2, w_out, w_q, keys):
    b, l, d = x.shape
    tl = min(l, 256)
    nhp = 2 * PEER_HEADS
    return pl.pallas_call(
        _outproj_kernel,
        grid=(b, l // tl),
        in_specs=[pl.BlockSpec((1, tl, GDN_WIDTH), lambda i, j: (i, j, 0)),
                  pl.BlockSpec((1, tl, DIFF_WIDTH), lambda i, j: (i, j, 0)),
                  pl.BlockSpec((1, tl, d), lambda i, j: (i, j, 0)),
                  pl.BlockSpec((1, 6, d), lambda i, j: (i, 0, 0)),
                  pl.BlockSpec((1, d), lambda i, j: (0, 0)),
                  pl.BlockSpec((d, d), lambda i, j: (0, 0)),
                  pl.BlockSpec((d, nhp * PEER_HALF), lambda i, j: (0, 0)),
                  pl.BlockSpec((nhp, N_KEYS, PEER_HALF), lambda i, j: (0, 0, 0))],
        out_specs=[pl.BlockSpec((1, tl, d), lambda i, j: (i, j, 0)),
                   pl.BlockSpec((1, tl, d), lambda i, j: (i, j, 0)),
                   pl.BlockSpec((1, nhp, N_KEYS, tl), lambda i, j: (i, 0, 0, j))],
        out_shape=[jax.ShapeDtypeStruct((b, l, d), F32),
                   jax.ShapeDtypeStruct((b, l, d), F32),
                   jax.ShapeDtypeStruct((b, nhp, N_KEYS, l), F32)],
        compiler_params=_params(("parallel", "parallel")),
        name="outproj",
    )(o_a, o_b, x, mod, norm2.reshape(1, d), w_out, w_q, keys)


def _top16_rows(s, ids, n):
    vals, idxs = [], []
    for _ in range(PEER_TOPK):
        m = jnp.max(s, axis=0, keepdims=True)
        i = jnp.min(jnp.where(s == m, ids, n), axis=0, keepdims=True)
        vals.append(m)
        idxs.append(i)
        s = jnp.where(ids == i, -jnp.inf, s)
    return jnp.concatenate(vals, axis=0), jnp.concatenate(idxs, axis=0)


def _pair_candidates(s1, s2):
    t = s1.shape[1]
    sub16 = lax.broadcasted_iota(jnp.int32, (PEER_TOPK, t), 0)
    sub8 = sub16[:8]
    cand = [s1[0:1] + s2] + [s1[a:a + 1] + s2[:8] for a in range(1, 8)] + [s1[8:] + s2[0:1]]
    pos = [sub16] + [a * PEER_TOPK + sub8 for a in range(1, 8)] + [(8 + sub8) * PEER_TOPK]
    return jnp.concatenate(cand, axis=0), jnp.concatenate(pos, axis=0)


def _pick_rows(table, sel):
    out = jnp.zeros_like(table)
    for a in range(PEER_TOPK):
        out = jnp.where(sel == a, table[a:a + 1, :], out)
    return out


def _topk_kernel(sc_ref, eidx_ref, gate_ref):
    eidx, gates = [], []
    key_ids = lax.broadcasted_iota(jnp.int32, sc_ref.shape[2:], 0)
    for h in range(PEER_HEADS):
        s1, i1 = _top16_rows(sc_ref[0, 2 * h], key_ids, N_KEYS)
        s2, i2 = _top16_rows(sc_ref[0, 2 * h + 1], key_ids, N_KEYS)
        cand, cand_pos = _pair_candidates(s1, s2)
        top_s, pos = _top16_rows(cand, cand_pos, PEER_TOPK * PEER_TOPK)
        eidx.append(_pick_rows(i1, pos // PEER_TOPK) * N_KEYS + _pick_rows(i2, pos % PEER_TOPK))
        e = jnp.exp(top_s - top_s[0:1, :])
        gates.append(e / jnp.sum(e, axis=0, keepdims=True))
    eidx_ref[...] = jnp.concatenate(eidx, axis=0).T
    gate_ref[...] = jnp.concatenate(gates, axis=0).T


def _topk(scores):
    b, nhp, nk, l = scores.shape
    tt = min(l, LANES)
    nt = l // tt
    return pl.pallas_call(
        _topk_kernel,
        grid=(b, nt),
        in_specs=[pl.BlockSpec((1, nhp, nk, tt), lambda i, j: (i, 0, 0, j))],
        out_specs=[pl.BlockSpec((tt, PEER_SLOTS), lambda i, j: (i * nt + j, 0)),
                   pl.BlockSpec((tt, PEER_SLOTS), lambda i, j: (i * nt + j, 0))],
        out_shape=[jax.ShapeDtypeStruct((b * l, PEER_SLOTS), jnp.int32),
                   jax.ShapeDtypeStruct((b * l, PEER_SLOTS), F32)],
        compiler_params=_params(("parallel", "parallel")),
        name="topk",
    )(scores)


_SC_ROWS = 32
_SC_SLOTS = 4


def _pack_kernel(t_ref, o_ref):
    half = t_ref.shape[2] // 2
    bits = lax.bitcast_convert_type(t_ref[0].astype(BF16).astype(F32), jnp.int32)
    o_ref[...] = lax.shift_right_logical(bits[:, :half], 16) | bits[:, half:]


def _pack_bf16_halves(t):
    _, v, d = t.shape
    tv = 512
    return pl.pallas_call(
        _pack_kernel,
        grid=(v // tv,),
        in_specs=[pl.BlockSpec((1, tv, d), lambda i: (0, i, 0))],
        out_specs=pl.BlockSpec((tv, d // 2), lambda i: (i, 0)),
        out_shape=jax.ShapeDtypeStruct((v, d // 2), jnp.int32),
        compiler_params=_params(("parallel",)),
        name="pack",
    )(t)


def _sc_mesh():
    info = plsc.get_sparse_core_info()
    mesh = plsc.VectorSubcoreMesh(core_axis_name="c", subcore_axis_name="s")
    return mesh, info.num_cores, info.num_cores * info.num_subcores


def _sc_gather(tab, idx):
    n = idx.shape[0]
    w = tab.shape[1]
    mesh, ncores, nw = _sc_mesh()
    per_w = n // nw
    ns, nr = _SC_SLOTS, _SC_ROWS
    ngroups = per_w // (ns * nr)
    assert n % (nw * ns * nr) == 0

    @functools.partial(
        pl.kernel, mesh=mesh, out_type=jax.ShapeDtypeStruct((n, w), tab.dtype),
        scratch_types=([pltpu.VMEM((per_w,), jnp.int32)] + [pltpu.VMEM((nr, w), tab.dtype)] * ns
                       + [pltpu.SemaphoreType.DMA] * (2 * ns)),
    )
    def k(tab_hbm, idx_hbm, out_hbm, idx_v, *scratch):
        bufs, gsem, wsem = scratch[:ns], scratch[ns:2 * ns], scratch[2 * ns:]
        base = (lax.axis_index("s") * ncores + lax.axis_index("c")) * per_w
        pltpu.sync_copy(idx_hbm.at[pl.ds(pl.multiple_of(base, nr), per_w)], idx_v)

        def gather(c, s):
            picks = idx_v.at[pl.ds(pl.multiple_of(c * nr, nr), nr)]
            return pltpu.make_async_copy(tab_hbm.at[picks], bufs[s], gsem[s])

        def write(c, s):
            span = pl.ds(pl.multiple_of(base + c * nr, nr), nr)
            return pltpu.make_async_copy(bufs[s], out_hbm.at[span], wsem[s])

        for s in range(ns):
            gather(s, s).start()

        @pl.loop(0, ngroups)
        def _(g):
            for s in range(ns):
                gather(g * ns + s, s).wait()
                write(g * ns + s, s).start()
            for s in range(ns):
                write(g * ns + s, s).wait()

                @pl.when(g < ngroups - 1)
                def _():
                    gather((g + 1) * ns + s, s).start()

    return k(tab, idx)


_ACC_ROWS = 64
_ACC_TOKENS = 8


def _sc_accumulate(tab, idx, coef):
    n = idx.shape[0]
    w = tab.shape[1]
    d = 2 * w
    lanes = 16
    mesh, ncores, nw = _sc_mesh()
    ntok = n // PEER_SLOTS
    tok_w = ntok // nw
    per_w = tok_w * PEER_SLOTS
    nchunks = per_w // _ACC_ROWS
    assert ntok % (nw * _ACC_TOKENS) == 0 and PEER_SLOTS == 2 * _ACC_ROWS and w % (4 * lanes) == 0

    @functools.partial(
        pl.kernel, mesh=mesh, out_type=jax.ShapeDtypeStruct((ntok, d), F32),
        scratch_types=[pltpu.VMEM((per_w,), jnp.int32), pltpu.VMEM((per_w,), F32),
                       pltpu.VMEM((_ACC_ROWS, w), tab.dtype), pltpu.VMEM((_ACC_ROWS, w), tab.dtype),
                       pltpu.VMEM((_ACC_TOKENS, d), F32), pltpu.SemaphoreType.DMA, pltpu.SemaphoreType.DMA],
        compiler_params=pltpu.CompilerParams(needs_layout_passes=False),
    )
    def k(tab_hbm, idx_hbm, coef_hbm, out_hbm, idx_v, coef_v, rows0, rows1, acc_v, sem0, sem1):
        rows, sems = (rows0, rows1), (sem0, sem1)
        wid = lax.axis_index("s") * ncores + lax.axis_index("c")
        base = pl.multiple_of(wid * per_w, _ACC_ROWS)
        pltpu.sync_copy(idx_hbm.at[pl.ds(base, per_w)], idx_v)
        pltpu.sync_copy(coef_hbm.at[pl.ds(base, per_w)], coef_v)

        def gather(c, s):
            picks = idx_v.at[pl.ds(pl.multiple_of(c * _ACC_ROWS, _ACC_ROWS), _ACC_ROWS)]
            return pltpu.make_async_copy(tab_hbm.at[picks], rows[s], sems[s])

        def accumulate(rows_ref, c, arow):
            nrow, ncol = 4, 4

            @pl.loop(0, _ACC_ROWS // nrow)
            def _(q):
                r0 = q * nrow
                cvec = [plsc.load_gather(coef_v, [jnp.full((lanes,), c * _ACC_ROWS + r0 + i, jnp.int32)])
                        for i in range(nrow)]
                groups = [[col0 + j * lanes for j in range(ncol)] for col0 in range(0, w, ncol * lanes)]

                def load(cols):
                    return [[rows_ref[r0 + i, pl.ds(col, lanes)] for i in range(nrow)] for col in cols]

                ahead = load(groups[0])
                for g, cols in enumerate(groups):
                    words = ahead
                    if g + 1 < len(groups):
                        ahead = load(groups[g + 1])
                    sums = []
                    for wds in words:
                        lo = [cvec[i] * lax.bitcast_convert_type(wds[i] << 16, F32) for i in range(nrow)]
                        hi = [cvec[i] * lax.bitcast_convert_type(wds[i] & jnp.int32(-65536), F32)
                              for i in range(nrow)]
                        sums.append((functools.reduce(jnp.add, lo), functools.reduce(jnp.add, hi)))
                    for col, (lo, hi) in zip(cols, sums):
                        plsc.addupdate(acc_v.at[arow, pl.ds(col, lanes)], lo)
                        plsc.addupdate(acc_v.at[arow, pl.ds(w + col, lanes)], hi)

        gather(0, 0).start()

        @pl.loop(0, tok_w)
        def _(t):
            arow = t % _ACC_TOKENS
            for col in range(0, d, lanes):
                acc_v[arow, pl.ds(col, lanes)] = jnp.zeros((lanes,), F32)
            for half in range(2):
                c = 2 * t + half

                @pl.when(c + 1 < nchunks)
                def _():
                    gather(c + 1, 1 - half).start()

                gather(c, half).wait()
                accumulate(rows[half], c, arow)

            @pl.when(arow == _ACC_TOKENS - 1)
            def _():
                first_tok = pl.multiple_of(wid * tok_w + t - (_ACC_TOKENS - 1), _ACC_TOKENS)
                pltpu.sync_copy(acc_v, out_hbm.at[pl.ds(first_tok, _ACC_TOKENS)])

    return k(tab, idx, coef)


_DOT_TOKENS = 8


def _sc_dot(tab, idx, x):
    n = idx.shape[0]
    w = tab.shape[1]
    d = 2 * w
    lanes = 16
    mesh, ncores, nw = _sc_mesh()
    ntok = n // PEER_SLOTS
    tok_w = ntok // nw
    per_w = tok_w * PEER_SLOTS
    nchunks = per_w // _ACC_ROWS
    assert ntok % (nw * _DOT_TOKENS) == 0 and PEER_SLOTS == 2 * _ACC_ROWS and _ACC_ROWS % lanes == 0

    @functools.partial(
        pl.kernel, mesh=mesh, out_type=jax.ShapeDtypeStruct((n,), F32),
        scratch_types=[pltpu.VMEM((per_w,), jnp.int32), pltpu.VMEM((per_w,), F32),
                       pltpu.VMEM((_ACC_ROWS, w), tab.dtype), pltpu.VMEM((_ACC_ROWS, w), tab.dtype),
                       pltpu.VMEM((_DOT_TOKENS, d), F32), pltpu.SemaphoreType.DMA, pltpu.SemaphoreType.DMA],
        compiler_params=pltpu.CompilerParams(needs_layout_passes=False),
    )
    def k(tab_hbm, idx_hbm, x_hbm, pre_hbm, idx_v, pre_v, rows0, rows1, x_v, sem0, sem1):
        rows, sems = (rows0, rows1), (sem0, sem1)
        wid = lax.axis_index("s") * ncores + lax.axis_index("c")
        base = pl.multiple_of(wid * per_w, _ACC_ROWS)
        pltpu.sync_copy(idx_hbm.at[pl.ds(base, per_w)], idx_v)

        def gather(c, s):
            picks = idx_v.at[pl.ds(pl.multiple_of(c * _ACC_ROWS, _ACC_ROWS), _ACC_ROWS)]
            return pltpu.make_async_copy(tab_hbm.at[picks], rows[s], sems[s])

        def dots(rows_ref, c, xrow):
            lane_id = lax.iota(jnp.int32, lanes)
            nrow = 4

            @pl.loop(0, _ACC_ROWS // lanes)
            def _(q):
                out = jnp.zeros((lanes,), F32)
                for sub in range(lanes // nrow):
                    r0 = q * lanes + sub * nrow
                    accs = [None] * nrow
                    for col in range(0, w, lanes):
                        x_lo = x_v[xrow, pl.ds(col, lanes)]
                        x_hi = x_v[xrow, pl.ds(w + col, lanes)]
                        for i in range(nrow):
                            word = rows_ref[r0 + i, pl.ds(col, lanes)]
                            term = (lax.bitcast_convert_type(word << 16, F32) * x_lo
                                    + lax.bitcast_convert_type(word & jnp.int32(-65536), F32) * x_hi)
                            accs[i] = term if accs[i] is None else accs[i] + term
                    for i in range(nrow):
                        out = jnp.where(lane_id == sub * nrow + i, jnp.sum(accs[i]), out)
                pre_v[pl.ds(pl.multiple_of(c * _ACC_ROWS + q * lanes, lanes), lanes)] = out

        gather(0, 0).start()

        @pl.loop(0, tok_w)
        def _(t):
            xrow = t % _DOT_TOKENS

            @pl.when(xrow == 0)
            def _():
                first_tok = pl.multiple_of(wid * tok_w + t, _DOT_TOKENS)
                pltpu.sync_copy(x_hbm.at[pl.ds(first_tok, _DOT_TOKENS)], x_v)

            for half in range(2):
                c = 2 * t + half

                @pl.when(c + 1 < nchunks)
                def _():
                    gather(c + 1, 1 - half).start()

                gather(c, half).wait()
                dots(rows[half], c, xrow)

        pltpu.sync_copy(pre_v, pre_hbm.at[pl.ds(base, per_w)])

    return k(tab, idx, x)


def _unpack_bf16_halves(w):
    lo = lax.bitcast_convert_type(w << 16, F32)
    hi = lax.bitcast_convert_type(w & jnp.int32(-65536), F32)
    return lo, hi


def _coef_kernel(pre_ref, gate_ref, coef_ref):
    pre = pre_ref[...]
    coef_ref[...] = gate_ref[...] * (0.5 * pre * (1.0 + lax.erf(pre * (2.0 ** -0.5))))


def _peer_coef(pre, gate):
    n, s = gate.shape
    tn = min(n, 1024)
    return pl.pallas_call(
        _coef_kernel,
        grid=(n // tn,),
        in_specs=[pl.BlockSpec((tn, s), lambda i: (i, 0)), pl.BlockSpec((tn, s), lambda i: (i, 0))],
        out_specs=pl.BlockSpec((tn, s), lambda i: (i, 0)),
        out_shape=jax.ShapeDtypeStruct((n, s), F32),
        compiler_params=_params(("parallel",)),
        name="peer_coef",
    )(pre, gate)


def _final_kernel(acc_ref, x1_ref, g2_ref, fn_ref, y_ref):
    x2 = x1_ref[0] + g2_ref[0, 5:6, :] * acc_ref[0]
    y_ref[0] = x2 * lax.rsqrt(jnp.mean(x2 * x2, axis=-1, keepdims=True) + EPS) * fn_ref[...]


def _peer_final(acc, x1, mod, final_norm):
    b, l, d = x1.shape
    tl = min(l, 512)
    return pl.pallas_call(
        _final_kernel,
        grid=(b, l // tl),
        in_specs=[pl.BlockSpec((1, tl, d), lambda i, j: (i, j, 0)),
                  pl.BlockSpec((1, tl, d), lambda i, j: (i, j, 0)),
                  pl.BlockSpec((1, 6, d), lambda i, j: (i, 0, 0)),
                  pl.BlockSpec((1, d), lambda i, j: (0, 0))],
        out_specs=pl.BlockSpec((1, tl, d), lambda i, j: (i, j, 0)),
        out_shape=jax.ShapeDtypeStruct((b, l, d), F32),
        compiler_params=_params(("parallel", "parallel")),
        name="peer_final",
    )(acc, x1, mod, final_norm.reshape(1, d))


def _front(x, mod, conv0, s0, k_past, v_past, wts, prompt):
    b, l, d = x.shape
    qkv, z, ab, qb, kb, vb, kh, vh = _inproj(x, mod, wts["norm1"], wts["w_in"])
    chunk = CHUNK if prompt else l
    o_a, s_new = _gdn(qkv, z, ab, conv0, s0, wts["conv_w"], wts["alog"], wts["dtb"], wts["onorm"], chunk)
    conv_new = qkv[:, l - (CONV_W - 1):, :]
    if prompt:
        o_b = _attn_prompt(qb, kh, vh, wts["bias_prompt"], wts["lam"], wts["subln"], wts["out_scale"])
    else:
        p = k_past.shape[1]
        bias_past, bias_new = _sample_bias(wts["rel_table"], p, l)
        o_b = _attn_sample(qb, k_past.reshape(b, p, DIFF_WIDTH), v_past.reshape(b, p, DIFF_WIDTH), kb, vb,
                           bias_past, bias_new, wts["lam"], wts["subln"], wts["out_scale"])
    x1, h2, scores = _outproj(o_a, o_b, x, mod, wts["norm2"], wts["w_out"], wts["w_q"], wts["keys"])
    eidx, gate = _topk(scores)
    return (eidx.reshape(-1), gate, h2, x1, mod), (kb, vb, s_new, conv_new)


def _back(pending, wts, coef_hook=None):
    eidx, gate, h2, x1, mod = pending
    b, l, d = h2.shape
    pre = _sc_dot(wts["peer_u"], eidx, h2.reshape(b * l, d))
    coef = _peer_coef(pre.reshape(b * l, PEER_SLOTS), gate)
    if coef_hook is not None:
        coef = coef_hook(coef)
    acc = _sc_accumulate(wts["peer_v"], eidx, coef.reshape(-1))
    return _peer_final(acc.reshape(b, l, d), x1, mod, wts["final_norm"])


def _prompt_rows(x, mod, wts, sample_front):
    b, l, d = x.shape
    conv0 = jnp.zeros((1, CONV_W - 1, QKV_A), F32)
    s0 = jnp.zeros((1, GDN_HEADS, GDN_HEAD_DIM, GDN_HEAD_DIM), F32)
    news, ys, sample = [], [], []

    def hook(coef):
        coef, result = sample_front(coef)
        sample.append(result)
        return coef

    for t in range(b):
        pend, new = _front(lax.dynamic_slice_in_dim(x, t, 1, 0), lax.dynamic_slice_in_dim(mod, t, 1, 0),
                           conv0, s0, None, None, wts, True)
        news.append(new)
        ys.append(_back(pend, wts, hook if t == 0 else None))
    y = jnp.concatenate(ys, axis=0)
    return y, [jnp.concatenate([n[k] for n in news], axis=0) for k in range(4)], sample[0]


def _cache_entries(new, b, l):
    kb, vb, s_new, conv_new = new
    return (kb.reshape(1, b, l, DIFF_HEADS, 2 * DIFF_HEAD_DIM), vb.reshape(1, b, l, DIFF_HEADS, DIFF_V_DIM),
            s_new[None], conv_new[None])


def kernel(x_prompt, x_sample, c_prompt, c_sample, cache_k, cache_v, state_gdn, state_conv, w_ada, b_ada,
           norm1, norm2, w_in, conv_w, a_log, dt_bias, gdn_onorm, lam_q1, lam_k1, lam_q2, lam_k2, diff_subln,
           w_out, peer_wq, peer_keys, peer_u, peer_v, rel_table, final_norm):
    assert w_ada.shape[0] == 1, "single-layer step"
    bp = x_prompt.shape[0]
    d = D_MODEL
    lam_init = 0.8 - 0.6 * math.exp(-0.3 * 0)
    lam = (jnp.exp(jnp.sum(lam_q1[0] * lam_k1[0])) - jnp.exp(jnp.sum(lam_q2[0] * lam_k2[0])) + lam_init)
    w = w_in[0]
    w_packed = jnp.concatenate(
        [w[:, :_C_AB], jnp.pad(w[:, 2048:2056], ((0, 0), (0, LANES - 2 * GDN_HEADS))), w[:, 2056:]],
        axis=1).astype(BF16)
    wts = dict(
        norm1=norm1[0], norm2=norm2[0], w_in=w_packed, conv_w=conv_w[0],
        alog=jnp.pad(a_log[0], (0, LANES - GDN_HEADS)).reshape(1, LANES),
        dtb=jnp.pad(dt_bias[0], (0, LANES - GDN_HEADS)).reshape(1, LANES),
        onorm=gdn_onorm[0], lam=jnp.full((1, DIFF_V_DIM), lam, F32), subln=diff_subln[0],
        out_scale=1.0 - lam_init, bias_prompt=_prompt_bias_tiles(rel_table), rel_table=rel_table,
        w_out=w_out[0].astype(BF16), w_q=peer_wq[0].astype(BF16),
        keys=peer_keys[0].reshape(2 * PEER_HEADS, N_KEYS, PEER_HALF).astype(BF16),
        peer_u=_pack_bf16_halves(peer_u), peer_v=_pack_bf16_halves(peer_v), final_norm=final_norm)

    mod = _ada(jnp.concatenate([c_prompt, c_sample], axis=0), w_ada[0], b_ada[0]).reshape(-1, 6, d)
    def sample_front(dep):
        xs, dep = lax.optimization_barrier((x_sample, dep))
        return dep, _front(xs, mod[bp:], state_conv[0], state_gdn[0], cache_k[0], cache_v[0], wts, False)

    yp, new_p, (pend_s, new_s) = _prompt_rows(x_prompt, mod[:bp], wts, sample_front)
    ys = _back(pend_s, wts)
    kp, vp, sp, cp = _cache_entries(new_p, *x_prompt.shape[:2])
    ks, vs, ss, cs = _cache_entries(new_s, *x_sample.shape[:2])
    return yp, ys, kp, vp, sp, cp, ks, vs, ss, cs
```

```python
import functools
import math

import jax
import jax.numpy as jnp
from jax import lax
from jax.experimental import pallas as pl
from jax.experimental.pallas import tpu as pltpu
from jax.experimental.pallas import tpu_sc as plsc

F32 = jnp.float32
BF16 = jnp.bfloat16
EPS = 1e-6

D_MODEL = 1024
CHUNK = 64
GDN_HEADS = 4
GDN_HEAD_DIM = 128
GDN_WIDTH = GDN_HEADS * GDN_HEAD_DIM
CONV_W = 4
QKV_A = 3 * GDN_WIDTH
DIFF_HEADS = 4
DIFF_HEAD_DIM = 64
DIFF_V_DIM = 128
DIFF_WIDTH = DIFF_HEADS * 2 * DIFF_HEAD_DIM
ATT_BLOCK = 256
N_BUCKETS = 32
REL_MAX_DIST = 128
PEER_HEADS = 8
N_KEYS = 128
PEER_HALF = 128
PEER_TOPK = 16
PEER_SLOTS = PEER_HEADS * PEER_TOPK
LANES = 128
NEG_BIG = -1e30
VMEM_LIMIT = 56 * 1024 * 1024

_C_QKV, _C_Z, _C_AB, _C_QB, _C_KB, _C_VB = 0, 1536, 2048, 2176, 2688, 3200
_C_END = 3712


def _params(sem):
    return pltpu.CompilerParams(dimension_semantics=sem, vmem_limit_bytes=VMEM_LIMIT)


def _dot(a, b, precision=None):
    return jnp.dot(a, b, preferred_element_type=F32, precision=precision)


def _dot_nt(a, b, precision=None):
    return lax.dot_general(a, b, (((1,), (1,)), ((), ())), preferred_element_type=F32, precision=precision)


def _silu(x):
    return x * jax.nn.sigmoid(x)


def _ada_kernel(c_ref, w_ref, b_ref, o_ref):
    a = _silu(c_ref[...]).astype(BF16)
    o_ref[...] = _dot(a, w_ref[...].astype(BF16)) + b_ref[...]


def _ada(c, w_ada, b_ada):
    n, d = c.shape
    cols = w_ada.shape[1]
    tn = 1024
    return pl.pallas_call(
        _ada_kernel,
        grid=(cols // tn,),
        in_specs=[pl.BlockSpec((n, d), lambda j: (0, 0)),
                  pl.BlockSpec((d, tn), lambda j: (0, j)),
                  pl.BlockSpec((1, tn), lambda j: (0, j))],
        out_specs=pl.BlockSpec((n, tn), lambda j: (0, j)),
        out_shape=jax.ShapeDtypeStruct((n, cols), F32),
        compiler_params=_params(("parallel",)),
        name="ada",
    )(c, w_ada, b_ada.reshape(1, cols))


def _modulated_norm(x, gain, shift, scale):
    y = x * lax.rsqrt(jnp.mean(x * x, axis=-1, keepdims=True) + EPS)
    return (y * gain) * (1.0 + scale) + shift


def _inproj_kernel(x_ref, mod_ref, n1_ref, w_ref, qkv_ref, z_ref, ab_ref, qb_ref, kb_ref, vb_ref, kh_ref, vh_ref):
    h = _modulated_norm(x_ref[0], n1_ref[...], mod_ref[0, 0:1, :], mod_ref[0, 1:2, :]).astype(BF16)
    qkv_ref[0] = _dot(h, w_ref[:, _C_QKV:_C_Z])
    z_ref[0] = _dot(h, w_ref[:, _C_Z:_C_AB])
    ab_ref[0] = _dot(h, w_ref[:, _C_AB:_C_QB])
    qb_ref[0] = _dot(h, w_ref[:, _C_QB:_C_KB])
    kb = _dot(h, w_ref[:, _C_KB:_C_VB])
    vb = _dot(h, w_ref[:, _C_VB:_C_END])
    kb_ref[0] = kb
    vb_ref[0] = vb
    kh_ref[0] = kb.astype(BF16)
    vh_ref[0] = vb.astype(BF16)


def _inproj(x, mod, norm1, w_packed):
    b, l, d = x.shape
    tl = min(l, 256)
    widths = (QKV_A, GDN_WIDTH, LANES, DIFF_WIDTH, DIFF_WIDTH, DIFF_WIDTH, DIFF_WIDTH, DIFF_WIDTH)
    dtypes = (F32,) * 6 + (BF16,) * 2
    return pl.pallas_call(
        _inproj_kernel,
        grid=(b, l // tl),
        in_specs=[pl.BlockSpec((1, tl, d), lambda i, j: (i, j, 0)),
                  pl.BlockSpec((1, 6, d), lambda i, j: (i, 0, 0)),
                  pl.BlockSpec((1, d), lambda i, j: (0, 0)),
                  pl.BlockSpec((d, _C_END), lambda i, j: (0, 0))],
        out_specs=[pl.BlockSpec((1, tl, w), lambda i, j: (i, j, 0)) for w in widths],
        out_shape=[jax.ShapeDtypeStruct((b, l, w), dt) for w, dt in zip(widths, dtypes)],
        compiler_params=_params(("parallel", "parallel")),
        name="inproj",
    )(x, mod, norm1.reshape(1, d), w_packed)


_HI = lax.Precision.HIGHEST


def _unit_lower_inverses(mats, n):
    r = lax.broadcasted_iota(jnp.int32, (n, n), 0)
    c = lax.broadcasted_iota(jnp.int32, (n, n), 1)
    eye = (r == c).astype(F32)
    ad = [jnp.where((r // 8) == (c // 8), a, 0.0) for a in mats]
    a2 = [_dot(m, m, _HI) for m in ad]
    a4 = [_dot(m, m, _HI) for m in a2]
    xs = [eye - m for m in ad]
    xs = [x + _dot(x, m, _HI) for x, m in zip(xs, a2)]
    xs = [x + _dot(x, m, _HI) for x, m in zip(xs, a4)]
    bs = 8
    while bs < n:
        off = ((r // (2 * bs)) == (c // (2 * bs))) & ((r // bs) != (c // bs))
        ys = [_dot(jnp.where(off, a, 0.0), x, _HI) for a, x in zip(mats, xs)]
        xs = [x - _dot(x, y, _HI) for x, y in zip(xs, ys)]
        bs *= 2
    return xs


def _mm(a, b):
    return _dot(a.astype(BF16), b.astype(BF16))


def _mm_nt(a, b):
    return _dot_nt(a.astype(BF16), b.astype(BF16))


def _gdn_kernel(qkv_ref, z_ref, ab_ref, conv0_ref, s0_ref, cw_ref, alog_ref, dtb_ref, onorm_ref,
                o_ref, s_ref, xbuf, *, chunk, nch):
    hd = GDN_HEAD_DIM
    rows = chunk * nch

    @pl.when(pl.program_id(1) == 0)
    def _():
        xbuf[5:8, :] = conv0_ref[0]
        s_ref[0] = s0_ref[0]

    x = qkv_ref[0]
    xbuf[8:8 + rows, :] = x
    y = (xbuf[5:5 + rows, :] * cw_ref[0:1, :] + xbuf[6:6 + rows, :] * cw_ref[1:2, :]
         + xbuf[7:7 + rows, :] * cw_ref[2:3, :] + x * cw_ref[3:4, :])
    xbuf[5:8, :] = x[rows - 3:rows, :]
    y = _silu(y)

    ab = ab_ref[0]
    t = ab + dtb_ref[...]
    softplus = jnp.maximum(t, 0.0) + jnp.log(1.0 + jnp.exp(-jnp.abs(t)))
    g = -jnp.exp(alog_ref[...]) * softplus
    beta = jax.nn.sigmoid(ab)

    r = lax.broadcasted_iota(jnp.int32, (chunk, chunk), 0)
    c = lax.broadcasted_iota(jnp.int32, (chunk, chunk), 1)
    lower = r >= c
    tri = lower.astype(F32)

    heads = range(GDN_HEADS)
    pairs = [(ci, h) for ci in range(nch) for h in heads]
    rows_of = {ci: slice(ci * chunk, (ci + 1) * chunk) for ci in range(nch)}
    gc = {ci: _dot(tri, g[rows_of[ci]], _HI) for ci in range(nch)}
    gc_t = {ci: gc[ci].T for ci in range(nch)}
    q, k, vb, kb, decay, egc, g_last = {}, {}, {}, {}, {}, {}, {}
    for ci, h in pairs:
        sl = rows_of[ci]
        qh = y[sl, h * hd:(h + 1) * hd]
        kh = y[sl, GDN_WIDTH + h * hd:GDN_WIDTH + (h + 1) * hd]
        p = ci, h
        q[p] = qh * lax.rsqrt(jnp.sum(qh * qh, axis=-1, keepdims=True) + EPS) * (hd ** -0.5)
        k[p] = kh * lax.rsqrt(jnp.sum(kh * kh, axis=-1, keepdims=True) + EPS)
        gcol = gc[ci][:, h:h + 1]
        bcol = beta[sl, GDN_HEADS + h:GDN_HEADS + h + 1]
        decay[p] = jnp.exp(jnp.where(lower, gcol - gc_t[ci][h:h + 1, :], NEG_BIG))
        kb[p] = k[p] * bcol
        vb[p] = y[sl, 2 * GDN_WIDTH + h * hd:2 * GDN_WIDTH + (h + 1) * hd] * bcol
        egc[p] = jnp.exp(gcol)
        g_last[p] = gcol[chunk - 1:chunk, :]
    kk = {p: _mm_nt(kb[p], k[p]) for p in pairs}
    tinv = dict(zip(pairs, _unit_lower_inverses([jnp.where(r > c, kk[p] * decay[p], 0.0) for p in pairs], chunk)))
    u_v = {p: _dot(tinv[p], vb[p], _HI) for p in pairs}
    w = {p: _dot(tinv[p], kb[p] * egc[p], _HI) for p in pairs}
    qk = {p: _mm_nt(q[p], k[p]) * decay[p] for p in pairs}
    k_dec_t = {p: (k[p] * jnp.exp(g_last[p] - gc[p[0]][:, p[1]:p[1] + 1])).T for p in pairs}

    s = [s_ref[0, h] for h in heads]
    for ci in range(nch):
        ws = [_mm(w[ci, h], s[h]) for h in heads]
        qs = [_mm(q[ci, h] * egc[ci, h], s[h]) for h in heads]
        v_new = [u_v[ci, h] - ws[h] for h in heads]
        o = [qs[h] + _mm(qk[ci, h], v_new[h]) for h in heads]
        s = [s[h] * jnp.exp(g_last[ci, h]) + _mm(k_dec_t[ci, h], v_new[h]) for h in heads]
        for h in heads:
            oh = o[h] * lax.rsqrt(jnp.mean(o[h] * o[h], axis=-1, keepdims=True) + EPS) * onorm_ref[...]
            o_ref[0, rows_of[ci], h * hd:(h + 1) * hd] = oh * _silu(z_ref[0, rows_of[ci], h * hd:(h + 1) * hd])
    for h in heads:
        s_ref[0, h] = s[h]


def _gdn(qkv, z, ab, conv0, s0, conv_w, alog_pad, dtb_pad, onorm, chunk):
    b, l, _ = qkv.shape
    hd = GDN_HEAD_DIM
    nch = 2 if l % (2 * chunk) == 0 else 1
    rows = chunk * nch
    return pl.pallas_call(
        functools.partial(_gdn_kernel, chunk=chunk, nch=nch),
        grid=(b, l // rows),
        in_specs=[pl.BlockSpec((1, rows, QKV_A), lambda i, j: (i, j, 0)),
                  pl.BlockSpec((1, rows, GDN_WIDTH), lambda i, j: (i, j, 0)),
                  pl.BlockSpec((1, rows, LANES), lambda i, j: (i, j, 0)),
                  pl.BlockSpec((1, CONV_W - 1, QKV_A), lambda i, j: (i, 0, 0)),
                  pl.BlockSpec((1, GDN_HEADS, hd, hd), lambda i, j: (i, 0, 0, 0)),
                  pl.BlockSpec((CONV_W, QKV_A), lambda i, j: (0, 0)),
                  pl.BlockSpec((1, LANES), lambda i, j: (0, 0)),
                  pl.BlockSpec((1, LANES), lambda i, j: (0, 0)),
                  pl.BlockSpec((1, hd), lambda i, j: (0, 0))],
        out_specs=[pl.BlockSpec((1, rows, GDN_WIDTH), lambda i, j: (i, j, 0)),
                   pl.BlockSpec((1, GDN_HEADS, hd, hd), lambda i, j: (i, 0, 0, 0))],
        out_shape=[jax.ShapeDtypeStruct((b, l, GDN_WIDTH), F32),
                   jax.ShapeDtypeStruct((b, GDN_HEADS, hd, hd), F32)],
        scratch_shapes=[pltpu.VMEM((8 + rows, QKV_A), F32)],
        compiler_params=_params(("parallel", "arbitrary")),
        name="gdn",
    )(qkv, z, ab, conv0, s0, conv_w, alog_pad, dtb_pad, onorm.reshape(1, hd))


def _rel_bucket(rel):
    nb = N_BUCKETS // 2
    max_exact = nb // 2
    ret = jnp.where(rel > 0, nb, 0)
    n = jnp.abs(rel)
    large = max_exact + (jnp.log(jnp.maximum(n, 1).astype(F32) / max_exact)
                         / math.log(REL_MAX_DIST / max_exact) * (nb - max_exact)).astype(jnp.int32)
    large = jnp.minimum(large, nb - 1)
    return ret + jnp.where(n < max_exact, n, large)


def _diff_finish(o1, o2, lam_ref, subln_ref, out_scale):
    o = o1 - lam_ref[...] * o2
    return o * lax.rsqrt(jnp.mean(o * o, axis=-1, keepdims=True) + EPS) * subln_ref[...] * out_scale


def _attn_prompt_kernel(q_ref, k_ref, v_ref, bias_ref, lam_ref, subln_ref, o_ref, m_ref, l_ref, acc_ref,
                        *, out_scale):
    i = pl.program_id(2)
    tb = ATT_BLOCK
    dh = DIFF_HEAD_DIM
    q = q_ref[0] * (dh ** -0.5)
    lane = lax.broadcasted_iota(jnp.int32, q.shape, 1)
    q2s = jnp.concatenate([jnp.where(lane < dh, q, 0.0), jnp.where(lane >= dh, q, 0.0)], axis=0).astype(BF16)

    def score_tiles(j, tile):
        start = pl.multiple_of(j * tb, tb)
        s = _dot_nt(q2s, k_ref[0, pl.ds(start, tb), :]) + bias_ref[0, tile]
        return [s[:, c:c + LANES] for c in range(0, tb, LANES)]

    def visible_blocks(fn, unroll):
        n_far = jnp.maximum(i - 1, 0)

        @pl.loop(0, n_far // unroll)
        def _(g):
            fn([(unroll * g + u, 0) for u in range(unroll)])

        @pl.loop((n_far // unroll) * unroll, n_far)
        def _(j):
            fn([(j, 0)])

        @pl.when(i > 0)
        def _():
            fn([(i - 1, 1), (i, 2)])

        @pl.when(i == 0)
        def _():
            fn([(i, 2)])

    m_ref[...] = jnp.full(m_ref.shape, NEG_BIG, F32)

    def track_max(blocks):
        tiles = [s for j, tile in blocks for s in score_tiles(j, tile)]
        m_ref[...] = functools.reduce(jnp.maximum, tiles, m_ref[...])

    visible_blocks(track_max, 4)
    m_ref[...] = jnp.broadcast_to(jnp.max(m_ref[...], axis=-1, keepdims=True), m_ref.shape)

    l_ref[...] = jnp.zeros(l_ref.shape, F32)
    acc_ref[...] = jnp.zeros(acc_ref.shape, F32)

    def accumulate(blocks):
        m = m_ref[...]
        l_add, acc_add = [], []
        for j, tile in blocks:
            p = [jnp.exp(s - m) for s in score_tiles(j, tile)]
            l_add.extend(p)
            start = pl.multiple_of(j * tb, tb)
            acc_add.append(_dot(jnp.concatenate(p, axis=-1).astype(BF16), v_ref[0, pl.ds(start, tb), :]))
        l_ref[...] += sum(l_add)
        acc_ref[...] += sum(acc_add)

    visible_blocks(accumulate, 4)
    o = acc_ref[...] / jnp.sum(l_ref[...], axis=-1, keepdims=True)
    o_ref[0] = _diff_finish(o[:tb], o[tb:], lam_ref, subln_ref, out_scale)


def _attn_prompt(qb, kh, vh, bias_tiles, lam_row, subln, out_scale):
    b, l, _ = qb.shape
    tb = ATT_BLOCK
    hw = 2 * DIFF_HEAD_DIM
    return pl.pallas_call(
        functools.partial(_attn_prompt_kernel, out_scale=out_scale),
        grid=(b, DIFF_HEADS, l // tb),
        in_specs=[pl.BlockSpec((1, tb, hw), lambda bi, h, i: (bi, i, h)),
                  pl.BlockSpec((1, l, hw), lambda bi, h, i: (bi, 0, h)),
                  pl.BlockSpec((1, l, DIFF_V_DIM), lambda bi, h, i: (bi, 0, h)),
                  pl.BlockSpec((1, 3, 2 * tb, tb), lambda bi, h, i: (h, 0, 0, 0)),
                  pl.BlockSpec((1, DIFF_V_DIM), lambda bi, h, i: (0, 0)),
                  pl.BlockSpec((1, DIFF_V_DIM), lambda bi, h, i: (0, 0))],
        out_specs=pl.BlockSpec((1, tb, DIFF_V_DIM), lambda bi, h, i: (bi, i, h)),
        out_shape=jax.ShapeDtypeStruct((b, l, DIFF_WIDTH), F32),
        scratch_shapes=[pltpu.VMEM((2 * tb, LANES), F32), pltpu.VMEM((2 * tb, LANES), F32),
                        pltpu.VMEM((2 * tb, DIFF_V_DIM), F32)],
        compiler_params=_params(("parallel", "parallel", "arbitrary")),
        name="attn_prompt",
    )(qb, kh, vh, bias_tiles, lam_row, subln.reshape(1, DIFF_V_DIM))


def _prompt_bias_tiles(rel_table):
    tb = ATT_BLOCK
    qi = jnp.arange(tb)[:, None]
    ki = jnp.arange(tb)[None, :]
    far = jnp.broadcast_to(rel_table[_rel_bucket(jnp.array(-2 * tb))], (tb, tb, DIFF_HEADS))
    prev = rel_table[_rel_bucket(ki - qi - tb)]
    diag = jnp.where(((ki // CHUNK) <= (qi // CHUNK))[..., None], rel_table[_rel_bucket(ki - qi)], NEG_BIG)
    tiles = jnp.moveaxis(jnp.stack([far, prev, diag]).astype(F32), -1, 0)
    return jnp.concatenate([tiles, tiles], axis=2)


def _attn_sample_kernel(q_ref, kp_ref, vp_ref, kn_ref, vn_ref, bp_ref, bn_ref, lam_ref, subln_ref, o_ref,
                        *, out_scale):
    dh = DIFF_HEAD_DIM
    q = q_ref[0] * (dh ** -0.5)
    kp = kp_ref[0].astype(BF16)
    kn = kn_ref[0].astype(BF16)
    vp = vp_ref[0].astype(BF16)
    vn = vn_ref[0].astype(BF16)
    outs = []
    for t in range(2):
        qt = q[:, t * dh:(t + 1) * dh].astype(BF16)
        sp = _dot_nt(qt, kp[:, t * dh:(t + 1) * dh]) + bp_ref[0]
        sn = _dot_nt(qt, kn[:, t * dh:(t + 1) * dh]) + bn_ref[0]
        m = jnp.maximum(jnp.max(sp, axis=-1, keepdims=True), jnp.max(sn, axis=-1, keepdims=True))
        pp = jnp.exp(sp - m)
        pn = jnp.exp(sn - m)
        den = jnp.sum(pp, axis=-1, keepdims=True) + jnp.sum(pn, axis=-1, keepdims=True)
        outs.append((_dot(pp.astype(BF16), vp) + _dot(pn.astype(BF16), vn)) / den)
    o_ref[0] = _diff_finish(outs[0], outs[1], lam_ref, subln_ref, out_scale)


def _attn_sample(qb, k_past, v_past, k_new, v_new, bias_past, bias_new, lam_row, subln, out_scale):
    b, l, _ = qb.shape
    p = k_past.shape[1]
    hw = 2 * DIFF_HEAD_DIM
    return pl.pallas_call(
        functools.partial(_attn_sample_kernel, out_scale=out_scale),
        grid=(b, DIFF_HEADS),
        in_specs=[pl.BlockSpec((1, l, hw), lambda bi, h: (bi, 0, h)),
                  pl.BlockSpec((1, p, hw), lambda bi, h: (bi, 0, h)),
                  pl.BlockSpec((1, p, DIFF_V_DIM), lambda bi, h: (bi, 0, h)),
                  pl.BlockSpec((1, l, hw), lambda bi, h: (bi, 0, h)),
                  pl.BlockSpec((1, l, DIFF_V_DIM), lambda bi, h: (bi, 0, h)),
                  pl.BlockSpec((1, l, p), lambda bi, h: (h, 0, 0)),
                  pl.BlockSpec((1, l, l), lambda bi, h: (h, 0, 0)),
                  pl.BlockSpec((1, DIFF_V_DIM), lambda bi, h: (0, 0)),
                  pl.BlockSpec((1, DIFF_V_DIM), lambda bi, h: (0, 0))],
        out_specs=pl.BlockSpec((1, l, DIFF_V_DIM), lambda bi, h: (bi, 0, h)),
        out_shape=jax.ShapeDtypeStruct((b, l, DIFF_WIDTH), F32),
        compiler_params=_params(("parallel", "parallel")),
        name="attn_sample",
    )(qb, k_past, v_past, k_new, v_new, bias_past, bias_new, lam_row, subln.reshape(1, DIFF_V_DIM))


def _sample_bias(rel_table, p, l):
    rel = jnp.arange(-(p + l - 1), l)
    by_rel = rel_table[_rel_bucket(rel)].astype(F32).T
    bias = jnp.stack([lax.slice_in_dim(by_rel, l - 1 - i, p + 2 * l - 1 - i, axis=1) for i in range(l)], axis=1)
    return bias[:, :, :p], bias[:, :, p:]


def _outproj_kernel(oa_ref, ob_ref, x_ref, mod_ref, n2_ref, wo_ref, wq_ref, keys_ref,
                    x1_ref, h2_ref, sc_ref):
    mixed = jnp.concatenate([oa_ref[0], ob_ref[0]], axis=-1).astype(BF16)
    x1 = x_ref[0] + mod_ref[0, 2:3, :] * _dot(mixed, wo_ref[...])
    x1_ref[0] = x1
    h2 = _modulated_norm(x1, n2_ref[...], mod_ref[0, 3:4, :], mod_ref[0, 4:5, :])
    h2_ref[0] = h2
    qh = _dot(h2.astype(BF16), wq_ref[...]).astype(BF16)
    for hp in range(2 * PEER_HEADS):
        sc_ref[0, hp] = _dot_nt(keys_ref[hp], qh[:, hp * PEER_HALF:(hp + 1) * PEER_HALF])


def _outproj(o_a, o_b, x, mod, norm2, w_out, w_q, keys):
    b, l, d = x.shape
    tl = min(l, 256)
    nhp = 2 * PEER_HEADS
    return pl.pallas_call(
        _outproj_kernel,
        grid=(b, l // tl),
        in_specs=[pl.BlockSpec((1, tl, GDN_WIDTH), lambda i, j: (i, j, 0)),
                  pl.BlockSpec((1, tl, DIFF_WIDTH), lambda i, j: (i, j, 0)),
                  pl.BlockSpec((1, tl, d), lambda i, j: (i, j, 0)),
                  pl.BlockSpec((1, 6, d), lambda i, j: (i, 0, 0)),
                  pl.BlockSpec((1, d), lambda i, j: (0, 0)),
                  pl.BlockSpec((d, d), lambda i, j: (0, 0)),
                  pl.BlockSpec((d, nhp * PEER_HALF), lambda i, j: (0, 0)),
                  pl.BlockSpec((nhp, N_KEYS, PEER_HALF), lambda i, j: (0, 0, 0))],
        out_specs=[pl.BlockSpec((1, tl, d), lambda i, j: (i, j, 0)),
                   pl.BlockSpec((1, tl, d), lambda i, j: (i, j, 0)),
                   pl.BlockSpec((1, nhp, N_KEYS, tl), lambda i, j: (i, 0, 0, j))],
        out_shape=[jax.ShapeDtypeStruct((b, l, d), F32),
                   jax.ShapeDtypeStruct((b, l, d), F32),
                   jax.ShapeDtypeStruct((b, nhp, N_KEYS, l), F32)],
        compiler_params=_params(("parallel", "parallel")),
        name="outproj",
    )(o_a, o_b, x, mod, norm2.reshape(1, d), w_out, w_q, keys)


def _top16_rows(s, ids, n):
    vals, idxs = [], []
    for _ in range(PEER_TOPK):
        m = jnp.max(s, axis=0, keepdims=True)
        i = jnp.min(jnp.where(s == m, ids, n), axis=0, keepdims=True)
        vals.append(m)
        idxs.append(i)
        s = jnp.where(ids == i, -jnp.inf, s)
    return jnp.concatenate(vals, axis=0), jnp.concatenate(idxs, axis=0)


def _pair_candidates(s1, s2):
    t = s1.shape[1]
    sub16 = lax.broadcasted_iota(jnp.int32, (PEER_TOPK, t), 0)
    sub8 = sub16[:8]
    cand = [s1[0:1] + s2] + [s1[a:a + 1] + s2[:8] for a in range(1, 8)] + [s1[8:] + s2[0:1]]
    pos = [sub16] + [a * PEER_TOPK + sub8 for a in range(1, 8)] + [(8 + sub8) * PEER_TOPK]
    return jnp.concatenate(cand, axis=0), jnp.concatenate(pos, axis=0)


def _pick_rows(table, sel):
    out = jnp.zeros_like(table)
    for a in range(PEER_TOPK):
        out = jnp.where(sel == a, table[a:a + 1, :], out)
    return out


def _topk_kernel(sc_ref, eidx_ref, gate_ref):
    eidx, gates = [], []
    key_ids = lax.broadcasted_iota(jnp.int32, sc_ref.shape[2:], 0)
    for h in range(PEER_HEADS):
        s1, i1 = _top16_rows(sc_ref[0, 2 * h], key_ids, N_KEYS)
        s2, i2 = _top16_rows(sc_ref[0, 2 * h + 1], key_ids, N_KEYS)
        cand, cand_pos = _pair_candidates(s1, s2)
        top_s, pos = _top16_rows(cand, cand_pos, PEER_TOPK * PEER_TOPK)
        eidx.append(_pick_rows(i1, pos // PEER_TOPK) * N_KEYS + _pick_rows(i2, pos % PEER_TOPK))
        e = jnp.exp(top_s - top_s[0:1, :])
        gates.append(e / jnp.sum(e, axis=0, keepdims=True))
    eidx_ref[...] = jnp.concatenate(eidx, axis=0).T
    gate_ref[...] = jnp.concatenate(gates, axis=0).T


def _topk(scores):
    b, nhp, nk, l = scores.shape
    tt = min(l, LANES)
    nt = l // tt
    return pl.pallas_call(
        _topk_kernel,
        grid=(b, nt),
        in_specs=[pl.BlockSpec((1, nhp, nk, tt), lambda i, j: (i, 0, 0, j))],
        out_specs=[pl.BlockSpec((tt, PEER_SLOTS), lambda i, j: (i * nt + j, 0)),
                   pl.BlockSpec((tt, PEER_SLOTS), lambda i, j: (i * nt + j, 0))],
        out_shape=[jax.ShapeDtypeStruct((b * l, PEER_SLOTS), jnp.int32),
                   jax.ShapeDtypeStruct((b * l, PEER_SLOTS), F32)],
        compiler_params=_params(("parallel", "parallel")),
        name="topk",
    )(scores)


_SC_ROWS = 32
_SC_SLOTS = 4


def _pack_kernel(t_ref, o_ref):
    half = t_ref.shape[2] // 2
    bits = lax.bitcast_convert_type(t_ref[0].astype(BF16).astype(F32), jnp.int32)
    o_ref[...] = lax.shift_right_logical(bits[:, :half], 16) | bits[:, half:]


def _pack_bf16_halves(t):
    _, v, d = t.shape
    tv = 512
    return pl.pallas_call(
        _pack_kernel,
        grid=(v // tv,),
        in_specs=[pl.BlockSpec((1, tv, d), lambda i: (0, i, 0))],
        out_specs=pl.BlockSpec((tv, d // 2), lambda i: (i, 0)),
        out_shape=jax.ShapeDtypeStruct((v, d // 2), jnp.int32),
        compiler_params=_params(("parallel",)),
        name="pack",
    )(t)


def _sc_mesh():
    info = plsc.get_sparse_core_info()
    mesh = plsc.VectorSubcoreMesh(core_axis_name="c", subcore_axis_name="s")
    return mesh, info.num_cores, info.num_cores * info.num_subcores


def _sc_gather(tab, idx):
    n = idx.shape[0]
    w = tab.shape[1]
    mesh, ncores, nw = _sc_mesh()
    per_w = n // nw
    ns, nr = _SC_SLOTS, _SC_ROWS
    ngroups = per_w // (ns * nr)
    assert n % (nw * ns * nr) == 0

    @functools.partial(
        pl.kernel, mesh=mesh, out_type=jax.ShapeDtypeStruct((n, w), tab.dtype),
        scratch_types=([pltpu.VMEM((per_w,), jnp.int32)] + [pltpu.VMEM((nr, w), tab.dtype)] * ns
                       + [pltpu.SemaphoreType.DMA] * (2 * ns)),
    )
    def k(tab_hbm, idx_hbm, out_hbm, idx_v, *scratch):
        bufs, gsem, wsem = scratch[:ns], scratch[ns:2 * ns], scratch[2 * ns:]
        base = (lax.axis_index("s") * ncores + lax.axis_index("c")) * per_w
        pltpu.sync_copy(idx_hbm.at[pl.ds(pl.multiple_of(base, nr), per_w)], idx_v)

        def gather(c, s):
            picks = idx_v.at[pl.ds(pl.multiple_of(c * nr, nr), nr)]
            return pltpu.make_async_copy(tab_hbm.at[picks], bufs[s], gsem[s])

        def write(c, s):
            span = pl.ds(pl.multiple_of(base + c * nr, nr), nr)
            return pltpu.make_async_copy(bufs[s], out_hbm.at[span], wsem[s])

        for s in range(ns):
            gather(s, s).start()

        @pl.loop(0, ngroups)
        def _(g):
            for s in range(ns):
                gather(g * ns + s, s).wait()
                write(g * ns + s, s).start()
            for s in range(ns):
                write(g * ns + s, s).wait()

                @pl.when(g < ngroups - 1)
                def _():
                    gather((g + 1) * ns + s, s).start()

    return k(tab, idx)


_ACC_ROWS = 32
_ACC_SLOTS = PEER_SLOTS // _ACC_ROWS
_ACC_TOKENS = 8


def _row_prefetch(tab_hbm, idx_v, rows, sems, nchunks):
    ns = _ACC_SLOTS

    def gather(c):
        picks = idx_v.at[pl.ds(pl.multiple_of(c * _ACC_ROWS, _ACC_ROWS), _ACC_ROWS)]
        return pltpu.make_async_copy(tab_hbm.at[picks], rows.at[c % ns], sems.at[c % ns])

    def prime():
        for c in range(ns - 1):
            gather(c).start()

    def step(c):
        @pl.when(c + ns - 1 < nchunks)
        def _():
            gather(c + ns - 1).start()

        gather(c).wait()

    return prime, step


def _sc_accumulate(tab, idx, coef):
    n = idx.shape[0]
    w = tab.shape[1]
    d = 2 * w
    lanes = 16
    ns = _ACC_SLOTS
    mesh, ncores, nw = _sc_mesh()
    ntok = n // PEER_SLOTS
    tok_w = ntok // nw
    per_w = tok_w * PEER_SLOTS
    nchunks = per_w // _ACC_ROWS
    assert ntok % (nw * _ACC_TOKENS) == 0 and w % (4 * lanes) == 0

    @functools.partial(
        pl.kernel, mesh=mesh, out_type=jax.ShapeDtypeStruct((ntok, d), F32),
        scratch_types=[pltpu.VMEM((per_w,), jnp.int32), pltpu.VMEM((per_w,), F32),
                       pltpu.VMEM((_ACC_TOKENS, d), F32), pltpu.VMEM((ns, _ACC_ROWS, w), tab.dtype),
                       pltpu.SemaphoreType.DMA((ns,))],
        compiler_params=pltpu.CompilerParams(needs_layout_passes=False),
    )
    def k(tab_hbm, idx_hbm, coef_hbm, out_hbm, idx_v, coef_v, acc_v, rows, sems):
        wid = lax.axis_index("s") * ncores + lax.axis_index("c")
        base = pl.multiple_of(wid * per_w, _ACC_ROWS)
        pltpu.sync_copy(idx_hbm.at[pl.ds(base, per_w)], idx_v)
        pltpu.sync_copy(coef_hbm.at[pl.ds(base, per_w)], coef_v)
        prime, fetch_step = _row_prefetch(tab_hbm, idx_v, rows, sems, nchunks)

        def accumulate(rows_ref, c, arow):
            nrow, ncol = 4, 4

            @pl.loop(0, _ACC_ROWS // nrow)
            def _(q):
                r0 = q * nrow
                cvec = [plsc.load_gather(coef_v, [jnp.full((lanes,), c * _ACC_ROWS + r0 + i, jnp.int32)])
                        for i in range(nrow)]
                groups = [[col0 + j * lanes for j in range(ncol)] for col0 in range(0, w, ncol * lanes)]

                def load(cols):
                    return [[rows_ref[r0 + i, pl.ds(col, lanes)] for i in range(nrow)] for col in cols]

                ahead = load(groups[0])
                for g, cols in enumerate(groups):
                    words = ahead
                    if g + 1 < len(groups):
                        ahead = load(groups[g + 1])
                    sums = []
                    for wds in words:
                        lo = [cvec[i] * lax.bitcast_convert_type(wds[i] << 16, F32) for i in range(nrow)]
                        hi = [cvec[i] * lax.bitcast_convert_type(wds[i] & jnp.int32(-65536), F32)
                              for i in range(nrow)]
                        sums.append((functools.reduce(jnp.add, lo), functools.reduce(jnp.add, hi)))
                    for col, (lo, hi) in zip(cols, sums):
                        plsc.addupdate(acc_v.at[arow, pl.ds(col, lanes)], lo)
                        plsc.addupdate(acc_v.at[arow, pl.ds(w + col, lanes)], hi)

        prime()

        @pl.loop(0, tok_w)
        def _(t):
            arow = t % _ACC_TOKENS
            for col in range(0, d, lanes):
                acc_v[arow, pl.ds(col, lanes)] = jnp.zeros((lanes,), F32)
            @pl.loop(0, ns)
            def _(q):
                c = ns * t + q
                fetch_step(c)
                accumulate(rows.at[q], c, arow)

            @pl.when(arow == _ACC_TOKENS - 1)
            def _():
                first_tok = pl.multiple_of(wid * tok_w + t - (_ACC_TOKENS - 1), _ACC_TOKENS)
                pltpu.sync_copy(acc_v, out_hbm.at[pl.ds(first_tok, _ACC_TOKENS)])

    return k(tab, idx, coef)


_DOT_TOKENS = 8


def _sc_dot(tab, idx, x):
    n = idx.shape[0]
    w = tab.shape[1]
    d = 2 * w
    lanes = 16
    ns = _ACC_SLOTS
    mesh, ncores, nw = _sc_mesh()
    ntok = n // PEER_SLOTS
    tok_w = ntok // nw
    per_w = tok_w * PEER_SLOTS
    nchunks = per_w // _ACC_ROWS
    assert ntok % (nw * _DOT_TOKENS) == 0 and _ACC_ROWS % lanes == 0

    @functools.partial(
        pl.kernel, mesh=mesh, out_type=jax.ShapeDtypeStruct((n,), F32),
        scratch_types=[pltpu.VMEM((per_w,), jnp.int32), pltpu.VMEM((per_w,), F32),
                       pltpu.VMEM((_DOT_TOKENS, d), F32), pltpu.VMEM((ns, _ACC_ROWS, w), tab.dtype),
                       pltpu.SemaphoreType.DMA((ns,))],
        compiler_params=pltpu.CompilerParams(needs_layout_passes=False),
    )
    def k(tab_hbm, idx_hbm, x_hbm, pre_hbm, idx_v, pre_v, x_v, rows, sems):
        wid = lax.axis_index("s") * ncores + lax.axis_index("c")
        base = pl.multiple_of(wid * per_w, _ACC_ROWS)
        pltpu.sync_copy(idx_hbm.at[pl.ds(base, per_w)], idx_v)
        prime, fetch_step = _row_prefetch(tab_hbm, idx_v, rows, sems, nchunks)

        def dots(rows_ref, c, xrow):
            lane_id = lax.iota(jnp.int32, lanes)
            nrow = 4

            @pl.loop(0, _ACC_ROWS // lanes)
            def _(q):
                out = jnp.zeros((lanes,), F32)
                for sub in range(lanes // nrow):
                    r0 = q * lanes + sub * nrow
                    accs = [None] * nrow
                    for col in range(0, w, lanes):
                        x_lo = x_v[xrow, pl.ds(col, lanes)]
                        x_hi = x_v[xrow, pl.ds(w + col, lanes)]
                        for i in range(nrow):
                            word = rows_ref[r0 + i, pl.ds(col, lanes)]
                            term = (lax.bitcast_convert_type(word << 16, F32) * x_lo
                                    + lax.bitcast_convert_type(word & jnp.int32(-65536), F32) * x_hi)
                            accs[i] = term if accs[i] is None else accs[i] + term
                    for i in range(nrow):
                        out = jnp.where(lane_id == sub * nrow + i, jnp.sum(accs[i]), out)
                pre_v[pl.ds(pl.multiple_of(c * _ACC_ROWS + q * lanes, lanes), lanes)] = out

        prime()

        @pl.loop(0, tok_w)
        def _(t):
            xrow = t % _DOT_TOKENS

            @pl.when(xrow == 0)
            def _():
                first_tok = pl.multiple_of(wid * tok_w + t, _DOT_TOKENS)
                pltpu.sync_copy(x_hbm.at[pl.ds(first_tok, _DOT_TOKENS)], x_v)

            @pl.loop(0, ns)
            def _(q):
                c = ns * t + q
                fetch_step(c)
                dots(rows.at[q], c, xrow)

        pltpu.sync_copy(pre_v, pre_hbm.at[pl.ds(base, per_w)])

    return k(tab, idx, x)


def _unpack_bf16_halves(w):
    lo = lax.bitcast_convert_type(w << 16, F32)
    hi = lax.bitcast_convert_type(w & jnp.int32(-65536), F32)
    return lo, hi


def _coef_kernel(pre_ref, gate_ref, coef_ref):
    pre = pre_ref[...]
    coef_ref[...] = gate_ref[...] * (0.5 * pre * (1.0 + lax.erf(pre * (2.0 ** -0.5))))


def _peer_coef(pre, gate):
    n, s = gate.shape
    tn = min(n, 1024)
    return pl.pallas_call(
        _coef_kernel,
        grid=(n // tn,),
        in_specs=[pl.BlockSpec((tn, s), lambda i: (i, 0)), pl.BlockSpec((tn, s), lambda i: (i, 0))],
        out_specs=pl.BlockSpec((tn, s), lambda i: (i, 0)),
        out_shape=jax.ShapeDtypeStruct((n, s), F32),
        compiler_params=_params(("parallel",)),
        name="peer_coef",
    )(pre, gate)


def _final_kernel(acc_ref, x1_ref, g2_ref, fn_ref, y_ref):
    x2 = x1_ref[0] + g2_ref[0, 5:6, :] * acc_ref[0]
    y_ref[0] = x2 * lax.rsqrt(jnp.mean(x2 * x2, axis=-1, keepdims=True) + EPS) * fn_ref[...]


def _peer_final(acc, x1, mod, final_norm):
    b, l, d = x1.shape
    tl = min(l, 512)
    return pl.pallas_call(
        _final_kernel,
        grid=(b, l // tl),
        in_specs=[pl.BlockSpec((1, tl, d), lambda i, j: (i, j, 0)),
                  pl.BlockSpec((1, tl, d), lambda i, j: (i, j, 0)),
                  pl.BlockSpec((1, 6, d), lambda i, j: (i, 0, 0)),
                  pl.BlockSpec((1, d), lambda i, j: (0, 0))],
        out_specs=pl.BlockSpec((1, tl, d), lambda i, j: (i, j, 0)),
        out_shape=jax.ShapeDtypeStruct((b, l, d), F32),
        compiler_params=_params(("parallel", "parallel")),
        name="peer_final",
    )(acc, x1, mod, final_norm.reshape(1, d))


def _front(x, mod, conv0, s0, k_past, v_past, wts, prompt):
    b, l, d = x.shape
    qkv, z, ab, qb, kb, vb, kh, vh = _inproj(x, mod, wts["norm1"], wts["w_in"])
    chunk = CHUNK if prompt else l
    o_a, s_new = _gdn(qkv, z, ab, conv0, s0, wts["conv_w"], wts["alog"], wts["dtb"], wts["onorm"], chunk)
    conv_new = qkv[:, l - (CONV_W - 1):, :]
    if prompt:
        o_b = _attn_prompt(qb, kh, vh, wts["bias_prompt"], wts["lam"], wts["subln"], wts["out_scale"])
    else:
        p = k_past.shape[1]
        bias_past, bias_new = _sample_bias(wts["rel_table"], p, l)
        o_b = _attn_sample(qb, k_past.reshape(b, p, DIFF_WIDTH), v_past.reshape(b, p, DIFF_WIDTH), kb, vb,
                           bias_past, bias_new, wts["lam"], wts["subln"], wts["out_scale"])
    x1, h2, scores = _outproj(o_a, o_b, x, mod, wts["norm2"], wts["w_out"], wts["w_q"], wts["keys"])
    eidx, gate = _topk(scores)
    return (eidx.reshape(-1), gate, h2, x1, mod), (kb, vb, s_new, conv_new)


def _back(pending, wts, coef_hook=None):
    eidx, gate, h2, x1, mod = pending
    b, l, d = h2.shape
    pre = _sc_dot(wts["peer_u"], eidx, h2.reshape(b * l, d))
    coef = _peer_coef(pre.reshape(b * l, PEER_SLOTS), gate)
    if coef_hook is not None:
        coef = coef_hook(coef)
    acc = _sc_accumulate(wts["peer_v"], eidx, coef.reshape(-1))
    return _peer_final(acc.reshape(b, l, d), x1, mod, wts["final_norm"])


def _prompt_rows(x, mod, wts, sample_front):
    b, l, d = x.shape
    conv0 = jnp.zeros((1, CONV_W - 1, QKV_A), F32)
    s0 = jnp.zeros((1, GDN_HEADS, GDN_HEAD_DIM, GDN_HEAD_DIM), F32)
    news, ys, sample = [], [], []

    def hook(coef):
        coef, result = sample_front(coef)
        sample.append(result)
        return coef

    for t in range(b):
        pend, new = _front(lax.dynamic_slice_in_dim(x, t, 1, 0), lax.dynamic_slice_in_dim(mod, t, 1, 0),
                           conv0, s0, None, None, wts, True)
        news.append(new)
        ys.append(_back(pend, wts, hook if t == 0 else None))
    y = jnp.concatenate(ys, axis=0)
    return y, [jnp.concatenate([n[k] for n in news], axis=0) for k in range(4)], sample[0]


def _cache_entries(new, b, l):
    kb, vb, s_new, conv_new = new
    return (kb.reshape(1, b, l, DIFF_HEADS, 2 * DIFF_HEAD_DIM), vb.reshape(1, b, l, DIFF_HEADS, DIFF_V_DIM),
            s_new[None], conv_new[None])


def kernel(x_prompt, x_sample, c_prompt, c_sample, cache_k, cache_v, state_gdn, state_conv, w_ada, b_ada,
           norm1, norm2, w_in, conv_w, a_log, dt_bias, gdn_onorm, lam_q1, lam_k1, lam_q2, lam_k2, diff_subln,
           w_out, peer_wq, peer_keys, peer_u, peer_v, rel_table, final_norm):
    assert w_ada.shape[0] == 1, "single-layer step"
    bp = x_prompt.shape[0]
    d = D_MODEL
    lam_init = 0.8 - 0.6 * math.exp(-0.3 * 0)
    lam = (jnp.exp(jnp.sum(lam_q1[0] * lam_k1[0])) - jnp.exp(jnp.sum(lam_q2[0] * lam_k2[0])) + lam_init)
    w = w_in[0]
    w_packed = jnp.concatenate(
        [w[:, :_C_AB], jnp.pad(w[:, 2048:2056], ((0, 0), (0, LANES - 2 * GDN_HEADS))), w[:, 2056:]],
        axis=1).astype(BF16)
    wts = dict(
        norm1=norm1[0], norm2=norm2[0], w_in=w_packed, conv_w=conv_w[0],
        alog=jnp.pad(a_log[0], (0, LANES - GDN_HEADS)).reshape(1, LANES),
        dtb=jnp.pad(dt_bias[0], (0, LANES - GDN_HEADS)).reshape(1, LANES),
        onorm=gdn_onorm[0], lam=jnp.full((1, DIFF_V_DIM), lam, F32), subln=diff_subln[0],
        out_scale=1.0 - lam_init, bias_prompt=_prompt_bias_tiles(rel_table), rel_table=rel_table,
        w_out=w_out[0].astype(BF16), w_q=peer_wq[0].astype(BF16),
        keys=peer_keys[0].reshape(2 * PEER_HEADS, N_KEYS, PEER_HALF).astype(BF16),
        peer_u=_pack_bf16_halves(peer_u), peer_v=_pack_bf16_halves(peer_v), final_norm=final_norm)

    mod = _ada(jnp.concatenate([c_prompt, c_sample], axis=0), w_ada[0], b_ada[0]).reshape(-1, 6, d)
    def sample_front(dep):
        xs, dep = lax.optimization_barrier((x_sample, dep))
        return dep, _front(xs, mod[bp:], state_conv[0], state_gdn[0], cache_k[0], cache_v[0], wts, False)

    yp, new_p, (pend_s, new_s) = _prompt_rows(x_prompt, mod[:bp], wts, sample_front)
    ys = _back(pend_s, wts)
    kp, vp, sp, cp = _cache_entries(new_p, *x_prompt.shape[:2])
    ks, vs, ss, cs = _cache_entries(new_s, *x_sample.shape[:2])
    return yp, ys, kp, vp, sp, cp, ks, vs, ss, cs
```

```python
import functools
import math

import jax
import jax.numpy as jnp
from jax import lax
from jax.experimental import pallas as pl
from jax.experimental.pallas import tpu as pltpu
from jax.experimental.pallas import tpu_sc as plsc

F32 = jnp.float32
BF16 = jnp.bfloat16
EPS = 1e-6

D_MODEL = 1024
CHUNK = 64
GDN_HEADS = 4
GDN_HEAD_DIM = 128
GDN_WIDTH = GDN_HEADS * GDN_HEAD_DIM
CONV_W = 4
QKV_A = 3 * GDN_WIDTH
DIFF_HEADS = 4
DIFF_HEAD_DIM = 64
DIFF_V_DIM = 128
DIFF_WIDTH = DIFF_HEADS * 2 * DIFF_HEAD_DIM
ATT_BLOCK = 256
N_BUCKETS = 32
REL_MAX_DIST = 128
PEER_HEADS = 8
N_KEYS = 128
PEER_HALF = 128
PEER_TOPK = 16
PEER_SLOTS = PEER_HEADS * PEER_TOPK
LANES = 128
NEG_BIG = -1e30
VMEM_LIMIT = 56 * 1024 * 1024

_C_QKV, _C_Z, _C_AB, _C_QB, _C_KB, _C_VB = 0, 1536, 2048, 2176, 2688, 3200
_C_END = 3712


def _params(sem):
    return pltpu.CompilerParams(dimension_semantics=sem, vmem_limit_bytes=VMEM_LIMIT)


def _dot(a, b, precision=None):
    return jnp.dot(a, b, preferred_element_type=F32, precision=precision)


def _dot_nt(a, b, precision=None):
    return lax.dot_general(a, b, (((1,), (1,)), ((), ())), preferred_element_type=F32, precision=precision)


def _silu(x):
    return x * jax.nn.sigmoid(x)


def _ada_kernel(c_ref, w_ref, b_ref, o_ref):
    a = _silu(c_ref[...]).astype(BF16)
    o_ref[...] = _dot(a, w_ref[...].astype(BF16)) + b_ref[...]


def _ada(c, w_ada, b_ada):
    n, d = c.shape
    cols = w_ada.shape[1]
    tn = 1024
    return pl.pallas_call(
        _ada_kernel,
        grid=(cols // tn,),
        in_specs=[pl.BlockSpec((n, d), lambda j: (0, 0)),
                  pl.BlockSpec((d, tn), lambda j: (0, j)),
                  pl.BlockSpec((1, tn), lambda j: (0, j))],
        out_specs=pl.BlockSpec((n, tn), lambda j: (0, j)),
        out_shape=jax.ShapeDtypeStruct((n, cols), F32),
        compiler_params=_params(("parallel",)),
        name="ada",
    )(c, w_ada, b_ada.reshape(1, cols))


def _modulated_norm(x, gain, shift, scale):
    y = x * lax.rsqrt(jnp.mean(x * x, axis=-1, keepdims=True) + EPS)
    return (y * gain) * (1.0 + scale) + shift


def _inproj_kernel(x_ref, mod_ref, n1_ref, w_ref, qkv_ref, z_ref, ab_ref, qb_ref, kb_ref, vb_ref, kh_ref, vh_ref):
    h = _modulated_norm(x_ref[0], n1_ref[...], mod_ref[0, 0:1, :], mod_ref[0, 1:2, :]).astype(BF16)
    qkv_ref[0] = _dot(h, w_ref[:, _C_QKV:_C_Z])
    z_ref[0] = _dot(h, w_ref[:, _C_Z:_C_AB])
    ab_ref[0] = _dot(h, w_ref[:, _C_AB:_C_QB])
    qb_ref[0] = _dot(h, w_ref[:, _C_QB:_C_KB])
    kb = _dot(h, w_ref[:, _C_KB:_C_VB])
    vb = _dot(h, w_ref[:, _C_VB:_C_END])
    kb_ref[0] = kb
    vb_ref[0] = vb
    kh_ref[0] = kb.astype(BF16)
    vh_ref[0] = vb.astype(BF16)


def _inproj(x, mod, norm1, w_packed):
    b, l, d = x.shape
    tl = min(l, 256)
    widths = (QKV_A, GDN_WIDTH, LANES, DIFF_WIDTH, DIFF_WIDTH, DIFF_WIDTH, DIFF_WIDTH, DIFF_WIDTH)
    dtypes = (F32,) * 6 + (BF16,) * 2
    return pl.pallas_call(
        _inproj_kernel,
        grid=(b, l // tl),
        in_specs=[pl.BlockSpec((1, tl, d), lambda i, j: (i, j, 0)),
                  pl.BlockSpec((1, 6, d), lambda i, j: (i, 0, 0)),
                  pl.BlockSpec((1, d), lambda i, j: (0, 0)),
                  pl.BlockSpec((d, _C_END), lambda i, j: (0, 0))],
        out_specs=[pl.BlockSpec((1, tl, w), lambda i, j: (i, j, 0)) for w in widths],
        out_shape=[jax.ShapeDtypeStruct((b, l, w), dt) for w, dt in zip(widths, dtypes)],
        compiler_params=_params(("parallel", "parallel")),
        name="inproj",
    )(x, mod, norm1.reshape(1, d), w_packed)


_HI = lax.Precision.HIGHEST


def _unit_lower_inverses(mats, n):
    r = lax.broadcasted_iota(jnp.int32, (n, n), 0)
    c = lax.broadcasted_iota(jnp.int32, (n, n), 1)
    eye = (r == c).astype(F32)
    ad = [jnp.where((r // 8) == (c // 8), a, 0.0) for a in mats]
    a2 = [_dot(m, m, _HI) for m in ad]
    a4 = [_dot(m, m, _HI) for m in a2]
    xs = [eye - m for m in ad]
    xs = [x + _dot(x, m, _HI) for x, m in zip(xs, a2)]
    xs = [x + _dot(x, m, _HI) for x, m in zip(xs, a4)]
    bs = 8
    while bs < n:
        off = ((r // (2 * bs)) == (c // (2 * bs))) & ((r // bs) != (c // bs))
        ys = [_dot(jnp.where(off, a, 0.0), x, _HI) for a, x in zip(mats, xs)]
        xs = [x - _dot(x, y, _HI) for x, y in zip(xs, ys)]
        bs *= 2
    return xs


def _mm(a, b):
    return _dot(a.astype(BF16), b.astype(BF16))


def _mm_nt(a, b):
    return _dot_nt(a.astype(BF16), b.astype(BF16))


def _gdn_kernel(qkv_ref, z_ref, ab_ref, conv0_ref, s0_ref, cw_ref, alog_ref, dtb_ref, onorm_ref,
                o_ref, s_ref, xbuf, *, chunk, nch):
    hd = GDN_HEAD_DIM
    rows = chunk * nch

    @pl.when(pl.program_id(1) == 0)
    def _():
        xbuf[5:8, :] = conv0_ref[0]
        s_ref[0] = s0_ref[0]

    x = qkv_ref[0]
    xbuf[8:8 + rows, :] = x
    y = (xbuf[5:5 + rows, :] * cw_ref[0:1, :] + xbuf[6:6 + rows, :] * cw_ref[1:2, :]
         + xbuf[7:7 + rows, :] * cw_ref[2:3, :] + x * cw_ref[3:4, :])
    xbuf[5:8, :] = x[rows - 3:rows, :]
    y = _silu(y)

    ab = ab_ref[0]
    t = ab + dtb_ref[...]
    softplus = jnp.maximum(t, 0.0) + jnp.log(1.0 + jnp.exp(-jnp.abs(t)))
    g = -jnp.exp(alog_ref[...]) * softplus
    beta = jax.nn.sigmoid(ab)

    r = lax.broadcasted_iota(jnp.int32, (chunk, chunk), 0)
    c = lax.broadcasted_iota(jnp.int32, (chunk, chunk), 1)
    lower = r >= c
    tri = lower.astype(F32)

    heads = range(GDN_HEADS)
    pairs = [(ci, h) for ci in range(nch) for h in heads]
    rows_of = {ci: slice(ci * chunk, (ci + 1) * chunk) for ci in range(nch)}
    gc = {ci: _dot(tri, g[rows_of[ci]], _HI) for ci in range(nch)}
    gc_t = {ci: gc[ci].T for ci in range(nch)}
    q, k, vb, kb, decay, egc, g_last = {}, {}, {}, {}, {}, {}, {}
    for ci, h in pairs:
        sl = rows_of[ci]
        qh = y[sl, h * hd:(h + 1) * hd]
        kh = y[sl, GDN_WIDTH + h * hd:GDN_WIDTH + (h + 1) * hd]
        p = ci, h
        q[p] = qh * lax.rsqrt(jnp.sum(qh * qh, axis=-1, keepdims=True) + EPS) * (hd ** -0.5)
        k[p] = kh * lax.rsqrt(jnp.sum(kh * kh, axis=-1, keepdims=True) + EPS)
        gcol = gc[ci][:, h:h + 1]
        bcol = beta[sl, GDN_HEADS + h:GDN_HEADS + h + 1]
        decay[p] = jnp.exp(jnp.where(lower, gcol - gc_t[ci][h:h + 1, :], NEG_BIG))
        kb[p] = k[p] * bcol
        vb[p] = y[sl, 2 * GDN_WIDTH + h * hd:2 * GDN_WIDTH + (h + 1) * hd] * bcol
        egc[p] = jnp.exp(gcol)
        g_last[p] = gcol[chunk - 1:chunk, :]
    kk = {p: _mm_nt(kb[p], k[p]) for p in pairs}
    tinv = dict(zip(pairs, _unit_lower_inverses([jnp.where(r > c, kk[p] * decay[p], 0.0) for p in pairs], chunk)))
    u_v = {p: _dot(tinv[p], vb[p], _HI) for p in pairs}
    w = {p: _dot(tinv[p], kb[p] * egc[p], _HI) for p in pairs}
    qk = {p: _mm_nt(q[p], k[p]) * decay[p] for p in pairs}
    k_dec_t = {p: (k[p] * jnp.exp(g_last[p] - gc[p[0]][:, p[1]:p[1] + 1])).T for p in pairs}

    s = [s_ref[0, h] for h in heads]
    for ci in range(nch):
        ws = [_mm(w[ci, h], s[h]) for h in heads]
        qs = [_mm(q[ci, h] * egc[ci, h], s[h]) for h in heads]
        v_new = [u_v[ci, h] - ws[h] for h in heads]
        o = [qs[h] + _mm(qk[ci, h], v_new[h]) for h in heads]
        s = [s[h] * jnp.exp(g_last[ci, h]) + _mm(k_dec_t[ci, h], v_new[h]) for h in heads]
        for h in heads:
            oh = o[h] * lax.rsqrt(jnp.mean(o[h] * o[h], axis=-1, keepdims=True) + EPS) * onorm_ref[...]
            o_ref[0, rows_of[ci], h * hd:(h + 1) * hd] = oh * _silu(z_ref[0, rows_of[ci], h * hd:(h + 1) * hd])
    for h in heads:
        s_ref[0, h] = s[h]


def _gdn(qkv, z, ab, conv0, s0, conv_w, alog_pad, dtb_pad, onorm, chunk):
    b, l, _ = qkv.shape
    hd = GDN_HEAD_DIM
    nch = 2 if l % (2 * chunk) == 0 else 1
    rows = chunk * nch
    return pl.pallas_call(
        functools.partial(_gdn_kernel, chunk=chunk, nch=nch),
        grid=(b, l // rows),
        in_specs=[pl.BlockSpec((1, rows, QKV_A), lambda i, j: (i, j, 0)),
                  pl.BlockSpec((1, rows, GDN_WIDTH), lambda i, j: (i, j, 0)),
                  pl.BlockSpec((1, rows, LANES), lambda i, j: (i, j, 0)),
                  pl.BlockSpec((1, CONV_W - 1, QKV_A), lambda i, j: (i, 0, 0)),
                  pl.BlockSpec((1, GDN_HEADS, hd, hd), lambda i, j: (i, 0, 0, 0)),
                  pl.BlockSpec((CONV_W, QKV_A), lambda i, j: (0, 0)),
                  pl.BlockSpec((1, LANES), lambda i, j: (0, 0)),
                  pl.BlockSpec((1, LANES), lambda i, j: (0, 0)),
                  pl.BlockSpec((1, hd), lambda i, j: (0, 0))],
        out_specs=[pl.BlockSpec((1, rows, GDN_WIDTH), lambda i, j: (i, j, 0)),
                   pl.BlockSpec((1, GDN_HEADS, hd, hd), lambda i, j: (i, 0, 0, 0))],
        out_shape=[jax.ShapeDtypeStruct((b, l, GDN_WIDTH), F32),
                   jax.ShapeDtypeStruct((b, GDN_HEADS, hd, hd), F32)],
        scratch_shapes=[pltpu.VMEM((8 + rows, QKV_A), F32)],
        compiler_params=_params(("parallel", "arbitrary")),
        name="gdn",
    )(qkv, z, ab, conv0, s0, conv_w, alog_pad, dtb_pad, onorm.reshape(1, hd))


def _rel_bucket(rel):
    nb = N_BUCKETS // 2
    max_exact = nb // 2
    ret = jnp.where(rel > 0, nb, 0)
    n = jnp.abs(rel)
    large = max_exact + (jnp.log(jnp.maximum(n, 1).astype(F32) / max_exact)
                         / math.log(REL_MAX_DIST / max_exact) * (nb - max_exact)).astype(jnp.int32)
    large = jnp.minimum(large, nb - 1)
    return ret + jnp.where(n < max_exact, n, large)


def _diff_finish(o1, o2, lam_ref, subln_ref, out_scale):
    o = o1 - lam_ref[...] * o2
    return o * lax.rsqrt(jnp.mean(o * o, axis=-1, keepdims=True) + EPS) * subln_ref[...] * out_scale


def _attn_prompt_kernel(q_ref, k_ref, v_ref, bias_ref, lam_ref, subln_ref, o_ref, m_ref, l_ref, acc_ref,
                        *, out_scale):
    i = pl.program_id(2)
    tb = ATT_BLOCK
    dh = DIFF_HEAD_DIM
    q = q_ref[0] * (dh ** -0.5)
    lane = lax.broadcasted_iota(jnp.int32, q.shape, 1)
    q2s = jnp.concatenate([jnp.where(lane < dh, q, 0.0), jnp.where(lane >= dh, q, 0.0)], axis=0).astype(BF16)

    def score_tiles(j, tile):
        start = pl.multiple_of(j * tb, tb)
        s = _dot_nt(q2s, k_ref[0, pl.ds(start, tb), :]) + bias_ref[0, tile]
        return [s[:, c:c + LANES] for c in range(0, tb, LANES)]

    def visible_blocks(fn, unroll):
        n_far = jnp.maximum(i - 1, 0)

        @pl.loop(0, n_far // unroll)
        def _(g):
            fn([(unroll * g + u, 0) for u in range(unroll)])

        @pl.loop((n_far // unroll) * unroll, n_far)
        def _(j):
            fn([(j, 0)])

        @pl.when(i > 0)
        def _():
            fn([(i - 1, 1), (i, 2)])

        @pl.when(i == 0)
        def _():
            fn([(i, 2)])

    m_ref[...] = jnp.full(m_ref.shape, NEG_BIG, F32)

    def track_max(blocks):
        tiles = [s for j, tile in blocks for s in score_tiles(j, tile)]
        m_ref[...] = functools.reduce(jnp.maximum, tiles, m_ref[...])

    visible_blocks(track_max, 4)
    m_ref[...] = jnp.broadcast_to(jnp.max(m_ref[...], axis=-1, keepdims=True), m_ref.shape)

    l_ref[...] = jnp.zeros(l_ref.shape, F32)
    acc_ref[...] = jnp.zeros(acc_ref.shape, F32)

    def accumulate(blocks):
        m = m_ref[...]
        l_add, acc_add = [], []
        for j, tile in blocks:
            p = [jnp.exp(s - m) for s in score_tiles(j, tile)]
            l_add.extend(p)
            start = pl.multiple_of(j * tb, tb)
            acc_add.append(_dot(jnp.concatenate(p, axis=-1).astype(BF16), v_ref[0, pl.ds(start, tb), :]))
        l_ref[...] += sum(l_add)
        acc_ref[...] += sum(acc_add)

    visible_blocks(accumulate, 4)
    o = acc_ref[...] / jnp.sum(l_ref[...], axis=-1, keepdims=True)
    o_ref[0] = _diff_finish(o[:tb], o[tb:], lam_ref, subln_ref, out_scale)


def _attn_prompt(qb, kh, vh, bias_tiles, lam_row, subln, out_scale):
    b, l, _ = qb.shape
    tb = ATT_BLOCK
    hw = 2 * DIFF_HEAD_DIM
    return pl.pallas_call(
        functools.partial(_attn_prompt_kernel, out_scale=out_scale),
        grid=(b, DIFF_HEADS, l // tb),
        in_specs=[pl.BlockSpec((1, tb, hw), lambda bi, h, i: (bi, i, h)),
                  pl.BlockSpec((1, l, hw), lambda bi, h, i: (bi, 0, h)),
                  pl.BlockSpec((1, l, DIFF_V_DIM), lambda bi, h, i: (bi, 0, h)),
                  pl.BlockSpec((1, 3, 2 * tb, tb), lambda bi, h, i: (h, 0, 0, 0)),
                  pl.BlockSpec((1, DIFF_V_DIM), lambda bi, h, i: (0, 0)),
                  pl.BlockSpec((1, DIFF_V_DIM), lambda bi, h, i: (0, 0))],
        out_specs=pl.BlockSpec((1, tb, DIFF_V_DIM), lambda bi, h, i: (bi, i, h)),
        out_shape=jax.ShapeDtypeStruct((b, l, DIFF_WIDTH), F32),
        scratch_shapes=[pltpu.VMEM((2 * tb, LANES), F32), pltpu.VMEM((2 * tb, LANES), F32),
                        pltpu.VMEM((2 * tb, DIFF_V_DIM), F32)],
        compiler_params=_params(("parallel", "parallel", "arbitrary")),
        name="attn_prompt",
    )(qb, kh, vh, bias_tiles, lam_row, subln.reshape(1, DIFF_V_DIM))


def _prompt_bias_tiles(rel_table):
    tb = ATT_BLOCK
    qi = jnp.arange(tb)[:, None]
    ki = jnp.arange(tb)[None, :]
    far = jnp.broadcast_to(_bias_lookup(rel_table, jnp.full((1, 1), -2 * tb)), (DIFF_HEADS, tb, tb))
    prev = _bias_lookup(rel_table, ki - qi - tb)
    diag = jnp.where((ki // CHUNK) <= (qi // CHUNK), _bias_lookup(rel_table, ki - qi), NEG_BIG)
    tiles = jnp.stack([far, prev, diag], axis=1)
    return jnp.concatenate([tiles, tiles], axis=2)


def _bias_lookup(rel_table, rel):
    onehot = (_rel_bucket(rel)[..., None] == jnp.arange(N_BUCKETS)).astype(F32)
    return jnp.einsum("...b,bh->h...", onehot, rel_table.astype(F32), precision=lax.Precision.HIGHEST)


def _attn_sample_kernel(q_ref, kp_ref, vp_ref, kn_ref, vn_ref, bp_ref, bn_ref, lam_ref, subln_ref, o_ref,
                        *, out_scale):
    dh = DIFF_HEAD_DIM
    q = q_ref[0] * (dh ** -0.5)
    kp = kp_ref[0].astype(BF16)
    kn = kn_ref[0].astype(BF16)
    vp = vp_ref[0].astype(BF16)
    vn = vn_ref[0].astype(BF16)
    outs = []
    for t in range(2):
        qt = q[:, t * dh:(t + 1) * dh].astype(BF16)
        sp = _dot_nt(qt, kp[:, t * dh:(t + 1) * dh]) + bp_ref[0]
        sn = _dot_nt(qt, kn[:, t * dh:(t + 1) * dh]) + bn_ref[0]
        m = jnp.maximum(jnp.max(sp, axis=-1, keepdims=True), jnp.max(sn, axis=-1, keepdims=True))
        pp = jnp.exp(sp - m)
        pn = jnp.exp(sn - m)
        den = jnp.sum(pp, axis=-1, keepdims=True) + jnp.sum(pn, axis=-1, keepdims=True)
        outs.append((_dot(pp.astype(BF16), vp) + _dot(pn.astype(BF16), vn)) / den)
    o_ref[0] = _diff_finish(outs[0], outs[1], lam_ref, subln_ref, out_scale)


def _attn_sample(qb, k_past, v_past, k_new, v_new, bias_past, bias_new, lam_row, subln, out_scale):
    b, l, _ = qb.shape
    p = k_past.shape[1]
    hw = 2 * DIFF_HEAD_DIM
    return pl.pallas_call(
        functools.partial(_attn_sample_kernel, out_scale=out_scale),
        grid=(b, DIFF_HEADS),
        in_specs=[pl.BlockSpec((1, l, hw), lambda bi, h: (bi, 0, h)),
                  pl.BlockSpec((1, p, hw), lambda bi, h: (bi, 0, h)),
                  pl.BlockSpec((1, p, DIFF_V_DIM), lambda bi, h: (bi, 0, h)),
                  pl.BlockSpec((1, l, hw), lambda bi, h: (bi, 0, h)),
                  pl.BlockSpec((1, l, DIFF_V_DIM), lambda bi, h: (bi, 0, h)),
                  pl.BlockSpec((1, l, p), lambda bi, h: (h, 0, 0)),
                  pl.BlockSpec((1, l, l), lambda bi, h: (h, 0, 0)),
                  pl.BlockSpec((1, DIFF_V_DIM), lambda bi, h: (0, 0)),
                  pl.BlockSpec((1, DIFF_V_DIM), lambda bi, h: (0, 0))],
        out_specs=pl.BlockSpec((1, l, DIFF_V_DIM), lambda bi, h: (bi, 0, h)),
        out_shape=jax.ShapeDtypeStruct((b, l, DIFF_WIDTH), F32),
        compiler_params=_params(("parallel", "parallel")),
        name="attn_sample",
    )(qb, k_past, v_past, k_new, v_new, bias_past, bias_new, lam_row, subln.reshape(1, DIFF_V_DIM))


def _sample_bias(rel_table, p, l):
    rel = jnp.arange(-(p + l - 1), l)
    by_rel = _bias_lookup(rel_table, rel)
    bias = jnp.stack([lax.slice_in_dim(by_rel, l - 1 - i, p + 2 * l - 1 - i, axis=1) for i in range(l)], axis=1)
    return bias[:, :, :p], bias[:, :, p:]


def _outproj_kernel(oa_ref, ob_ref, x_ref, mod_ref, n2_ref, wo_ref, wq_ref, keys_ref,
                    x1_ref, h2_ref, sc_ref):
    mixed = jnp.concatenate([oa_ref[0], ob_ref[0]], axis=-1).astype(BF16)
    x1 = x_ref[0] + mod_ref[0, 2:3, :] * _dot(mixed, wo_ref[...])
    x1_ref[0] = x1
    h2 = _modulated_norm(x1, n2_ref[...], mod_ref[0, 3:4, :], mod_ref[0, 4:5, :])
    h2_ref[0] = h2
    qh = _dot(h2.astype(BF16), wq_ref[...]).astype(BF16)
    for hp in range(2 * PEER_HEADS):
        sc_ref[0, hp] = _dot_nt(keys_ref[hp], qh[:, hp * PEER_HALF:(hp + 1) * PEER_HALF])


def _outproj(o_a, o_b, x, mod, norm2, w_out, w_q, keys):
    b, l, d = x.shape
    tl = min(l, 256)
    nhp = 2 * PEER_HEADS
    return pl.pallas_call(
        _outproj_kernel,
        grid=(b, l // tl),
        in_specs=[pl.BlockSpec((1, tl, GDN_WIDTH), lambda i, j: (i, j, 0)),
                  pl.BlockSpec((1, tl, DIFF_WIDTH), lambda i, j: (i, j, 0)),
                  pl.BlockSpec((1, tl, d), lambda i, j: (i, j, 0)),
                  pl.BlockSpec((1, 6, d), lambda i, j: (i, 0, 0)),
                  pl.BlockSpec((1, d), lambda i, j: (0, 0)),
                  pl.BlockSpec((d, d), lambda i, j: (0, 0)),
                  pl.BlockSpec((d, nhp * PEER_HALF), lambda i, j: (0, 0)),
                  pl.BlockSpec((nhp, N_KEYS, PEER_HALF), lambda i, j: (0, 0, 0))],
        out_specs=[pl.BlockSpec((1, tl, d), lambda i, j: (i, j, 0)),
                   pl.BlockSpec((1, tl, d), lambda i, j: (i, j, 0)),
                   pl.BlockSpec((1, nhp, N_KEYS, tl), lambda i, j: (i, 0, 0, j))],
        out_shape=[jax.ShapeDtypeStruct((b, l, d), F32),
                   jax.ShapeDtypeStruct((b, l, d), F32),
                   jax.ShapeDtypeStruct((b, nhp, N_KEYS, l), F32)],
        compiler_params=_params(("parallel", "parallel")),
        name="outproj",
    )(o_a, o_b, x, mod, norm2.reshape(1, d), w_out, w_q, keys)


def _top16_rows(s, ids, n):
    vals, idxs = [], []
    for _ in range(PEER_TOPK):
        m = jnp.max(s, axis=0, keepdims=True)
        i = jnp.min(jnp.where(s == m, ids, n), axis=0, keepdims=True)
        vals.append(m)
        idxs.append(i)
        s = jnp.where(ids == i, -jnp.inf, s)
    return jnp.concatenate(vals, axis=0), jnp.concatenate(idxs, axis=0)


def _pair_candidates(s1, s2):
    t = s1.shape[1]
    sub16 = lax.broadcasted_iota(jnp.int32, (PEER_TOPK, t), 0)
    sub8 = sub16[:8]
    cand = [s1[0:1] + s2] + [s1[a:a + 1] + s2[:8] for a in range(1, 8)] + [s1[8:] + s2[0:1]]
    pos = [sub16] + [a * PEER_TOPK + sub8 for a in range(1, 8)] + [(8 + sub8) * PEER_TOPK]
    return jnp.concatenate(cand, axis=0), jnp.concatenate(pos, axis=0)


def _pick_rows(table, sel):
    out = jnp.zeros_like(table)
    for a in range(PEER_TOPK):
        out = jnp.where(sel == a, table[a:a + 1, :], out)
    return out


def _topk_kernel(sc_ref, eidx_ref, gate_ref):
    eidx, gates = [], []
    key_ids = lax.broadcasted_iota(jnp.int32, sc_ref.shape[2:], 0)
    for h in range(PEER_HEADS):
        s1, i1 = _top16_rows(sc_ref[0, 2 * h], key_ids, N_KEYS)
        s2, i2 = _top16_rows(sc_ref[0, 2 * h + 1], key_ids, N_KEYS)
        cand, cand_pos = _pair_candidates(s1, s2)
        top_s, pos = _top16_rows(cand, cand_pos, PEER_TOPK * PEER_TOPK)
        eidx.append(_pick_rows(i1, pos // PEER_TOPK) * N_KEYS + _pick_rows(i2, pos % PEER_TOPK))
        e = jnp.exp(top_s - top_s[0:1, :])
        gates.append(e / jnp.sum(e, axis=0, keepdims=True))
    eidx_ref[...] = jnp.concatenate(eidx, axis=0).T
    gate_ref[...] = jnp.concatenate(gates, axis=0).T


def _topk(scores):
    b, nhp, nk, l = scores.shape
    tt = min(l, LANES)
    nt = l // tt
    return pl.pallas_call(
        _topk_kernel,
        grid=(b, nt),
        in_specs=[pl.BlockSpec((1, nhp, nk, tt), lambda i, j: (i, 0, 0, j))],
        out_specs=[pl.BlockSpec((tt, PEER_SLOTS), lambda i, j: (i * nt + j, 0)),
                   pl.BlockSpec((tt, PEER_SLOTS), lambda i, j: (i * nt + j, 0))],
        out_shape=[jax.ShapeDtypeStruct((b * l, PEER_SLOTS), jnp.int32),
                   jax.ShapeDtypeStruct((b * l, PEER_SLOTS), F32)],
        compiler_params=_params(("parallel", "parallel")),
        name="topk",
    )(scores)


_SC_ROWS = 32
_SC_SLOTS = 4


def _pack_kernel(t_ref, o_ref):
    half = t_ref.shape[2] // 2
    bits = lax.bitcast_convert_type(t_ref[0].astype(BF16).astype(F32), jnp.int32)
    o_ref[...] = lax.shift_right_logical(bits[:, :half], 16) | bits[:, half:]


def _pack_bf16_halves(t):
    _, v, d = t.shape
    tv = 512
    return pl.pallas_call(
        _pack_kernel,
        grid=(v // tv,),
        in_specs=[pl.BlockSpec((1, tv, d), lambda i: (0, i, 0))],
        out_specs=pl.BlockSpec((tv, d // 2), lambda i: (i, 0)),
        out_shape=jax.ShapeDtypeStruct((v, d // 2), jnp.int32),
        compiler_params=_params(("parallel",)),
        name="pack",
    )(t)


def _sc_mesh():
    info = plsc.get_sparse_core_info()
    mesh = plsc.VectorSubcoreMesh(core_axis_name="c", subcore_axis_name="s")
    return mesh, info.num_cores, info.num_cores * info.num_subcores


def _sc_gather(tab, idx):
    n = idx.shape[0]
    w = tab.shape[1]
    mesh, ncores, nw = _sc_mesh()
    per_w = n // nw
    ns, nr = _SC_SLOTS, _SC_ROWS
    ngroups = per_w // (ns * nr)
    assert n % (nw * ns * nr) == 0

    @functools.partial(
        pl.kernel, mesh=mesh, out_type=jax.ShapeDtypeStruct((n, w), tab.dtype),
        scratch_types=([pltpu.VMEM((per_w,), jnp.int32)] + [pltpu.VMEM((nr, w), tab.dtype)] * ns
                       + [pltpu.SemaphoreType.DMA] * (2 * ns)),
    )
    def k(tab_hbm, idx_hbm, out_hbm, idx_v, *scratch):
        bufs, gsem, wsem = scratch[:ns], scratch[ns:2 * ns], scratch[2 * ns:]
        base = (lax.axis_index("s") * ncores + lax.axis_index("c")) * per_w
        pltpu.sync_copy(idx_hbm.at[pl.ds(pl.multiple_of(base, nr), per_w)], idx_v)

        def gather(c, s):
            picks = idx_v.at[pl.ds(pl.multiple_of(c * nr, nr), nr)]
            return pltpu.make_async_copy(tab_hbm.at[picks], bufs[s], gsem[s])

        def write(c, s):
            span = pl.ds(pl.multiple_of(base + c * nr, nr), nr)
            return pltpu.make_async_copy(bufs[s], out_hbm.at[span], wsem[s])

        for s in range(ns):
            gather(s, s).start()

        @pl.loop(0, ngroups)
        def _(g):
            for s in range(ns):
                gather(g * ns + s, s).wait()
                write(g * ns + s, s).start()
            for s in range(ns):
                write(g * ns + s, s).wait()

                @pl.when(g < ngroups - 1)
                def _():
                    gather((g + 1) * ns + s, s).start()

    return k(tab, idx)


_ACC_ROWS = 32
_ACC_SLOTS = PEER_SLOTS // _ACC_ROWS
_ACC_TOKENS = 8


def _row_prefetch(tab_hbm, idx_v, rows, sems, nchunks):
    ns = _ACC_SLOTS

    def gather(c):
        picks = idx_v.at[pl.ds(pl.multiple_of(c * _ACC_ROWS, _ACC_ROWS), _ACC_ROWS)]
        return pltpu.make_async_copy(tab_hbm.at[picks], rows.at[c % ns], sems.at[c % ns])

    def prime():
        for c in range(ns - 1):
            gather(c).start()

    def step(c):
        @pl.when(c + ns - 1 < nchunks)
        def _():
            gather(c + ns - 1).start()

        gather(c).wait()

    return prime, step


def _sc_accumulate(tab, idx, coef):
    n = idx.shape[0]
    w = tab.shape[1]
    d = 2 * w
    lanes = 16
    ns = _ACC_SLOTS
    mesh, ncores, nw = _sc_mesh()
    ntok = n // PEER_SLOTS
    tok_w = ntok // nw
    per_w = tok_w * PEER_SLOTS
    nchunks = per_w // _ACC_ROWS
    assert ntok % (nw * _ACC_TOKENS) == 0 and w % (4 * lanes) == 0

    @functools.partial(
        pl.kernel, mesh=mesh, out_type=jax.ShapeDtypeStruct((ntok, d), F32),
        scratch_types=[pltpu.VMEM((per_w,), jnp.int32), pltpu.VMEM((per_w,), F32),
                       pltpu.VMEM((_ACC_TOKENS, d), F32), pltpu.VMEM((ns, _ACC_ROWS, w), tab.dtype),
                       pltpu.SemaphoreType.DMA((ns,))],
        compiler_params=pltpu.CompilerParams(needs_layout_passes=False),
    )
    def k(tab_hbm, idx_hbm, coef_hbm, out_hbm, idx_v, coef_v, acc_v, rows, sems):
        wid = lax.axis_index("s") * ncores + lax.axis_index("c")
        base = pl.multiple_of(wid * per_w, _ACC_ROWS)
        pltpu.sync_copy(idx_hbm.at[pl.ds(base, per_w)], idx_v)
        pltpu.sync_copy(coef_hbm.at[pl.ds(base, per_w)], coef_v)
        prime, fetch_step = _row_prefetch(tab_hbm, idx_v, rows, sems, nchunks)

        def accumulate(rows_ref, c, arow):
            nrow, ncol = 4, 4

            @pl.loop(0, _ACC_ROWS // nrow)
            def _(q):
                r0 = q * nrow
                cvec = [plsc.load_gather(coef_v, [jnp.full((lanes,), c * _ACC_ROWS + r0 + i, jnp.int32)])
                        for i in range(nrow)]
                groups = [[col0 + j * lanes for j in range(ncol)] for col0 in range(0, w, ncol * lanes)]

                def load(cols):
                    return [[rows_ref[r0 + i, pl.ds(col, lanes)] for i in range(nrow)] for col in cols]

                ahead = load(groups[0])
                for g, cols in enumerate(groups):
                    words = ahead
                    if g + 1 < len(groups):
                        ahead = load(groups[g + 1])
                    sums = []
                    for wds in words:
                        lo = [cvec[i] * lax.bitcast_convert_type(wds[i] << 16, F32) for i in range(nrow)]
                        hi = [cvec[i] * lax.bitcast_convert_type(wds[i] & jnp.int32(-65536), F32)
                              for i in range(nrow)]
                        sums.append((functools.reduce(jnp.add, lo), functools.reduce(jnp.add, hi)))
                    for col, (lo, hi) in zip(cols, sums):
                        plsc.addupdate(acc_v.at[arow, pl.ds(col, lanes)], lo)
                        plsc.addupdate(acc_v.at[arow, pl.ds(w + col, lanes)], hi)

        prime()

        @pl.loop(0, tok_w)
        def _(t):
            arow = t % _ACC_TOKENS
            for col in range(0, d, lanes):
                acc_v[arow, pl.ds(col, lanes)] = jnp.zeros((lanes,), F32)
            @pl.loop(0, ns)
            def _(q):
                c = ns * t + q
                fetch_step(c)
                accumulate(rows.at[q], c, arow)

            @pl.when(arow == _ACC_TOKENS - 1)
            def _():
                first_tok = pl.multiple_of(wid * tok_w + t - (_ACC_TOKENS - 1), _ACC_TOKENS)
                pltpu.sync_copy(acc_v, out_hbm.at[pl.ds(first_tok, _ACC_TOKENS)])

    return k(tab, idx, coef)


_DOT_TOKENS = 8


def _sc_dot(tab, idx, x):
    n = idx.shape[0]
    w = tab.shape[1]
    d = 2 * w
    lanes = 16
    ns = _ACC_SLOTS
    mesh, ncores, nw = _sc_mesh()
    ntok = n // PEER_SLOTS
    tok_w = ntok // nw
    per_w = tok_w * PEER_SLOTS
    nchunks = per_w // _ACC_ROWS
    assert ntok % (nw * _DOT_TOKENS) == 0 and _ACC_ROWS % lanes == 0

    @functools.partial(
        pl.kernel, mesh=mesh, out_type=jax.ShapeDtypeStruct((n,), F32),
        scratch_types=[pltpu.VMEM((per_w,), jnp.int32), pltpu.VMEM((per_w,), F32),
                       pltpu.VMEM((_DOT_TOKENS, d), F32), pltpu.VMEM((ns, _ACC_ROWS, w), tab.dtype),
                       pltpu.SemaphoreType.DMA((ns,))],
        compiler_params=pltpu.CompilerParams(needs_layout_passes=False),
    )
    def k(tab_hbm, idx_hbm, x_hbm, pre_hbm, idx_v, pre_v, x_v, rows, sems):
        wid = lax.axis_index("s") * ncores + lax.axis_index("c")
        base = pl.multiple_of(wid * per_w, _ACC_ROWS)
        pltpu.sync_copy(idx_hbm.at[pl.ds(base, per_w)], idx_v)
        prime, fetch_step = _row_prefetch(tab_hbm, idx_v, rows, sems, nchunks)

        def dots(rows_ref, c, xrow):
            lane_id = lax.iota(jnp.int32, lanes)
            nrow = 4

            @pl.loop(0, _ACC_ROWS // lanes)
            def _(q):
                out = jnp.zeros((lanes,), F32)
                for sub in range(lanes // nrow):
                    r0 = q * lanes + sub * nrow
                    accs = [None] * nrow
                    for col in range(0, w, lanes):
                        x_lo = x_v[xrow, pl.ds(col, lanes)]
                        x_hi = x_v[xrow, pl.ds(w + col, lanes)]
                        for i in range(nrow):
                            word = rows_ref[r0 + i, pl.ds(col, lanes)]
                            term = (lax.bitcast_convert_type(word << 16, F32) * x_lo
                                    + lax.bitcast_convert_type(word & jnp.int32(-65536), F32) * x_hi)
                            accs[i] = term if accs[i] is None else accs[i] + term
                    for i in range(nrow):
                        out = jnp.where(lane_id == sub * nrow + i, jnp.sum(accs[i]), out)
                pre_v[pl.ds(pl.multiple_of(c * _ACC_ROWS + q * lanes, lanes), lanes)] = out

        prime()

        @pl.loop(0, tok_w)
        def _(t):
            xrow = t % _DOT_TOKENS

            @pl.when(xrow == 0)
            def _():
                first_tok = pl.multiple_of(wid * tok_w + t, _DOT_TOKENS)
                pltpu.sync_copy(x_hbm.at[pl.ds(first_tok, _DOT_TOKENS)], x_v)

            @pl.loop(0, ns)
            def _(q):
                c = ns * t + q
                fetch_step(c)
                dots(rows.at[q], c, xrow)

        pltpu.sync_copy(pre_v, pre_hbm.at[pl.ds(base, per_w)])

    return k(tab, idx, x)


def _unpack_bf16_halves(w):
    lo = lax.bitcast_convert_type(w << 16, F32)
    hi = lax.bitcast_convert_type(w & jnp.int32(-65536), F32)
    return lo, hi


def _coef_kernel(pre_ref, gate_ref, coef_ref):
    pre = pre_ref[...]
    coef_ref[...] = gate_ref[...] * (0.5 * pre * (1.0 + lax.erf(pre * (2.0 ** -0.5))))


def _peer_coef(pre, gate):
    n, s = gate.shape
    tn = min(n, 1024)
    return pl.pallas_call(
        _coef_kernel,
        grid=(n // tn,),
        in_specs=[pl.BlockSpec((tn, s), lambda i: (i, 0)), pl.BlockSpec((tn, s), lambda i: (i, 0))],
        out_specs=pl.BlockSpec((tn, s), lambda i: (i, 0)),
        out_shape=jax.ShapeDtypeStruct((n, s), F32),
        compiler_params=_params(("parallel",)),
        name="peer_coef",
    )(pre, gate)


def _final_kernel(acc_ref, x1_ref, g2_ref, fn_ref, y_ref):
    x2 = x1_ref[0] + g2_ref[0, 5:6, :] * acc_ref[0]
    y_ref[0] = x2 * lax.rsqrt(jnp.mean(x2 * x2, axis=-1, keepdims=True) + EPS) * fn_ref[...]


def _peer_final(acc, x1, mod, final_norm):
    b, l, d = x1.shape
    tl = min(l, 512)
    return pl.pallas_call(
        _final_kernel,
        grid=(b, l // tl),
        in_specs=[pl.BlockSpec((1, tl, d), lambda i, j: (i, j, 0)),
                  pl.BlockSpec((1, tl, d), lambda i, j: (i, j, 0)),
                  pl.BlockSpec((1, 6, d), lambda i, j: (i, 0, 0)),
                  pl.BlockSpec((1, d), lambda i, j: (0, 0))],
        out_specs=pl.BlockSpec((1, tl, d), lambda i, j: (i, j, 0)),
        out_shape=jax.ShapeDtypeStruct((b, l, d), F32),
        compiler_params=_params(("parallel", "parallel")),
        name="peer_final",
    )(acc, x1, mod, final_norm.reshape(1, d))


def _front(x, mod, conv0, s0, k_past, v_past, wts, prompt):
    b, l, d = x.shape
    qkv, z, ab, qb, kb, vb, kh, vh = _inproj(x, mod, wts["norm1"], wts["w_in"])
    chunk = CHUNK if prompt else l
    o_a, s_new = _gdn(qkv, z, ab, conv0, s0, wts["conv_w"], wts["alog"], wts["dtb"], wts["onorm"], chunk)
    conv_new = qkv[:, l - (CONV_W - 1):, :]
    if prompt:
        o_b = _attn_prompt(qb, kh, vh, wts["bias_prompt"], wts["lam"], wts["subln"], wts["out_scale"])
    else:
        p = k_past.shape[1]
        bias_past, bias_new = _sample_bias(wts["rel_table"], p, l)
        o_b = _attn_sample(qb, k_past.reshape(b, p, DIFF_WIDTH), v_past.reshape(b, p, DIFF_WIDTH), kb, vb,
                           bias_past, bias_new, wts["lam"], wts["subln"], wts["out_scale"])
    x1, h2, scores = _outproj(o_a, o_b, x, mod, wts["norm2"], wts["w_out"], wts["w_q"], wts["keys"])
    eidx, gate = _topk(scores)
    return (eidx.reshape(-1), gate, h2, x1, mod), (kb, vb, s_new, conv_new)


def _back(pending, wts, coef_hook=None):
    eidx, gate, h2, x1, mod = pending
    b, l, d = h2.shape
    pre = _sc_dot(wts["peer_u"], eidx, h2.reshape(b * l, d))
    coef = _peer_coef(pre.reshape(b * l, PEER_SLOTS), gate)
    if coef_hook is not None:
        coef = coef_hook(coef)
    acc = _sc_accumulate(wts["peer_v"], eidx, coef.reshape(-1))
    return _peer_final(acc.reshape(b, l, d), x1, mod, wts["final_norm"])


def _prompt_rows(x, mod, wts, sample_front):
    b, l, d = x.shape
    conv0 = jnp.zeros((1, CONV_W - 1, QKV_A), F32)
    s0 = jnp.zeros((1, GDN_HEADS, GDN_HEAD_DIM, GDN_HEAD_DIM), F32)
    news, ys, sample = [], [], []
    xs = [lax.dynamic_slice_in_dim(x, t, 1, 0) for t in range(b)]

    def hook(t, coef):
        if t + 2 < b:
            coef, xs[t + 2] = lax.optimization_barrier((coef, xs[t + 2]))
        if t == 0:
            coef, result = sample_front(coef)
            sample.append(result)
        return coef

    for t in range(b):
        pend, new = _front(xs[t], lax.dynamic_slice_in_dim(mod, t, 1, 0), conv0, s0, None, None, wts, True)
        news.append(new)
        ys.append(_back(pend, wts, functools.partial(hook, t)))
    y = jnp.concatenate(ys, axis=0)
    return y, [jnp.concatenate([n[k] for n in news], axis=0) for k in range(4)], sample[0]


def _cache_entries(new, b, l):
    kb, vb, s_new, conv_new = new
    return (kb.reshape(1, b, l, DIFF_HEADS, 2 * DIFF_HEAD_DIM), vb.reshape(1, b, l, DIFF_HEADS, DIFF_V_DIM),
            s_new[None], conv_new[None])


def kernel(x_prompt, x_sample, c_prompt, c_sample, cache_k, cache_v, state_gdn, state_conv, w_ada, b_ada,
           norm1, norm2, w_in, conv_w, a_log, dt_bias, gdn_onorm, lam_q1, lam_k1, lam_q2, lam_k2, diff_subln,
           w_out, peer_wq, peer_keys, peer_u, peer_v, rel_table, final_norm):
    assert w_ada.shape[0] == 1, "single-layer step"
    bp = x_prompt.shape[0]
    d = D_MODEL
    lam_init = 0.8 - 0.6 * math.exp(-0.3 * 0)
    lam = (jnp.exp(jnp.sum(lam_q1[0] * lam_k1[0])) - jnp.exp(jnp.sum(lam_q2[0] * lam_k2[0])) + lam_init)
    w = w_in[0]
    w_packed = jnp.concatenate(
        [w[:, :_C_AB], jnp.pad(w[:, 2048:2056], ((0, 0), (0, LANES - 2 * GDN_HEADS))), w[:, 2056:]],
        axis=1).astype(BF16)
    wts = dict(
        norm1=norm1[0], norm2=norm2[0], w_in=w_packed, conv_w=conv_w[0],
        alog=jnp.pad(a_log[0], (0, LANES - GDN_HEADS)).reshape(1, LANES),
        dtb=jnp.pad(dt_bias[0], (0, LANES - GDN_HEADS)).reshape(1, LANES),
        onorm=gdn_onorm[0], lam=jnp.full((1, DIFF_V_DIM), lam, F32), subln=diff_subln[0],
        out_scale=1.0 - lam_init, bias_prompt=_prompt_bias_tiles(rel_table), rel_table=rel_table,
        w_out=w_out[0].astype(BF16), w_q=peer_wq[0].astype(BF16),
        keys=peer_keys[0].reshape(2 * PEER_HEADS, N_KEYS, PEER_HALF).astype(BF16),
        peer_u=_pack_bf16_halves(peer_u), peer_v=_pack_bf16_halves(peer_v), final_norm=final_norm)

    mod = _ada(jnp.concatenate([c_prompt, c_sample], axis=0), w_ada[0], b_ada[0]).reshape(-1, 6, d)
    def sample_front(dep):
        xs, dep = lax.optimization_barrier((x_sample, dep))
        return dep, _front(xs, mod[bp:], state_conv[0], state_gdn[0], cache_k[0], cache_v[0], wts, False)

    yp, new_p, (pend_s, new_s) = _prompt_rows(x_prompt, mod[:bp], wts, sample_front)
    ys = _back(pend_s, wts)
    kp, vp, sp, cp = _cache_entries(new_p, *x_prompt.shape[:2])
    ks, vs, ss, cs = _cache_entries(new_s, *x_sample.shape[:2])
    return yp, ys, kp, vp, sp, cp, ks, vs, ss, cs
```

```python
import functools
import math

import jax
import jax.numpy as jnp
from jax import lax
from jax.experimental import pallas as pl
from jax.experimental.pallas import tpu as pltpu
from jax.experimental.pallas import tpu_sc as plsc

F32 = jnp.float32
BF16 = jnp.bfloat16
EPS = 1e-6

D_MODEL = 1024
CHUNK = 64
GDN_HEADS = 4
GDN_HEAD_DIM = 128
GDN_WIDTH = GDN_HEADS * GDN_HEAD_DIM
CONV_W = 4
QKV_A = 3 * GDN_WIDTH
DIFF_HEADS = 4
DIFF_HEAD_DIM = 64
DIFF_V_DIM = 128
DIFF_WIDTH = DIFF_HEADS * 2 * DIFF_HEAD_DIM
ATT_BLOCK = 256
N_BUCKETS = 32
REL_MAX_DIST = 128
PEER_HEADS = 8
N_KEYS = 128
PEER_HALF = 128
PEER_TOPK = 16
PEER_SLOTS = PEER_HEADS * PEER_TOPK
LANES = 128
NEG_BIG = -1e30
VMEM_LIMIT = 56 * 1024 * 1024

_C_QKV, _C_Z, _C_AB, _C_QB, _C_KB, _C_VB = 0, 1536, 2048, 2176, 2688, 3200
_C_END = 3712


def _params(sem):
    return pltpu.CompilerParams(dimension_semantics=sem, vmem_limit_bytes=VMEM_LIMIT)


def _dot(a, b, precision=None):
    return jnp.dot(a, b, preferred_element_type=F32, precision=precision)


def _dot_nt(a, b, precision=None):
    return lax.dot_general(a, b, (((1,), (1,)), ((), ())), preferred_element_type=F32, precision=precision)


def _silu(x):
    return x * jax.nn.sigmoid(x)


def _ada_kernel(c_ref, w_ref, b_ref, o_ref):
    a = _silu(c_ref[...]).astype(BF16)
    o_ref[...] = _dot(a, w_ref[...].astype(BF16)) + b_ref[...]


def _ada(c, w_ada, b_ada):
    n, d = c.shape
    cols = w_ada.shape[1]
    tn = 1024
    return pl.pallas_call(
        _ada_kernel,
        grid=(cols // tn,),
        in_specs=[pl.BlockSpec((n, d), lambda j: (0, 0)),
                  pl.BlockSpec((d, tn), lambda j: (0, j)),
                  pl.BlockSpec((1, tn), lambda j: (0, j))],
        out_specs=pl.BlockSpec((n, tn), lambda j: (0, j)),
        out_shape=jax.ShapeDtypeStruct((n, cols), F32),
        compiler_params=_params(("parallel",)),
        name="ada",
    )(c, w_ada, b_ada.reshape(1, cols))


def _modulated_norm(x, gain, shift, scale):
    y = x * lax.rsqrt(jnp.mean(x * x, axis=-1, keepdims=True) + EPS)
    return (y * gain) * (1.0 + scale) + shift


def _inproj_kernel(x_ref, mod_ref, n1_ref, w_ref, qkv_ref, z_ref, ab_ref, qb_ref, kb_ref, vb_ref, kh_ref, vh_ref):
    h = _modulated_norm(x_ref[0], n1_ref[...], mod_ref[0, 0:1, :], mod_ref[0, 1:2, :]).astype(BF16)
    qkv_ref[0] = _dot(h, w_ref[:, _C_QKV:_C_Z])
    z_ref[0] = _dot(h, w_ref[:, _C_Z:_C_AB])
    ab_ref[0] = _dot(h, w_ref[:, _C_AB:_C_QB])
    qb_ref[0] = _dot(h, w_ref[:, _C_QB:_C_KB])
    kb = _dot(h, w_ref[:, _C_KB:_C_VB])
    vb = _dot(h, w_ref[:, _C_VB:_C_END])
    kb_ref[0] = kb
    vb_ref[0] = vb
    kh_ref[0] = kb.astype(BF16)
    vh_ref[0] = vb.astype(BF16)


def _inproj(x, mod, norm1, w_packed):
    b, l, d = x.shape
    tl = min(l, 256)
    widths = (QKV_A, GDN_WIDTH, LANES, DIFF_WIDTH, DIFF_WIDTH, DIFF_WIDTH, DIFF_WIDTH, DIFF_WIDTH)
    dtypes = (F32,) * 6 + (BF16,) * 2
    return pl.pallas_call(
        _inproj_kernel,
        grid=(b, l // tl),
        in_specs=[pl.BlockSpec((1, tl, d), lambda i, j: (i, j, 0)),
                  pl.BlockSpec((1, 6, d), lambda i, j: (i, 0, 0)),
                  pl.BlockSpec((1, d), lambda i, j: (0, 0)),
                  pl.BlockSpec((d, _C_END), lambda i, j: (0, 0))],
        out_specs=[pl.BlockSpec((1, tl, w), lambda i, j: (i, j, 0)) for w in widths],
        out_shape=[jax.ShapeDtypeStruct((b, l, w), dt) for w, dt in zip(widths, dtypes)],
        compiler_params=_params(("parallel", "parallel")),
        name="inproj",
    )(x, mod, norm1.reshape(1, d), w_packed)


_HI = lax.Precision.HIGHEST


def _unit_lower_inverses(mats, n):
    r = lax.broadcasted_iota(jnp.int32, (n, n), 0)
    c = lax.broadcasted_iota(jnp.int32, (n, n), 1)
    eye = (r == c).astype(F32)
    ad = [jnp.where((r // 8) == (c // 8), a, 0.0) for a in mats]
    a2 = [_dot(m, m, _HI) for m in ad]
    a4 = [_dot(m, m, _HI) for m in a2]
    xs = [eye - m for m in ad]
    xs = [x + _dot(x, m, _HI) for x, m in zip(xs, a2)]
    xs = [x + _dot(x, m, _HI) for x, m in zip(xs, a4)]
    bs = 8
    while bs < n:
        off = ((r // (2 * bs)) == (c // (2 * bs))) & ((r // bs) != (c // bs))
        ys = [_dot(jnp.where(off, a, 0.0), x, _HI) for a, x in zip(mats, xs)]
        xs = [x - _dot(x, y, _HI) for x, y in zip(xs, ys)]
        bs *= 2
    return xs


def _mm(a, b):
    return _dot(a.astype(BF16), b.astype(BF16))


def _mm_nt(a, b):
    return _dot_nt(a.astype(BF16), b.astype(BF16))


def _gdn_kernel(qkv_ref, z_ref, ab_ref, conv0_ref, s0_ref, cw_ref, alog_ref, dtb_ref, onorm_ref,
                o_ref, s_ref, xbuf, *, chunk, nch):
    hd = GDN_HEAD_DIM
    rows = chunk * nch

    @pl.when(pl.program_id(1) == 0)
    def _():
        xbuf[5:8, :] = conv0_ref[0]
        s_ref[0] = s0_ref[0]

    x = qkv_ref[0]
    xbuf[8:8 + rows, :] = x
    y = (xbuf[5:5 + rows, :] * cw_ref[0:1, :] + xbuf[6:6 + rows, :] * cw_ref[1:2, :]
         + xbuf[7:7 + rows, :] * cw_ref[2:3, :] + x * cw_ref[3:4, :])
    xbuf[5:8, :] = x[rows - 3:rows, :]
    y = _silu(y)

    ab = ab_ref[0]
    t = ab + dtb_ref[...]
    softplus = jnp.maximum(t, 0.0) + jnp.log(1.0 + jnp.exp(-jnp.abs(t)))
    g = -jnp.exp(alog_ref[...]) * softplus
    beta = jax.nn.sigmoid(ab)

    r = lax.broadcasted_iota(jnp.int32, (chunk, chunk), 0)
    c = lax.broadcasted_iota(jnp.int32, (chunk, chunk), 1)
    lower = r >= c
    tri = lower.astype(F32)

    heads = range(GDN_HEADS)
    pairs = [(ci, h) for ci in range(nch) for h in heads]
    rows_of = {ci: slice(ci * chunk, (ci + 1) * chunk) for ci in range(nch)}
    gc = {ci: _dot(tri, g[rows_of[ci]], _HI) for ci in range(nch)}
    gc_t = {ci: gc[ci].T for ci in range(nch)}
    q, k, vb, kb, decay, egc, g_last = {}, {}, {}, {}, {}, {}, {}
    for ci, h in pairs:
        sl = rows_of[ci]
        qh = y[sl, h * hd:(h + 1) * hd]
        kh = y[sl, GDN_WIDTH + h * hd:GDN_WIDTH + (h + 1) * hd]
        p = ci, h
        q[p] = qh * lax.rsqrt(jnp.sum(qh * qh, axis=-1, keepdims=True) + EPS) * (hd ** -0.5)
        k[p] = kh * lax.rsqrt(jnp.sum(kh * kh, axis=-1, keepdims=True) + EPS)
        gcol = gc[ci][:, h:h + 1]
        bcol = beta[sl, GDN_HEADS + h:GDN_HEADS + h + 1]
        decay[p] = jnp.exp(jnp.where(lower, gcol - gc_t[ci][h:h + 1, :], NEG_BIG))
        kb[p] = k[p] * bcol
        vb[p] = y[sl, 2 * GDN_WIDTH + h * hd:2 * GDN_WIDTH + (h + 1) * hd] * bcol
        egc[p] = jnp.exp(gcol)
        g_last[p] = gcol[chunk - 1:chunk, :]
    kk = {p: _mm_nt(kb[p], k[p]) for p in pairs}
    tinv = dict(zip(pairs, _unit_lower_inverses([jnp.where(r > c, kk[p] * decay[p], 0.0) for p in pairs], chunk)))
    u_v = {p: _dot(tinv[p], vb[p], _HI) for p in pairs}
    w = {p: _dot(tinv[p], kb[p] * egc[p], _HI) for p in pairs}
    qk = {p: _mm_nt(q[p], k[p]) * decay[p] for p in pairs}
    k_dec_t = {p: (k[p] * jnp.exp(g_last[p] - gc[p[0]][:, p[1]:p[1] + 1])).T for p in pairs}

    s = [s_ref[0, h] for h in heads]
    for ci in range(nch):
        ws = [_mm(w[ci, h], s[h]) for h in heads]
        qs = [_mm(q[ci, h] * egc[ci, h], s[h]) for h in heads]
        v_new = [u_v[ci, h] - ws[h] for h in heads]
        o = [qs[h] + _mm(qk[ci, h], v_new[h]) for h in heads]
        s = [s[h] * jnp.exp(g_last[ci, h]) + _mm(k_dec_t[ci, h], v_new[h]) for h in heads]
        for h in heads:
            oh = o[h] * lax.rsqrt(jnp.mean(o[h] * o[h], axis=-1, keepdims=True) + EPS) * onorm_ref[...]
            o_ref[0, rows_of[ci], h * hd:(h + 1) * hd] = oh * _silu(z_ref[0, rows_of[ci], h * hd:(h + 1) * hd])
    for h in heads:
        s_ref[0, h] = s[h]


def _gdn(qkv, z, ab, conv0, s0, conv_w, alog_pad, dtb_pad, onorm, chunk):
    b, l, _ = qkv.shape
    hd = GDN_HEAD_DIM
    nch = 2 if l % (2 * chunk) == 0 else 1
    rows = chunk * nch
    return pl.pallas_call(
        functools.partial(_gdn_kernel, chunk=chunk, nch=nch),
        grid=(b, l // rows),
        in_specs=[pl.BlockSpec((1, rows, QKV_A), lambda i, j: (i, j, 0)),
                  pl.BlockSpec((1, rows, GDN_WIDTH), lambda i, j: (i, j, 0)),
                  pl.BlockSpec((1, rows, LANES), lambda i, j: (i, j, 0)),
                  pl.BlockSpec((1, CONV_W - 1, QKV_A), lambda i, j: (i, 0, 0)),
                  pl.BlockSpec((1, GDN_HEADS, hd, hd), lambda i, j: (i, 0, 0, 0)),
                  pl.BlockSpec((CONV_W, QKV_A), lambda i, j: (0, 0)),
                  pl.BlockSpec((1, LANES), lambda i, j: (0, 0)),
                  pl.BlockSpec((1, LANES), lambda i, j: (0, 0)),
                  pl.BlockSpec((1, hd), lambda i, j: (0, 0))],
        out_specs=[pl.BlockSpec((1, rows, GDN_WIDTH), lambda i, j: (i, j, 0)),
                   pl.BlockSpec((1, GDN_HEADS, hd, hd), lambda i, j: (i, 0, 0, 0))],
        out_shape=[jax.ShapeDtypeStruct((b, l, GDN_WIDTH), F32),
                   jax.ShapeDtypeStruct((b, GDN_HEADS, hd, hd), F32)],
        scratch_shapes=[pltpu.VMEM((8 + rows, QKV_A), F32)],
        compiler_params=_params(("parallel", "arbitrary")),
        name="gdn",
    )(qkv, z, ab, conv0, s0, conv_w, alog_pad, dtb_pad, onorm.reshape(1, hd))


def _rel_bucket(rel):
    nb = N_BUCKETS // 2
    max_exact = nb // 2
    ret = jnp.where(rel > 0, nb, 0)
    n = jnp.abs(rel)
    large = max_exact + (jnp.log(jnp.maximum(n, 1).astype(F32) / max_exact)
                         / math.log(REL_MAX_DIST / max_exact) * (nb - max_exact)).astype(jnp.int32)
    large = jnp.minimum(large, nb - 1)
    return ret + jnp.where(n < max_exact, n, large)


def _diff_finish(o1, o2, lam_ref, subln_ref, out_scale):
    o = o1 - lam_ref[...] * o2
    return o * lax.rsqrt(jnp.mean(o * o, axis=-1, keepdims=True) + EPS) * subln_ref[...] * out_scale


def _attn_prompt_kernel(q_ref, k_ref, v_ref, bias_ref, lam_ref, subln_ref, o_ref, m_ref, l_ref, acc_ref,
                        *, out_scale):
    i = pl.program_id(2)
    tb = ATT_BLOCK
    dh = DIFF_HEAD_DIM
    q = q_ref[0] * (dh ** -0.5)
    lane = lax.broadcasted_iota(jnp.int32, q.shape, 1)
    q2s = jnp.concatenate([jnp.where(lane < dh, q, 0.0), jnp.where(lane >= dh, q, 0.0)], axis=0).astype(BF16)

    def score_tiles(j, tile):
        start = pl.multiple_of(j * tb, tb)
        s = _dot_nt(q2s, k_ref[0, pl.ds(start, tb), :]) + bias_ref[0, tile]
        return [s[:, c:c + LANES] for c in range(0, tb, LANES)]

    def visible_blocks(fn, unroll):
        n_far = jnp.maximum(i - 1, 0)

        @pl.loop(0, n_far // unroll)
        def _(g):
            fn([(unroll * g + u, 0) for u in range(unroll)])

        @pl.loop((n_far // unroll) * unroll, n_far)
        def _(j):
            fn([(j, 0)])

        @pl.when(i > 0)
        def _():
            fn([(i - 1, 1), (i, 2)])

        @pl.when(i == 0)
        def _():
            fn([(i, 2)])

    m_ref[...] = jnp.full(m_ref.shape, NEG_BIG, F32)

    def track_max(blocks):
        tiles = [s for j, tile in blocks for s in score_tiles(j, tile)]
        m_ref[...] = functools.reduce(jnp.maximum, tiles, m_ref[...])

    visible_blocks(track_max, 4)
    m_ref[...] = jnp.broadcast_to(jnp.max(m_ref[...], axis=-1, keepdims=True), m_ref.shape)

    l_ref[...] = jnp.zeros(l_ref.shape, F32)
    acc_ref[...] = jnp.zeros(acc_ref.shape, F32)

    def accumulate(blocks):
        m = m_ref[...]
        l_add, acc_add = [], []
        for j, tile in blocks:
            p = [jnp.exp(s - m) for s in score_tiles(j, tile)]
            l_add.extend(p)
            start = pl.multiple_of(j * tb, tb)
            acc_add.append(_dot(jnp.concatenate(p, axis=-1).astype(BF16), v_ref[0, pl.ds(start, tb), :]))
        l_ref[...] += sum(l_add)
        acc_ref[...] += sum(acc_add)

    visible_blocks(accumulate, 4)
    o = acc_ref[...] / jnp.sum(l_ref[...], axis=-1, keepdims=True)
    o_ref[0] = _diff_finish(o[:tb], o[tb:], lam_ref, subln_ref, out_scale)


def _attn_prompt(qb, kh, vh, bias_tiles, lam_row, subln, out_scale):
    b, l, _ = qb.shape
    tb = ATT_BLOCK
    hw = 2 * DIFF_HEAD_DIM
    return pl.pallas_call(
        functools.partial(_attn_prompt_kernel, out_scale=out_scale),
        grid=(b, DIFF_HEADS, l // tb),
        in_specs=[pl.BlockSpec((1, tb, hw), lambda bi, h, i: (bi, i, h)),
                  pl.BlockSpec((1, l, hw), lambda bi, h, i: (bi, 0, h)),
                  pl.BlockSpec((1, l, DIFF_V_DIM), lambda bi, h, i: (bi, 0, h)),
                  pl.BlockSpec((1, 3, 2 * tb, tb), lambda bi, h, i: (h, 0, 0, 0)),
                  pl.BlockSpec((1, DIFF_V_DIM), lambda bi, h, i: (0, 0)),
                  pl.BlockSpec((1, DIFF_V_DIM), lambda bi, h, i: (0, 0))],
        out_specs=pl.BlockSpec((1, tb, DIFF_V_DIM), lambda bi, h, i: (bi, i, h)),
        out_shape=jax.ShapeDtypeStruct((b, l, DIFF_WIDTH), F32),
        scratch_shapes=[pltpu.VMEM((2 * tb, LANES), F32), pltpu.VMEM((2 * tb, LANES), F32),
                        pltpu.VMEM((2 * tb, DIFF_V_DIM), F32)],
        compiler_params=_params(("parallel", "parallel", "arbitrary")),
        name="attn_prompt",
    )(qb, kh, vh, bias_tiles, lam_row, subln.reshape(1, DIFF_V_DIM))


def _prompt_bias_tiles(rel_table):
    tb = ATT_BLOCK
    qi = jnp.arange(tb)[:, None]
    ki = jnp.arange(tb)[None, :]
    far = jnp.broadcast_to(_bias_lookup(rel_table, jnp.full((1, 1), -2 * tb)), (DIFF_HEADS, tb, tb))
    prev = _bias_lookup(rel_table, ki - qi - tb)
    diag = jnp.where((ki // CHUNK) <= (qi // CHUNK), _bias_lookup(rel_table, ki - qi), NEG_BIG)
    tiles = jnp.stack([far, prev, diag], axis=1)
    return jnp.concatenate([tiles, tiles], axis=2)


def _bias_lookup(rel_table, rel):
    onehot = (_rel_bucket(rel)[..., None] == jnp.arange(N_BUCKETS)).astype(F32)
    return jnp.einsum("...b,bh->h...", onehot, rel_table.astype(F32), precision=lax.Precision.HIGHEST)


def _attn_sample_kernel(q_ref, kp_ref, vp_ref, kn_ref, vn_ref, bp_ref, bn_ref, lam_ref, subln_ref, o_ref,
                        *, out_scale):
    dh = DIFF_HEAD_DIM
    q = q_ref[0] * (dh ** -0.5)
    kp = kp_ref[0].astype(BF16)
    kn = kn_ref[0].astype(BF16)
    vp = vp_ref[0].astype(BF16)
    vn = vn_ref[0].astype(BF16)
    outs = []
    for t in range(2):
        qt = q[:, t * dh:(t + 1) * dh].astype(BF16)
        sp = _dot_nt(qt, kp[:, t * dh:(t + 1) * dh]) + bp_ref[0]
        sn = _dot_nt(qt, kn[:, t * dh:(t + 1) * dh]) + bn_ref[0]
        m = jnp.maximum(jnp.max(sp, axis=-1, keepdims=True), jnp.max(sn, axis=-1, keepdims=True))
        pp = jnp.exp(sp - m)
        pn = jnp.exp(sn - m)
        den = jnp.sum(pp, axis=-1, keepdims=True) + jnp.sum(pn, axis=-1, keepdims=True)
        outs.append((_dot(pp.astype(BF16), vp) + _dot(pn.astype(BF16), vn)) / den)
    o_ref[0] = _diff_finish(outs[0], outs[1], lam_ref, subln_ref, out_scale)


def _attn_sample(qb, k_past, v_past, k_new, v_new, bias_past, bias_new, lam_row, subln, out_scale):
    b, l, _ = qb.shape
    p = k_past.shape[1]
    hw = 2 * DIFF_HEAD_DIM
    return pl.pallas_call(
        functools.partial(_attn_sample_kernel, out_scale=out_scale),
        grid=(b, DIFF_HEADS),
        in_specs=[pl.BlockSpec((1, l, hw), lambda bi, h: (bi, 0, h)),
                  pl.BlockSpec((1, p, hw), lambda bi, h: (bi, 0, h)),
                  pl.BlockSpec((1, p, DIFF_V_DIM), lambda bi, h: (bi, 0, h)),
                  pl.BlockSpec((1, l, hw), lambda bi, h: (bi, 0, h)),
                  pl.BlockSpec((1, l, DIFF_V_DIM), lambda bi, h: (bi, 0, h)),
                  pl.BlockSpec((1, l, p), lambda bi, h: (h, 0, 0)),
                  pl.BlockSpec((1, l, l), lambda bi, h: (h, 0, 0)),
                  pl.BlockSpec((1, DIFF_V_DIM), lambda bi, h: (0, 0)),
                  pl.BlockSpec((1, DIFF_V_DIM), lambda bi, h: (0, 0))],
        out_specs=pl.BlockSpec((1, l, DIFF_V_DIM), lambda bi, h: (bi, 0, h)),
        out_shape=jax.ShapeDtypeStruct((b, l, DIFF_WIDTH), F32),
        compiler_params=_params(("parallel", "parallel")),
        name="attn_sample",
    )(qb, k_past, v_past, k_new, v_new, bias_past, bias_new, lam_row, subln.reshape(1, DIFF_V_DIM))


def _sample_bias(rel_table, p, l):
    rel = jnp.arange(-(p + l - 1), l)
    by_rel = _bias_lookup(rel_table, rel)
    bias = jnp.stack([lax.slice_in_dim(by_rel, l - 1 - i, p + 2 * l - 1 - i, axis=1) for i in range(l)], axis=1)
    return bias[:, :, :p], bias[:, :, p:]


def _outproj_kernel(oa_ref, ob_ref, x_ref, mod_ref, n2_ref, wo_ref, wq_ref, keys_ref,
                    x1_ref, h2_ref, sc_ref):
    mixed = jnp.concatenate([oa_ref[0], ob_ref[0]], axis=-1).astype(BF16)
    x1 = x_ref[0] + mod_ref[0, 2:3, :] * _dot(mixed, wo_ref[...])
    x1_ref[0] = x1
    h2 = _modulated_norm(x1, n2_ref[...], mod_ref[0, 3:4, :], mod_ref[0, 4:5, :])
    h2_ref[0] = h2
    qh = _dot(h2.astype(BF16), wq_ref[...]).astype(BF16)
    for hp in range(2 * PEER_HEADS):
        sc_ref[0, hp] = _dot_nt(keys_ref[hp], qh[:, hp * PEER_HALF:(hp + 1) * PEER_HALF])


def _outproj(o_a, o_b, x, mod, norm2, w_out, w_q, keys):
    b, l, d = x.shape
    tl = min(l, 256)
    nhp = 2 * PEER_HEADS
    return pl.pallas_call(
        _outproj_kernel,
        grid=(b, l // tl),
        in_specs=[pl.BlockSpec((1, tl, GDN_WIDTH), lambda i, j: (i, j, 0)),
                  pl.BlockSpec((1, tl, DIFF_WIDTH), lambda i, j: (i, j, 0)),
                  pl.BlockSpec((1, tl, d), lambda i, j: (i, j, 0)),
                  pl.BlockSpec((1, 6, d), lambda i, j: (i, 0, 0)),
                  pl.BlockSpec((1, d), lambda i, j: (0, 0)),
                  pl.BlockSpec((d, d), lambda i, j: (0, 0)),
                  pl.BlockSpec((d, nhp * PEER_HALF), lambda i, j: (0, 0)),
                  pl.BlockSpec((nhp, N_KEYS, PEER_HALF), lambda i, j: (0, 0, 0))],
        out_specs=[pl.BlockSpec((1, tl, d), lambda i, j: (i, j, 0)),
                   pl.BlockSpec((1, tl, d), lambda i, j: (i, j, 0)),
                   pl.BlockSpec((1, nhp, N_KEYS, tl), lambda i, j: (i, 0, 0, j))],
        out_shape=[jax.ShapeDtypeStruct((b, l, d), F32),
                   jax.ShapeDtypeStruct((b, l, d), F32),
                   jax.ShapeDtypeStruct((b, nhp, N_KEYS, l), F32)],
        compiler_params=_params(("parallel", "parallel")),
        name="outproj",
    )(o_a, o_b, x, mod, norm2.reshape(1, d), w_out, w_q, keys)


def _top16_rows(s, ids, n):
    vals, idxs = [], []
    for _ in range(PEER_TOPK):
        m = jnp.max(s, axis=0, keepdims=True)
        i = jnp.min(jnp.where(s == m, ids, n), axis=0, keepdims=True)
        vals.append(m)
        idxs.append(i)
        s = jnp.where(ids == i, -jnp.inf, s)
    return jnp.concatenate(vals, axis=0), jnp.concatenate(idxs, axis=0)


def _pair_candidates(s1, s2):
    t = s1.shape[1]
    sub16 = lax.broadcasted_iota(jnp.int32, (PEER_TOPK, t), 0)
    sub8 = sub16[:8]
    cand = [s1[0:1] + s2] + [s1[a:a + 1] + s2[:8] for a in range(1, 8)] + [s1[8:] + s2[0:1]]
    pos = [sub16] + [a * PEER_TOPK + sub8 for a in range(1, 8)] + [(8 + sub8) * PEER_TOPK]
    return jnp.concatenate(cand, axis=0), jnp.concatenate(pos, axis=0)


def _pick_rows(table, sel):
    out = jnp.zeros_like(table)
    for a in range(PEER_TOPK):
        out = jnp.where(sel == a, table[a:a + 1, :], out)
    return out


def _topk_kernel(sc_ref, eidx_ref, gate_ref):
    eidx, gates = [], []
    key_ids = lax.broadcasted_iota(jnp.int32, sc_ref.shape[2:], 0)
    for h in range(PEER_HEADS):
        s1, i1 = _top16_rows(sc_ref[0, 2 * h], key_ids, N_KEYS)
        s2, i2 = _top16_rows(sc_ref[0, 2 * h + 1], key_ids, N_KEYS)
        cand, cand_pos = _pair_candidates(s1, s2)
        top_s, pos = _top16_rows(cand, cand_pos, PEER_TOPK * PEER_TOPK)
        eidx.append(_pick_rows(i1, pos // PEER_TOPK) * N_KEYS + _pick_rows(i2, pos % PEER_TOPK))
        e = jnp.exp(top_s - top_s[0:1, :])
        gates.append(e / jnp.sum(e, axis=0, keepdims=True))
    eidx_ref[...] = jnp.concatenate(eidx, axis=0).T
    gate_ref[...] = jnp.concatenate(gates, axis=0).T


def _topk(scores):
    b, nhp, nk, l = scores.shape
    tt = min(l, LANES)
    nt = l // tt
    return pl.pallas_call(
        _topk_kernel,
        grid=(b, nt),
        in_specs=[pl.BlockSpec((1, nhp, nk, tt), lambda i, j: (i, 0, 0, j))],
        out_specs=[pl.BlockSpec((tt, PEER_SLOTS), lambda i, j: (i * nt + j, 0)),
                   pl.BlockSpec((tt, PEER_SLOTS), lambda i, j: (i * nt + j, 0))],
        out_shape=[jax.ShapeDtypeStruct((b * l, PEER_SLOTS), jnp.int32),
                   jax.ShapeDtypeStruct((b * l, PEER_SLOTS), F32)],
        compiler_params=_params(("parallel", "parallel")),
        name="topk",
    )(scores)


_SC_ROWS = 32
_SC_SLOTS = 4


def _pack_kernel(t_ref, o_ref):
    half = t_ref.shape[2] // 2
    bits = lax.bitcast_convert_type(t_ref[0].astype(BF16).astype(F32), jnp.int32)
    o_ref[...] = lax.shift_right_logical(bits[:, :half], 16) | bits[:, half:]


def _pack_bf16_halves(t):
    _, v, d = t.shape
    tv = 512
    return pl.pallas_call(
        _pack_kernel,
        grid=(v // tv,),
        in_specs=[pl.BlockSpec((1, tv, d), lambda i: (0, i, 0))],
        out_specs=pl.BlockSpec((tv, d // 2), lambda i: (i, 0)),
        out_shape=jax.ShapeDtypeStruct((v, d // 2), jnp.int32),
        compiler_params=_params(("parallel",)),
        name="pack",
    )(t)


def _sc_mesh():
    info = plsc.get_sparse_core_info()
    mesh = plsc.VectorSubcoreMesh(core_axis_name="c", subcore_axis_name="s")
    return mesh, info.num_cores, info.num_cores * info.num_subcores


def _sc_gather(tab, idx):
    n = idx.shape[0]
    w = tab.shape[1]
    mesh, ncores, nw = _sc_mesh()
    per_w = n // nw
    ns, nr = _SC_SLOTS, _SC_ROWS
    ngroups = per_w // (ns * nr)
    assert n % (nw * ns * nr) == 0

    @functools.partial(
        pl.kernel, mesh=mesh, out_type=jax.ShapeDtypeStruct((n, w), tab.dtype),
        scratch_types=([pltpu.VMEM((per_w,), jnp.int32)] + [pltpu.VMEM((nr, w), tab.dtype)] * ns
                       + [pltpu.SemaphoreType.DMA] * (2 * ns)),
    )
    def k(tab_hbm, idx_hbm, out_hbm, idx_v, *scratch):
        bufs, gsem, wsem = scratch[:ns], scratch[ns:2 * ns], scratch[2 * ns:]
        base = (lax.axis_index("s") * ncores + lax.axis_index("c")) * per_w
        pltpu.sync_copy(idx_hbm.at[pl.ds(pl.multiple_of(base, nr), per_w)], idx_v)

        def gather(c, s):
            picks = idx_v.at[pl.ds(pl.multiple_of(c * nr, nr), nr)]
            return pltpu.make_async_copy(tab_hbm.at[picks], bufs[s], gsem[s])

        def write(c, s):
            span = pl.ds(pl.multiple_of(base + c * nr, nr), nr)
            return pltpu.make_async_copy(bufs[s], out_hbm.at[span], wsem[s])

        for s in range(ns):
            gather(s, s).start()

        @pl.loop(0, ngroups)
        def _(g):
            for s in range(ns):
                gather(g * ns + s, s).wait()
                write(g * ns + s, s).start()
            for s in range(ns):
                write(g * ns + s, s).wait()

                @pl.when(g < ngroups - 1)
                def _():
                    gather((g + 1) * ns + s, s).start()

    return k(tab, idx)


_ACC_ROWS = 32
_ACC_SLOTS = PEER_SLOTS // _ACC_ROWS
_ACC_TOKENS = 8


def _row_prefetch(tab_hbm, idx_v, rows, sems, nchunks):
    ns = _ACC_SLOTS

    def gather(c):
        picks = idx_v.at[pl.ds(pl.multiple_of(c * _ACC_ROWS, _ACC_ROWS), _ACC_ROWS)]
        return pltpu.make_async_copy(tab_hbm.at[picks], rows.at[c % ns], sems.at[c % ns])

    def prime():
        for c in range(ns - 1):
            gather(c).start()

    def step(c):
        @pl.when(c + ns - 1 < nchunks)
        def _():
            gather(c + ns - 1).start()

        gather(c).wait()

    return prime, step


def _sc_accumulate(tab, idx, coef):
    n = idx.shape[0]
    w = tab.shape[1]
    d = 2 * w
    lanes = 16
    ns = _ACC_SLOTS
    mesh, ncores, nw = _sc_mesh()
    ntok = n // PEER_SLOTS
    tok_w = ntok // nw
    per_w = tok_w * PEER_SLOTS
    nchunks = per_w // _ACC_ROWS
    assert ntok % (nw * _ACC_TOKENS) == 0 and w % (4 * lanes) == 0

    @functools.partial(
        pl.kernel, mesh=mesh, out_type=jax.ShapeDtypeStruct((ntok, d), F32),
        scratch_types=[pltpu.VMEM((per_w,), jnp.int32), pltpu.VMEM((per_w,), F32),
                       pltpu.VMEM((_ACC_TOKENS, d), F32), pltpu.VMEM((ns, _ACC_ROWS, w), tab.dtype),
                       pltpu.SemaphoreType.DMA((ns,))],
        compiler_params=pltpu.CompilerParams(needs_layout_passes=False),
    )
    def k(tab_hbm, idx_hbm, coef_hbm, out_hbm, idx_v, coef_v, acc_v, rows, sems):
        wid = lax.axis_index("s") * ncores + lax.axis_index("c")
        base = pl.multiple_of(wid * per_w, _ACC_ROWS)
        pltpu.sync_copy(idx_hbm.at[pl.ds(base, per_w)], idx_v)
        pltpu.sync_copy(coef_hbm.at[pl.ds(base, per_w)], coef_v)
        prime, fetch_step = _row_prefetch(tab_hbm, idx_v, rows, sems, nchunks)

        def accumulate(rows_ref, c, arow):
            nrow, ncol = 4, 4

            @pl.loop(0, _ACC_ROWS // nrow)
            def _(q):
                r0 = q * nrow
                cvec = [plsc.load_gather(coef_v, [jnp.full((lanes,), c * _ACC_ROWS + r0 + i, jnp.int32)])
                        for i in range(nrow)]
                groups = [[col0 + j * lanes for j in range(ncol)] for col0 in range(0, w, ncol * lanes)]

                def load(cols):
                    return [[rows_ref[r0 + i, pl.ds(col, lanes)] for i in range(nrow)] for col in cols]

                ahead = load(groups[0])
                for g, cols in enumerate(groups):
                    words = ahead
                    if g + 1 < len(groups):
                        ahead = load(groups[g + 1])
                    sums = []
                    for wds in words:
                        lo = [cvec[i] * lax.bitcast_convert_type(wds[i] << 16, F32) for i in range(nrow)]
                        hi = [cvec[i] * lax.bitcast_convert_type(wds[i] & jnp.int32(-65536), F32)
                              for i in range(nrow)]
                        sums.append((functools.reduce(jnp.add, lo), functools.reduce(jnp.add, hi)))
                    for col, (lo, hi) in zip(cols, sums):
                        plsc.addupdate(acc_v.at[arow, pl.ds(col, lanes)], lo)
                        plsc.addupdate(acc_v.at[arow, pl.ds(w + col, lanes)], hi)

        prime()

        @pl.loop(0, tok_w)
        def _(t):
            arow = t % _ACC_TOKENS
            for col in range(0, d, lanes):
                acc_v[arow, pl.ds(col, lanes)] = jnp.zeros((lanes,), F32)
            @pl.loop(0, ns)
            def _(q):
                c = ns * t + q
                fetch_step(c)
                accumulate(rows.at[q], c, arow)

            @pl.when(arow == _ACC_TOKENS - 1)
            def _():
                first_tok = pl.multiple_of(wid * tok_w + t - (_ACC_TOKENS - 1), _ACC_TOKENS)
                pltpu.sync_copy(acc_v, out_hbm.at[pl.ds(first_tok, _ACC_TOKENS)])

    return k(tab, idx, coef)


_DOT_TOKENS = 8


def _sc_dot(tab, idx, x):
    n = idx.shape[0]
    w = tab.shape[1]
    d = 2 * w
    lanes = 16
    ns = _ACC_SLOTS
    mesh, ncores, nw = _sc_mesh()
    ntok = n // PEER_SLOTS
    tok_w = ntok // nw
    per_w = tok_w * PEER_SLOTS
    nchunks = per_w // _ACC_ROWS
    assert ntok % (nw * _DOT_TOKENS) == 0 and _ACC_ROWS % lanes == 0

    @functools.partial(
        pl.kernel, mesh=mesh, out_type=jax.ShapeDtypeStruct((n,), F32),
        scratch_types=[pltpu.VMEM((per_w,), jnp.int32), pltpu.VMEM((per_w,), F32),
                       pltpu.VMEM((_DOT_TOKENS, d), F32), pltpu.VMEM((ns, _ACC_ROWS, w), tab.dtype),
                       pltpu.SemaphoreType.DMA((ns,))],
        compiler_params=pltpu.CompilerParams(needs_layout_passes=False),
    )
    def k(tab_hbm, idx_hbm, x_hbm, pre_hbm, idx_v, pre_v, x_v, rows, sems):
        wid = lax.axis_index("s") * ncores + lax.axis_index("c")
        base = pl.multiple_of(wid * per_w, _ACC_ROWS)
        pltpu.sync_copy(idx_hbm.at[pl.ds(base, per_w)], idx_v)
        prime, fetch_step = _row_prefetch(tab_hbm, idx_v, rows, sems, nchunks)

        def dots(rows_ref, c, xrow):
            lane_id = lax.iota(jnp.int32, lanes)
            nrow = 4

            @pl.loop(0, _ACC_ROWS // lanes)
            def _(q):
                out = jnp.zeros((lanes,), F32)
                for sub in range(lanes // nrow):
                    r0 = q * lanes + sub * nrow
                    accs = [None] * nrow
                    for col in range(0, w, lanes):
                        x_lo = x_v[xrow, pl.ds(col, lanes)]
                        x_hi = x_v[xrow, pl.ds(w + col, lanes)]
                        for i in range(nrow):
                            word = rows_ref[r0 + i, pl.ds(col, lanes)]
                            term = (lax.bitcast_convert_type(word << 16, F32) * x_lo
                                    + lax.bitcast_convert_type(word & jnp.int32(-65536), F32) * x_hi)
                            accs[i] = term if accs[i] is None else accs[i] + term
                    for i in range(nrow):
                        out = jnp.where(lane_id == sub * nrow + i, jnp.sum(accs[i]), out)
                pre_v[pl.ds(pl.multiple_of(c * _ACC_ROWS + q * lanes, lanes), lanes)] = out

        prime()

        @pl.loop(0, tok_w)
        def _(t):
            xrow = t % _DOT_TOKENS

            @pl.when(xrow == 0)
            def _():
                first_tok = pl.multiple_of(wid * tok_w + t, _DOT_TOKENS)
                pltpu.sync_copy(x_hbm.at[pl.ds(first_tok, _DOT_TOKENS)], x_v)

            @pl.loop(0, ns)
            def _(q):
                c = ns * t + q
                fetch_step(c)
                dots(rows.at[q], c, xrow)

        pltpu.sync_copy(pre_v, pre_hbm.at[pl.ds(base, per_w)])

    return k(tab, idx, x)


def _unpack_bf16_halves(w):
    lo = lax.bitcast_convert_type(w << 16, F32)
    hi = lax.bitcast_convert_type(w & jnp.int32(-65536), F32)
    return lo, hi


def _coef_kernel(pre_ref, gate_ref, coef_ref):
    pre = pre_ref[...]
    coef_ref[...] = gate_ref[...] * (0.5 * pre * (1.0 + lax.erf(pre * (2.0 ** -0.5))))


def _peer_coef(pre, gate):
    n, s = gate.shape
    tn = min(n, 1024)
    return pl.pallas_call(
        _coef_kernel,
        grid=(n // tn,),
        in_specs=[pl.BlockSpec((tn, s), lambda i: (i, 0)), pl.BlockSpec((tn, s), lambda i: (i, 0))],
        out_specs=pl.BlockSpec((tn, s), lambda i: (i, 0)),
        out_shape=jax.ShapeDtypeStruct((n, s), F32),
        compiler_params=_params(("parallel",)),
        name="peer_coef",
    )(pre, gate)


def _final_kernel(acc_ref, x1_ref, g2_ref, fn_ref, y_ref):
    x2 = x1_ref[0] + g2_ref[0, 5:6, :] * acc_ref[0]
    y_ref[0] = x2 * lax.rsqrt(jnp.mean(x2 * x2, axis=-1, keepdims=True) + EPS) * fn_ref[...]


def _peer_final(acc, x1, mod, final_norm):
    b, l, d = x1.shape
    tl = min(l, 512)
    return pl.pallas_call(
        _final_kernel,
        grid=(b, l // tl),
        in_specs=[pl.BlockSpec((1, tl, d), lambda i, j: (i, j, 0)),
                  pl.BlockSpec((1, tl, d), lambda i, j: (i, j, 0)),
                  pl.BlockSpec((1, 6, d), lambda i, j: (i, 0, 0)),
                  pl.BlockSpec((1, d), lambda i, j: (0, 0))],
        out_specs=pl.BlockSpec((1, tl, d), lambda i, j: (i, j, 0)),
        out_shape=jax.ShapeDtypeStruct((b, l, d), F32),
        compiler_params=_params(("parallel", "parallel")),
        name="peer_final",
    )(acc, x1, mod, final_norm.reshape(1, d))


def _front(x, mod, conv0, s0, k_past, v_past, wts, prompt):
    b, l, d = x.shape
    qkv, z, ab, qb, kb, vb, kh, vh = _inproj(x, mod, wts["norm1"], wts["w_in"])
    chunk = CHUNK if prompt else l
    o_a, s_new = _gdn(qkv, z, ab, conv0, s0, wts["conv_w"], wts["alog"], wts["dtb"], wts["onorm"], chunk)
    conv_new = qkv[:, l - (CONV_W - 1):, :]
    if prompt:
        o_b = _attn_prompt(qb, kh, vh, wts["bias_prompt"], wts["lam"], wts["subln"], wts["out_scale"])
    else:
        p = k_past.shape[1]
        bias_past, bias_new = _sample_bias(wts["rel_table"], p, l)
        o_b = _attn_sample(qb, k_past.reshape(b, p, DIFF_WIDTH), v_past.reshape(b, p, DIFF_WIDTH), kb, vb,
                           bias_past, bias_new, wts["lam"], wts["subln"], wts["out_scale"])
    x1, h2, scores = _outproj(o_a, o_b, x, mod, wts["norm2"], wts["w_out"], wts["w_q"], wts["keys"])
    eidx, gate = _topk(scores)
    return (eidx.reshape(-1), gate, h2, x1, mod), (kb, vb, s_new, conv_new)


def _back(pending, wts, coef_hook=None):
    eidx, gate, h2, x1, mod = pending
    b, l, d = h2.shape
    pre = _sc_dot(wts["peer_u"], eidx, h2.reshape(b * l, d))
    coef = _peer_coef(pre.reshape(b * l, PEER_SLOTS), gate)
    if coef_hook is not None:
        coef = coef_hook(coef)
    acc = _sc_accumulate(wts["peer_v"], eidx, coef.reshape(-1))
    return _peer_final(acc.reshape(b, l, d), x1, mod, wts["final_norm"])


def _prompt_rows(x, mod, wts, sample_front):
    b, l, d = x.shape
    conv0 = jnp.zeros((1, CONV_W - 1, QKV_A), F32)
    s0 = jnp.zeros((1, GDN_HEADS, GDN_HEAD_DIM, GDN_HEAD_DIM), F32)
    news, ys, sample = [], [], []
    xs = [lax.dynamic_slice_in_dim(x, t, 1, 0) for t in range(b)]

    def hook(t, coef):
        if t + 2 < b:
            coef, xs[t + 2] = lax.optimization_barrier((coef, xs[t + 2]))
        if t == 0:
            coef, result = sample_front(coef)
            sample.append(result)
        return coef

    for t in range(b):
        pend, new = _front(xs[t], lax.dynamic_slice_in_dim(mod, t, 1, 0), conv0, s0, None, None, wts, True)
        news.append(new)
        y_t = _back(pend, wts, functools.partial(hook, t))
        if t + 3 < b:
            y_t, xs[t + 3] = lax.optimization_barrier((y_t, xs[t + 3]))
        ys.append(y_t)
    y = jnp.concatenate(ys, axis=0)
    return y, [jnp.concatenate([n[k] for n in news], axis=0) for k in range(4)], sample[0]


def _cache_entries(new, b, l):
    kb, vb, s_new, conv_new = new
    return (kb.reshape(1, b, l, DIFF_HEADS, 2 * DIFF_HEAD_DIM), vb.reshape(1, b, l, DIFF_HEADS, DIFF_V_DIM),
            s_new[None], conv_new[None])


def kernel(x_prompt, x_sample, c_prompt, c_sample, cache_k, cache_v, state_gdn, state_conv, w_ada, b_ada,
           norm1, norm2, w_in, conv_w, a_log, dt_bias, gdn_onorm, lam_q1, lam_k1, lam_q2, lam_k2, diff_subln,
           w_out, peer_wq, peer_keys, peer_u, peer_v, rel_table, final_norm):
    assert w_ada.shape[0] == 1, "single-layer step"
    bp = x_prompt.shape[0]
    d = D_MODEL
    lam_init = 0.8 - 0.6 * math.exp(-0.3 * 0)
    lam = (jnp.exp(jnp.sum(lam_q1[0] * lam_k1[0])) - jnp.exp(jnp.sum(lam_q2[0] * lam_k2[0])) + lam_init)
    w = w_in[0]
    w_packed = jnp.concatenate(
        [w[:, :_C_AB], jnp.pad(w[:, 2048:2056], ((0, 0), (0, LANES - 2 * GDN_HEADS))), w[:, 2056:]],
        axis=1).astype(BF16)
    wts = dict(
        norm1=norm1[0], norm2=norm2[0], w_in=w_packed, conv_w=conv_w[0],
        alog=jnp.pad(a_log[0], (0, LANES - GDN_HEADS)).reshape(1, LANES),
        dtb=jnp.pad(dt_bias[0], (0, LANES - GDN_HEADS)).reshape(1, LANES),
        onorm=gdn_onorm[0], lam=jnp.full((1, DIFF_V_DIM), lam, F32), subln=diff_subln[0],
        out_scale=1.0 - lam_init, bias_prompt=_prompt_bias_tiles(rel_table), rel_table=rel_table,
        w_out=w_out[0].astype(BF16), w_q=peer_wq[0].astype(BF16),
        keys=peer_keys[0].reshape(2 * PEER_HEADS, N_KEYS, PEER_HALF).astype(BF16),
        peer_u=_pack_bf16_halves(peer_u), peer_v=_pack_bf16_halves(peer_v), final_norm=final_norm)

    mod = _ada(jnp.concatenate([c_prompt, c_sample], axis=0), w_ada[0], b_ada[0]).reshape(-1, 6, d)
    def sample_front(dep):
        xs, dep = lax.optimization_barrier((x_sample, dep))
        return dep, _front(xs, mod[bp:], state_conv[0], state_gdn[0], cache_k[0], cache_v[0], wts, False)

    yp, new_p, (pend_s, new_s) = _prompt_rows(x_prompt, mod[:bp], wts, sample_front)
    ys = _back(pend_s, wts)
    kp, vp, sp, cp = _cache_entries(new_p, *x_prompt.shape[:2])
    ks, vs, ss, cs = _cache_entries(new_s, *x_sample.shape[:2])
    return yp, ys, kp, vp, sp, cp, ks, vs, ss, cs
```

```python
import functools
import math

import jax
import jax.numpy as jnp
from jax import lax
from jax.experimental import pallas as pl
from jax.experimental.pallas import tpu as pltpu
from jax.experimental.pallas import tpu_sc as plsc

F32 = jnp.float32
BF16 = jnp.bfloat16
EPS = 1e-6

D_MODEL = 1024
CHUNK = 64
GDN_HEADS = 4
GDN_HEAD_DIM = 128
GDN_WIDTH = GDN_HEADS * GDN_HEAD_DIM
CONV_W = 4
QKV_A = 3 * GDN_WIDTH
DIFF_HEADS = 4
DIFF_HEAD_DIM = 64
DIFF_V_DIM = 128
DIFF_WIDTH = DIFF_HEADS * 2 * DIFF_HEAD_DIM
ATT_BLOCK = 256
N_BUCKETS = 32
REL_MAX_DIST = 128
PEER_HEADS = 8
N_KEYS = 128
PEER_HALF = 128
PEER_TOPK = 16
PEER_SLOTS = PEER_HEADS * PEER_TOPK
LANES = 128
NEG_BIG = -1e30
VMEM_LIMIT = 56 * 1024 * 1024

_C_QKV, _C_Z, _C_AB, _C_QB, _C_KB, _C_VB = 0, 1536, 2048, 2176, 2688, 3200
_C_END = 3712


def _params(sem):
    return pltpu.CompilerParams(dimension_semantics=sem, vmem_limit_bytes=VMEM_LIMIT)


def _dot(a, b, precision=None):
    return jnp.dot(a, b, preferred_element_type=F32, precision=precision)


def _dot_nt(a, b, precision=None):
    return lax.dot_general(a, b, (((1,), (1,)), ((), ())), preferred_element_type=F32, precision=precision)


def _silu(x):
    return x * jax.nn.sigmoid(x)


def _ada_kernel(c_ref, w_ref, b_ref, o_ref):
    a = _silu(c_ref[...]).astype(BF16)
    o_ref[...] = _dot(a, w_ref[...].astype(BF16)) + b_ref[...]


def _ada(c, w_ada, b_ada):
    n, d = c.shape
    cols = w_ada.shape[1]
    tn = 1024
    return pl.pallas_call(
        _ada_kernel,
        grid=(cols // tn,),
        in_specs=[pl.BlockSpec((n, d), lambda j: (0, 0)),
                  pl.BlockSpec((d, tn), lambda j: (0, j)),
                  pl.BlockSpec((1, tn), lambda j: (0, j))],
        out_specs=pl.BlockSpec((n, tn), lambda j: (0, j)),
        out_shape=jax.ShapeDtypeStruct((n, cols), F32),
        compiler_params=_params(("parallel",)),
        name="ada",
    )(c, w_ada, b_ada.reshape(1, cols))


def _modulated_norm(x, gain, shift, scale):
    y = x * lax.rsqrt(jnp.mean(x * x, axis=-1, keepdims=True) + EPS)
    return (y * gain) * (1.0 + scale) + shift


def _inproj_kernel(x_ref, mod_ref, n1_ref, w_ref, qkv_ref, z_ref, ab_ref, qb_ref, kb_ref, vb_ref, kh_ref, vh_ref):
    h = _modulated_norm(x_ref[0], n1_ref[...], mod_ref[0, 0:1, :], mod_ref[0, 1:2, :]).astype(BF16)
    qkv_ref[0] = _dot(h, w_ref[:, _C_QKV:_C_Z])
    z_ref[0] = _dot(h, w_ref[:, _C_Z:_C_AB])
    ab_ref[0] = _dot(h, w_ref[:, _C_AB:_C_QB])
    qb_ref[0] = _dot(h, w_ref[:, _C_QB:_C_KB])
    kb = _dot(h, w_ref[:, _C_KB:_C_VB])
    vb = _dot(h, w_ref[:, _C_VB:_C_END])
    kb_ref[0] = kb
    vb_ref[0] = vb
    kh_ref[0] = kb.astype(BF16)
    vh_ref[0] = vb.astype(BF16)


def _inproj(x, mod, norm1, w_packed):
    b, l, d = x.shape
    tl = min(l, 256)
    widths = (QKV_A, GDN_WIDTH, LANES, DIFF_WIDTH, DIFF_WIDTH, DIFF_WIDTH, DIFF_WIDTH, DIFF_WIDTH)
    dtypes = (F32,) * 6 + (BF16,) * 2
    return pl.pallas_call(
        _inproj_kernel,
        grid=(b, l // tl),
        in_specs=[pl.BlockSpec((1, tl, d), lambda i, j: (i, j, 0)),
                  pl.BlockSpec((1, 6, d), lambda i, j: (i, 0, 0)),
                  pl.BlockSpec((1, d), lambda i, j: (0, 0)),
                  pl.BlockSpec((d, _C_END), lambda i, j: (0, 0))],
        out_specs=[pl.BlockSpec((1, tl, w), lambda i, j: (i, j, 0)) for w in widths],
        out_shape=[jax.ShapeDtypeStruct((b, l, w), dt) for w, dt in zip(widths, dtypes)],
        compiler_params=_params(("parallel", "parallel")),
        name="inproj",
    )(x, mod, norm1.reshape(1, d), w_packed)


_HI = lax.Precision.HIGHEST


def _unit_lower_inverses(mats, n):
    r = lax.broadcasted_iota(jnp.int32, (n, n), 0)
    c = lax.broadcasted_iota(jnp.int32, (n, n), 1)
    eye = (r == c).astype(F32)
    ad = [jnp.where((r // 8) == (c // 8), a, 0.0) for a in mats]
    a2 = [_dot(m, m, _HI) for m in ad]
    a4 = [_dot(m, m, _HI) for m in a2]
    xs = [eye - m for m in ad]
    xs = [x + _dot(x, m, _HI) for x, m in zip(xs, a2)]
    xs = [x + _dot(x, m, _HI) for x, m in zip(xs, a4)]
    bs = 8
    while bs < n:
        off = ((r // (2 * bs)) == (c // (2 * bs))) & ((r // bs) != (c // bs))
        ys = [_dot(jnp.where(off, a, 0.0), x, _HI) for a, x in zip(mats, xs)]
        xs = [x - _dot(x, y, _HI) for x, y in zip(xs, ys)]
        bs *= 2
    return xs


def _mm(a, b):
    return _dot(a.astype(BF16), b.astype(BF16))


def _mm_nt(a, b):
    return _dot_nt(a.astype(BF16), b.astype(BF16))


def _gdn_kernel(qkv_ref, z_ref, ab_ref, conv0_ref, s0_ref, cw_ref, alog_ref, dtb_ref, onorm_ref,
                o_ref, s_ref, xbuf, *, chunk, nch):
    hd = GDN_HEAD_DIM
    rows = chunk * nch

    @pl.when(pl.program_id(1) == 0)
    def _():
        xbuf[5:8, :] = conv0_ref[0]
        s_ref[0] = s0_ref[0]

    x = qkv_ref[0]
    xbuf[8:8 + rows, :] = x
    y = (xbuf[5:5 + rows, :] * cw_ref[0:1, :] + xbuf[6:6 + rows, :] * cw_ref[1:2, :]
         + xbuf[7:7 + rows, :] * cw_ref[2:3, :] + x * cw_ref[3:4, :])
    xbuf[5:8, :] = x[rows - 3:rows, :]
    y = _silu(y)

    ab = ab_ref[0]
    t = ab + dtb_ref[...]
    softplus = jnp.maximum(t, 0.0) + jnp.log(1.0 + jnp.exp(-jnp.abs(t)))
    g = -jnp.exp(alog_ref[...]) * softplus
    beta = jax.nn.sigmoid(ab)

    r = lax.broadcasted_iota(jnp.int32, (chunk, chunk), 0)
    c = lax.broadcasted_iota(jnp.int32, (chunk, chunk), 1)
    lower = r >= c
    tri = lower.astype(F32)

    heads = range(GDN_HEADS)
    pairs = [(ci, h) for ci in range(nch) for h in heads]
    rows_of = {ci: slice(ci * chunk, (ci + 1) * chunk) for ci in range(nch)}
    gc = {ci: _dot(tri, g[rows_of[ci]], _HI) for ci in range(nch)}
    gc_t = {ci: gc[ci].T for ci in range(nch)}
    q, k, vb, kb, decay, egc, g_last = {}, {}, {}, {}, {}, {}, {}
    for ci, h in pairs:
        sl = rows_of[ci]
        qh = y[sl, h * hd:(h + 1) * hd]
        kh = y[sl, GDN_WIDTH + h * hd:GDN_WIDTH + (h + 1) * hd]
        p = ci, h
        q[p] = qh * lax.rsqrt(jnp.sum(qh * qh, axis=-1, keepdims=True) + EPS) * (hd ** -0.5)
        k[p] = kh * lax.rsqrt(jnp.sum(kh * kh, axis=-1, keepdims=True) + EPS)
        gcol = gc[ci][:, h:h + 1]
        bcol = beta[sl, GDN_HEADS + h:GDN_HEADS + h + 1]
        decay[p] = jnp.exp(jnp.where(lower, gcol - gc_t[ci][h:h + 1, :], NEG_BIG))
        kb[p] = k[p] * bcol
        vb[p] = y[sl, 2 * GDN_WIDTH + h * hd:2 * GDN_WIDTH + (h + 1) * hd] * bcol
        egc[p] = jnp.exp(gcol)
        g_last[p] = gcol[chunk - 1:chunk, :]
    kk = {p: _mm_nt(kb[p], k[p]) for p in pairs}
    tinv = dict(zip(pairs, _unit_lower_inverses([jnp.where(r > c, kk[p] * decay[p], 0.0) for p in pairs], chunk)))
    u_v = {p: _dot(tinv[p], vb[p], _HI) for p in pairs}
    w = {p: _dot(tinv[p], kb[p] * egc[p], _HI) for p in pairs}
    qk = {p: _mm_nt(q[p], k[p]) * decay[p] for p in pairs}
    k_dec_t = {p: (k[p] * jnp.exp(g_last[p] - gc[p[0]][:, p[1]:p[1] + 1])).T for p in pairs}

    s = [s_ref[0, h] for h in heads]
    for ci in range(nch):
        ws = [_mm(w[ci, h], s[h]) for h in heads]
        qs = [_mm(q[ci, h] * egc[ci, h], s[h]) for h in heads]
        v_new = [u_v[ci, h] - ws[h] for h in heads]
        o = [qs[h] + _mm(qk[ci, h], v_new[h]) for h in heads]
        s = [s[h] * jnp.exp(g_last[ci, h]) + _mm(k_dec_t[ci, h], v_new[h]) for h in heads]
        for h in heads:
            oh = o[h] * lax.rsqrt(jnp.mean(o[h] * o[h], axis=-1, keepdims=True) + EPS) * onorm_ref[...]
            o_ref[0, rows_of[ci], h * hd:(h + 1) * hd] = oh * _silu(z_ref[0, rows_of[ci], h * hd:(h + 1) * hd])
    for h in heads:
        s_ref[0, h] = s[h]


def _gdn(qkv, z, ab, conv0, s0, conv_w, alog_pad, dtb_pad, onorm, chunk):
    b, l, _ = qkv.shape
    hd = GDN_HEAD_DIM
    nch = 2 if l % (2 * chunk) == 0 else 1
    rows = chunk * nch
    return pl.pallas_call(
        functools.partial(_gdn_kernel, chunk=chunk, nch=nch),
        grid=(b, l // rows),
        in_specs=[pl.BlockSpec((1, rows, QKV_A), lambda i, j: (i, j, 0)),
                  pl.BlockSpec((1, rows, GDN_WIDTH), lambda i, j: (i, j, 0)),
                  pl.BlockSpec((1, rows, LANES), lambda i, j: (i, j, 0)),
                  pl.BlockSpec((1, CONV_W - 1, QKV_A), lambda i, j: (i, 0, 0)),
                  pl.BlockSpec((1, GDN_HEADS, hd, hd), lambda i, j: (i, 0, 0, 0)),
                  pl.BlockSpec((CONV_W, QKV_A), lambda i, j: (0, 0)),
                  pl.BlockSpec((1, LANES), lambda i, j: (0, 0)),
                  pl.BlockSpec((1, LANES), lambda i, j: (0, 0)),
                  pl.BlockSpec((1, hd), lambda i, j: (0, 0))],
        out_specs=[pl.BlockSpec((1, rows, GDN_WIDTH), lambda i, j: (i, j, 0)),
                   pl.BlockSpec((1, GDN_HEADS, hd, hd), lambda i, j: (i, 0, 0, 0))],
        out_shape=[jax.ShapeDtypeStruct((b, l, GDN_WIDTH), F32),
                   jax.ShapeDtypeStruct((b, GDN_HEADS, hd, hd), F32)],
        scratch_shapes=[pltpu.VMEM((8 + rows, QKV_A), F32)],
        compiler_params=_params(("parallel", "arbitrary")),
        name="gdn",
    )(qkv, z, ab, conv0, s0, conv_w, alog_pad, dtb_pad, onorm.reshape(1, hd))


def _rel_bucket(rel):
    nb = N_BUCKETS // 2
    max_exact = nb // 2
    ret = jnp.where(rel > 0, nb, 0)
    n = jnp.abs(rel)
    large = max_exact + (jnp.log(jnp.maximum(n, 1).astype(F32) / max_exact)
                         / math.log(REL_MAX_DIST / max_exact) * (nb - max_exact)).astype(jnp.int32)
    large = jnp.minimum(large, nb - 1)
    return ret + jnp.where(n < max_exact, n, large)


def _diff_finish(o1, o2, lam_ref, subln_ref, out_scale):
    o = o1 - lam_ref[...] * o2
    return o * lax.rsqrt(jnp.mean(o * o, axis=-1, keepdims=True) + EPS) * subln_ref[...] * out_scale


def _attn_prompt_kernel(q_ref, k_ref, v_ref, bias_ref, lam_ref, subln_ref, o_ref, m_ref, l_ref, acc_ref,
                        *, out_scale):
    i = pl.program_id(2)
    tb = ATT_BLOCK
    dh = DIFF_HEAD_DIM
    q = q_ref[0] * (dh ** -0.5)
    lane = lax.broadcasted_iota(jnp.int32, q.shape, 1)
    q2s = jnp.concatenate([jnp.where(lane < dh, q, 0.0), jnp.where(lane >= dh, q, 0.0)], axis=0).astype(BF16)

    def score_tiles(j, tile):
        start = pl.multiple_of(j * tb, tb)
        s = _dot_nt(q2s, k_ref[0, pl.ds(start, tb), :]) + bias_ref[0, tile]
        return [s[:, c:c + LANES] for c in range(0, tb, LANES)]

    def visible_blocks(fn, unroll):
        n_far = jnp.maximum(i - 1, 0)

        @pl.loop(0, n_far // unroll)
        def _(g):
            fn([(unroll * g + u, 0) for u in range(unroll)])

        @pl.loop((n_far // unroll) * unroll, n_far)
        def _(j):
            fn([(j, 0)])

        @pl.when(i > 0)
        def _():
            fn([(i - 1, 1), (i, 2)])

        @pl.when(i == 0)
        def _():
            fn([(i, 2)])

    m_ref[...] = jnp.full(m_ref.shape, NEG_BIG, F32)

    def track_max(blocks):
        tiles = [s for j, tile in blocks for s in score_tiles(j, tile)]
        m_ref[...] = functools.reduce(jnp.maximum, tiles, m_ref[...])

    visible_blocks(track_max, 4)
    m_ref[...] = jnp.broadcast_to(jnp.max(m_ref[...], axis=-1, keepdims=True), m_ref.shape)

    l_ref[...] = jnp.zeros(l_ref.shape, F32)
    acc_ref[...] = jnp.zeros(acc_ref.shape, F32)

    def accumulate(blocks):
        m = m_ref[...]
        l_add, acc_add = [], []
        for j, tile in blocks:
            p = [jnp.exp(s - m) for s in score_tiles(j, tile)]
            l_add.extend(p)
            start = pl.multiple_of(j * tb, tb)
            acc_add.append(_dot(jnp.concatenate(p, axis=-1).astype(BF16), v_ref[0, pl.ds(start, tb), :]))
        l_ref[...] += sum(l_add)
        acc_ref[...] += sum(acc_add)

    visible_blocks(accumulate, 4)
    o = acc_ref[...] / jnp.sum(l_ref[...], axis=-1, keepdims=True)
    o_ref[0] = _diff_finish(o[:tb], o[tb:], lam_ref, subln_ref, out_scale)


def _attn_prompt(qb, kh, vh, bias_tiles, lam_row, subln, out_scale):
    b, l, _ = qb.shape
    tb = ATT_BLOCK
    hw = 2 * DIFF_HEAD_DIM
    return pl.pallas_call(
        functools.partial(_attn_prompt_kernel, out_scale=out_scale),
        grid=(b, DIFF_HEADS, l // tb),
        in_specs=[pl.BlockSpec((1, tb, hw), lambda bi, h, i: (bi, i, h)),
                  pl.BlockSpec((1, l, hw), lambda bi, h, i: (bi, 0, h)),
                  pl.BlockSpec((1, l, DIFF_V_DIM), lambda bi, h, i: (bi, 0, h)),
                  pl.BlockSpec((1, 3, 2 * tb, tb), lambda bi, h, i: (h, 0, 0, 0)),
                  pl.BlockSpec((1, DIFF_V_DIM), lambda bi, h, i: (0, 0)),
                  pl.BlockSpec((1, DIFF_V_DIM), lambda bi, h, i: (0, 0))],
        out_specs=pl.BlockSpec((1, tb, DIFF_V_DIM), lambda bi, h, i: (bi, i, h)),
        out_shape=jax.ShapeDtypeStruct((b, l, DIFF_WIDTH), F32),
        scratch_shapes=[pltpu.VMEM((2 * tb, LANES), F32), pltpu.VMEM((2 * tb, LANES), F32),
                        pltpu.VMEM((2 * tb, DIFF_V_DIM), F32)],
        compiler_params=_params(("parallel", "parallel", "arbitrary")),
        name="attn_prompt",
    )(qb, kh, vh, bias_tiles, lam_row, subln.reshape(1, DIFF_V_DIM))


def _prompt_bias_tiles(rel_table):
    tb = ATT_BLOCK
    qi = jnp.arange(tb)[:, None]
    ki = jnp.arange(tb)[None, :]
    far = jnp.broadcast_to(_bias_lookup(rel_table, jnp.full((1, 1), -2 * tb)), (DIFF_HEADS, tb, tb))
    prev = _bias_lookup(rel_table, ki - qi - tb)
    diag = jnp.where((ki // CHUNK) <= (qi // CHUNK), _bias_lookup(rel_table, ki - qi), NEG_BIG)
    tiles = jnp.stack([far, prev, diag], axis=1)
    return jnp.concatenate([tiles, tiles], axis=2)


def _bias_lookup(rel_table, rel):
    onehot = (_rel_bucket(rel)[..., None] == jnp.arange(N_BUCKETS)).astype(F32)
    return jnp.einsum("...b,bh->h...", onehot, rel_table.astype(F32), precision=lax.Precision.HIGHEST)


def _attn_sample_kernel(q_ref, kp_ref, vp_ref, kn_ref, vn_ref, bp_ref, bn_ref, lam_ref, subln_ref, o_ref,
                        *, out_scale):
    dh = DIFF_HEAD_DIM
    q = q_ref[0] * (dh ** -0.5)
    kp = kp_ref[0].astype(BF16)
    kn = kn_ref[0].astype(BF16)
    vp = vp_ref[0].astype(BF16)
    vn = vn_ref[0].astype(BF16)
    outs = []
    for t in range(2):
        qt = q[:, t * dh:(t + 1) * dh].astype(BF16)
        sp = _dot_nt(qt, kp[:, t * dh:(t + 1) * dh]) + bp_ref[0]
        sn = _dot_nt(qt, kn[:, t * dh:(t + 1) * dh]) + bn_ref[0]
        m = jnp.maximum(jnp.max(sp, axis=-1, keepdims=True), jnp.max(sn, axis=-1, keepdims=True))
        pp = jnp.exp(sp - m)
        pn = jnp.exp(sn - m)
        den = jnp.sum(pp, axis=-1, keepdims=True) + jnp.sum(pn, axis=-1, keepdims=True)
        outs.append((_dot(pp.astype(BF16), vp) + _dot(pn.astype(BF16), vn)) / den)
    o_ref[0] = _diff_finish(outs[0], outs[1], lam_ref, subln_ref, out_scale)


def _attn_sample(qb, k_past, v_past, k_new, v_new, bias_past, bias_new, lam_row, subln, out_scale):
    b, l, _ = qb.shape
    p = k_past.shape[1]
    hw = 2 * DIFF_HEAD_DIM
    return pl.pallas_call(
        functools.partial(_attn_sample_kernel, out_scale=out_scale),
        grid=(b, DIFF_HEADS),
        in_specs=[pl.BlockSpec((1, l, hw), lambda bi, h: (bi, 0, h)),
                  pl.BlockSpec((1, p, hw), lambda bi, h: (bi, 0, h)),
                  pl.BlockSpec((1, p, DIFF_V_DIM), lambda bi, h: (bi, 0, h)),
                  pl.BlockSpec((1, l, hw), lambda bi, h: (bi, 0, h)),
                  pl.BlockSpec((1, l, DIFF_V_DIM), lambda bi, h: (bi, 0, h)),
                  pl.BlockSpec((1, l, p), lambda bi, h: (h, 0, 0)),
                  pl.BlockSpec((1, l, l), lambda bi, h: (h, 0, 0)),
                  pl.BlockSpec((1, DIFF_V_DIM), lambda bi, h: (0, 0)),
                  pl.BlockSpec((1, DIFF_V_DIM), lambda bi, h: (0, 0))],
        out_specs=pl.BlockSpec((1, l, DIFF_V_DIM), lambda bi, h: (bi, 0, h)),
        out_shape=jax.ShapeDtypeStruct((b, l, DIFF_WIDTH), F32),
        compiler_params=_params(("parallel", "parallel")),
        name="attn_sample",
    )(qb, k_past, v_past, k_new, v_new, bias_past, bias_new, lam_row, subln.reshape(1, DIFF_V_DIM))


def _sample_bias(rel_table, p, l):
    rel = jnp.arange(-(p + l - 1), l)
    by_rel = _bias_lookup(rel_table, rel)
    bias = jnp.stack([lax.slice_in_dim(by_rel, l - 1 - i, p + 2 * l - 1 - i, axis=1) for i in range(l)], axis=1)
    return bias[:, :, :p], bias[:, :, p:]


def _outproj_kernel(oa_ref, ob_ref, x_ref, mod_ref, n2_ref, wo_ref, wq_ref, keys_ref,
                    x1_ref, h2_ref, sc_ref):
    mixed = jnp.concatenate([oa_ref[0], ob_ref[0]], axis=-1).astype(BF16)
    x1 = x_ref[0] + mod_ref[0, 2:3, :] * _dot(mixed, wo_ref[...])
    x1_ref[0] = x1
    h2 = _modulated_norm(x1, n2_ref[...], mod_ref[0, 3:4, :], mod_ref[0, 4:5, :])
    h2_ref[0] = h2
    qh = _dot(h2.astype(BF16), wq_ref[...]).astype(BF16)
    for hp in range(2 * PEER_HEADS):
        sc_ref[0, hp] = _dot_nt(keys_ref[hp], qh[:, hp * PEER_HALF:(hp + 1) * PEER_HALF])


def _outproj(o_a, o_b, x, mod, norm2, w_out, w_q, keys):
    b, l, d = x.shape
    tl = min(l, 256)
    nhp = 2 * PEER_HEADS
    return pl.pallas_call(
        _outproj_kernel,
        grid=(b, l // tl),
        in_specs=[pl.BlockSpec((1, tl, GDN_WIDTH), lambda i, j: (i, j, 0)),
                  pl.BlockSpec((1, tl, DIFF_WIDTH), lambda i, j: (i, j, 0)),
                  pl.BlockSpec((1, tl, d), lambda i, j: (i, j, 0)),
                  pl.BlockSpec((1, 6, d), lambda i, j: (i, 0, 0)),
                  pl.BlockSpec((1, d), lambda i, j: (0, 0)),
                  pl.BlockSpec((d, d), lambda i, j: (0, 0)),
                  pl.BlockSpec((d, nhp * PEER_HALF), lambda i, j: (0, 0)),
                  pl.BlockSpec((nhp, N_KEYS, PEER_HALF), lambda i, j: (0, 0, 0))],
        out_specs=[pl.BlockSpec((1, tl, d), lambda i, j: (i, j, 0)),
                   pl.BlockSpec((1, tl, d), lambda i, j: (i, j, 0)),
                   pl.BlockSpec((1, nhp, N_KEYS, tl), lambda i, j: (i, 0, 0, j))],
        out_shape=[jax.ShapeDtypeStruct((b, l, d), F32),
                   jax.ShapeDtypeStruct((b, l, d), F32),
                   jax.ShapeDtypeStruct((b, nhp, N_KEYS, l), F32)],
        compiler_params=_params(("parallel", "parallel")),
        name="outproj",
    )(o_a, o_b, x, mod, norm2.reshape(1, d), w_out, w_q, keys)


def _top16_rows(s, ids, n):
    vals, idxs = [], []
    for _ in range(PEER_TOPK):
        m = jnp.max(s, axis=0, keepdims=True)
        i = jnp.min(jnp.where(s == m, ids, n), axis=0, keepdims=True)
        vals.append(m)
        idxs.append(i)
        s = jnp.where(ids == i, -jnp.inf, s)
    return jnp.concatenate(vals, axis=0), jnp.concatenate(idxs, axis=0)


def _pair_candidates(s1, s2):
    t = s1.shape[1]
    sub16 = lax.broadcasted_iota(jnp.int32, (PEER_TOPK, t), 0)
    sub8 = sub16[:8]
    cand = [s1[0:1] + s2] + [s1[a:a + 1] + s2[:8] for a in range(1, 8)] + [s1[8:] + s2[0:1]]
    pos = [sub16] + [a * PEER_TOPK + sub8 for a in range(1, 8)] + [(8 + sub8) * PEER_TOPK]
    return jnp.concatenate(cand, axis=0), jnp.concatenate(pos, axis=0)


def _pick_rows(table, sel):
    out = jnp.zeros_like(table)
    for a in range(PEER_TOPK):
        out = jnp.where(sel == a, table[a:a + 1, :], out)
    return out


def _topk_kernel(sc_ref, eidx_ref, gate_ref):
    eidx, gates = [], []
    key_ids = lax.broadcasted_iota(jnp.int32, sc_ref.shape[2:], 0)
    for h in range(PEER_HEADS):
        s1, i1 = _top16_rows(sc_ref[0, 2 * h], key_ids, N_KEYS)
        s2, i2 = _top16_rows(sc_ref[0, 2 * h + 1], key_ids, N_KEYS)
        cand, cand_pos = _pair_candidates(s1, s2)
        top_s, pos = _top16_rows(cand, cand_pos, PEER_TOPK * PEER_TOPK)
        eidx.append(_pick_rows(i1, pos // PEER_TOPK) * N_KEYS + _pick_rows(i2, pos % PEER_TOPK))
        e = jnp.exp(top_s - top_s[0:1, :])
        gates.append(e / jnp.sum(e, axis=0, keepdims=True))
    eidx_ref[...] = jnp.concatenate(eidx, axis=0).T
    gate_ref[...] = jnp.concatenate(gates, axis=0).T


def _topk(scores):
    b, nhp, nk, l = scores.shape
    tt = min(l, LANES)
    nt = l // tt
    return pl.pallas_call(
        _topk_kernel,
        grid=(b, nt),
        in_specs=[pl.BlockSpec((1, nhp, nk, tt), lambda i, j: (i, 0, 0, j))],
        out_specs=[pl.BlockSpec((tt, PEER_SLOTS), lambda i, j: (i * nt + j, 0)),
                   pl.BlockSpec((tt, PEER_SLOTS), lambda i, j: (i * nt + j, 0))],
        out_shape=[jax.ShapeDtypeStruct((b * l, PEER_SLOTS), jnp.int32),
                   jax.ShapeDtypeStruct((b * l, PEER_SLOTS), F32)],
        compiler_params=_params(("parallel", "parallel")),
        name="topk",
    )(scores)


def _pack_kernel(t_ref, o_ref):
    half = t_ref.shape[2] // 2
    bits = lax.bitcast_convert_type(t_ref[0].astype(BF16).astype(F32), jnp.int32)
    o_ref[...] = lax.shift_right_logical(bits[:, :half], 16) | bits[:, half:]


def _pack_bf16_halves(t):
    _, v, d = t.shape
    tv = 512
    return pl.pallas_call(
        _pack_kernel,
        grid=(v // tv,),
        in_specs=[pl.BlockSpec((1, tv, d), lambda i: (0, i, 0))],
        out_specs=pl.BlockSpec((tv, d // 2), lambda i: (i, 0)),
        out_shape=jax.ShapeDtypeStruct((v, d // 2), jnp.int32),
        compiler_params=_params(("parallel",)),
        name="pack",
    )(t)


def _unpack_bf16_halves(word):
    lo = lax.bitcast_convert_type(word << 16, F32)
    hi = lax.bitcast_convert_type(word & jnp.int32(-65536), F32)
    return lo, hi


def _sc_mesh():
    info = plsc.get_sparse_core_info()
    mesh = plsc.VectorSubcoreMesh(core_axis_name="c", subcore_axis_name="s")
    return mesh, info.num_cores, info.num_cores * info.num_subcores


_ACC_ROWS = 32
_ACC_SLOTS = PEER_SLOTS // _ACC_ROWS
_ACC_TOKENS = 8


def _row_prefetch(tab_hbm, idx_v, rows, sems, nchunks):
    ns = _ACC_SLOTS

    def gather(c):
        picks = idx_v.at[pl.ds(pl.multiple_of(c * _ACC_ROWS, _ACC_ROWS), _ACC_ROWS)]
        return pltpu.make_async_copy(tab_hbm.at[picks], rows.at[c % ns], sems.at[c % ns])

    def prime():
        for c in range(ns - 1):
            gather(c).start()

    def step(c):
        @pl.when(c + ns - 1 < nchunks)
        def _():
            gather(c + ns - 1).start()

        gather(c).wait()

    return prime, step


def _sc_accumulate(tab, idx, coef):
    n = idx.shape[0]
    w = tab.shape[1]
    d = 2 * w
    lanes = 16
    ns = _ACC_SLOTS
    mesh, ncores, nw = _sc_mesh()
    ntok = n // PEER_SLOTS
    tok_w = ntok // nw
    per_w = tok_w * PEER_SLOTS
    nchunks = per_w // _ACC_ROWS
    assert ntok % (nw * _ACC_TOKENS) == 0 and w % (4 * lanes) == 0

    @functools.partial(
        pl.kernel, mesh=mesh, out_type=jax.ShapeDtypeStruct((ntok, d), F32),
        scratch_types=[pltpu.VMEM((per_w,), jnp.int32), pltpu.VMEM((per_w,), F32),
                       pltpu.VMEM((_ACC_TOKENS, d), F32), pltpu.VMEM((ns, _ACC_ROWS, w), tab.dtype),
                       pltpu.SemaphoreType.DMA((ns,))],
        compiler_params=pltpu.CompilerParams(needs_layout_passes=False),
    )
    def k(tab_hbm, idx_hbm, coef_hbm, out_hbm, idx_v, coef_v, acc_v, rows, sems):
        wid = lax.axis_index("s") * ncores + lax.axis_index("c")
        base = pl.multiple_of(wid * per_w, _ACC_ROWS)
        pltpu.sync_copy(idx_hbm.at[pl.ds(base, per_w)], idx_v)
        pltpu.sync_copy(coef_hbm.at[pl.ds(base, per_w)], coef_v)
        prime, fetch_step = _row_prefetch(tab_hbm, idx_v, rows, sems, nchunks)

        def accumulate(rows_ref, c, arow):
            nrow, ncol = 4, 4

            @pl.loop(0, _ACC_ROWS // nrow)
            def _(q):
                r0 = q * nrow
                cvec = [plsc.load_gather(coef_v, [jnp.full((lanes,), c * _ACC_ROWS + r0 + i, jnp.int32)])
                        for i in range(nrow)]
                groups = [[col0 + j * lanes for j in range(ncol)] for col0 in range(0, w, ncol * lanes)]

                def load(cols):
                    return [[rows_ref[r0 + i, pl.ds(col, lanes)] for i in range(nrow)] for col in cols]

                ahead = load(groups[0])
                for g, cols in enumerate(groups):
                    words = ahead
                    if g + 1 < len(groups):
                        ahead = load(groups[g + 1])
                    sums = []
                    for wds in words:
                        halves = [_unpack_bf16_halves(wds[i]) for i in range(nrow)]
                        lo = [cvec[i] * halves[i][0] for i in range(nrow)]
                        hi = [cvec[i] * halves[i][1] for i in range(nrow)]
                        sums.append((functools.reduce(jnp.add, lo), functools.reduce(jnp.add, hi)))
                    for col, (lo, hi) in zip(cols, sums):
                        plsc.addupdate(acc_v.at[arow, pl.ds(col, lanes)], lo)
                        plsc.addupdate(acc_v.at[arow, pl.ds(w + col, lanes)], hi)

        prime()

        @pl.loop(0, tok_w)
        def _(t):
            arow = t % _ACC_TOKENS
            for col in range(0, d, lanes):
                acc_v[arow, pl.ds(col, lanes)] = jnp.zeros((lanes,), F32)
            @pl.loop(0, ns)
            def _(q):
                c = ns * t + q
                fetch_step(c)
                accumulate(rows.at[q], c, arow)

            @pl.when(arow == _ACC_TOKENS - 1)
            def _():
                first_tok = pl.multiple_of(wid * tok_w + t - (_ACC_TOKENS - 1), _ACC_TOKENS)
                pltpu.sync_copy(acc_v, out_hbm.at[pl.ds(first_tok, _ACC_TOKENS)])

    return k(tab, idx, coef)


_DOT_TOKENS = 8


def _sc_dot(tab, idx, x):
    n = idx.shape[0]
    w = tab.shape[1]
    d = 2 * w
    lanes = 16
    ns = _ACC_SLOTS
    mesh, ncores, nw = _sc_mesh()
    ntok = n // PEER_SLOTS
    tok_w = ntok // nw
    per_w = tok_w * PEER_SLOTS
    nchunks = per_w // _ACC_ROWS
    assert ntok % (nw * _DOT_TOKENS) == 0 and _ACC_ROWS % lanes == 0

    @functools.partial(
        pl.kernel, mesh=mesh, out_type=jax.ShapeDtypeStruct((n,), F32),
        scratch_types=[pltpu.VMEM((per_w,), jnp.int32), pltpu.VMEM((per_w,), F32),
                       pltpu.VMEM((_DOT_TOKENS, d), F32), pltpu.VMEM((ns, _ACC_ROWS, w), tab.dtype),
                       pltpu.SemaphoreType.DMA((ns,))],
        compiler_params=pltpu.CompilerParams(needs_layout_passes=False),
    )
    def k(tab_hbm, idx_hbm, x_hbm, pre_hbm, idx_v, pre_v, x_v, rows, sems):
        wid = lax.axis_index("s") * ncores + lax.axis_index("c")
        base = pl.multiple_of(wid * per_w, _ACC_ROWS)
        pltpu.sync_copy(idx_hbm.at[pl.ds(base, per_w)], idx_v)
        prime, fetch_step = _row_prefetch(tab_hbm, idx_v, rows, sems, nchunks)

        def dots(rows_ref, c, xrow):
            lane_id = lax.iota(jnp.int32, lanes)
            nrow = 4

            @pl.loop(0, _ACC_ROWS // lanes)
            def _(q):
                out = jnp.zeros((lanes,), F32)
                for sub in range(lanes // nrow):
                    r0 = q * lanes + sub * nrow
                    accs = [None] * nrow
                    for col in range(0, w, lanes):
                        x_lo = x_v[xrow, pl.ds(col, lanes)]
                        x_hi = x_v[xrow, pl.ds(w + col, lanes)]
                        for i in range(nrow):
                            lo, hi = _unpack_bf16_halves(rows_ref[r0 + i, pl.ds(col, lanes)])
                            term = lo * x_lo + hi * x_hi
                            accs[i] = term if accs[i] is None else accs[i] + term
                    for i in range(nrow):
                        out = jnp.where(lane_id == sub * nrow + i, jnp.sum(accs[i]), out)
                pre_v[pl.ds(pl.multiple_of(c * _ACC_ROWS + q * lanes, lanes), lanes)] = out

        prime()

        @pl.loop(0, tok_w)
        def _(t):
            xrow = t % _DOT_TOKENS

            @pl.when(xrow == 0)
            def _():
                first_tok = pl.multiple_of(wid * tok_w + t, _DOT_TOKENS)
                pltpu.sync_copy(x_hbm.at[pl.ds(first_tok, _DOT_TOKENS)], x_v)

            @pl.loop(0, ns)
            def _(q):
                c = ns * t + q
                fetch_step(c)
                dots(rows.at[q], c, xrow)

        pltpu.sync_copy(pre_v, pre_hbm.at[pl.ds(base, per_w)])

    return k(tab, idx, x)


def _coef_kernel(pre_ref, gate_ref, coef_ref):
    pre = pre_ref[...]
    coef_ref[...] = gate_ref[...] * (0.5 * pre * (1.0 + lax.erf(pre * (2.0 ** -0.5))))


def _peer_coef(pre, gate):
    n, s = gate.shape
    tn = min(n, 1024)
    return pl.pallas_call(
        _coef_kernel,
        grid=(n // tn,),
        in_specs=[pl.BlockSpec((tn, s), lambda i: (i, 0)), pl.BlockSpec((tn, s), lambda i: (i, 0))],
        out_specs=pl.BlockSpec((tn, s), lambda i: (i, 0)),
        out_shape=jax.ShapeDtypeStruct((n, s), F32),
        compiler_params=_params(("parallel",)),
        name="peer_coef",
    )(pre, gate)


def _final_kernel(acc_ref, x1_ref, g2_ref, fn_ref, y_ref):
    x2 = x1_ref[0] + g2_ref[0, 5:6, :] * acc_ref[0]
    y_ref[0] = x2 * lax.rsqrt(jnp.mean(x2 * x2, axis=-1, keepdims=True) + EPS) * fn_ref[...]


def _peer_final(acc, x1, mod, final_norm):
    b, l, d = x1.shape
    tl = min(l, 512)
    return pl.pallas_call(
        _final_kernel,
        grid=(b, l // tl),
        in_specs=[pl.BlockSpec((1, tl, d), lambda i, j: (i, j, 0)),
                  pl.BlockSpec((1, tl, d), lambda i, j: (i, j, 0)),
                  pl.BlockSpec((1, 6, d), lambda i, j: (i, 0, 0)),
                  pl.BlockSpec((1, d), lambda i, j: (0, 0))],
        out_specs=pl.BlockSpec((1, tl, d), lambda i, j: (i, j, 0)),
        out_shape=jax.ShapeDtypeStruct((b, l, d), F32),
        compiler_params=_params(("parallel", "parallel")),
        name="peer_final",
    )(acc, x1, mod, final_norm.reshape(1, d))


def _front(x, mod, conv0, s0, k_past, v_past, wts, prompt):
    b, l, d = x.shape
    qkv, z, ab, qb, kb, vb, kh, vh = _inproj(x, mod, wts["norm1"], wts["w_in"])
    chunk = CHUNK if prompt else l
    o_a, s_new = _gdn(qkv, z, ab, conv0, s0, wts["conv_w"], wts["alog"], wts["dtb"], wts["onorm"], chunk)
    conv_new = qkv[:, l - (CONV_W - 1):, :]
    if prompt:
        o_b = _attn_prompt(qb, kh, vh, wts["bias_prompt"], wts["lam"], wts["subln"], wts["out_scale"])
    else:
        p = k_past.shape[1]
        bias_past, bias_new = _sample_bias(wts["rel_table"], p, l)
        o_b = _attn_sample(qb, k_past.reshape(b, p, DIFF_WIDTH), v_past.reshape(b, p, DIFF_WIDTH), kb, vb,
                           bias_past, bias_new, wts["lam"], wts["subln"], wts["out_scale"])
    x1, h2, scores = _outproj(o_a, o_b, x, mod, wts["norm2"], wts["w_out"], wts["w_q"], wts["keys"])
    eidx, gate = _topk(scores)
    return (eidx.reshape(-1), gate, h2, x1, mod), (kb, vb, s_new, conv_new)


def _back(pending, wts, coef_hook=None):
    eidx, gate, h2, x1, mod = pending
    b, l, d = h2.shape
    pre = _sc_dot(wts["peer_u"], eidx, h2.reshape(b * l, d))
    coef = _peer_coef(pre.reshape(b * l, PEER_SLOTS), gate)
    if coef_hook is not None:
        coef = coef_hook(coef)
    acc = _sc_accumulate(wts["peer_v"], eidx, coef.reshape(-1))
    return _peer_final(acc.reshape(b, l, d), x1, mod, wts["final_norm"])


def _prompt_rows(x, mod, wts, sample_front):
    b, l, d = x.shape
    conv0 = jnp.zeros((1, CONV_W - 1, QKV_A), F32)
    s0 = jnp.zeros((1, GDN_HEADS, GDN_HEAD_DIM, GDN_HEAD_DIM), F32)
    news, ys, sample = [], [], []
    xs = [lax.dynamic_slice_in_dim(x, t, 1, 0) for t in range(b)]

    def hook(t, coef):
        if t + 2 < b:
            coef, xs[t + 2] = lax.optimization_barrier((coef, xs[t + 2]))
        if t == 0:
            coef, result = sample_front(coef)
            sample.append(result)
        return coef

    for t in range(b):
        pend, new = _front(xs[t], lax.dynamic_slice_in_dim(mod, t, 1, 0), conv0, s0, None, None, wts, True)
        news.append(new)
        y_t = _back(pend, wts, functools.partial(hook, t))
        if t + 3 < b:
            y_t, xs[t + 3] = lax.optimization_barrier((y_t, xs[t + 3]))
        ys.append(y_t)
    y = jnp.concatenate(ys, axis=0)
    return y, [jnp.concatenate([n[k] for n in news], axis=0) for k in range(4)], sample[0]


def _cache_entries(new, b, l):
    kb, vb, s_new, conv_new = new
    return (kb.reshape(1, b, l, DIFF_HEADS, 2 * DIFF_HEAD_DIM), vb.reshape(1, b, l, DIFF_HEADS, DIFF_V_DIM),
            s_new[None], conv_new[None])


def kernel(x_prompt, x_sample, c_prompt, c_sample, cache_k, cache_v, state_gdn, state_conv, w_ada, b_ada,
           norm1, norm2, w_in, conv_w, a_log, dt_bias, gdn_onorm, lam_q1, lam_k1, lam_q2, lam_k2, diff_subln,
           w_out, peer_wq, peer_keys, peer_u, peer_v, rel_table, final_norm):
    assert w_ada.shape[0] == 1, "single-layer step"
    bp = x_prompt.shape[0]
    d = D_MODEL
    lam_init = 0.8 - 0.6 * math.exp(-0.3 * 0)
    lam = (jnp.exp(jnp.sum(lam_q1[0] * lam_k1[0])) - jnp.exp(jnp.sum(lam_q2[0] * lam_k2[0])) + lam_init)
    w = w_in[0]
    w_packed = jnp.concatenate(
        [w[:, :_C_AB], jnp.pad(w[:, 2048:2056], ((0, 0), (0, LANES - 2 * GDN_HEADS))), w[:, 2056:]],
        axis=1).astype(BF16)
    wts = dict(
        norm1=norm1[0], norm2=norm2[0], w_in=w_packed, conv_w=conv_w[0],
        alog=jnp.pad(a_log[0], (0, LANES - GDN_HEADS)).reshape(1, LANES),
        dtb=jnp.pad(dt_bias[0], (0, LANES - GDN_HEADS)).reshape(1, LANES),
        onorm=gdn_onorm[0], lam=jnp.full((1, DIFF_V_DIM), lam, F32), subln=diff_subln[0],
        out_scale=1.0 - lam_init, bias_prompt=_prompt_bias_tiles(rel_table), rel_table=rel_table,
        w_out=w_out[0].astype(BF16), w_q=peer_wq[0].astype(BF16),
        keys=peer_keys[0].reshape(2 * PEER_HEADS, N_KEYS, PEER_HALF).astype(BF16),
        peer_u=_pack_bf16_halves(peer_u), peer_v=_pack_bf16_halves(peer_v), final_norm=final_norm)

    mod = _ada(jnp.concatenate([c_prompt, c_sample], axis=0), w_ada[0], b_ada[0]).reshape(-1, 6, d)
    def sample_front(dep):
        xs, dep = lax.optimization_barrier((x_sample, dep))
        return dep, _front(xs, mod[bp:], state_conv[0], state_gdn[0], cache_k[0], cache_v[0], wts, False)

    yp, new_p, (pend_s, new_s) = _prompt_rows(x_prompt, mod[:bp], wts, sample_front)
    ys = _back(pend_s, wts)
    kp, vp, sp, cp = _cache_entries(new_p, *x_prompt.shape[:2])
    ks, vs, ss, cs = _cache_entries(new_s, *x_sample.shape[:2])
    return yp, ys, kp, vp, sp, cp, ks, vs, ss, cs
```

```python
import functools
import math

import jax
import jax.numpy as jnp
from jax import lax
from jax.experimental import pallas as pl
from jax.experimental.pallas import tpu as pltpu
from jax.experimental.pallas import tpu_sc as plsc

F32 = jnp.float32
BF16 = jnp.bfloat16
EPS = 1e-6

D_MODEL = 1024
CHUNK = 64
GDN_HEADS = 4
GDN_HEAD_DIM = 128
GDN_WIDTH = GDN_HEADS * GDN_HEAD_DIM
CONV_W = 4
QKV_A = 3 * GDN_WIDTH
DIFF_HEADS = 4
DIFF_HEAD_DIM = 64
DIFF_V_DIM = 128
DIFF_WIDTH = DIFF_HEADS * 2 * DIFF_HEAD_DIM
ATT_BLOCK = 256
N_BUCKETS = 32
REL_MAX_DIST = 128
PEER_HEADS = 8
N_KEYS = 128
PEER_HALF = 128
PEER_TOPK = 16
PEER_SLOTS = PEER_HEADS * PEER_TOPK
LANES = 128
NEG_BIG = -1e30
VMEM_LIMIT = 56 * 1024 * 1024

_C_QKV, _C_Z, _C_AB, _C_QB, _C_KB, _C_VB = 0, 1536, 2048, 2176, 2688, 3200
_C_END = 3712


def _params(sem):
    return pltpu.CompilerParams(dimension_semantics=sem, vmem_limit_bytes=VMEM_LIMIT)


def _dot(a, b, precision=None):
    return jnp.dot(a, b, preferred_element_type=F32, precision=precision)


def _dot_nt(a, b, precision=None):
    return lax.dot_general(a, b, (((1,), (1,)), ((), ())), preferred_element_type=F32, precision=precision)


def _silu(x):
    return x * jax.nn.sigmoid(x)


def _ada_kernel(c_ref, w_ref, b_ref, o_ref):
    a = _silu(c_ref[...]).astype(BF16)
    o_ref[...] = _dot(a, w_ref[...].astype(BF16)) + b_ref[...]


def _ada(c, w_ada, b_ada):
    n, d = c.shape
    cols = w_ada.shape[1]
    tn = 1024
    return pl.pallas_call(
        _ada_kernel,
        grid=(cols // tn,),
        in_specs=[pl.BlockSpec((n, d), lambda j: (0, 0)),
                  pl.BlockSpec((d, tn), lambda j: (0, j)),
                  pl.BlockSpec((1, tn), lambda j: (0, j))],
        out_specs=pl.BlockSpec((n, tn), lambda j: (0, j)),
        out_shape=jax.ShapeDtypeStruct((n, cols), F32),
        compiler_params=_params(("parallel",)),
        name="ada",
    )(c, w_ada, b_ada.reshape(1, cols))


def _modulated_norm(x, gain, shift, scale):
    y = x * lax.rsqrt(jnp.mean(x * x, axis=-1, keepdims=True) + EPS)
    return (y * gain) * (1.0 + scale) + shift


def _inproj_kernel(x_ref, mod_ref, n1_ref, w_ref, qkv_ref, z_ref, ab_ref, qb_ref, kb_ref, vb_ref, kh_ref, vh_ref):
    h = _modulated_norm(x_ref[0], n1_ref[...], mod_ref[0, 0:1, :], mod_ref[0, 1:2, :]).astype(BF16)
    qkv_ref[0] = _dot(h, w_ref[:, _C_QKV:_C_Z])
    z_ref[0] = _dot(h, w_ref[:, _C_Z:_C_AB])
    ab_ref[0] = _dot(h, w_ref[:, _C_AB:_C_QB])
    qb_ref[0] = _dot(h, w_ref[:, _C_QB:_C_KB])
    kb = _dot(h, w_ref[:, _C_KB:_C_VB])
    vb = _dot(h, w_ref[:, _C_VB:_C_END])
    kb_ref[0] = kb
    vb_ref[0] = vb
    kh_ref[0] = kb.astype(BF16)
    vh_ref[0] = vb.astype(BF16)


def _inproj(x, mod, norm1, w_packed):
    b, l, d = x.shape
    tl = min(l, 256)
    widths = (QKV_A, GDN_WIDTH, LANES, DIFF_WIDTH, DIFF_WIDTH, DIFF_WIDTH, DIFF_WIDTH, DIFF_WIDTH)
    dtypes = (F32,) * 6 + (BF16,) * 2
    return pl.pallas_call(
        _inproj_kernel,
        grid=(b, l // tl),
        in_specs=[pl.BlockSpec((1, tl, d), lambda i, j: (i, j, 0)),
                  pl.BlockSpec((1, 6, d), lambda i, j: (i, 0, 0)),
                  pl.BlockSpec((1, d), lambda i, j: (0, 0)),
                  pl.BlockSpec((d, _C_END), lambda i, j: (0, 0))],
        out_specs=[pl.BlockSpec((1, tl, w), lambda i, j: (i, j, 0)) for w in widths],
        out_shape=[jax.ShapeDtypeStruct((b, l, w), dt) for w, dt in zip(widths, dtypes)],
        compiler_params=_params(("parallel", "parallel")),
        name="inproj",
    )(x, mod, norm1.reshape(1, d), w_packed)


_HI = lax.Precision.HIGHEST


def _unit_lower_inverses(mats, n):
    r = lax.broadcasted_iota(jnp.int32, (n, n), 0)
    c = lax.broadcasted_iota(jnp.int32, (n, n), 1)
    eye = (r == c).astype(F32)
    ad = [jnp.where((r // 8) == (c // 8), a, 0.0) for a in mats]
    a2 = [_dot(m, m, _HI) for m in ad]
    a4 = [_dot(m, m, _HI) for m in a2]
    xs = [eye - m for m in ad]
    xs = [x + _dot(x, m, _HI) for x, m in zip(xs, a2)]
    xs = [x + _dot(x, m, _HI) for x, m in zip(xs, a4)]
    bs = 8
    while bs < n:
        off = ((r // (2 * bs)) == (c // (2 * bs))) & ((r // bs) != (c // bs))
        ys = [_dot(jnp.where(off, a, 0.0), x, _HI) for a, x in zip(mats, xs)]
        xs = [x - _dot(x, y, _HI) for x, y in zip(xs, ys)]
        bs *= 2
    return xs


def _mm(a, b):
    return _dot(a.astype(BF16), b.astype(BF16))


def _mm_nt(a, b):
    return _dot_nt(a.astype(BF16), b.astype(BF16))


def _gdn_kernel(qkv_ref, z_ref, ab_ref, conv0_ref, s0_ref, cw_ref, alog_ref, dtb_ref, onorm_ref,
                o_ref, s_ref, xbuf, *, chunk, nch):
    hd = GDN_HEAD_DIM
    rows = chunk * nch

    @pl.when(pl.program_id(1) == 0)
    def _():
        xbuf[5:8, :] = conv0_ref[0]
        s_ref[0] = s0_ref[0]

    x = qkv_ref[0]
    xbuf[8:8 + rows, :] = x
    y = (xbuf[5:5 + rows, :] * cw_ref[0:1, :] + xbuf[6:6 + rows, :] * cw_ref[1:2, :]
         + xbuf[7:7 + rows, :] * cw_ref[2:3, :] + x * cw_ref[3:4, :])
    xbuf[5:8, :] = x[rows - 3:rows, :]
    y = _silu(y)

    ab = ab_ref[0]
    t = ab + dtb_ref[...]
    softplus = jnp.maximum(t, 0.0) + jnp.log(1.0 + jnp.exp(-jnp.abs(t)))
    g = -jnp.exp(alog_ref[...]) * softplus
    beta = jax.nn.sigmoid(ab)

    r = lax.broadcasted_iota(jnp.int32, (chunk, chunk), 0)
    c = lax.broadcasted_iota(jnp.int32, (chunk, chunk), 1)
    lower = r >= c
    tri = lower.astype(F32)

    heads = range(GDN_HEADS)
    pairs = [(ci, h) for ci in range(nch) for h in heads]
    rows_of = {ci: slice(ci * chunk, (ci + 1) * chunk) for ci in range(nch)}
    gc = {ci: _dot(tri, g[rows_of[ci]], _HI) for ci in range(nch)}
    gc_t = {ci: gc[ci].T for ci in range(nch)}
    q, k, vb, kb, decay, egc, g_last = {}, {}, {}, {}, {}, {}, {}
    for ci, h in pairs:
        sl = rows_of[ci]
        qh = y[sl, h * hd:(h + 1) * hd]
        kh = y[sl, GDN_WIDTH + h * hd:GDN_WIDTH + (h + 1) * hd]
        p = ci, h
        q[p] = qh * lax.rsqrt(jnp.sum(qh * qh, axis=-1, keepdims=True) + EPS) * (hd ** -0.5)
        k[p] = kh * lax.rsqrt(jnp.sum(kh * kh, axis=-1, keepdims=True) + EPS)
        gcol = gc[ci][:, h:h + 1]
        bcol = beta[sl, GDN_HEADS + h:GDN_HEADS + h + 1]
        decay[p] = jnp.exp(jnp.where(lower, gcol - gc_t[ci][h:h + 1, :], NEG_BIG))
        kb[p] = k[p] * bcol
        vb[p] = y[sl, 2 * GDN_WIDTH + h * hd:2 * GDN_WIDTH + (h + 1) * hd] * bcol
        egc[p] = jnp.exp(gcol)
        g_last[p] = gcol[chunk - 1:chunk, :]
    kk = {p: _mm_nt(kb[p], k[p]) for p in pairs}
    tinv = dict(zip(pairs, _unit_lower_inverses([jnp.where(r > c, kk[p] * decay[p], 0.0) for p in pairs], chunk)))
    u_v = {p: _dot(tinv[p], vb[p], _HI) for p in pairs}
    w = {p: _dot(tinv[p], kb[p] * egc[p], _HI) for p in pairs}
    qk = {p: _mm_nt(q[p], k[p]) * decay[p] for p in pairs}
    k_dec_t = {p: (k[p] * jnp.exp(g_last[p] - gc[p[0]][:, p[1]:p[1] + 1])).T for p in pairs}

    s = [s_ref[0, h] for h in heads]
    for ci in range(nch):
        ws = [_mm(w[ci, h], s[h]) for h in heads]
        qs = [_mm(q[ci, h] * egc[ci, h], s[h]) for h in heads]
        v_new = [u_v[ci, h] - ws[h] for h in heads]
        o = [qs[h] + _mm(qk[ci, h], v_new[h]) for h in heads]
        s = [s[h] * jnp.exp(g_last[ci, h]) + _mm(k_dec_t[ci, h], v_new[h]) for h in heads]
        for h in heads:
            oh = o[h] * lax.rsqrt(jnp.mean(o[h] * o[h], axis=-1, keepdims=True) + EPS) * onorm_ref[...]
            o_ref[0, rows_of[ci], h * hd:(h + 1) * hd] = oh * _silu(z_ref[0, rows_of[ci], h * hd:(h + 1) * hd])
    for h in heads:
        s_ref[0, h] = s[h]


def _gdn(qkv, z, ab, conv0, s0, conv_w, alog_pad, dtb_pad, onorm, chunk):
    b, l, _ = qkv.shape
    hd = GDN_HEAD_DIM
    nch = 2 if l % (2 * chunk) == 0 else 1
    rows = chunk * nch
    return pl.pallas_call(
        functools.partial(_gdn_kernel, chunk=chunk, nch=nch),
        grid=(b, l // rows),
        in_specs=[pl.BlockSpec((1, rows, QKV_A), lambda i, j: (i, j, 0)),
                  pl.BlockSpec((1, rows, GDN_WIDTH), lambda i, j: (i, j, 0)),
                  pl.BlockSpec((1, rows, LANES), lambda i, j: (i, j, 0)),
                  pl.BlockSpec((1, CONV_W - 1, QKV_A), lambda i, j: (i, 0, 0)),
                  pl.BlockSpec((1, GDN_HEADS, hd, hd), lambda i, j: (i, 0, 0, 0)),
                  pl.BlockSpec((CONV_W, QKV_A), lambda i, j: (0, 0)),
                  pl.BlockSpec((1, LANES), lambda i, j: (0, 0)),
                  pl.BlockSpec((1, LANES), lambda i, j: (0, 0)),
                  pl.BlockSpec((1, hd), lambda i, j: (0, 0))],
        out_specs=[pl.BlockSpec((1, rows, GDN_WIDTH), lambda i, j: (i, j, 0)),
                   pl.BlockSpec((1, GDN_HEADS, hd, hd), lambda i, j: (i, 0, 0, 0))],
        out_shape=[jax.ShapeDtypeStruct((b, l, GDN_WIDTH), F32),
                   jax.ShapeDtypeStruct((b, GDN_HEADS, hd, hd), F32)],
        scratch_shapes=[pltpu.VMEM((8 + rows, QKV_A), F32)],
        compiler_params=_params(("parallel", "arbitrary")),
        name="gdn",
    )(qkv, z, ab, conv0, s0, conv_w, alog_pad, dtb_pad, onorm.reshape(1, hd))


def _rel_bucket(rel):
    nb = N_BUCKETS // 2
    max_exact = nb // 2
    ret = jnp.where(rel > 0, nb, 0)
    n = jnp.abs(rel)
    large = max_exact + (jnp.log(jnp.maximum(n, 1).astype(F32) / max_exact)
                         / math.log(REL_MAX_DIST / max_exact) * (nb - max_exact)).astype(jnp.int32)
    large = jnp.minimum(large, nb - 1)
    return ret + jnp.where(n < max_exact, n, large)


def _diff_finish(o1, o2, lam_ref, subln_ref, out_scale):
    o = o1 - lam_ref[...] * o2
    return o * lax.rsqrt(jnp.mean(o * o, axis=-1, keepdims=True) + EPS) * subln_ref[...] * out_scale


def _attn_prompt_kernel(q_ref, k_ref, v_ref, bias_ref, lam_ref, subln_ref, o_ref, m_ref, l_ref, acc_ref,
                        *, out_scale):
    i = pl.program_id(2)
    tb = ATT_BLOCK
    dh = DIFF_HEAD_DIM
    q = q_ref[0] * (dh ** -0.5)
    lane = lax.broadcasted_iota(jnp.int32, q.shape, 1)
    q2s = jnp.concatenate([jnp.where(lane < dh, q, 0.0), jnp.where(lane >= dh, q, 0.0)], axis=0).astype(BF16)

    def score_tiles(j, tile):
        start = pl.multiple_of(j * tb, tb)
        s = _dot_nt(q2s, k_ref[0, pl.ds(start, tb), :]) + bias_ref[0, tile]
        return [s[:, c:c + LANES] for c in range(0, tb, LANES)]

    def visible_blocks(fn, unroll):
        n_far = jnp.maximum(i - 1, 0)

        @pl.loop(0, n_far // unroll)
        def _(g):
            fn([(unroll * g + u, 0) for u in range(unroll)])

        @pl.loop((n_far // unroll) * unroll, n_far)
        def _(j):
            fn([(j, 0)])

        @pl.when(i > 0)
        def _():
            fn([(i - 1, 1), (i, 2)])

        @pl.when(i == 0)
        def _():
            fn([(i, 2)])

    m_ref[...] = jnp.full(m_ref.shape, NEG_BIG, F32)

    def track_max(blocks):
        tiles = [s for j, tile in blocks for s in score_tiles(j, tile)]
        m_ref[...] = functools.reduce(jnp.maximum, tiles, m_ref[...])

    visible_blocks(track_max, 4)
    m_ref[...] = jnp.broadcast_to(jnp.max(m_ref[...], axis=-1, keepdims=True), m_ref.shape)

    l_ref[...] = jnp.zeros(l_ref.shape, F32)
    acc_ref[...] = jnp.zeros(acc_ref.shape, F32)

    def accumulate(blocks):
        m = m_ref[...]
        l_add, acc_add = [], []
        for j, tile in blocks:
            p = [jnp.exp(s - m) for s in score_tiles(j, tile)]
            l_add.extend(p)
            start = pl.multiple_of(j * tb, tb)
            acc_add.append(_dot(jnp.concatenate(p, axis=-1).astype(BF16), v_ref[0, pl.ds(start, tb), :]))
        l_ref[...] += sum(l_add)
        acc_ref[...] += sum(acc_add)

    visible_blocks(accumulate, 4)
    o = acc_ref[...] / jnp.sum(l_ref[...], axis=-1, keepdims=True)
    o_ref[0] = _diff_finish(o[:tb], o[tb:], lam_ref, subln_ref, out_scale)


def _attn_prompt(qb, kh, vh, bias_tiles, lam_row, subln, out_scale):
    b, l, _ = qb.shape
    tb = ATT_BLOCK
    hw = 2 * DIFF_HEAD_DIM
    return pl.pallas_call(
        functools.partial(_attn_prompt_kernel, out_scale=out_scale),
        grid=(b, DIFF_HEADS, l // tb),
        in_specs=[pl.BlockSpec((1, tb, hw), lambda bi, h, i: (bi, i, h)),
                  pl.BlockSpec((1, l, hw), lambda bi, h, i: (bi, 0, h)),
                  pl.BlockSpec((1, l, DIFF_V_DIM), lambda bi, h, i: (bi, 0, h)),
                  pl.BlockSpec((1, 3, 2 * tb, tb), lambda bi, h, i: (h, 0, 0, 0)),
                  pl.BlockSpec((1, DIFF_V_DIM), lambda bi, h, i: (0, 0)),
                  pl.BlockSpec((1, DIFF_V_DIM), lambda bi, h, i: (0, 0))],
        out_specs=pl.BlockSpec((1, tb, DIFF_V_DIM), lambda bi, h, i: (bi, i, h)),
        out_shape=jax.ShapeDtypeStruct((b, l, DIFF_WIDTH), F32),
        scratch_shapes=[pltpu.VMEM((2 * tb, LANES), F32), pltpu.VMEM((2 * tb, LANES), F32),
                        pltpu.VMEM((2 * tb, DIFF_V_DIM), F32)],
        compiler_params=_params(("parallel", "parallel", "arbitrary")),
        name="attn_prompt",
    )(qb, kh, vh, bias_tiles, lam_row, subln.reshape(1, DIFF_V_DIM))


def _prompt_bias_tiles(rel_table):
    tb = ATT_BLOCK
    qi = jnp.arange(tb)[:, None]
    ki = jnp.arange(tb)[None, :]
    far = jnp.broadcast_to(_bias_lookup(rel_table, jnp.full((1, 1), -2 * tb)), (DIFF_HEADS, tb, tb))
    prev = _bias_lookup(rel_table, ki - qi - tb)
    diag = jnp.where((ki // CHUNK) <= (qi // CHUNK), _bias_lookup(rel_table, ki - qi), NEG_BIG)
    tiles = jnp.stack([far, prev, diag], axis=1)
    return jnp.concatenate([tiles, tiles], axis=2)


def _bias_lookup(rel_table, rel):
    onehot = (_rel_bucket(rel)[..., None] == jnp.arange(N_BUCKETS)).astype(F32)
    return jnp.einsum("...b,bh->h...", onehot, rel_table.astype(F32), precision=lax.Precision.HIGHEST)


def _attn_sample_kernel(q_ref, kp_ref, vp_ref, kn_ref, vn_ref, bp_ref, bn_ref, lam_ref, subln_ref, o_ref,
                        *, out_scale):
    dh = DIFF_HEAD_DIM
    q = q_ref[0] * (dh ** -0.5)
    kp = kp_ref[0].astype(BF16)
    kn = kn_ref[0].astype(BF16)
    vp = vp_ref[0].astype(BF16)
    vn = vn_ref[0].astype(BF16)
    outs = []
    for t in range(2):
        qt = q[:, t * dh:(t + 1) * dh].astype(BF16)
        sp = _dot_nt(qt, kp[:, t * dh:(t + 1) * dh]) + bp_ref[0]
        sn = _dot_nt(qt, kn[:, t * dh:(t + 1) * dh]) + bn_ref[0]
        m = jnp.maximum(jnp.max(sp, axis=-1, keepdims=True), jnp.max(sn, axis=-1, keepdims=True))
        pp = jnp.exp(sp - m)
        pn = jnp.exp(sn - m)
        den = jnp.sum(pp, axis=-1, keepdims=True) + jnp.sum(pn, axis=-1, keepdims=True)
        outs.append((_dot(pp.astype(BF16), vp) + _dot(pn.astype(BF16), vn)) / den)
    o_ref[0] = _diff_finish(outs[0], outs[1], lam_ref, subln_ref, out_scale)


def _attn_sample(qb, k_past, v_past, k_new, v_new, bias_past, bias_new, lam_row, subln, out_scale):
    b, l, _ = qb.shape
    p = k_past.shape[1]
    hw = 2 * DIFF_HEAD_DIM
    return pl.pallas_call(
        functools.partial(_attn_sample_kernel, out_scale=out_scale),
        grid=(b, DIFF_HEADS),
        in_specs=[pl.BlockSpec((1, l, hw), lambda bi, h: (bi, 0, h)),
                  pl.BlockSpec((1, p, hw), lambda bi, h: (bi, 0, h)),
                  pl.BlockSpec((1, p, DIFF_V_DIM), lambda bi, h: (bi, 0, h)),
                  pl.BlockSpec((1, l, hw), lambda bi, h: (bi, 0, h)),
                  pl.BlockSpec((1, l, DIFF_V_DIM), lambda bi, h: (bi, 0, h)),
                  pl.BlockSpec((1, l, p), lambda bi, h: (h, 0, 0)),
                  pl.BlockSpec((1, l, l), lambda bi, h: (h, 0, 0)),
                  pl.BlockSpec((1, DIFF_V_DIM), lambda bi, h: (0, 0)),
                  pl.BlockSpec((1, DIFF_V_DIM), lambda bi, h: (0, 0))],
        out_specs=pl.BlockSpec((1, l, DIFF_V_DIM), lambda bi, h: (bi, 0, h)),
        out_shape=jax.ShapeDtypeStruct((b, l, DIFF_WIDTH), F32),
        compiler_params=_params(("parallel", "parallel")),
        name="attn_sample",
    )(qb, k_past, v_past, k_new, v_new, bias_past, bias_new, lam_row, subln.reshape(1, DIFF_V_DIM))


def _sample_bias(rel_table, p, l):
    rel = jnp.arange(-(p + l - 1), l)
    by_rel = _bias_lookup(rel_table, rel)
    bias = jnp.stack([lax.slice_in_dim(by_rel, l - 1 - i, p + 2 * l - 1 - i, axis=1) for i in range(l)], axis=1)
    return bias[:, :, :p], bias[:, :, p:]


def _outproj_kernel(oa_ref, ob_ref, x_ref, mod_ref, n2_ref, wo_ref, wq_ref, keys_ref,
                    x1_ref, h2_ref, sc_ref):
    mixed = jnp.concatenate([oa_ref[0], ob_ref[0]], axis=-1).astype(BF16)
    x1 = x_ref[0] + mod_ref[0, 2:3, :] * _dot(mixed, wo_ref[...])
    x1_ref[0] = x1
    h2 = _modulated_norm(x1, n2_ref[...], mod_ref[0, 3:4, :], mod_ref[0, 4:5, :])
    h2_ref[0] = h2
    qh = _dot(h2.astype(BF16), wq_ref[...]).astype(BF16)
    for hp in range(2 * PEER_HEADS):
        sc_ref[0, hp] = _dot_nt(keys_ref[hp], qh[:, hp * PEER_HALF:(hp + 1) * PEER_HALF])


def _outproj(o_a, o_b, x, mod, norm2, w_out, w_q, keys):
    b, l, d = x.shape
    tl = min(l, 256)
    nhp = 2 * PEER_HEADS
    return pl.pallas_call(
        _outproj_kernel,
        grid=(b, l // tl),
        in_specs=[pl.BlockSpec((1, tl, GDN_WIDTH), lambda i, j: (i, j, 0)),
                  pl.BlockSpec((1, tl, DIFF_WIDTH), lambda i, j: (i, j, 0)),
                  pl.BlockSpec((1, tl, d), lambda i, j: (i, j, 0)),
                  pl.BlockSpec((1, 6, d), lambda i, j: (i, 0, 0)),
                  pl.BlockSpec((1, d), lambda i, j: (0, 0)),
                  pl.BlockSpec((d, d), lambda i, j: (0, 0)),
                  pl.BlockSpec((d, nhp * PEER_HALF), lambda i, j: (0, 0)),
                  pl.BlockSpec((nhp, N_KEYS, PEER_HALF), lambda i, j: (0, 0, 0))],
        out_specs=[pl.BlockSpec((1, tl, d), lambda i, j: (i, j, 0)),
                   pl.BlockSpec((1, tl, d), lambda i, j: (i, j, 0)),
                   pl.BlockSpec((1, nhp, N_KEYS, tl), lambda i, j: (i, 0, 0, j))],
        out_shape=[jax.ShapeDtypeStruct((b, l, d), F32),
                   jax.ShapeDtypeStruct((b, l, d), F32),
                   jax.ShapeDtypeStruct((b, nhp, N_KEYS, l), F32)],
        compiler_params=_params(("parallel", "parallel")),
        name="outproj",
    )(o_a, o_b, x, mod, norm2.reshape(1, d), w_out, w_q, keys)


def _top16_rows(s, ids, n):
    vals, idxs = [], []
    for _ in range(PEER_TOPK):
        m = jnp.max(s, axis=0, keepdims=True)
        i = jnp.min(jnp.where(s == m, ids, n), axis=0, keepdims=True)
        vals.append(m)
        idxs.append(i)
        s = jnp.where(ids == i, -jnp.inf, s)
    return jnp.concatenate(vals, axis=0), jnp.concatenate(idxs, axis=0)


def _pair_candidates(s1, s2):
    t = s1.shape[1]
    sub16 = lax.broadcasted_iota(jnp.int32, (PEER_TOPK, t), 0)
    sub8 = sub16[:8]
    cand = [s1[0:1] + s2] + [s1[a:a + 1] + s2[:8] for a in range(1, 8)] + [s1[8:] + s2[0:1]]
    pos = [sub16] + [a * PEER_TOPK + sub8 for a in range(1, 8)] + [(8 + sub8) * PEER_TOPK]
    return jnp.concatenate(cand, axis=0), jnp.concatenate(pos, axis=0)


def _pick_rows(table, sel):
    out = jnp.zeros_like(table)
    for a in range(PEER_TOPK):
        out = jnp.where(sel == a, table[a:a + 1, :], out)
    return out


def _topk_kernel(sc_ref, eidx_ref, gate_ref):
    eidx, gates = [], []
    key_ids = lax.broadcasted_iota(jnp.int32, sc_ref.shape[2:], 0)
    for h in range(PEER_HEADS):
        s1, i1 = _top16_rows(sc_ref[0, 2 * h], key_ids, N_KEYS)
        s2, i2 = _top16_rows(sc_ref[0, 2 * h + 1], key_ids, N_KEYS)
        cand, cand_pos = _pair_candidates(s1, s2)
        top_s, pos = _top16_rows(cand, cand_pos, PEER_TOPK * PEER_TOPK)
        eidx.append(_pick_rows(i1, pos // PEER_TOPK) * N_KEYS + _pick_rows(i2, pos % PEER_TOPK))
        e = jnp.exp(top_s - top_s[0:1, :])
        gates.append(e / jnp.sum(e, axis=0, keepdims=True))
    eidx_ref[...] = jnp.concatenate(eidx, axis=0).T
    gate_ref[...] = jnp.concatenate(gates, axis=0).T


def _topk(scores):
    b, nhp, nk, l = scores.shape
    tt = min(l, LANES)
    nt = l // tt
    return pl.pallas_call(
        _topk_kernel,
        grid=(b, nt),
        in_specs=[pl.BlockSpec((1, nhp, nk, tt), lambda i, j: (i, 0, 0, j))],
        out_specs=[pl.BlockSpec((tt, PEER_SLOTS), lambda i, j: (i * nt + j, 0)),
                   pl.BlockSpec((tt, PEER_SLOTS), lambda i, j: (i * nt + j, 0))],
        out_shape=[jax.ShapeDtypeStruct((b * l, PEER_SLOTS), jnp.int32),
                   jax.ShapeDtypeStruct((b * l, PEER_SLOTS), F32)],
        compiler_params=_params(("parallel", "parallel")),
        name="topk",
    )(scores)


def _pack_kernel(t_ref, o_ref):
    half = t_ref.shape[2] // 2
    bits = lax.bitcast_convert_type(t_ref[0].astype(BF16).astype(F32), jnp.int32)
    o_ref[...] = lax.shift_right_logical(bits[:, :half], 16) | bits[:, half:]


def _pack_bf16_halves(t):
    _, v, d = t.shape
    tv = 512
    return pl.pallas_call(
        _pack_kernel,
        grid=(v // tv,),
        in_specs=[pl.BlockSpec((1, tv, d), lambda i: (0, i, 0))],
        out_specs=pl.BlockSpec((tv, d // 2), lambda i: (i, 0)),
        out_shape=jax.ShapeDtypeStruct((v, d // 2), jnp.int32),
        compiler_params=_params(("parallel",)),
        name="pack",
    )(t)


def _unpack_bf16_halves(word):
    lo = lax.bitcast_convert_type(word << 16, F32)
    hi = lax.bitcast_convert_type(word & jnp.int32(-65536), F32)
    return lo, hi


def _sc_mesh():
    info = plsc.get_sparse_core_info()
    mesh = plsc.VectorSubcoreMesh(core_axis_name="c", subcore_axis_name="s")
    return mesh, info.num_cores, info.num_cores * info.num_subcores


_ACC_ROWS = 32
_ACC_SLOTS = PEER_SLOTS // _ACC_ROWS
_ACC_TOKENS = 8


def _row_prefetch(tab_hbm, idx_v, rows, sems, nchunks):
    ns = _ACC_SLOTS

    def gather(c):
        picks = idx_v.at[pl.ds(pl.multiple_of(c * _ACC_ROWS, _ACC_ROWS), _ACC_ROWS)]
        return pltpu.make_async_copy(tab_hbm.at[picks], rows.at[c % ns], sems.at[c % ns])

    def prime():
        for c in range(ns - 1):
            gather(c).start()

    def step(c):
        @pl.when(c + ns - 1 < nchunks)
        def _():
            gather(c + ns - 1).start()

        gather(c).wait()

    return prime, step


def _sc_accumulate(tab, idx, coef):
    n = idx.shape[0]
    w = tab.shape[1]
    d = 2 * w
    lanes = 16
    ns = _ACC_SLOTS
    mesh, ncores, nw = _sc_mesh()
    ntok = n // PEER_SLOTS
    tok_w = ntok // nw
    per_w = tok_w * PEER_SLOTS
    nchunks = per_w // _ACC_ROWS
    assert ntok % (nw * _ACC_TOKENS) == 0 and w % (4 * lanes) == 0

    @functools.partial(
        pl.kernel, mesh=mesh, out_type=jax.ShapeDtypeStruct((ntok, d), F32),
        scratch_types=[pltpu.VMEM((per_w,), jnp.int32), pltpu.VMEM((per_w,), F32),
                       pltpu.VMEM((_ACC_TOKENS, d), F32), pltpu.VMEM((ns, _ACC_ROWS, w), tab.dtype),
                       pltpu.SemaphoreType.DMA((ns,))],
        compiler_params=pltpu.CompilerParams(needs_layout_passes=False),
    )
    def k(tab_hbm, idx_hbm, coef_hbm, out_hbm, idx_v, coef_v, acc_v, rows, sems):
        wid = lax.axis_index("s") * ncores + lax.axis_index("c")
        base = pl.multiple_of(wid * per_w, _ACC_ROWS)
        pltpu.sync_copy(idx_hbm.at[pl.ds(base, per_w)], idx_v)
        pltpu.sync_copy(coef_hbm.at[pl.ds(base, per_w)], coef_v)
        prime, fetch_step = _row_prefetch(tab_hbm, idx_v, rows, sems, nchunks)

        def accumulate(rows_ref, c, arow):
            nrow, ncol = 4, 4

            @pl.loop(0, _ACC_ROWS // nrow)
            def _(q):
                r0 = q * nrow
                cvec = [plsc.load_gather(coef_v, [jnp.full((lanes,), c * _ACC_ROWS + r0 + i, jnp.int32)])
                        for i in range(nrow)]
                groups = [[col0 + j * lanes for j in range(ncol)] for col0 in range(0, w, ncol * lanes)]

                def load(cols):
                    return [[rows_ref[r0 + i, pl.ds(col, lanes)] for i in range(nrow)] for col in cols]

                ahead = load(groups[0])
                for g, cols in enumerate(groups):
                    words = ahead
                    if g + 1 < len(groups):
                        ahead = load(groups[g + 1])
                    sums = []
                    for wds in words:
                        halves = [_unpack_bf16_halves(wds[i]) for i in range(nrow)]
                        lo = [cvec[i] * halves[i][0] for i in range(nrow)]
                        hi = [cvec[i] * halves[i][1] for i in range(nrow)]
                        sums.append((functools.reduce(jnp.add, lo), functools.reduce(jnp.add, hi)))
                    for col, (lo, hi) in zip(cols, sums):
                        plsc.addupdate(acc_v.at[arow, pl.ds(col, lanes)], lo)
                        plsc.addupdate(acc_v.at[arow, pl.ds(w + col, lanes)], hi)

        prime()

        @pl.loop(0, tok_w)
        def _(t):
            arow = t % _ACC_TOKENS
            for col in range(0, d, lanes):
                acc_v[arow, pl.ds(col, lanes)] = jnp.zeros((lanes,), F32)
            @pl.loop(0, ns)
            def _(q):
                c = ns * t + q
                fetch_step(c)
                accumulate(rows.at[q], c, arow)

            @pl.when(arow == _ACC_TOKENS - 1)
            def _():
                first_tok = pl.multiple_of(wid * tok_w + t - (_ACC_TOKENS - 1), _ACC_TOKENS)
                pltpu.sync_copy(acc_v, out_hbm.at[pl.ds(first_tok, _ACC_TOKENS)])

    return k(tab, idx, coef)


_DOT_TOKENS = 8


def _sc_dot(tab, idx, x):
    n = idx.shape[0]
    w = tab.shape[1]
    d = 2 * w
    lanes = 16
    ns = _ACC_SLOTS
    mesh, ncores, nw = _sc_mesh()
    ntok = n // PEER_SLOTS
    tok_w = ntok // nw
    per_w = tok_w * PEER_SLOTS
    nchunks = per_w // _ACC_ROWS
    assert ntok % (nw * _DOT_TOKENS) == 0 and _ACC_ROWS % lanes == 0

    @functools.partial(
        pl.kernel, mesh=mesh, out_type=jax.ShapeDtypeStruct((n,), F32),
        scratch_types=[pltpu.VMEM((per_w,), jnp.int32), pltpu.VMEM((per_w,), F32),
                       pltpu.VMEM((_DOT_TOKENS, d), F32), pltpu.VMEM((lanes * lanes,), F32),
                       pltpu.VMEM((ns, _ACC_ROWS, w), tab.dtype), pltpu.SemaphoreType.DMA((ns,))],
        compiler_params=pltpu.CompilerParams(needs_layout_passes=False),
    )
    def k(tab_hbm, idx_hbm, x_hbm, pre_hbm, idx_v, pre_v, x_v, t_v, rows, sems):
        wid = lax.axis_index("s") * ncores + lax.axis_index("c")
        base = pl.multiple_of(wid * per_w, _ACC_ROWS)
        pltpu.sync_copy(idx_hbm.at[pl.ds(base, per_w)], idx_v)
        prime, fetch_step = _row_prefetch(tab_hbm, idx_v, rows, sems, nchunks)

        def dots(rows_ref, c, xrow):
            lane_id = lax.iota(jnp.int32, lanes)
            nrow = 4

            @pl.loop(0, _ACC_ROWS // lanes)
            def _(q):
                for sub in range(lanes // nrow):
                    r0 = q * lanes + sub * nrow
                    accs = [None] * nrow
                    for col in range(0, w, lanes):
                        x_lo = x_v[xrow, pl.ds(col, lanes)]
                        x_hi = x_v[xrow, pl.ds(w + col, lanes)]
                        for i in range(nrow):
                            lo, hi = _unpack_bf16_halves(rows_ref[r0 + i, pl.ds(col, lanes)])
                            term = lo * x_lo + hi * x_hi
                            accs[i] = term if accs[i] is None else accs[i] + term
                    for i in range(nrow):
                        t_v[pl.ds((sub * nrow + i) * lanes, lanes)] = accs[i]
                cols = [plsc.load_gather(t_v, [lane_id * lanes + j]) for j in range(lanes)]
                while len(cols) > 1:
                    cols = [a + b for a, b in zip(cols[::2], cols[1::2])]
                pre_v[pl.ds(pl.multiple_of(c * _ACC_ROWS + q * lanes, lanes), lanes)] = cols[0]

        prime()

        @pl.loop(0, tok_w)
        def _(t):
            xrow = t % _DOT_TOKENS

            @pl.when(xrow == 0)
            def _():
                first_tok = pl.multiple_of(wid * tok_w + t, _DOT_TOKENS)
                pltpu.sync_copy(x_hbm.at[pl.ds(first_tok, _DOT_TOKENS)], x_v)

            @pl.loop(0, ns)
            def _(q):
                c = ns * t + q
                fetch_step(c)
                dots(rows.at[q], c, xrow)

        pltpu.sync_copy(pre_v, pre_hbm.at[pl.ds(base, per_w)])

    return k(tab, idx, x)


def _coef_kernel(pre_ref, gate_ref, coef_ref):
    pre = pre_ref[...]
    coef_ref[...] = gate_ref[...] * (0.5 * pre * (1.0 + lax.erf(pre * (2.0 ** -0.5))))


def _peer_coef(pre, gate):
    n, s = gate.shape
    tn = min(n, 1024)
    return pl.pallas_call(
        _coef_kernel,
        grid=(n // tn,),
        in_specs=[pl.BlockSpec((tn, s), lambda i: (i, 0)), pl.BlockSpec((tn, s), lambda i: (i, 0))],
        out_specs=pl.BlockSpec((tn, s), lambda i: (i, 0)),
        out_shape=jax.ShapeDtypeStruct((n, s), F32),
        compiler_params=_params(("parallel",)),
        name="peer_coef",
    )(pre, gate)


def _final_kernel(acc_ref, x1_ref, g2_ref, fn_ref, y_ref):
    x2 = x1_ref[0] + g2_ref[0, 5:6, :] * acc_ref[0]
    y_ref[0] = x2 * lax.rsqrt(jnp.mean(x2 * x2, axis=-1, keepdims=True) + EPS) * fn_ref[...]


def _peer_final(acc, x1, mod, final_norm):
    b, l, d = x1.shape
    tl = min(l, 512)
    return pl.pallas_call(
        _final_kernel,
        grid=(b, l // tl),
        in_specs=[pl.BlockSpec((1, tl, d), lambda i, j: (i, j, 0)),
                  pl.BlockSpec((1, tl, d), lambda i, j: (i, j, 0)),
                  pl.BlockSpec((1, 6, d), lambda i, j: (i, 0, 0)),
                  pl.BlockSpec((1, d), lambda i, j: (0, 0))],
        out_specs=pl.BlockSpec((1, tl, d), lambda i, j: (i, j, 0)),
        out_shape=jax.ShapeDtypeStruct((b, l, d), F32),
        compiler_params=_params(("parallel", "parallel")),
        name="peer_final",
    )(acc, x1, mod, final_norm.reshape(1, d))


def _front(x, mod, conv0, s0, k_past, v_past, wts, prompt):
    b, l, d = x.shape
    qkv, z, ab, qb, kb, vb, kh, vh = _inproj(x, mod, wts["norm1"], wts["w_in"])
    chunk = CHUNK if prompt else l
    o_a, s_new = _gdn(qkv, z, ab, conv0, s0, wts["conv_w"], wts["alog"], wts["dtb"], wts["onorm"], chunk)
    conv_new = qkv[:, l - (CONV_W - 1):, :]
    if prompt:
        o_b = _attn_prompt(qb, kh, vh, wts["bias_prompt"], wts["lam"], wts["subln"], wts["out_scale"])
    else:
        p = k_past.shape[1]
        bias_past, bias_new = _sample_bias(wts["rel_table"], p, l)
        o_b = _attn_sample(qb, k_past.reshape(b, p, DIFF_WIDTH), v_past.reshape(b, p, DIFF_WIDTH), kb, vb,
                           bias_past, bias_new, wts["lam"], wts["subln"], wts["out_scale"])
    x1, h2, scores = _outproj(o_a, o_b, x, mod, wts["norm2"], wts["w_out"], wts["w_q"], wts["keys"])
    eidx, gate = _topk(scores)
    return (eidx.reshape(-1), gate, h2, x1, mod), (kb, vb, s_new, conv_new)


def _back(pending, wts, coef_hook=None):
    eidx, gate, h2, x1, mod = pending
    b, l, d = h2.shape
    pre = _sc_dot(wts["peer_u"], eidx, h2.reshape(b * l, d))
    coef = _peer_coef(pre.reshape(b * l, PEER_SLOTS), gate)
    if coef_hook is not None:
        coef = coef_hook(coef)
    acc = _sc_accumulate(wts["peer_v"], eidx, coef.reshape(-1))
    return _peer_final(acc.reshape(b, l, d), x1, mod, wts["final_norm"])


def _prompt_rows(x, mod, wts, sample_front):
    b, l, d = x.shape
    conv0 = jnp.zeros((1, CONV_W - 1, QKV_A), F32)
    s0 = jnp.zeros((1, GDN_HEADS, GDN_HEAD_DIM, GDN_HEAD_DIM), F32)
    news, ys, sample = [], [], []
    xs = [lax.dynamic_slice_in_dim(x, t, 1, 0) for t in range(b)]

    def hook(t, coef):
        if t + 2 < b:
            coef, xs[t + 2] = lax.optimization_barrier((coef, xs[t + 2]))
        if t == 0:
            coef, result = sample_front(coef)
            sample.append(result)
        return coef

    for t in range(b):
        pend, new = _front(xs[t], lax.dynamic_slice_in_dim(mod, t, 1, 0), conv0, s0, None, None, wts, True)
        news.append(new)
        y_t = _back(pend, wts, functools.partial(hook, t))
        if t + 3 < b:
            y_t, xs[t + 3] = lax.optimization_barrier((y_t, xs[t + 3]))
        ys.append(y_t)
    y = jnp.concatenate(ys, axis=0)
    return y, [jnp.concatenate([n[k] for n in news], axis=0) for k in range(4)], sample[0]


def _cache_entries(new, b, l):
    kb, vb, s_new, conv_new = new
    return (kb.reshape(1, b, l, DIFF_HEADS, 2 * DIFF_HEAD_DIM), vb.reshape(1, b, l, DIFF_HEADS, DIFF_V_DIM),
            s_new[None], conv_new[None])


def kernel(x_prompt, x_sample, c_prompt, c_sample, cache_k, cache_v, state_gdn, state_conv, w_ada, b_ada,
           norm1, norm2, w_in, conv_w, a_log, dt_bias, gdn_onorm, lam_q1, lam_k1, lam_q2, lam_k2, diff_subln,
           w_out, peer_wq, peer_keys, peer_u, peer_v, rel_table, final_norm):
    assert w_ada.shape[0] == 1, "single-layer step"
    bp = x_prompt.shape[0]
    d = D_MODEL
    lam_init = 0.8 - 0.6 * math.exp(-0.3 * 0)
    lam = (jnp.exp(jnp.sum(lam_q1[0] * lam_k1[0])) - jnp.exp(jnp.sum(lam_q2[0] * lam_k2[0])) + lam_init)
    w = w_in[0]
    w_packed = jnp.concatenate(
        [w[:, :_C_AB], jnp.pad(w[:, 2048:2056], ((0, 0), (0, LANES - 2 * GDN_HEADS))), w[:, 2056:]],
        axis=1).astype(BF16)
    wts = dict(
        norm1=norm1[0], norm2=norm2[0], w_in=w_packed, conv_w=conv_w[0],
        alog=jnp.pad(a_log[0], (0, LANES - GDN_HEADS)).reshape(1, LANES),
        dtb=jnp.pad(dt_bias[0], (0, LANES - GDN_HEADS)).reshape(1, LANES),
        onorm=gdn_onorm[0], lam=jnp.full((1, DIFF_V_DIM), lam, F32), subln=diff_subln[0],
        out_scale=1.0 - lam_init, bias_prompt=_prompt_bias_tiles(rel_table), rel_table=rel_table,
        w_out=w_out[0].astype(BF16), w_q=peer_wq[0].astype(BF16),
        keys=peer_keys[0].reshape(2 * PEER_HEADS, N_KEYS, PEER_HALF).astype(BF16),
        peer_u=_pack_bf16_halves(peer_u), peer_v=_pack_bf16_halves(peer_v), final_norm=final_norm)

    mod = _ada(jnp.concatenate([c_prompt, c_sample], axis=0), w_ada[0], b_ada[0]).reshape(-1, 6, d)
    def sample_front(dep):
        xs, dep = lax.optimization_barrier((x_sample, dep))
        return dep, _front(xs, mod[bp:], state_conv[0], state_gdn[0], cache_k[0], cache_v[0], wts, False)

    yp, new_p, (pend_s, new_s) = _prompt_rows(x_prompt, mod[:bp], wts, sample_front)
    ys = _back(pend_s, wts)
    kp, vp, sp, cp = _cache_entries(new_p, *x_prompt.shape[:2])
    ks, vs, ss, cs = _cache_entries(new_s, *x_sample.shape[:2])
    return yp, ys, kp, vp, sp, cp, ks, vs, ss, cs
```
